```python
import jax
import jax.numpy as jnp
from jax import lax
import numpy as np

D_MODEL = 4096
BATCH = 16
SEQ = 2048
DEPTH = 4

W_A = D_MODEL // 4
W_B = D_MODEL // 2
W_C = D_MODEL // 4
D_MIX = W_A + W_B + W_C
HEAD_DIM = 64
N_Q_HEADS = W_B // HEAD_DIM
KV_GROUP = 8
N_KV_HEADS = N_Q_HEADS // KV_GROUP
KV_W = N_KV_HEADS * HEAD_DIM
WINDOW = 128
BLOCK = 128
CONV_A = 3
CONV_C = 4
N_RG_HEADS = 8
RG_BLOCK = W_C // N_RG_HEADS
RG_C = 8.0
IN_W = 4 * W_A + (W_B + 2 * KV_W + W_B) + 2 * W_C
DEEPNORM_ALPHA = (2.0 * DEPTH) ** 0.25
DEEPNORM_BETA = (8.0 * DEPTH) ** -0.25
LN_EPS = 1e-5
RMS_EPS = 1e-6
NEG_INF = -1e30

kernel_name = "hybrid_shortconv_swa_rglru_deepnorm"


def layer_norm(x, g, b):
    xf = x.astype(jnp.float32)
    mu = xf.mean(-1, keepdims=True)
    var = jnp.mean(jnp.square(xf - mu), -1, keepdims=True)
    y = (xf - mu) * lax.rsqrt(var + LN_EPS) * g.astype(jnp.float32) + b.astype(jnp.float32)
    return y.astype(x.dtype)


def rms_norm(x, g):
    xf = x.astype(jnp.float32)
    y = xf * lax.rsqrt(jnp.mean(xf * xf, -1, keepdims=True) + RMS_EPS) * g.astype(jnp.float32)
    return y.astype(x.dtype)


def causal_depthwise_conv(u, w):
    K = w.shape[0]
    S = u.shape[1]
    up = jnp.pad(u, ((0, 0), (K - 1, 0), (0, 0)))
    y = up[:, 0:S] * w[0]
    for k in range(1, K):
        y = y + up[:, k:k + S] * w[k]
    return y


def sliding_window_attention(q, k, v, sinks):
    B, S = q.shape[0], q.shape[1]
    nb = S // BLOCK
    qb = q.reshape(B, nb, BLOCK, N_KV_HEADS, KV_GROUP, HEAD_DIM)

    def with_prev(t):
        tb = t.reshape(B, nb, BLOCK, N_KV_HEADS, HEAD_DIM)
        prev = jnp.pad(tb, ((0, 0), (1, 0), (0, 0), (0, 0), (0, 0)))[:, :-1]
        return jnp.concatenate([prev, tb], axis=2)

    kk = with_prev(k)
    vv = with_prev(v)
    s = jnp.einsum("bnqhgd,bnkhd->bnhgqk", qb, kk).astype(jnp.float32) * (HEAD_DIM ** -0.5)
    qi = jnp.arange(BLOCK)[:, None]
    kj = jnp.arange(2 * BLOCK)[None, :]
    dist = qi + BLOCK - kj
    band = (dist >= 0) & (dist < WINDOW)
    blk = jnp.arange(nb)[:, None, None]
    valid = band[None] & ((blk > 0) | (kj[None] >= BLOCK))
    s = jnp.where(valid[None, :, None, None], s, NEG_INF)
    sink = sinks.astype(jnp.float32).reshape(N_KV_HEADS, KV_GROUP)[None, None, :, :, None, None]
    m = jnp.maximum(s.max(-1, keepdims=True), sink)
    p = jnp.exp(s - m)
    p = p / (p.sum(-1, keepdims=True) + jnp.exp(sink - m))
    o = jnp.einsum("bnhgqk,bnkhd->bnqhgd", p.astype(v.dtype), vv)
    return o.reshape(B, S, N_Q_HEADS * HEAD_DIM)


def rg_lru(xc, w_r, b_r, w_i, b_i, lam):
    B, S, _ = xc.shape
    xh = xc.reshape(B, S, N_RG_HEADS, RG_BLOCK)
    r = jax.nn.sigmoid(jnp.einsum("bshi,hij->bshj", xh, w_r).reshape(B, S, W_C) + b_r)
    i = jax.nn.sigmoid(jnp.einsum("bshi,hij->bshj", xh, w_i).reshape(B, S, W_C) + b_i)
    log_a = RG_C * r.astype(jnp.float32) * jax.nn.log_sigmoid(lam.astype(jnp.float32))
    a = jnp.exp(log_a)
    u = jnp.sqrt(-jnp.expm1(2.0 * log_a)) * (i * xc).astype(jnp.float32)

    def combine(c1, c2):
        a1, b1 = c1
        a2, b2 = c2
        return a1 * a2, a2 * b1 + b2

    _, hs = lax.associative_scan(combine, (a, u), axis=1)
    return hs.astype(xc.dtype)


def hybrid_layer(x, w_in, conv_a_w, sinks, conv_c_w, conv_c_b, gate_r_w, gate_r_b,
                 gate_i_w, gate_i_b, rg_lambda, norm_a, norm_b, norm_c, w_out, ln_g, ln_b):
    B, S, _ = x.shape
    h = jnp.einsum("bsd,de->bse", x, w_in)
    sizes = [W_A, W_A, W_A, W_A, W_B, KV_W, KV_W, W_B, W_C, W_C]
    offs = np.cumsum(sizes)[:-1].tolist()
    a_b, a_c, a_x, a_g, q, k, v, b_g, c_x, c_g = jnp.split(h, offs, axis=-1)
    y_a = a_b * causal_depthwise_conv(a_c * a_x, conv_a_w)
    y_b = sliding_window_attention(q.reshape(B, S, N_Q_HEADS, HEAD_DIM),
                                   k.reshape(B, S, N_KV_HEADS, HEAD_DIM),
                                   v.reshape(B, S, N_KV_HEADS, HEAD_DIM), sinks)
    xc = causal_depthwise_conv(c_x, conv_c_w) + conv_c_b
    y_c = rg_lru(xc, gate_r_w, gate_r_b, gate_i_w, gate_i_b, rg_lambda)
    mix = jnp.concatenate([rms_norm(y_a, norm_a) * jax.nn.silu(a_g),
                           rms_norm(y_b, norm_b) * jax.nn.silu(b_g),
                           rms_norm(y_c, norm_c) * jax.nn.silu(c_g)], axis=-1)
    out = jnp.einsum("bse,ed->bsd", mix, w_out)
    return layer_norm(DEEPNORM_ALPHA * x + out, ln_g, ln_b)


def _fwd_setup_inputs(seed: int = 0) -> dict:
    key = jax.random.key(seed)
    ks = jax.random.split(key, 20)
    f32 = jnp.float32
    x = jax.random.normal(ks[0], (BATCH, SEQ, D_MODEL), f32)
    w_in = jax.random.normal(ks[1], (DEPTH, D_MODEL, IN_W), f32) * D_MODEL ** -0.5
    conv_a_w = jax.random.normal(ks[2], (DEPTH, CONV_A, W_A), f32) * CONV_A ** -0.5
    sinks = jax.random.normal(ks[3], (DEPTH, N_Q_HEADS), f32) * 0.5
    conv_c_w = jax.random.normal(ks[4], (DEPTH, CONV_C, W_C), f32) * CONV_C ** -0.5
    conv_c_b = jax.random.normal(ks[5], (DEPTH, W_C), f32) * 0.01
    gate_r_w = jax.random.normal(ks[6], (DEPTH, N_RG_HEADS, RG_BLOCK, RG_BLOCK), f32) * RG_BLOCK ** -0.5
    gate_r_b = jax.random.normal(ks[7], (DEPTH, W_C), f32) * 0.01
    gate_i_w = jax.random.normal(ks[8], (DEPTH, N_RG_HEADS, RG_BLOCK, RG_BLOCK), f32) * RG_BLOCK ** -0.5
    gate_i_b = jax.random.normal(ks[9], (DEPTH, W_C), f32) * 0.01
    a_pow_c = jax.random.uniform(ks[10], (DEPTH, W_C), f32, minval=0.9, maxval=0.999)
    a0 = a_pow_c ** (1.0 / RG_C)
    rg_lambda = jnp.log(a0) - jnp.log1p(-a0)
    norm_a = 1.0 + 0.01 * jax.random.normal(ks[11], (DEPTH, W_A), f32)
    norm_b = 1.0 + 0.01 * jax.random.normal(ks[12], (DEPTH, W_B), f32)
    norm_c = 1.0 + 0.01 * jax.random.normal(ks[13], (DEPTH, W_C), f32)
    w_out = jax.random.normal(ks[14], (DEPTH, D_MIX, D_MODEL), f32) * (D_MIX ** -0.5) * DEEPNORM_BETA
    ln_g = 1.0 + 0.01 * jax.random.normal(ks[15], (DEPTH, D_MODEL), f32)
    ln_b = 0.01 * jax.random.normal(ks[16], (DEPTH, D_MODEL), f32)
    return {"x": x, "w_in": w_in, "conv_a_w": conv_a_w, "sinks": sinks,
            "conv_c_w": conv_c_w, "conv_c_b": conv_c_b,
            "gate_r_w": gate_r_w, "gate_r_b": gate_r_b,
            "gate_i_w": gate_i_w, "gate_i_b": gate_i_b, "rg_lambda": rg_lambda,
            "norm_a": norm_a, "norm_b": norm_b, "norm_c": norm_c,
            "w_out": w_out, "ln_g": ln_g, "ln_b": ln_b}


def _fwd_reference(x, w_in, conv_a_w, sinks, conv_c_w, conv_c_b, gate_r_w, gate_r_b,
              gate_i_w, gate_i_b, rg_lambda, norm_a, norm_b, norm_c, w_out, ln_g, ln_b):
    for l in range(DEPTH):
        x = hybrid_layer(x, w_in[l], conv_a_w[l], sinks[l], conv_c_w[l], conv_c_b[l],
                         gate_r_w[l], gate_r_b[l], gate_i_w[l], gate_i_b[l], rg_lambda[l],
                         norm_a[l], norm_b[l], norm_c[l], w_out[l], ln_g[l], ln_b[l])
    return x


import jax as _jax
import jax.numpy as _jnp

TWIN_FORMAT = 'train_step'
FWD_PARAMS = ['x', 'w_in', 'conv_a_w', 'sinks', 'conv_c_w', 'conv_c_b', 'gate_r_w', 'gate_r_b', 'gate_i_w', 'gate_i_b', 'rg_lambda', 'norm_a', 'norm_b', 'norm_c', 'w_out', 'ln_g', 'ln_b']
TWIN_WEIGHTS = ['w_in', 'conv_a_w', 'sinks', 'conv_c_w', 'conv_c_b', 'gate_r_w', 'gate_r_b', 'gate_i_w', 'gate_i_b', 'rg_lambda', 'norm_a', 'norm_b', 'norm_c', 'w_out', 'ln_g', 'ln_b']
TWIN_DIFF_INPUT = 'x'
TWIN_INPUTS = ['x', 'w_in', 'conv_a_w', 'sinks', 'conv_c_w', 'conv_c_b', 'gate_r_w', 'gate_r_b', 'gate_i_w', 'gate_i_b', 'rg_lambda', 'norm_a', 'norm_b', 'norm_c', 'w_out', 'ln_g', 'ln_b', 'loss_target', 'm_w_in', 'm_conv_a_w', 'm_sinks', 'm_conv_c_w', 'm_conv_c_b', 'm_gate_r_w', 'm_gate_r_b', 'm_gate_i_w', 'm_gate_i_b', 'm_rg_lambda', 'm_norm_a', 'm_norm_b', 'm_norm_c', 'm_w_out', 'm_ln_g', 'm_ln_b', 'v_w_in', 'v_conv_a_w', 'v_sinks', 'v_conv_c_w', 'v_conv_c_b', 'v_gate_r_w', 'v_gate_r_b', 'v_gate_i_w', 'v_gate_i_b', 'v_rg_lambda', 'v_norm_a', 'v_norm_b', 'v_norm_c', 'v_w_out', 'v_ln_g', 'v_ln_b']
TWIN_OUTPUTS = ['loss', 'grad_x', 'grad_w_in', 'grad_conv_a_w', 'grad_sinks', 'grad_conv_c_w', 'grad_conv_c_b', 'grad_gate_r_w', 'grad_gate_r_b', 'grad_gate_i_w', 'grad_gate_i_b', 'grad_rg_lambda', 'grad_norm_a', 'grad_norm_b', 'grad_norm_c', 'grad_w_out', 'grad_ln_g', 'grad_ln_b', 'delta_w_in', 'delta_conv_a_w', 'delta_sinks', 'delta_conv_c_w', 'delta_conv_c_b', 'delta_gate_r_w', 'delta_gate_r_b', 'delta_gate_i_w', 'delta_gate_i_b', 'delta_rg_lambda', 'delta_norm_a', 'delta_norm_b', 'delta_norm_c', 'delta_w_out', 'delta_ln_g', 'delta_ln_b', 'new_m_w_in', 'new_m_conv_a_w', 'new_m_sinks', 'new_m_conv_c_w', 'new_m_conv_c_b', 'new_m_gate_r_w', 'new_m_gate_r_b', 'new_m_gate_i_w', 'new_m_gate_i_b', 'new_m_rg_lambda', 'new_m_norm_a', 'new_m_norm_b', 'new_m_norm_c', 'new_m_w_out', 'new_m_ln_g', 'new_m_ln_b', 'new_v_w_in', 'new_v_conv_a_w', 'new_v_sinks', 'new_v_conv_c_w', 'new_v_conv_c_b', 'new_v_gate_r_w', 'new_v_gate_r_b', 'new_v_gate_i_w', 'new_v_gate_i_b', 'new_v_rg_lambda', 'new_v_norm_a', 'new_v_norm_b', 'new_v_norm_c', 'new_v_w_out', 'new_v_ln_g', 'new_v_ln_b']
TWIN_LEAF_KINDS = {'loss': 'loss', 'grad_x': 'grad_x', 'grad_w_in': 'grad_w', 'grad_conv_a_w': 'grad_w', 'grad_sinks': 'grad_w', 'grad_conv_c_w': 'grad_w', 'grad_conv_c_b': 'grad_w', 'grad_gate_r_w': 'grad_w', 'grad_gate_r_b': 'grad_w', 'grad_gate_i_w': 'grad_w', 'grad_gate_i_b': 'grad_w', 'grad_rg_lambda': 'grad_w', 'grad_norm_a': 'grad_w', 'grad_norm_b': 'grad_w', 'grad_norm_c': 'grad_w', 'grad_w_out': 'grad_w', 'grad_ln_g': 'grad_w', 'grad_ln_b': 'grad_w', 'delta_w_in': 'delta_w', 'delta_conv_a_w': 'delta_w', 'delta_sinks': 'delta_w', 'delta_conv_c_w': 'delta_w', 'delta_conv_c_b': 'delta_w', 'delta_gate_r_w': 'delta_w', 'delta_gate_r_b': 'delta_w', 'delta_gate_i_w': 'delta_w', 'delta_gate_i_b': 'delta_w', 'delta_rg_lambda': 'delta_w', 'delta_norm_a': 'delta_w', 'delta_norm_b': 'delta_w', 'delta_norm_c': 'delta_w', 'delta_w_out': 'delta_w', 'delta_ln_g': 'delta_w', 'delta_ln_b': 'delta_w', 'new_m_w_in': 'new_m', 'new_m_conv_a_w': 'new_m', 'new_m_sinks': 'new_m', 'new_m_conv_c_w': 'new_m', 'new_m_conv_c_b': 'new_m', 'new_m_gate_r_w': 'new_m', 'new_m_gate_r_b': 'new_m', 'new_m_gate_i_w': 'new_m', 'new_m_gate_i_b': 'new_m', 'new_m_rg_lambda': 'new_m', 'new_m_norm_a': 'new_m', 'new_m_norm_b': 'new_m', 'new_m_norm_c': 'new_m', 'new_m_w_out': 'new_m', 'new_m_ln_g': 'new_m', 'new_m_ln_b': 'new_m', 'new_v_w_in': 'new_v', 'new_v_conv_a_w': 'new_v', 'new_v_sinks': 'new_v', 'new_v_conv_c_w': 'new_v', 'new_v_conv_c_b': 'new_v', 'new_v_gate_r_w': 'new_v', 'new_v_gate_r_b': 'new_v', 'new_v_gate_i_w': 'new_v', 'new_v_gate_i_b': 'new_v', 'new_v_rg_lambda': 'new_v', 'new_v_norm_a': 'new_v', 'new_v_norm_b': 'new_v', 'new_v_norm_c': 'new_v', 'new_v_w_out': 'new_v', 'new_v_ln_g': 'new_v', 'new_v_ln_b': 'new_v'}


def _forward(args):
    return _fwd_reference(*[args[k] for k in FWD_PARAMS])


def _output_shape():
    def fwd():
        inp = _fwd_setup_inputs(0)
        return _fwd_reference(*[inp[k] for k in FWD_PARAMS])
    out = _jax.eval_shape(fwd)
    return out.shape, out.dtype

N_MICROBATCH = 1
ADAM_LR = 0.001
ADAM_B1 = 0.9
ADAM_B2 = 0.999
ADAM_EPS = 1e-08
ADAM_WD = 0.01
ADAM_STEP = 10
PER_EXAMPLE_BATCH_AXIS = {'x': 0, 'loss_target': 0}
SHARED_INPUTS = []
_WEIGHT_DTYPES = {'w_in': _jnp.float32, 'conv_a_w': _jnp.float32, 'sinks': _jnp.float32, 'conv_c_w': _jnp.float32, 'conv_c_b': _jnp.float32, 'gate_r_w': _jnp.float32, 'gate_r_b': _jnp.float32, 'gate_i_w': _jnp.float32, 'gate_i_b': _jnp.float32, 'rg_lambda': _jnp.float32, 'norm_a': _jnp.float32, 'norm_b': _jnp.float32, 'norm_c': _jnp.float32, 'w_out': _jnp.float32, 'ln_g': _jnp.float32, 'ln_b': _jnp.float32}
MOMENT_SCALE = {'w_in': 7.826491e-03, 'conv_a_w': 6.759713e-03, 'sinks': 2.323470e-03, 'conv_c_w': 7.317472e-03, 'conv_c_b': 8.151798e-02, 'gate_r_w': 2.216088e-03, 'gate_r_b': 1.876255e-03, 'gate_i_w': 3.990539e-03, 'gate_i_b': 2.614695e-03, 'rg_lambda': 3.576494e-03, 'norm_a': 6.825100e-03, 'norm_b': 7.165447e-03, 'norm_c': 7.001975e-03, 'w_out': 1.654474e-02, 'ln_g': 4.002102e+00, 'ln_b': 1.412565e-01}


def _to_microbatches(a, axis):
    t = _jnp.moveaxis(a, axis, 0)
    t = t.reshape((N_MICROBATCH, t.shape[0] // N_MICROBATCH) + t.shape[1:])
    return _jnp.moveaxis(t, 1, axis + 1)


def setup_inputs(seed: int = 0) -> dict:
    inp = _fwd_setup_inputs(seed)
    key = _jax.random.fold_in(_jax.random.key(seed), 7919)
    shape, _ = _output_shape()
    out = dict(inp)
    out["loss_target"] = _jax.random.normal(_jax.random.fold_in(key, 0), shape, _jnp.float32)
    for i, name in enumerate(TWIN_WEIGHTS):
        w = inp[name].astype(_jnp.float32)
        if MOMENT_SCALE is None:
            s = _jnp.sqrt(_jnp.mean(_jnp.square(w)) + 1e-30)
        else:
            s = MOMENT_SCALE[name]
        km, kv = _jax.random.split(_jax.random.fold_in(key, i + 1))
        out[name] = w
        out["m_" + name] = s * _jax.random.normal(km, w.shape, _jnp.float32)
        out["v_" + name] = (s * s) * _jax.random.uniform(kv, w.shape, _jnp.float32, 0.5, 1.5)
    if N_MICROBATCH > 1:
        for name, axis in PER_EXAMPLE_BATCH_AXIS.items():
            out[name] = _to_microbatches(out[name], axis)
    return {'x': out['x'], 'w_in': out['w_in'], 'conv_a_w': out['conv_a_w'], 'sinks': out['sinks'], 'conv_c_w': out['conv_c_w'], 'conv_c_b': out['conv_c_b'], 'gate_r_w': out['gate_r_w'], 'gate_r_b': out['gate_r_b'], 'gate_i_w': out['gate_i_w'], 'gate_i_b': out['gate_i_b'], 'rg_lambda': out['rg_lambda'], 'norm_a': out['norm_a'], 'norm_b': out['norm_b'], 'norm_c': out['norm_c'], 'w_out': out['w_out'], 'ln_g': out['ln_g'], 'ln_b': out['ln_b'], 'loss_target': out['loss_target'], 'm_w_in': out['m_w_in'], 'm_conv_a_w': out['m_conv_a_w'], 'm_sinks': out['m_sinks'], 'm_conv_c_w': out['m_conv_c_w'], 'm_conv_c_b': out['m_conv_c_b'], 'm_gate_r_w': out['m_gate_r_w'], 'm_gate_r_b': out['m_gate_r_b'], 'm_gate_i_w': out['m_gate_i_w'], 'm_gate_i_b': out['m_gate_i_b'], 'm_rg_lambda': out['m_rg_lambda'], 'm_norm_a': out['m_norm_a'], 'm_norm_b': out['m_norm_b'], 'm_norm_c': out['m_norm_c'], 'm_w_out': out['m_w_out'], 'm_ln_g': out['m_ln_g'], 'm_ln_b': out['m_ln_b'], 'v_w_in': out['v_w_in'], 'v_conv_a_w': out['v_conv_a_w'], 'v_sinks': out['v_sinks'], 'v_conv_c_w': out['v_conv_c_w'], 'v_conv_c_b': out['v_conv_c_b'], 'v_gate_r_w': out['v_gate_r_w'], 'v_gate_r_b': out['v_gate_r_b'], 'v_gate_i_w': out['v_gate_i_w'], 'v_gate_i_b': out['v_gate_i_b'], 'v_rg_lambda': out['v_rg_lambda'], 'v_norm_a': out['v_norm_a'], 'v_norm_b': out['v_norm_b'], 'v_norm_c': out['v_norm_c'], 'v_w_out': out['v_w_out'], 'v_ln_g': out['v_ln_g'], 'v_ln_b': out['v_ln_b']}


def _loss(weights, diff, rest, loss_target):
    with _jax.named_scope("forward"):
        args = {**rest, TWIN_DIFF_INPUT: diff, **{k: w.astype(_WEIGHT_DTYPES[k]) for k, w in weights.items()}}
        y = _forward(args)
    with _jax.named_scope("loss_head"):
        err = _jnp.square(y.astype(_jnp.float32) - loss_target)
        return 0.5 * _jnp.sum(_jnp.mean(err, axis=-1)) if err.ndim else 0.5 * err


def _adamw(w, g, m, v):
    m = ADAM_B1 * m + (1.0 - ADAM_B1) * g
    v = ADAM_B2 * v + (1.0 - ADAM_B2) * _jnp.square(g)
    m_hat = m / (1.0 - ADAM_B1 ** ADAM_STEP)
    v_hat = v / (1.0 - ADAM_B2 ** ADAM_STEP)
    delta = -ADAM_LR * (m_hat / (_jnp.sqrt(v_hat) + ADAM_EPS) + ADAM_WD * w)
    return delta, m, v


def reference(x, w_in, conv_a_w, sinks, conv_c_w, conv_c_b, gate_r_w, gate_r_b, gate_i_w, gate_i_b, rg_lambda, norm_a, norm_b, norm_c, w_out, ln_g, ln_b, loss_target, m_w_in, m_conv_a_w, m_sinks, m_conv_c_w, m_conv_c_b, m_gate_r_w, m_gate_r_b, m_gate_i_w, m_gate_i_b, m_rg_lambda, m_norm_a, m_norm_b, m_norm_c, m_w_out, m_ln_g, m_ln_b, v_w_in, v_conv_a_w, v_sinks, v_conv_c_w, v_conv_c_b, v_gate_r_w, v_gate_r_b, v_gate_i_w, v_gate_i_b, v_rg_lambda, v_norm_a, v_norm_b, v_norm_c, v_w_out, v_ln_g, v_ln_b):
    given = dict(x=x, w_in=w_in, conv_a_w=conv_a_w, sinks=sinks, conv_c_w=conv_c_w, conv_c_b=conv_c_b, gate_r_w=gate_r_w, gate_r_b=gate_r_b, gate_i_w=gate_i_w, gate_i_b=gate_i_b, rg_lambda=rg_lambda, norm_a=norm_a, norm_b=norm_b, norm_c=norm_c, w_out=w_out, ln_g=ln_g, ln_b=ln_b, loss_target=loss_target, m_w_in=m_w_in, m_conv_a_w=m_conv_a_w, m_sinks=m_sinks, m_conv_c_w=m_conv_c_w, m_conv_c_b=m_conv_c_b, m_gate_r_w=m_gate_r_w, m_gate_r_b=m_gate_r_b, m_gate_i_w=m_gate_i_w, m_gate_i_b=m_gate_i_b, m_rg_lambda=m_rg_lambda, m_norm_a=m_norm_a, m_norm_b=m_norm_b, m_norm_c=m_norm_c, m_w_out=m_w_out, m_ln_g=m_ln_g, m_ln_b=m_ln_b, v_w_in=v_w_in, v_conv_a_w=v_conv_a_w, v_sinks=v_sinks, v_conv_c_w=v_conv_c_w, v_conv_c_b=v_conv_c_b, v_gate_r_w=v_gate_r_w, v_gate_r_b=v_gate_r_b, v_gate_i_w=v_gate_i_w, v_gate_i_b=v_gate_i_b, v_rg_lambda=v_rg_lambda, v_norm_a=v_norm_a, v_norm_b=v_norm_b, v_norm_c=v_norm_c, v_w_out=v_w_out, v_ln_g=v_ln_g, v_ln_b=v_ln_b)
    weights = {n: given[n] for n in TWIN_WEIGHTS}
    shared = {n: given[n] for n in SHARED_INPUTS}
    per_example = {n: given[n] for n in ['x']}
    grad_fn = _jax.value_and_grad(_loss, argnums=(0, 1))

    def one_microbatch(ex, loss_target):
        ex = dict(ex)
        diff = ex.pop(TWIN_DIFF_INPUT)
        return grad_fn(weights, diff, {**shared, **ex}, loss_target)

    if N_MICROBATCH == 1:
        loss, (grad_w, grad_x) = one_microbatch(per_example, given["loss_target"])
    else:
        def body(carry, xs):
            loss_sum, grad_sum = carry
            l_k, (gw_k, gx_k) = one_microbatch(xs[0], xs[1])
            with _jax.named_scope("update"):
                return (loss_sum + l_k, _jax.tree.map(_jnp.add, grad_sum, gw_k)), gx_k

        init = (_jnp.zeros((), _jnp.float32), _jax.tree.map(_jnp.zeros_like, weights))
        (loss, grad_w), grad_x = _jax.lax.scan(body, init, (per_example, given["loss_target"]))
    with _jax.named_scope("update"):
        delta_w, new_m, new_v = {}, {}, {}
        for n in TWIN_WEIGHTS:
            delta_w[n], new_m[n], new_v[n] = _adamw(weights[n], grad_w[n], given["m_" + n], given["v_" + n])
    return (loss, grad_x, *[grad_w[n] for n in TWIN_WEIGHTS], *[delta_w[n] for n in TWIN_WEIGHTS],
            *[new_m[n] for n in TWIN_WEIGHTS], *[new_v[n] for n in TWIN_WEIGHTS])
```

```python
import functools
import math

import jax
import jax.numpy as jnp
from jax import lax
from jax.experimental import pallas as pl
from jax.experimental.pallas import tpu as pltpu

F32 = jnp.float32
BF16 = jnp.bfloat16
_MXU_DTYPE = jnp.bfloat16

HEAD_DIM = 64
KV_GROUP = 8
BLOCK = 128
N_RG_HEADS = 8
RG_C = 8.0
LN_EPS = 1e-5
RMS_EPS = 1e-6
NEG_INF = -1e30
ADAM_LR, ADAM_B1, ADAM_B2, ADAM_EPS, ADAM_WD, ADAM_STEP = 0.001, 0.9, 0.999, 1e-08, 0.01, 10
N_CHIPS = 4
N_DEV = 8
SMALL_ROWS = 280
VMEM_LIMIT = 56 * 1024 * 1024

MESH = pl.DeviceIdType.MESH
ANY = pl.BlockSpec(memory_space=pl.ANY)


def _pcall(body, *, name, **kw):
    return pl.pallas_call(body, name=name, **kw)


def _params(sem=None):
    return pltpu.CompilerParams(dimension_semantics=sem, vmem_limit_bytes=VMEM_LIMIT)


def _tile(dim, pref, mult=128):
    best = None
    for t in range(mult, min(dim, pref) + 1, mult):
        if dim % t == 0:
            best = t
    return best if best is not None else dim


def _dot(a, b, dims):
    return lax.dot_general(a.astype(_MXU_DTYPE), b.astype(_MXU_DTYPE), (dims, ((), ())),
                           preferred_element_type=F32)


NN = ((1,), (0,))
NT = ((1,), (1,))
TN = ((0,), (0,))


def _mm(a, b, *, mode, out_dtype, name, tm=1024, tn=1024, tk=512, b_layer=None, add=None, add_scale=1.0,
        out_buf=None, out_layer=None, out_col=0):
    if mode == "nn":
        (M, K), N = a.shape, b.shape[-1]
    elif mode == "nt":
        (M, K), N = a.shape, b.shape[-2]
    else:
        (K, M), N = a.shape, b.shape[-1]
    tm, tn, tk = _tile(M, tm), _tile(N, tn), _tile(K, tk)
    nk = K // tk
    dims = {"nn": NN, "nt": NT, "tn": TN}[mode]

    def body(*refs):
        refs = list(refs)
        a_ref, b_ref = refs.pop(0), refs.pop(0)
        add_ref = refs.pop(0) if add is not None else None
        if out_buf is not None:
            refs.pop(0)
        o_ref, acc = refs
        k = pl.program_id(2)

        @pl.when(k == 0)
        def _():
            acc[...] = jnp.zeros_like(acc)

        acc[...] += _dot(a_ref[...], b_ref[...], dims)

        @pl.when(k == nk - 1)
        def _():
            r = acc[...]
            if add_ref is not None:
                r = r + add_scale * add_ref[...]
            o_ref[...] = r.astype(out_dtype)

    a_spec = {"nn": pl.BlockSpec((tm, tk), lambda i, j, k: (i, k)),
              "nt": pl.BlockSpec((tm, tk), lambda i, j, k: (i, k)),
              "tn": pl.BlockSpec((tk, tm), lambda i, j, k: (k, i))}[mode]
    b_blk, b_idx = {"nn": ((tk, tn), lambda i, j, k: (k, j)),
                    "nt": ((tn, tk), lambda i, j, k: (j, k)),
                    "tn": ((tk, tn), lambda i, j, k: (k, j))}[mode]
    if b_layer is not None:
        b_spec = pl.BlockSpec((None,) + b_blk, lambda i, j, k: (b_layer,) + b_idx(i, j, k))
    else:
        b_spec = pl.BlockSpec(b_blk, b_idx)
    in_specs, operands = [a_spec, b_spec], [a, b]
    if add is not None:
        in_specs.append(pl.BlockSpec((tm, tn), lambda i, j, k: (i, j)))
        operands.append(add)
    aliases = {}
    if out_buf is not None:
        in_specs.append(ANY)
        operands.append(out_buf)
        aliases = {len(operands) - 1: 0}
        cb = out_col // tn
        assert out_col % tn == 0
        out_shape = jax.ShapeDtypeStruct(out_buf.shape, out_buf.dtype)
        out_spec = pl.BlockSpec((None, tm, tn), lambda i, j, k: (out_layer, i, cb + j))
    else:
        out_shape = jax.ShapeDtypeStruct((M, N), out_dtype)
        out_spec = pl.BlockSpec((tm, tn), lambda i, j, k: (i, j))
    return _pcall(body, name=name, out_shape=out_shape, grid=(M // tm, N // tn, nk), in_specs=in_specs,
                  out_specs=out_spec, scratch_shapes=[pltpu.VMEM((tm, tn), F32)], input_output_aliases=aliases,
                  compiler_params=_params(("parallel", "parallel", "arbitrary")))(*operands)


def _colspecs(off, width, rows, rowmap):
    bw = math.gcd(off, width) if off else width
    specs = [pl.BlockSpec((rows, bw), functools.partial(lambda cb, *g: (rowmap(*g), cb), off // bw + i))
             for i in range(width // bw)]
    return specs, bw


def _cat(refs):
    vals = [r[...] for r in refs]
    return vals[0] if len(vals) == 1 else jnp.concatenate(vals, axis=1)


def _take(refs, n):
    out = refs[:n]
    del refs[:n]
    return out


def _sigmoid(x):
    return 1.0 / (1.0 + jnp.exp(-x))


def _rms(y, gamma):
    rstd = lax.rsqrt(jnp.mean(y * y, axis=-1, keepdims=True) + RMS_EPS)
    xn = y * rstd
    return xn, rstd, xn * gamma


def _rms_bwd(dn, xn, rstd, gamma):
    dng = dn * gamma
    return rstd * (dng - xn * jnp.mean(dng * xn, axis=-1, keepdims=True))


def _shift_down(x, s, carry8):
    rolled = pltpu.roll(x, s, 0)
    cr = pltpu.roll(carry8, s, 0)
    row8 = lax.broadcasted_iota(jnp.int32, carry8.shape, 0)
    top = jnp.where(row8 < s, cr, rolled[0:8])
    return jnp.concatenate([top, rolled[8:]], axis=0)


def _shift_up(x, s, carry8):
    n = x.shape[0]
    rolled = pltpu.roll(x, n - s, 0)
    cr = pltpu.roll(carry8, 8 - s, 0)
    row8 = lax.broadcasted_iota(jnp.int32, carry8.shape, 0)
    bot = jnp.where(row8 >= 8 - s, cr, rolled[n - 8:])
    return jnp.concatenate([rolled[:n - 8], bot], axis=0)


def _chunk_scan(a, b):
    n = a.shape[0]
    r8 = lax.broadcasted_iota(jnp.int32, a.shape, 0) & 7
    for d in (1, 2, 4):
        ok = r8 >= d
        a_sh = jnp.where(ok, pltpu.roll(a, d, 0), 1.0)
        b_sh = jnp.where(ok, pltpu.roll(b, d, 0), 0.0)
        b = a * b_sh + b
        a = a * a_sh
    return a, b


def _chunk_scan_rev(c, b):
    n = c.shape[0]
    r8 = lax.broadcasted_iota(jnp.int32, c.shape, 0) & 7
    for d in (1, 2, 4):
        ok = r8 + d <= 7
        c_sh = jnp.where(ok, pltpu.roll(c, n - d, 0), 1.0)
        b_sh = jnp.where(ok, pltpu.roll(b, n - d, 0), 0.0)
        b = b + c * b_sh
        c = c * c_sh
    return c, b


def _log1p(x):
    w = 1.0 + x
    return jnp.where(w == 1.0, x, jnp.log(w) * (x / (w - 1.0)))


def _log_sigmoid(x):
    return jnp.minimum(x, 0.0) - _log1p(jnp.exp(-jnp.abs(x)))


def _expm1(x):
    u = jnp.exp(x)
    lu = jnp.log(u)
    small = jnp.where(u == 1.0, x, (u - 1.0) * (x / jnp.where(lu == 0.0, 1.0, lu)))
    return jnp.where(jnp.abs(x) < 0.5, small, u - 1.0)


def _gates(xc, wr_ref, wi_ref, br, bi, lam):
    hw = xc.shape[1] // N_RG_HEADS
    gr = jnp.concatenate([_dot(xc[:, h * hw:(h + 1) * hw], wr_ref[h], NN) for h in range(N_RG_HEADS)], axis=1) + br
    gi = jnp.concatenate([_dot(xc[:, h * hw:(h + 1) * hw], wi_ref[h], NN) for h in range(N_RG_HEADS)], axis=1) + bi
    r, i = _sigmoid(gr), _sigmoid(gi)
    ls = _log_sigmoid(lam)
    la = RG_C * r * ls
    a = jnp.exp(la)
    sq = jnp.sqrt(-_expm1(2.0 * la))
    return r, i, ls, a, sq


def _ac_fwd(h, caw, ccw, pv, wr, wi, *, S, D, tt, name):
    T = h.shape[0]
    W = D // 4
    nt = S // tt
    rowmap = lambda s, t: s * nt + t
    c_off = D + D // 2 + 2 * (D // 16) + D // 2
    offs = [0, W, 2 * W, 3 * W, c_off, c_off + W]
    in_specs, counts = [], []
    for off in offs:
        specs, _ = _colspecs(off, W, tt, rowmap)
        in_specs += specs
        counts.append(len(specs))
    full = lambda shape: pl.BlockSpec(shape, lambda s, t: (0,) * len(shape))
    in_specs += [full(caw.shape), full(ccw.shape), full(pv.shape), full(wr.shape), full(wi.shape)]

    def body(*refs):
        refs = list(refs)
        ab, ac, ax, ag, cx, cg = [_cat(_take(refs, n)) for n in counts]
        caw_ref, ccw_ref, pv_ref, wr_ref, wi_ref = _take(refs, 5)
        mixac_ref, cv_ref, xc_ref, yc_ref = _take(refs, 4)
        carry_p, carry_cx, carry_h, a_s, b_s = refs
        t = pl.program_id(1)

        @pl.when(t == 0)
        def _():
            carry_p[...] = jnp.zeros_like(carry_p)
            carry_cx[...] = jnp.zeros_like(carry_cx)
            carry_h[...] = jnp.zeros_like(carry_h)

        ccb, br, bi, lam, na, nc = [pv_ref[k:k + 1, :] for k in range(6)]
        p = ac * ax
        cp = carry_p[...]
        cv = caw_ref[2:3, :] * p + caw_ref[1:2, :] * _shift_down(p, 1, cp) + caw_ref[0:1, :] * _shift_down(p, 2, cp)
        carry_p[...] = p[tt - 8:tt]
        cv_ref[...] = cv
        _, _, n_a = _rms(ab * cv, na)
        mix_a = n_a * (ag * _sigmoid(ag))
        ccx = carry_cx[...]
        xc = (ccw_ref[3:4, :] * cx + ccw_ref[2:3, :] * _shift_down(cx, 1, ccx) + ccw_ref[1:2, :] * _shift_down(cx, 2, ccx)
              + ccw_ref[0:1, :] * _shift_down(cx, 3, ccx) + ccb)
        carry_cx[...] = cx[tt - 8:tt]
        xc_ref[...] = xc
        r, i, ls, a, sq = _gates(xc, wr_ref, wi_ref, br, bi, lam)
        u = sq * (i * xc)
        a_c, b_c = _chunk_scan(a, u)
        a_s[...] = a_c
        b_s[...] = b_c

        def step(k, hprev):
            rows = pl.ds(pl.multiple_of(k * 8, 8), 8)
            hc = a_s[rows, :] * hprev + b_s[rows, :]
            yc_ref[rows, :] = hc
            return hc[7:8, :]

        hlast = lax.fori_loop(0, tt // 8, step, carry_h[0:1, :])
        carry_h[...] = jnp.broadcast_to(hlast, carry_h.shape)
        _, _, n_c = _rms(yc_ref[...], nc)
        mix_c = n_c * (cg * _sigmoid(cg))
        mixac_ref[...] = jnp.concatenate([mix_a, mix_c], axis=1).astype(mixac_ref.dtype)

    row_blk = lambda w: pl.BlockSpec((tt, w), lambda s, t: (rowmap(s, t), 0))
    return _pcall(
        body, name=name, grid=(T // S, nt), in_specs=in_specs,
        out_shape=(jax.ShapeDtypeStruct((T, 2 * W), BF16), jax.ShapeDtypeStruct((T, W), F32),
                   jax.ShapeDtypeStruct((T, W), F32), jax.ShapeDtypeStruct((T, W), F32)),
        out_specs=(row_blk(2 * W), row_blk(W), row_blk(W), row_blk(W)),
        scratch_shapes=[pltpu.VMEM((8, W), F32), pltpu.VMEM((8, W), F32), pltpu.VMEM((8, W), F32),
                        pltpu.VMEM((tt, W), F32), pltpu.VMEM((tt, W), F32)],
        compiler_params=_params(("arbitrary", "arbitrary")),
    )(*([h] * sum(counts)), caw, ccw, pv, wr, wi)


def _ac_bwd(h, cv, xc, yc, dmix, caw, ccw, pv, wr, wi, *, S, D, tt, name):
    T = h.shape[0]
    W = D // 4
    nt = S // tt
    rowmap = lambda s, t: s * nt + (nt - 1 - t)
    c_off = D + D // 2 + 2 * (D // 16) + D // 2
    offs = [0, W, 2 * W, 3 * W, c_off, c_off + W]
    in_specs, counts = [], []
    for off in offs:
        specs, _ = _colspecs(off, W, tt, rowmap)
        in_specs += specs
        counts.append(len(specs))
    row_blk = lambda w, cb=0: pl.BlockSpec((tt, w), lambda s, t: (rowmap(s, t), cb))
    in_specs += [row_blk(W), row_blk(W), row_blk(W)]
    in_specs.append(pl.BlockSpec((8, W), lambda s, t: (jnp.maximum(rowmap(s, t) * (tt // 8) - 1, 0), 0)))
    in_specs += [row_blk(W, 0), row_blk(W, 3)]
    full = lambda shape: pl.BlockSpec(shape, lambda s, t: (0,) * len(shape))
    in_specs += [full(caw.shape), full(ccw.shape), full(pv.shape), full(wr.shape), full(wi.shape)]

    def body(*refs):
        refs = list(refs)
        ab, ac, ax, ag, cx, cg = [_cat(_take(refs, n)) for n in counts]
        cv_ref, xc_ref, yc_ref, halo_ref, dma_ref, dmc_ref, caw_ref, ccw_ref, pv_ref, wr_ref, wi_ref = _take(refs, 11)
        dha_ref, dhc_ref, vec_ref, dwr_ref, dwi_ref = _take(refs, 5)
        carry_dcv, carry_dxc, carry_a, carry_g, c_s, b_s, g_s = refs
        s_id, t = pl.program_id(0), pl.program_id(1)

        @pl.when(t == 0)
        def _():
            for cr in (carry_dcv, carry_dxc, carry_a, carry_g):
                cr[...] = jnp.zeros_like(cr)

        @pl.when((t == 0) & (s_id == 0))
        def _():
            vec_ref[...] = jnp.zeros_like(vec_ref)
            dwr_ref[...] = jnp.zeros_like(dwr_ref)
            dwi_ref[...] = jnp.zeros_like(dwi_ref)

        def acc_row(k, val):
            vec_ref[k:k + 1, :] += jnp.sum(val, axis=0, keepdims=True)

        ccb, br, bi, lam, na, nc = [pv_ref[k:k + 1, :] for k in range(6)]
        cv = cv_ref[...]
        dmix_a = dma_ref[...]
        p = ac * ax
        xn, rstd, n_a = _rms(ab * cv, na)
        sg = _sigmoid(ag)
        dn = dmix_a * (ag * sg)
        dag = dmix_a * n_a * (sg * (1.0 + ag * (1.0 - sg)))
        acc_row(3, dn * xn)
        dya = _rms_bwd(dn, xn, rstd, na)
        dab = dya * cv
        dcv = dya * ab
        cd = carry_dcv[...]
        d1, d2 = _shift_up(dcv, 1, cd), _shift_up(dcv, 2, cd)
        dp = caw_ref[2:3, :] * dcv + caw_ref[1:2, :] * d1 + caw_ref[0:1, :] * d2
        acc_row(2, p * dcv)
        acc_row(1, p * d1)
        acc_row(0, p * d2)
        carry_dcv[...] = dcv[0:8]
        dha_ref[...] = jnp.concatenate([dab, dp * ax, dp * ac, dag], axis=1).astype(dha_ref.dtype)
        xc = xc_ref[...]
        yc = yc_ref[...]
        dmix_c = dmc_ref[...]
        xn, rstd, n_c = _rms(yc, nc)
        sg = _sigmoid(cg)
        dn = dmix_c * (cg * sg)
        dcg = dmix_c * n_c * (sg * (1.0 + cg * (1.0 - sg)))
        acc_row(12, dn * xn)
        dyc = _rms_bwd(dn, xn, rstd, nc)
        r, i, ls, a, sq = _gates(xc, wr_ref, wi_ref, br, bi, lam)
        halo = jnp.where(t == nt - 1, 0.0, halo_ref[...])
        hprev = _shift_down(yc, 1, halo)
        c_c, b_c = _chunk_scan_rev(_shift_up(a, 1, carry_a[...]), dyc)
        c_s[...] = c_c
        b_s[...] = b_c

        def step(k, gnext):
            rows = pl.ds(pl.multiple_of((tt // 8 - 1 - k) * 8, 8), 8)
            gc = b_s[rows, :] + c_s[rows, :] * gnext
            g_s[rows, :] = gc
            return gc[0:1, :]

        lax.fori_loop(0, tt // 8, step, carry_g[0:1, :])
        g = g_s[...]
        carry_g[...] = g[0:8]
        carry_a[...] = a[0:8]
        da = g * hprev
        ixc = i * xc
        dsq = g * ixc
        di = g * sq * xc
        dxc = g * sq * i
        dla = da * a - dsq * (a * a) / sq
        dr = dla * (RG_C * ls)
        acc_row(11, dla * (RG_C * r) * _sigmoid(-lam))
        dgr = dr * r * (1.0 - r)
        dgi = di * i * (1.0 - i)
        acc_row(9, dgr)
        acc_row(10, dgi)
        hw = W // N_RG_HEADS
        parts = []
        for hd in range(N_RG_HEADS):
            sl = slice(hd * hw, (hd + 1) * hw)
            dwr_ref[hd] += _dot(xc[:, sl], dgr[:, sl], TN)
            dwi_ref[hd] += _dot(xc[:, sl], dgi[:, sl], TN)
            parts.append(_dot(dgr[:, sl], wr_ref[hd], NT) + _dot(dgi[:, sl], wi_ref[hd], NT))
        dxc = dxc + jnp.concatenate(parts, axis=1)
        ce = carry_dxc[...]
        e1, e2, e3 = _shift_up(dxc, 1, ce), _shift_up(dxc, 2, ce), _shift_up(dxc, 3, ce)
        dcx = ccw_ref[3:4, :] * dxc + ccw_ref[2:3, :] * e1 + ccw_ref[1:2, :] * e2 + ccw_ref[0:1, :] * e3
        acc_row(7, cx * dxc)
        acc_row(6, cx * e1)
        acc_row(5, cx * e2)
        acc_row(4, cx * e3)
        acc_row(8, dxc)
        carry_dxc[...] = dxc[0:8]
        dhc_ref[...] = jnp.concatenate([dcx, dcg], axis=1).astype(dhc_ref.dtype)

    const = lambda shape: pl.BlockSpec(shape, lambda s, t: (0,) * len(shape))
    return _pcall(
        body, name=name, grid=(T // S, nt), in_specs=in_specs,
        out_shape=(jax.ShapeDtypeStruct((T, 4 * W), BF16), jax.ShapeDtypeStruct((T, 2 * W), BF16),
                   jax.ShapeDtypeStruct((16, W), F32), jax.ShapeDtypeStruct(wr.shape, F32),
                   jax.ShapeDtypeStruct(wi.shape, F32)),
        out_specs=(row_blk(4 * W), row_blk(2 * W), const((16, W)), const(wr.shape), const(wi.shape)),
        scratch_shapes=[pltpu.VMEM((8, W), F32)] * 4 + [pltpu.VMEM((tt, W), F32)] * 3,
        compiler_params=_params(("arbitrary", "arbitrary")),
    )(*([h] * sum(counts)), cv, xc, yc, yc, dmix, dmix, caw, ccw, pv, wr, wi)


def _lo_mask():
    return lax.broadcasted_iota(jnp.int32, (1, 2 * HEAD_DIM), 1) < HEAD_DIM


def _dup(blk, odd, lo):
    rot = pltpu.roll(blk, HEAD_DIM, 1)
    return jnp.where(lo, rot, blk) if odd else jnp.where(lo, blk, rot)


def _stack_heads(x, hh, lo, masked):
    parts = []
    for g in range(KV_GROUP):
        jq = hh * KV_GROUP + g
        pb = x[:, (jq // 2) * 128:(jq // 2 + 1) * 128]
        if masked:
            pb = jnp.where(lo if jq % 2 == 0 else jnp.logical_not(lo), pb, 0.0)
        parts.append(pb)
    return jnp.concatenate(parts, axis=0)


def _unstack_pairs(st, lo):
    return [jnp.where(lo, st[(2 * pi) * BLOCK:(2 * pi + 1) * BLOCK], st[(2 * pi + 1) * BLOCK:(2 * pi + 2) * BLOCK])
            for pi in range(KV_GROUP // 2)]


def _window(ref, n):
    prev = ref[pl.ds(pl.multiple_of(jnp.maximum(n - 1, 0) * BLOCK, BLOCK), BLOCK), :]
    cur = ref[pl.ds(pl.multiple_of(n * BLOCK, BLOCK), BLOCK), :]
    return jnp.concatenate([prev, cur], axis=0)


def _valid_mask(n):
    rows = KV_GROUP * BLOCK
    qi = lax.broadcasted_iota(jnp.int32, (rows, 2 * BLOCK), 0) & (BLOCK - 1)
    kj = lax.broadcasted_iota(jnp.int32, (rows, 2 * BLOCK), 1)
    dist = qi + BLOCK - kj
    return (dist >= 0) & (dist < BLOCK) & ((n > 0) | (kj >= BLOCK))


def _sink_col(sinks_ref, layer, hh):
    return jnp.concatenate([jnp.full((BLOCK, 1), sinks_ref[layer, hh * KV_GROUP + g], F32) for g in range(KV_GROUP)],
                           axis=0)


def _softmax(qs, kdup, valid, sink):
    s = _dot(qs, kdup, NT) * (HEAD_DIM ** -0.5)
    s = jnp.where(valid, s, NEG_INF)
    m = jnp.maximum(jnp.max(s, axis=-1, keepdims=True), sink)
    e = jnp.exp(s - m)
    es = jnp.exp(sink - m)
    den = jnp.sum(e, axis=-1, keepdims=True) + es
    return e / den, es / den


def _attn_fwd(h, sinks, layer, *, S, D, name):
    T = h.shape[0]
    WB, KVW = D // 2, D // 16
    nb = S // BLOCK
    n_kv = KVW // HEAD_DIM

    def body(q_ref, k_ref, v_ref, sinks_ref, o_ref):
        n = pl.program_id(1)
        lo = _lo_mask()
        q = q_ref[...]
        kk, vv = _window(k_ref, n), _window(v_ref, n)
        valid = _valid_mask(n)
        blocks = []
        for hh in range(n_kv):
            cb = slice((hh // 2) * 128, (hh // 2 + 1) * 128)
            kdup, vdup = _dup(kk[:, cb], hh % 2, lo), _dup(vv[:, cb], hh % 2, lo)
            p, _ = _softmax(_stack_heads(q, hh, lo, True), kdup, valid, _sink_col(sinks_ref, layer, hh))
            blocks += _unstack_pairs(_dot(p, vdup, NN), lo)
        o_ref[...] = jnp.concatenate(blocks, axis=1)

    return _pcall(
        body, name=name, grid=(T // S, nb),
        in_specs=[pl.BlockSpec((BLOCK, WB), lambda s, n: (s * nb + n, D // WB)),
                  pl.BlockSpec((S, KVW), lambda s, n: (s, (D + WB) // KVW)),
                  pl.BlockSpec((S, KVW), lambda s, n: (s, (D + WB) // KVW + 1)),
                  pl.BlockSpec(memory_space=pltpu.SMEM)],
        out_shape=jax.ShapeDtypeStruct((T, WB), F32),
        out_specs=pl.BlockSpec((BLOCK, WB), lambda s, n: (s * nb + n, 0)),
        compiler_params=_params(("arbitrary", "arbitrary")),
    )(h, h, h, sinks)


def _attn_bwd(h, yb, dyb, sinks, layer, *, S, D, name):
    T = h.shape[0]
    WB, KVW = D // 2, D // 16
    nb = S // BLOCK
    n_kv = KVW // HEAD_DIM

    def body(q_ref, k_ref, v_ref, o_ref, do_ref, sinks_ref, dq_ref, dk_ref, dv_ref, dsink_ref, dk_acc, dv_acc):
        s_id, n = pl.program_id(0), pl.program_id(1)
        lo = _lo_mask()

        @pl.when(n == 0)
        def _():
            dk_acc[...] = jnp.zeros_like(dk_acc)
            dv_acc[...] = jnp.zeros_like(dv_acc)

        @pl.when((n == 0) & (s_id == 0))
        def _():
            dsink_ref[...] = jnp.zeros_like(dsink_ref)

        q, o, do = q_ref[...], o_ref[...], do_ref[...]
        kk, vv = _window(k_ref, n), _window(v_ref, n)
        valid = _valid_mask(n)
        lane = lax.broadcasted_iota(jnp.int32, dsink_ref.shape, 1)
        dq_blocks, dk_heads, dv_heads = [], [], []
        dsink = jnp.zeros(dsink_ref.shape, F32)
        for hh in range(n_kv):
            cb = slice((hh // 2) * 128, (hh // 2 + 1) * 128)
            kdup, vdup = _dup(kk[:, cb], hh % 2, lo), _dup(vv[:, cb], hh % 2, lo)
            qs = _stack_heads(q, hh, lo, True)
            dos = _stack_heads(do, hh, lo, True)
            delta = jnp.sum(dos * _stack_heads(o, hh, lo, False), axis=-1, keepdims=True)
            p, psink = _softmax(qs, kdup, valid, _sink_col(sinks_ref, layer, hh))
            dvr = _dot(p, dos, TN)
            dv_heads.append(dvr + pltpu.roll(dvr, HEAD_DIM, 1))
            ds = p * (_dot(dos, vdup, NT) - delta) * (HEAD_DIM ** -0.5)
            dq_blocks += _unstack_pairs(_dot(ds, kdup, NN), lo)
            dkr = _dot(ds, qs, TN)
            dk_heads.append(dkr + pltpu.roll(dkr, HEAD_DIM, 1))
            dsk = -psink * delta
            for g in range(KV_GROUP):
                tot = jnp.sum(dsk[g * BLOCK:(g + 1) * BLOCK], axis=0, keepdims=True)
                dsink = dsink + jnp.where(lane == hh * KV_GROUP + g, tot, 0.0)
        dsink_ref[...] += dsink
        dq_ref[...] = jnp.concatenate(dq_blocks, axis=1).astype(dq_ref.dtype)
        pair = lambda hs: jnp.concatenate([jnp.where(lo, hs[2 * m], hs[2 * m + 1]) for m in range(n_kv // 2)], axis=1)
        dkk, dvv = pair(dk_heads), pair(dv_heads)
        prev = pl.ds(pl.multiple_of(jnp.maximum(n - 1, 0) * BLOCK, BLOCK), BLOCK)
        cur = pl.ds(pl.multiple_of(n * BLOCK, BLOCK), BLOCK)
        dk_acc[prev, :] += dkk[:BLOCK]
        dk_acc[cur, :] += dkk[BLOCK:]
        dv_acc[prev, :] += dvv[:BLOCK]
        dv_acc[cur, :] += dvv[BLOCK:]

        @pl.when(n == nb - 1)
        def _():
            dk_ref[...] = dk_acc[...].astype(dk_ref.dtype)
            dv_ref[...] = dv_acc[...].astype(dv_ref.dtype)

    blk = lambda cb=0: pl.BlockSpec((BLOCK, WB), lambda s, n: (s * nb + n, cb))
    seq = lambda cb=0: pl.BlockSpec((S, KVW), lambda s, n: (s, cb))
    return _pcall(
        body, name=name, grid=(T // S, nb),
        in_specs=[blk(D // WB), seq((D + WB) // KVW), seq((D + WB) // KVW + 1), blk(), blk(),
                  pl.BlockSpec(memory_space=pltpu.SMEM)],
        out_shape=(jax.ShapeDtypeStruct((T, WB), BF16), jax.ShapeDtypeStruct((T, KVW), BF16),
                   jax.ShapeDtypeStruct((T, KVW), BF16), jax.ShapeDtypeStruct((8, 128), F32)),
        out_specs=(blk(), seq(), seq(), pl.BlockSpec((8, 128), lambda s, n: (0, 0))),
        scratch_shapes=[pltpu.VMEM((S, KVW), F32), pltpu.VMEM((S, KVW), F32)],
        compiler_params=_params(("arbitrary", "arbitrary")),
    )(h, h, h, yb, dyb, sinks)


def _bg_specs(D, tm):
    return _colspecs(D + D // 2 + 2 * (D // 16), D // 2, tm, lambda i: i)


def _mixb_fwd(yb, h, nb_g, *, D, tm, name):
    T, WB = yb.shape
    bg_specs, _ = _bg_specs(D, tm)

    def body(*refs):
        refs = list(refs)
        yb_ref = refs.pop(0)
        bg = _cat(_take(refs, len(bg_specs)))
        g_ref, o_ref = refs
        _, _, nrm = _rms(yb_ref[...], g_ref[...])
        o_ref[...] = (nrm * (bg * _sigmoid(bg))).astype(o_ref.dtype)

    row = pl.BlockSpec((tm, WB), lambda i: (i, 0))
    return _pcall(body, name=name, grid=(T // tm,),
                  in_specs=[row] + bg_specs + [pl.BlockSpec((1, WB), lambda i: (0, 0))],
                  out_shape=jax.ShapeDtypeStruct((T, WB), BF16), out_specs=row,
                  compiler_params=_params(("arbitrary",)))(yb, *([h] * len(bg_specs)), nb_g)


def _mixb_bwd(yb, h, dmix, nb_g, *, D, tm, name):
    T, WB = yb.shape
    W = D // 4
    bg_specs, _ = _bg_specs(D, tm)
    dm_specs, _ = _colspecs(W, WB, tm, lambda i: i)

    def body(*refs):
        refs = list(refs)
        yb_ref = refs.pop(0)
        bg = _cat(_take(refs, len(bg_specs)))
        dmix_b = _cat(_take(refs, len(dm_specs)))
        g_ref, dyb_ref, dbg_ref, dg_ref = refs

        @pl.when(pl.program_id(0) == 0)
        def _():
            dg_ref[...] = jnp.zeros_like(dg_ref)

        gamma = g_ref[...]
        xn, rstd, nrm = _rms(yb_ref[...], gamma)
        sg = _sigmoid(bg)
        dn = dmix_b * (bg * sg)
        dbg_ref[...] = (dmix_b * nrm * (sg * (1.0 + bg * (1.0 - sg)))).astype(dbg_ref.dtype)
        dg_ref[0:1, :] += jnp.sum(dn * xn, axis=0, keepdims=True)
        dyb_ref[...] = _rms_bwd(dn, xn, rstd, gamma)

    row = pl.BlockSpec((tm, WB), lambda i: (i, 0))
    return _pcall(body, name=name, grid=(T // tm,),
                  in_specs=[row] + bg_specs + dm_specs + [pl.BlockSpec((1, WB), lambda i: (0, 0))],
                  out_shape=(jax.ShapeDtypeStruct((T, WB), F32), jax.ShapeDtypeStruct((T, WB), BF16),
                             jax.ShapeDtypeStruct((8, WB), F32)),
                  out_specs=(row, row, pl.BlockSpec((8, WB), lambda i: (0, 0))),
                  compiler_params=_params(("arbitrary",)))(yb, *([h] * len(bg_specs)), *([dmix] * len(dm_specs)), nb_g)


def _concat_cols(parts, *, tm, name):
    T = parts[0].shape[0]
    total = sum(p.shape[1] for p in parts)

    def body(*refs):
        refs[-1][...] = jnp.concatenate([r[...] for r in refs[:-1]], axis=1)

    return _pcall(body, name=name, grid=(T // tm,),
                  in_specs=[pl.BlockSpec((tm, p.shape[1]), lambda i: (i, 0)) for p in parts],
                  out_shape=jax.ShapeDtypeStruct((T, total), parts[0].dtype),
                  out_specs=pl.BlockSpec((tm, total), lambda i: (i, 0)),
                  compiler_params=_params(("parallel",)))(*parts)


def _ln_fwd(z, g, b, *, tm, name):
    T, D = z.shape

    def body(z_ref, g_ref, b_ref, y_ref, yb_ref):
        zv = z_ref[...]
        mu = jnp.mean(zv, axis=-1, keepdims=True)
        zc = zv - mu
        var = jnp.mean(zc * zc, axis=-1, keepdims=True)
        y = zc * lax.rsqrt(var + LN_EPS) * g_ref[...] + b_ref[...]
        y_ref[...] = y
        yb_ref[...] = y.astype(BF16)

    row = pl.BlockSpec((tm, D), lambda i: (i, 0))
    vec = pl.BlockSpec((1, D), lambda i: (0, 0))
    return _pcall(body, name=name, grid=(T // tm,), in_specs=[row, vec, vec],
                  out_shape=(jax.ShapeDtypeStruct((T, D), F32), jax.ShapeDtypeStruct((T, D), BF16)),
                  out_specs=(row, row), compiler_params=_params(("parallel",)))(z, g, b)


def _ln_bwd(z, dy, g, *, tm, name):
    T, D = z.shape

    def body(z_ref, dy_ref, g_ref, dz_ref, dzb_ref, dgb_ref):
        @pl.when(pl.program_id(0) == 0)
        def _():
            dgb_ref[...] = jnp.zeros_like(dgb_ref)

        zv, dyv = z_ref[...], dy_ref[...]
        mu = jnp.mean(zv, axis=-1, keepdims=True)
        zc = zv - mu
        rstd = lax.rsqrt(jnp.mean(zc * zc, axis=-1, keepdims=True) + LN_EPS)
        xh = zc * rstd
        dxh = dyv * g_ref[...]
        dz = rstd * (dxh - jnp.mean(dxh, axis=-1, keepdims=True) - xh * jnp.mean(dxh * xh, axis=-1, keepdims=True))
        dz_ref[...] = dz
        dzb_ref[...] = dz.astype(BF16)
        dgb_ref[0:1, :] += jnp.sum(dyv * xh, axis=0, keepdims=True)
        dgb_ref[1:2, :] += jnp.sum(dyv, axis=0, keepdims=True)

    row = pl.BlockSpec((tm, D), lambda i: (i, 0))
    return _pcall(body, name=name, grid=(T // tm,), in_specs=[row, row, pl.BlockSpec((1, D), lambda i: (0, 0))],
                  out_shape=(jax.ShapeDtypeStruct((T, D), F32), jax.ShapeDtypeStruct((T, D), BF16),
                             jax.ShapeDtypeStruct((8, D), F32)),
                  out_specs=(row, row, pl.BlockSpec((8, D), lambda i: (0, 0))),
                  compiler_params=_params(("arbitrary",)))(z, dy, g)


def _loss_head(y, target, *, tm, name):
    T, D = y.shape

    def body(y_ref, t_ref, dy_ref, loss_ref):
        @pl.when(pl.program_id(0) == 0)
        def _():
            loss_ref[...] = jnp.zeros_like(loss_ref)

        err = y_ref[...] - t_ref[...]
        dy_ref[...] = err / D
        loss_ref[...] += 0.5 * jnp.sum(jnp.mean(err * err, axis=-1, keepdims=True), axis=0, keepdims=True)

    row = pl.BlockSpec((tm, D), lambda i: (i, 0))
    return _pcall(body, name=name, grid=(T // tm,), in_specs=[row, row],
                  out_shape=(jax.ShapeDtypeStruct((T, D), F32), jax.ShapeDtypeStruct((1, 1), F32)),
                  out_specs=(row, pl.BlockSpec((1, 1), lambda i: (0, 0))),
                  compiler_params=_params(("arbitrary",)))(y, target)


def _cast_bf16(w, *, name):
    L, R, C = w.shape
    tr = _tile(R, 512, 8)

    def body(w_ref, o_ref):
        o_ref[...] = w_ref[...].astype(BF16)

    blk = pl.BlockSpec((None, tr, C), lambda l, i: (l, i, 0))
    return _pcall(body, name=name, grid=(L, R // tr), in_specs=[blk], out_shape=jax.ShapeDtypeStruct(w.shape, BF16),
                  out_specs=blk, compiler_params=_params(("parallel", "parallel")))(w)


def _adamw(g, w, m, v, *, name):
    R, C = g.shape
    tr = _tile(R, max(8, (1 << 19) // C // 8 * 8), 8)

    def body(g_ref, w_ref, m_ref, v_ref, d_ref, nm_ref, nv_ref):
        gv = g_ref[...]
        nm = ADAM_B1 * m_ref[...] + (1.0 - ADAM_B1) * gv
        nv = ADAM_B2 * v_ref[...] + (1.0 - ADAM_B2) * (gv * gv)
        m_hat = nm / (1.0 - ADAM_B1 ** ADAM_STEP)
        v_hat = nv / (1.0 - ADAM_B2 ** ADAM_STEP)
        d_ref[...] = -ADAM_LR * (m_hat / (jnp.sqrt(v_hat) + ADAM_EPS) + ADAM_WD * w_ref[...])
        nm_ref[...] = nm
        nv_ref[...] = nv

    blk = pl.BlockSpec((tr, C), lambda i: (i, 0))
    shp = jax.ShapeDtypeStruct((R, C), F32)
    return _pcall(body, name=name, grid=(R // tr,), in_specs=[blk] * 4, out_shape=(shp, shp, shp),
                  out_specs=(blk, blk, blk), compiler_params=_params(("parallel",)))(g, w, m, v)


def _pair_sum(mine, theirs, idx, *, half_axis, name):
    L, R, C = theirs.shape
    tr, tc = _tile(R, 512, 16), _tile(C, 2048)
    nrb, ncb = R // tr, C // tc

    def body(idx_ref, a_ref, b_ref, o_ref):
        o_ref[...] = (a_ref[...].astype(F32) + b_ref[...].astype(F32)).astype(BF16)

    if half_axis == 1:
        a_idx = lambda l, i, j, s: (l, s[0] * nrb + i, j)
    else:
        a_idx = lambda l, i, j, s: (l, i, s[0] * ncb + j)
    blk = pl.BlockSpec((None, tr, tc), lambda l, i, j, s: (l, i, j))
    grid_spec = pltpu.PrefetchScalarGridSpec(
        num_scalar_prefetch=1, grid=(L, nrb, ncb),
        in_specs=[pl.BlockSpec((None, tr, tc), a_idx), blk], out_specs=blk)
    return _pcall(body, name=name, grid_spec=grid_spec, out_shape=jax.ShapeDtypeStruct(theirs.shape, BF16),
                  compiler_params=_params(("parallel", "parallel", "parallel")))(idx, mine, theirs)


def _final_sum(own, got, idx, *, own_axis, out_shape, out_axis, name):
    _, L, R, C = got.shape
    tr, tc = _tile(R, 512, 16), _tile(C, 1024)
    nrb, ncb = R // tr, C // tc

    def body(idx_ref, a_ref, q_ref, o_ref):
        o_ref[...] = ((a_ref[...].astype(F32) + q_ref[0].astype(F32)) + q_ref[1].astype(F32)) + q_ref[2].astype(F32)

    if own_axis == 2:
        a_idx = lambda l, i, j, s: (l, i, s[0] * ncb + j)
    else:
        a_idx = lambda l, i, j, s: (l, s[0] * nrb + i, j)
    if out_axis == 1:
        o_idx = lambda l, i, j, s: (l, s[1] * nrb + i, j)
    else:
        o_idx = lambda l, i, j, s: (l, i, s[1] * ncb + j)
    grid_spec = pltpu.PrefetchScalarGridSpec(
        num_scalar_prefetch=1, grid=(L, nrb, ncb),
        in_specs=[pl.BlockSpec((None, tr, tc), a_idx),
                  pl.BlockSpec((3, None, tr, tc), lambda l, i, j, s: (0, l, i, j))],
        out_specs=pl.BlockSpec((None, tr, tc), o_idx))
    return _pcall(body, name=name, grid_spec=grid_spec, out_shape=jax.ShapeDtypeStruct(out_shape, F32),
                  compiler_params=_params(("parallel", "parallel", "parallel")))(idx, own, got)


def _sum_devices(gathered, *, name):
    _, R, C = gathered.shape
    tr = _tile(R, 280, 8)

    def body(g_ref, o_ref):
        acc = g_ref[0]
        for d in range(1, N_DEV):
            acc = acc + g_ref[d]
        o_ref[...] = acc

    return _pcall(body, name=name, grid=(R // tr,), in_specs=[pl.BlockSpec((N_DEV, tr, C), lambda i: (0, i, 0))],
                  out_shape=jax.ShapeDtypeStruct((R, C), F32), out_specs=pl.BlockSpec((tr, C), lambda i: (i, 0)),
                  compiler_params=_params(("parallel",)))(gathered)


def _position():
    x, y, c = lax.axis_index("x"), lax.axis_index("y"), lax.axis_index("c")
    chips = [(1 - x, y), (x, 1 - y), (1 - x, 1 - y)]
    return x, y, c, chips


def _remote(src, dst, send_sems, recv_sems, k, to):
    return pltpu.make_async_remote_copy(src_ref=src, dst_ref=dst, send_sem=send_sems.at[k], recv_sem=recv_sems.at[k],
                                        device_id=to, device_id_type=MESH)


def _cols(ref, j, n):
    return ref.at[:, :, pl.ds(pl.multiple_of(j * n, 128), n)]


def _rows(ref, j, n):
    return ref.at[:, pl.ds(pl.multiple_of(j * n, 16), n), :]


def _gather_weights(ws_in, ws_out, *, name):
    L, D, NS = ws_in.shape
    RS = ws_out.shape[1]
    HI, HO = D // 2, RS // 2

    def body(ws_in_ref, ws_out_ref, wg_in, wg_out, send_sems, recv_sems, local_sems):
        x, y, c, chips = _position()
        j = 2 * x + y
        sib = (x, y, 1 - c)
        in_slab = lambda chip, half: _cols(wg_in, chip, NS).at[:, pl.ds(pl.multiple_of(half * HI, 16), HI), :]
        out_slab = lambda chip, half: _rows(wg_out, 2 * chip + half, HO)
        own = [pltpu.make_async_copy(ws_in_ref, _cols(wg_in, j, NS), local_sems.at[0]),
               pltpu.make_async_copy(ws_out_ref, _rows(wg_out, j, RS), local_sems.at[1])]
        for cp in own:
            cp.start()
        first = []
        for kk, chip in enumerate(chips):
            first.append(_remote(_rows(ws_in_ref, c, HI), in_slab(j, c), send_sems, recv_sems, 2 * kk, (*chip, c)))
            first.append(_remote(_rows(ws_out_ref, c, HO), out_slab(j, c), send_sems, recv_sems, 2 * kk + 1, (*chip, c)))
        for cp in first:
            cp.start()
        passed = []
        for kk, (px, py) in enumerate(chips):
            jk = 2 * px + py
            _remote(in_slab(jk, c), in_slab(jk, c), send_sems, recv_sems, 2 * kk, sib).wait_recv()
            _remote(out_slab(jk, c), out_slab(jk, c), send_sems, recv_sems, 2 * kk + 1, sib).wait_recv()
            passed.append(_remote(in_slab(jk, c), in_slab(jk, c), send_sems, recv_sems, 6 + 2 * kk, sib))
            passed.append(_remote(out_slab(jk, c), out_slab(jk, c), send_sems, recv_sems, 7 + 2 * kk, sib))
            passed[-2].start()
            passed[-1].start()
        for kk, (px, py) in enumerate(chips):
            jk = 2 * px + py
            _remote(in_slab(jk, 1 - c), in_slab(jk, 1 - c), send_sems, recv_sems, 6 + 2 * kk, sib).wait_recv()
            _remote(out_slab(jk, 1 - c), out_slab(jk, 1 - c), send_sems, recv_sems, 7 + 2 * kk, sib).wait_recv()
        for cp in first + passed:
            cp.wait_send()
        for cp in own:
            cp.wait()

    return _pcall(
        body, name=name, in_specs=[ANY, ANY], out_specs=(ANY, ANY),
        out_shape=(jax.ShapeDtypeStruct((L, D, N_CHIPS * NS), BF16), jax.ShapeDtypeStruct((L, N_CHIPS * RS, D), BF16)),
        scratch_shapes=[pltpu.SemaphoreType.DMA((12,)), pltpu.SemaphoreType.DMA((12,)), pltpu.SemaphoreType.DMA((2,))],
    )(ws_in, ws_out)


def _pair_exchange(gw_in, gw_out, *, name):
    L, D, INW = gw_in.shape
    DM = gw_out.shape[1]

    def body(gw_in_ref, gw_out_ref, rb_in, rb_out, send_sems, recv_sems):
        x, y, c, _ = _position()
        sib = (x, y, 1 - c)
        copies = [_remote(_rows(gw_in_ref, 1 - c, D // 2), rb_in, send_sems, recv_sems, 0, sib),
                  _remote(_cols(gw_out_ref, 1 - c, D // 2), rb_out, send_sems, recv_sems, 1, sib)]
        for cp in copies:
            cp.start()
        for cp in copies:
            cp.wait()

    return _pcall(
        body, name=name, in_specs=[ANY, ANY], out_specs=(ANY, ANY),
        out_shape=(jax.ShapeDtypeStruct((L, D // 2, INW), BF16), jax.ShapeDtypeStruct((L, DM, D // 2), BF16)),
        scratch_shapes=[pltpu.SemaphoreType.DMA((2,)), pltpu.SemaphoreType.DMA((2,))],
    )(gw_in, gw_out)


def _chip_exchange(p_in, p_out, *, name):
    L, HI, INW = p_in.shape
    _, DM, HO = p_out.shape
    NS, RS = INW // N_CHIPS, DM // N_CHIPS

    def body(p_in_ref, p_out_ref, q_in, q_out, send_sems, recv_sems):
        x, y, c, chips = _position()
        copies = []
        for kk, (px, py) in enumerate(chips):
            jk = 2 * px + py
            copies.append(_remote(_cols(p_in_ref, jk, NS), q_in.at[kk], send_sems, recv_sems, 2 * kk, (px, py, c)))
            copies.append(_remote(_rows(p_out_ref, jk, RS), q_out.at[kk], send_sems, recv_sems, 2 * kk + 1, (px, py, c)))
        for cp in copies:
            cp.start()
        for cp in copies:
            cp.wait()

    return _pcall(
        body, name=name, in_specs=[ANY, ANY], out_specs=(ANY, ANY),
        out_shape=(jax.ShapeDtypeStruct((3, L, HI, NS), BF16), jax.ShapeDtypeStruct((3, L, RS, HO), BF16)),
        scratch_shapes=[pltpu.SemaphoreType.DMA((6,)), pltpu.SemaphoreType.DMA((6,))],
    )(p_in, p_out)


def _sibling_fill(g_in, g_out, *, name):
    L, D, NS = g_in.shape

    def body(g_in_ref, g_out_ref, o_in, o_out, send_sems, recv_sems):
        x, y, c, _ = _position()
        sib = (x, y, 1 - c)
        copies = [_remote(_rows(o_in, c, D // 2), _rows(o_in, c, D // 2), send_sems, recv_sems, 0, sib),
                  _remote(_cols(o_out, c, D // 2), _cols(o_out, c, D // 2), send_sems, recv_sems, 1, sib)]
        for cp in copies:
            cp.start()
        for cp in copies:
            cp.wait()

    return _pcall(
        body, name=name, in_specs=[ANY, ANY], out_specs=(ANY, ANY),
        out_shape=(jax.ShapeDtypeStruct(g_in.shape, F32), jax.ShapeDtypeStruct(g_out.shape, F32)),
        input_output_aliases={0: 0, 1: 1},
        scratch_shapes=[pltpu.SemaphoreType.DMA((2,)), pltpu.SemaphoreType.DMA((2,))],
    )(g_in, g_out)


def _gather_small(block, *, name):
    R, C = block.shape

    def body(x_ref, out_ref, send_sems, recv_sems, local_sem):
        x, y, c, chips = _position()
        me, sib = (x, y, c), (x, y, 1 - c)
        slot = lambda px, py, pc: out_ref.at[4 * px + 2 * py + pc]
        mine = pltpu.make_async_copy(x_ref, slot(*me), local_sem)
        mine.start()
        first = [_remote(x_ref, slot(*me), send_sems, recv_sems, 0, sib)]
        first += [_remote(x_ref, slot(*me), send_sems, recv_sems, 1 + kk, (*chip, c)) for kk, chip in enumerate(chips)]
        for cp in first:
            cp.start()
        passed = []
        for kk, chip in enumerate(chips):
            _remote(slot(*chip, c), slot(*chip, c), send_sems, recv_sems, 1 + kk, sib).wait_recv()
            passed.append(_remote(slot(*chip, c), slot(*chip, c), send_sems, recv_sems, 4 + kk, sib))
            passed[-1].start()
        _remote(slot(*sib), slot(*sib), send_sems, recv_sems, 0, sib).wait_recv()
        for kk, chip in enumerate(chips):
            _remote(slot(*chip, 1 - c), slot(*chip, 1 - c), send_sems, recv_sems, 4 + kk, sib).wait_recv()
        for cp in first + passed:
            cp.wait_send()
        mine.wait()

    return _pcall(
        body, name=name, in_specs=[ANY], out_specs=ANY, out_shape=jax.ShapeDtypeStruct((N_DEV, R, C), F32),
        scratch_shapes=[pltpu.SemaphoreType.DMA((7,)), pltpu.SemaphoreType.DMA((7,)), pltpu.SemaphoreType.DMA],
    )(block)


_SMALL = ["gate_r_w", "gate_i_w", "conv_a_w", "conv_c_w", "sinks", "conv_c_b", "gate_r_b", "gate_i_b", "rg_lambda",
          "norm_a", "norm_b", "norm_c", "ln_g", "ln_b"]


def _pack_small(p):
    L = p["ln_g"].shape[0]
    rows = []
    for n in _SMALL:
        a = p[n]
        if n in ("gate_r_w", "gate_i_w", "norm_b", "ln_g", "ln_b"):
            a = a.reshape(L, -1, 1024)
        elif a.ndim == 2:
            a = a[:, None, :]
        if a.shape[-1] < 1024:
            a = jnp.pad(a, ((0, 0), (0, 0), (0, 1024 - a.shape[-1])))
        rows.append(a)
    out = jnp.concatenate(rows, axis=1)
    assert out.shape[1] == SMALL_ROWS
    return out.reshape(L * SMALL_ROWS, 1024)


def _unpack_small(flat, like):
    L = like["ln_g"].shape[0]
    a = flat.reshape(L, SMALL_ROWS, 1024)
    out, r = {}, 0
    for n in _SMALL:
        shp = like[n].shape
        nrows = max(1, math.prod(shp[1:]) // 1024) if n in ("gate_r_w", "gate_i_w", "norm_b", "ln_g", "ln_b") else (
            shp[1] if len(shp) == 3 else 1)
        blk = a[:, r:r + nrows, :]
        if n in ("gate_r_w", "gate_i_w", "norm_b", "ln_g", "ln_b"):
            out[n] = blk.reshape(shp)
        elif len(shp) == 3:
            out[n] = blk[:, :, :shp[2]]
        else:
            out[n] = blk[:, 0, :shp[1]]
        r += nrows
    return out


def kernel(x, w_in, conv_a_w, sinks, conv_c_w, conv_c_b, gate_r_w, gate_r_b, gate_i_w, gate_i_b, rg_lambda, norm_a, norm_b, norm_c, w_out, ln_g, ln_b, loss_target, m_w_in, m_conv_a_w, m_sinks, m_conv_c_w, m_conv_c_b, m_gate_r_w, m_gate_r_b, m_gate_i_w, m_gate_i_b, m_rg_lambda, m_norm_a, m_norm_b, m_norm_c, m_w_out, m_ln_g, m_ln_b, v_w_in, v_conv_a_w, v_sinks, v_conv_c_w, v_conv_c_b, v_gate_r_w, v_gate_r_b, v_gate_i_w, v_gate_i_b, v_rg_lambda, v_norm_a, v_norm_b, v_norm_c, v_w_out, v_ln_g, v_ln_b):
    names = ["w_in", "conv_a_w", "sinks", "conv_c_w", "conv_c_b", "gate_r_w", "gate_r_b", "gate_i_w", "gate_i_b",
             "rg_lambda", "norm_a", "norm_b", "norm_c", "w_out", "ln_g", "ln_b"]
    w = dict(zip(names, [w_in, conv_a_w, sinks, conv_c_w, conv_c_b, gate_r_w, gate_r_b, gate_i_w, gate_i_b, rg_lambda,
                         norm_a, norm_b, norm_c, w_out, ln_g, ln_b]))
    mom = dict(zip(names, [m_w_in, m_conv_a_w, m_sinks, m_conv_c_w, m_conv_c_b, m_gate_r_w, m_gate_r_b, m_gate_i_w,
                           m_gate_i_b, m_rg_lambda, m_norm_a, m_norm_b, m_norm_c, m_w_out, m_ln_g, m_ln_b]))
    vel = dict(zip(names, [v_w_in, v_conv_a_w, v_sinks, v_conv_c_w, v_conv_c_b, v_gate_r_w, v_gate_r_b, v_gate_i_w,
                           v_gate_i_b, v_rg_lambda, v_norm_a, v_norm_b, v_norm_c, v_w_out, v_ln_g, v_ln_b]))
    B, S, D = x.shape
    T = B * S
    L, _, NS = w_in.shape
    RS = w_out.shape[1]
    INW, DM = N_CHIPS * NS, N_CHIPS * RS
    W = D // 4
    alpha = (2.0 * L) ** 0.25
    tt = _tile(S, 128, 8)
    tm_row = _tile(T, 256, 8)
    cx_, cy_, cc_ = lax.axis_index("x"), lax.axis_index("y"), lax.axis_index("c")
    chip = 2 * cx_ + cy_
    idx_c = jnp.stack([cc_]).astype(jnp.int32)
    idx_jc = jnp.stack([chip, cc_]).astype(jnp.int32)

    wg_in, wg_out = _gather_weights(_cast_bf16(w_in, name="cast_w_in"), _cast_bf16(w_out, name="cast_w_out"),
                                    name="gather_weights")
    conv_local = jnp.concatenate([conv_a_w, conv_c_w], axis=1).reshape(L * 7, W // N_CHIPS)
    conv_local = jnp.pad(conv_local, ((0, (-L * 7) % 8), (0, 0)))
    conv_all = _gather_small(conv_local, name="gather_conv")
    conv_full = jnp.concatenate([conv_all[2 * jj][:L * 7] for jj in range(N_CHIPS)], axis=1).reshape(L, 7, W)
    caw_full, ccw_full = conv_full[:, :3], conv_full[:, 3:]

    xf = x.reshape(T, D)
    xb = _cast_bf16(xf[None], name="cast_x")[0]
    saved = []
    for l in range(L):
        h = _mm(xb, wg_in, mode="nn", out_dtype=F32, name="proj_in", tm=1024, tn=1536, tk=512, b_layer=l)
        pv = jnp.stack([conv_c_b[l], gate_r_b[l], gate_i_b[l], rg_lambda[l], norm_a[l], norm_c[l]])
        mix_ac, cv, xc, yc = _ac_fwd(h, caw_full[l], ccw_full[l], pv, gate_r_w[l], gate_i_w[l], S=S, D=D, tt=tt,
                                     name="ac_fwd")
        yb = _attn_fwd(h, sinks, l, S=S, D=D, name="attn_fwd")
        mix_b = _mixb_fwd(yb, h, norm_b[l][None], D=D, tm=tm_row, name="mixb_fwd")
        mix = _concat_cols([mix_ac[:, :W], mix_b, mix_ac[:, W:]], tm=tm_row, name="concat_mix")
        z = _mm(mix, wg_out, mode="nn", out_dtype=F32, name="proj_out", b_layer=l, add=xf, add_scale=alpha)
        saved.append((xb, h, cv, xc, yc, yb, mix, z, pv))
        xf, xb = _ln_fwd(z, ln_g[l][None], ln_b[l][None], tm=tm_row, name="ln_fwd")
    dxn, loss_part = _loss_head(xf, loss_target.reshape(T, D), tm=tm_row, name="loss_head")
    loss = lax.psum(loss_part[0, 0], ("x", "y", "c"))

    gw_in = lax.empty((L, D, INW), BF16)
    gw_out = lax.empty((L, DM, D), BF16)
    small_g = [None] * L
    for l in reversed(range(L)):
        xb_l, h, cv, xc, yc, yb, mix, z, pv = saved[l]
        dz, dzb, dgb = _ln_bwd(z, dxn, ln_g[l][None], tm=tm_row, name="ln_bwd")
        dmix = _mm(dzb, wg_out, mode="nt", out_dtype=F32, name="d_mix", b_layer=l)
        gw_out = _mm(mix, dzb, mode="tn", out_dtype=BF16, name="d_w_out", out_buf=gw_out, out_layer=l)
        dha, dhc, vec, dwr, dwi = _ac_bwd(h, cv, xc, yc, dmix, caw_full[l], ccw_full[l], pv, gate_r_w[l], gate_i_w[l],
                                          S=S, D=D, tt=tt, name="ac_bwd")
        dyb, dbg, dnb = _mixb_bwd(yb, h, dmix, norm_b[l][None], D=D, tm=tm_row, name="mixb_bwd")
        dq, dk, dv, dsk = _attn_bwd(h, yb, dyb, sinks, l, S=S, D=D, name="attn_bwd")
        dh = _concat_cols([dha, dq, dk, dv, dbg, dhc], tm=tm_row, name="concat_dh")
        gw_in = _mm(xb_l, dh, mode="tn", out_dtype=BF16, name="d_w_in", tm=1024, tn=1536, tk=512, out_buf=gw_in,
                    out_layer=l)
        dxn = _mm(dh, wg_in, mode="nt", out_dtype=F32, name="d_x", b_layer=l, add=dz, add_scale=alpha)
        small_g[l] = dict(gate_r_w=dwr, gate_i_w=dwi, conv_a_w=vec[0:3], conv_c_w=vec[4:8], sinks=dsk[0, :2 * D // 256],
                          conv_c_b=vec[8], gate_r_b=vec[9], gate_i_b=vec[10], rg_lambda=vec[11], norm_a=vec[3],
                          norm_b=dnb[0], norm_c=vec[12], ln_g=dgb[0], ln_b=dgb[1])
    grad_x = dxn.reshape(B, S, D)

    rb_in, rb_out = _pair_exchange(gw_in, gw_out, name="pair_exchange")
    p_in = _pair_sum(gw_in, rb_in, idx_c, half_axis=1, name="pair_sum_in")
    p_out = _pair_sum(gw_out, rb_out, idx_c, half_axis=2, name="pair_sum_out")
    q_in, q_out = _chip_exchange(p_in, p_out, name="chip_exchange")
    g_in_half = _final_sum(p_in, q_in, idx_jc, own_axis=2, out_shape=(L, D, NS), out_axis=1, name="final_sum_in")
    g_out_half = _final_sum(p_out, q_out, idx_jc, own_axis=1, out_shape=(L, RS, D), out_axis=2, name="final_sum_out")
    g_w_in, g_w_out = _sibling_fill(g_in_half, g_out_half, name="sibling_fill")

    like = {n: w[n] for n in _SMALL}
    like_full = dict(like, conv_a_w=caw_full, conv_c_w=ccw_full)
    part = _pack_small({n: jnp.stack([small_g[l][n] for l in range(L)]) for n in _SMALL})
    g_small = _unpack_small(_sum_devices(_gather_small(part, name="gather_small"), name="sum_small"), like_full)
    for n in ("conv_a_w", "conv_c_w"):
        g_small[n] = lax.dynamic_slice_in_dim(g_small[n], chip * (W // N_CHIPS), W // N_CHIPS, axis=2)

    grads = dict(g_small, w_in=g_w_in, w_out=g_w_out)
    d_s, m_s, v_s = _adamw(_pack_small(g_small), _pack_small(like), _pack_small({n: mom[n] for n in _SMALL}),
                           _pack_small({n: vel[n] for n in _SMALL}), name="adamw_small")
    delta, new_m, new_v = _unpack_small(d_s, like), _unpack_small(m_s, like), _unpack_small(v_s, like)
    for n in ("w_in", "w_out"):
        shp = w[n].shape
        flat = lambda a: a.reshape(shp[0] * shp[1], shp[2])
        d_b, m_b, v_b = _adamw(flat(grads[n]), flat(w[n]), flat(mom[n]), flat(vel[n]), name="adamw_" + n)
        delta[n], new_m[n], new_v[n] = d_b.reshape(shp), m_b.reshape(shp), v_b.reshape(shp)

    return (loss, grad_x, *[grads[n] for n in names], *[delta[n] for n in names], *[new_m[n] for n in names],
            *[new_v[n] for n in names])
```

```python
import functools
import math

import jax
import jax.numpy as jnp
from jax import lax
from jax.experimental import pallas as pl
from jax.experimental.pallas import tpu as pltpu

F32 = jnp.float32
BF16 = jnp.bfloat16
_MXU_DTYPE = jnp.bfloat16

HEAD_DIM = 64
KV_GROUP = 8
BLOCK = 128
N_RG_HEADS = 8
RG_C = 8.0
LN_EPS = 1e-5
RMS_EPS = 1e-6
NEG_INF = -1e30
ADAM_LR, ADAM_B1, ADAM_B2, ADAM_EPS, ADAM_WD, ADAM_STEP = 0.001, 0.9, 0.999, 1e-08, 0.01, 10
N_CHIPS = 4
N_DEV = 8
SMALL_ROWS = 280
VMEM_LIMIT = 56 * 1024 * 1024

MESH = pl.DeviceIdType.MESH
ANY = pl.BlockSpec(memory_space=pl.ANY)


def _pcall(body, *, name, **kw):
    return pl.pallas_call(body, name=name, **kw)


def _params(sem=None):
    return pltpu.CompilerParams(dimension_semantics=sem, vmem_limit_bytes=VMEM_LIMIT)


def _tile(dim, pref, mult=128):
    best = None
    for t in range(mult, min(dim, pref) + 1, mult):
        if dim % t == 0:
            best = t
    return best if best is not None else dim


def _dot(a, b, dims):
    return lax.dot_general(a.astype(_MXU_DTYPE), b.astype(_MXU_DTYPE), (dims, ((), ())),
                           preferred_element_type=F32)


NN = ((1,), (0,))
NT = ((1,), (1,))
TN = ((0,), (0,))


def _mm(a, b, *, mode, out_dtype, name, tm=1024, tn=1024, tk=512, add=None, add_scale=1.0, comm=()):
    if mode == "nn":
        (M, K), N = a.shape, b.shape[1]
    elif mode == "nt":
        (M, K), N = a.shape, b.shape[0]
    else:
        (K, M), N = a.shape, b.shape[1]
    tm, tn, tk = _tile(M, tm), _tile(N, tn), _tile(K, tk)
    ni, nj, nk = M // tm, N // tn, K // tk
    dims = {"nn": NN, "nt": NT, "tn": TN}[mode]
    n_cin = sum(len(p.ins) for p in comm)
    n_cout = sum(len(p.outs) for p in comm)

    def body(*refs):
        refs = list(refs)
        a_ref, b_ref = _take(refs, 2)
        add_ref = refs.pop(0) if add is not None else None
        cin = _take(refs, n_cin)
        o_ref = refs.pop(0)
        cout = _take(refs, n_cout)
        acc = refs.pop(0)
        i, j, k = pl.program_id(0), pl.program_id(1), pl.program_id(2)

        if comm:
            @pl.when((i == 0) & (j == 0) & (k == 0))
            def _():
                _comm_run(comm, "start", cin, cout, *refs)

        @pl.when(k == 0)
        def _():
            acc[...] = jnp.zeros_like(acc)

        acc[...] += _dot(a_ref[...], b_ref[...], dims)

        @pl.when(k == nk - 1)
        def _():
            r = acc[...]
            if add_ref is not None:
                r = r + add_scale * add_ref[...]
            o_ref[...] = r.astype(out_dtype)

        if comm:
            @pl.when((i == ni - 1) & (j == nj - 1) & (k == nk - 1))
            def _():
                _comm_run(comm, "finish", cin, cout, *refs)

    a_spec = {"nn": pl.BlockSpec((tm, tk), lambda i, j, k: (i, k)),
              "nt": pl.BlockSpec((tm, tk), lambda i, j, k: (i, k)),
              "tn": pl.BlockSpec((tk, tm), lambda i, j, k: (k, i))}[mode]
    b_spec = {"nn": pl.BlockSpec((tk, tn), lambda i, j, k: (k, j)),
              "nt": pl.BlockSpec((tn, tk), lambda i, j, k: (j, k)),
              "tn": pl.BlockSpec((tk, tn), lambda i, j, k: (k, j))}[mode]
    in_specs, operands = [a_spec, b_spec], [a, b]
    if add is not None:
        in_specs.append(pl.BlockSpec((tm, tn), lambda i, j, k: (i, j)))
        operands.append(add)
    aliases = _comm_aliases(comm, len(operands), 1)
    in_specs += [ANY] * n_cin
    operands += [arr for p in comm for arr in p.ins]
    out_shape = [jax.ShapeDtypeStruct((M, N), out_dtype)] + [s for p in comm for s in p.outs]
    out_specs = [pl.BlockSpec((tm, tn), lambda i, j, k: (i, j))] + [ANY] * n_cout
    sem = ("arbitrary",) * 3 if comm else ("parallel", "parallel", "arbitrary")
    res = _pcall(body, name=name, out_shape=out_shape, grid=(ni, nj, nk), in_specs=in_specs, out_specs=out_specs,
                 scratch_shapes=[pltpu.VMEM((tm, tn), F32)] + _comm_scratch(comm), input_output_aliases=aliases,
                 compiler_params=_params(sem))(*operands)
    return list(res) if comm else res[0]


def _colspecs(off, width, rows, rowmap):
    bw = math.gcd(off, width) if off else width
    specs = [pl.BlockSpec((rows, bw), functools.partial(lambda cb, *g: (rowmap(*g), cb), off // bw + i))
             for i in range(width // bw)]
    return specs, bw


def _cat(refs):
    vals = [r[...] for r in refs]
    return vals[0] if len(vals) == 1 else jnp.concatenate(vals, axis=1)


def _take(refs, n):
    out = refs[:n]
    del refs[:n]
    return out


def _sigmoid(x):
    return 1.0 / (1.0 + jnp.exp(-x))


def _rms(y, gamma):
    rstd = lax.rsqrt(jnp.mean(y * y, axis=-1, keepdims=True) + RMS_EPS)
    xn = y * rstd
    return xn, rstd, xn * gamma


def _rms_bwd(dn, xn, rstd, gamma):
    dng = dn * gamma
    return rstd * (dng - xn * jnp.mean(dng * xn, axis=-1, keepdims=True))


def _shift_down(x, s, carry8):
    rolled = pltpu.roll(x, s, 0)
    cr = pltpu.roll(carry8, s, 0)
    row8 = lax.broadcasted_iota(jnp.int32, carry8.shape, 0)
    top = jnp.where(row8 < s, cr, rolled[0:8])
    return jnp.concatenate([top, rolled[8:]], axis=0)


def _shift_up(x, s, carry8):
    n = x.shape[0]
    rolled = pltpu.roll(x, n - s, 0)
    cr = pltpu.roll(carry8, 8 - s, 0)
    row8 = lax.broadcasted_iota(jnp.int32, carry8.shape, 0)
    bot = jnp.where(row8 >= 8 - s, cr, rolled[n - 8:])
    return jnp.concatenate([rolled[:n - 8], bot], axis=0)


def _chunk_scan(a, b):
    n = a.shape[0]
    r8 = lax.broadcasted_iota(jnp.int32, a.shape, 0) & 7
    for d in (1, 2, 4):
        ok = r8 >= d
        a_sh = jnp.where(ok, pltpu.roll(a, d, 0), 1.0)
        b_sh = jnp.where(ok, pltpu.roll(b, d, 0), 0.0)
        b = a * b_sh + b
        a = a * a_sh
    return a, b


def _chunk_scan_rev(c, b):
    n = c.shape[0]
    r8 = lax.broadcasted_iota(jnp.int32, c.shape, 0) & 7
    for d in (1, 2, 4):
        ok = r8 + d <= 7
        c_sh = jnp.where(ok, pltpu.roll(c, n - d, 0), 1.0)
        b_sh = jnp.where(ok, pltpu.roll(b, n - d, 0), 0.0)
        b = b + c * b_sh
        c = c * c_sh
    return c, b


def _log1p(x):
    w = 1.0 + x
    return jnp.where(w == 1.0, x, jnp.log(w) * (x / (w - 1.0)))


def _log_sigmoid(x):
    return jnp.minimum(x, 0.0) - _log1p(jnp.exp(-jnp.abs(x)))


def _expm1(x):
    u = jnp.exp(x)
    lu = jnp.log(u)
    small = jnp.where(u == 1.0, x, (u - 1.0) * (x / jnp.where(lu == 0.0, 1.0, lu)))
    return jnp.where(jnp.abs(x) < 0.5, small, u - 1.0)


def _gates(xc, wr_ref, wi_ref, br, bi, lam):
    hw = xc.shape[1] // N_RG_HEADS
    gr = jnp.concatenate([_dot(xc[:, h * hw:(h + 1) * hw], wr_ref[h], NN) for h in range(N_RG_HEADS)], axis=1) + br
    gi = jnp.concatenate([_dot(xc[:, h * hw:(h + 1) * hw], wi_ref[h], NN) for h in range(N_RG_HEADS)], axis=1) + bi
    r, i = _sigmoid(gr), _sigmoid(gi)
    ls = _log_sigmoid(lam)
    la = RG_C * r * ls
    a = jnp.exp(la)
    sq = jnp.sqrt(-_expm1(2.0 * la))
    return r, i, ls, a, sq


def _ac_fwd(h, caw, ccw, pv, wr, wi, *, S, D, tt, name):
    T = h.shape[0]
    W = D // 4
    nt = S // tt
    rowmap = lambda s, t: s * nt + t
    c_off = D + D // 2 + 2 * (D // 16) + D // 2
    offs = [0, W, 2 * W, 3 * W, c_off, c_off + W]
    in_specs, counts = [], []
    for off in offs:
        specs, _ = _colspecs(off, W, tt, rowmap)
        in_specs += specs
        counts.append(len(specs))
    full = lambda shape: pl.BlockSpec(shape, lambda s, t: (0,) * len(shape))
    in_specs += [full(caw.shape), full(ccw.shape), full(pv.shape), full(wr.shape), full(wi.shape)]

    def body(*refs):
        refs = list(refs)
        ab, ac, ax, ag, cx, cg = [_cat(_take(refs, n)) for n in counts]
        caw_ref, ccw_ref, pv_ref, wr_ref, wi_ref = _take(refs, 5)
        mixac_ref, cv_ref, xc_ref, yc_ref = _take(refs, 4)
        carry_p, carry_cx, carry_h, a_s, b_s = refs
        t = pl.program_id(1)

        @pl.when(t == 0)
        def _():
            carry_p[...] = jnp.zeros_like(carry_p)
            carry_cx[...] = jnp.zeros_like(carry_cx)
            carry_h[...] = jnp.zeros_like(carry_h)

        ccb, br, bi, lam, na, nc = [pv_ref[k:k + 1, :] for k in range(6)]
        p = ac * ax
        cp = carry_p[...]
        cv = caw_ref[2:3, :] * p + caw_ref[1:2, :] * _shift_down(p, 1, cp) + caw_ref[0:1, :] * _shift_down(p, 2, cp)
        carry_p[...] = p[tt - 8:tt]
        cv_ref[...] = cv
        _, _, n_a = _rms(ab * cv, na)
        mix_a = n_a * (ag * _sigmoid(ag))
        ccx = carry_cx[...]
        xc = (ccw_ref[3:4, :] * cx + ccw_ref[2:3, :] * _shift_down(cx, 1, ccx) + ccw_ref[1:2, :] * _shift_down(cx, 2, ccx)
              + ccw_ref[0:1, :] * _shift_down(cx, 3, ccx) + ccb)
        carry_cx[...] = cx[tt - 8:tt]
        xc_ref[...] = xc
        r, i, ls, a, sq = _gates(xc, wr_ref, wi_ref, br, bi, lam)
        u = sq * (i * xc)
        a_c, b_c = _chunk_scan(a, u)
        a_s[...] = a_c
        b_s[...] = b_c

        def step(k, hprev):
            rows = pl.ds(pl.multiple_of(k * 8, 8), 8)
            hc = a_s[rows, :] * hprev + b_s[rows, :]
            yc_ref[rows, :] = hc
            return hc[7:8, :]

        hlast = lax.fori_loop(0, tt // 8, step, carry_h[0:1, :])
        carry_h[...] = jnp.broadcast_to(hlast, carry_h.shape)
        _, _, n_c = _rms(yc_ref[...], nc)
        mix_c = n_c * (cg * _sigmoid(cg))
        mixac_ref[...] = jnp.concatenate([mix_a, mix_c], axis=1).astype(mixac_ref.dtype)

    row_blk = lambda w: pl.BlockSpec((tt, w), lambda s, t: (rowmap(s, t), 0))
    return _pcall(
        body, name=name, grid=(T // S, nt), in_specs=in_specs,
        out_shape=(jax.ShapeDtypeStruct((T, 2 * W), BF16), jax.ShapeDtypeStruct((T, W), F32),
                   jax.ShapeDtypeStruct((T, W), F32), jax.ShapeDtypeStruct((T, W), F32)),
        out_specs=(row_blk(2 * W), row_blk(W), row_blk(W), row_blk(W)),
        scratch_shapes=[pltpu.VMEM((8, W), F32), pltpu.VMEM((8, W), F32), pltpu.VMEM((8, W), F32),
                        pltpu.VMEM((tt, W), F32), pltpu.VMEM((tt, W), F32)],
        compiler_params=_params(("arbitrary", "arbitrary")),
    )(*([h] * sum(counts)), caw, ccw, pv, wr, wi)


def _ac_bwd(h, cv, xc, yc, dmix, caw, ccw, pv, wr, wi, *, S, D, tt, name):
    T = h.shape[0]
    W = D // 4
    nt = S // tt
    rowmap = lambda s, t: s * nt + (nt - 1 - t)
    c_off = D + D // 2 + 2 * (D // 16) + D // 2
    offs = [0, W, 2 * W, 3 * W, c_off, c_off + W]
    in_specs, counts = [], []
    for off in offs:
        specs, _ = _colspecs(off, W, tt, rowmap)
        in_specs += specs
        counts.append(len(specs))
    row_blk = lambda w, cb=0: pl.BlockSpec((tt, w), lambda s, t: (rowmap(s, t), cb))
    in_specs += [row_blk(W), row_blk(W), row_blk(W)]
    in_specs.append(pl.BlockSpec((8, W), lambda s, t: (jnp.maximum(rowmap(s, t) * (tt // 8) - 1, 0), 0)))
    in_specs += [row_blk(W, 0), row_blk(W, 3)]
    full = lambda shape: pl.BlockSpec(shape, lambda s, t: (0,) * len(shape))
    in_specs += [full(caw.shape), full(ccw.shape), full(pv.shape), full(wr.shape), full(wi.shape)]

    def body(*refs):
        refs = list(refs)
        ab, ac, ax, ag, cx, cg = [_cat(_take(refs, n)) for n in counts]
        cv_ref, xc_ref, yc_ref, halo_ref, dma_ref, dmc_ref, caw_ref, ccw_ref, pv_ref, wr_ref, wi_ref = _take(refs, 11)
        dha_ref, dhc_ref, vec_ref, dwr_ref, dwi_ref = _take(refs, 5)
        carry_dcv, carry_dxc, carry_a, carry_g, c_s, b_s, g_s = refs
        s_id, t = pl.program_id(0), pl.program_id(1)

        @pl.when(t == 0)
        def _():
            for cr in (carry_dcv, carry_dxc, carry_a, carry_g):
                cr[...] = jnp.zeros_like(cr)

        @pl.when((t == 0) & (s_id == 0))
        def _():
            vec_ref[...] = jnp.zeros_like(vec_ref)
            dwr_ref[...] = jnp.zeros_like(dwr_ref)
            dwi_ref[...] = jnp.zeros_like(dwi_ref)

        def acc_row(k, val):
            vec_ref[k:k + 1, :] += jnp.sum(val, axis=0, keepdims=True)

        ccb, br, bi, lam, na, nc = [pv_ref[k:k + 1, :] for k in range(6)]
        cv = cv_ref[...]
        dmix_a = dma_ref[...]
        p = ac * ax
        xn, rstd, n_a = _rms(ab * cv, na)
        sg = _sigmoid(ag)
        dn = dmix_a * (ag * sg)
        dag = dmix_a * n_a * (sg * (1.0 + ag * (1.0 - sg)))
        acc_row(3, dn * xn)
        dya = _rms_bwd(dn, xn, rstd, na)
        dab = dya * cv
        dcv = dya * ab
        cd = carry_dcv[...]
        d1, d2 = _shift_up(dcv, 1, cd), _shift_up(dcv, 2, cd)
        dp = caw_ref[2:3, :] * dcv + caw_ref[1:2, :] * d1 + caw_ref[0:1, :] * d2
        acc_row(2, p * dcv)
        acc_row(1, p * d1)
        acc_row(0, p * d2)
        carry_dcv[...] = dcv[0:8]
        dha_ref[...] = jnp.concatenate([dab, dp * ax, dp * ac, dag], axis=1).astype(dha_ref.dtype)
        xc = xc_ref[...]
        yc = yc_ref[...]
        dmix_c = dmc_ref[...]
        xn, rstd, n_c = _rms(yc, nc)
        sg = _sigmoid(cg)
        dn = dmix_c * (cg * sg)
        dcg = dmix_c * n_c * (sg * (1.0 + cg * (1.0 - sg)))
        acc_row(12, dn * xn)
        dyc = _rms_bwd(dn, xn, rstd, nc)
        r, i, ls, a, sq = _gates(xc, wr_ref, wi_ref, br, bi, lam)
        halo = jnp.where(t == nt - 1, 0.0, halo_ref[...])
        hprev = _shift_down(yc, 1, halo)
        c_c, b_c = _chunk_scan_rev(_shift_up(a, 1, carry_a[...]), dyc)
        c_s[...] = c_c
        b_s[...] = b_c

        def step(k, gnext):
            rows = pl.ds(pl.multiple_of((tt // 8 - 1 - k) * 8, 8), 8)
            gc = b_s[rows, :] + c_s[rows, :] * gnext
            g_s[rows, :] = gc
            return gc[0:1, :]

        lax.fori_loop(0, tt // 8, step, carry_g[0:1, :])
        g = g_s[...]
        carry_g[...] = g[0:8]
        carry_a[...] = a[0:8]
        da = g * hprev
        ixc = i * xc
        dsq = g * ixc
        di = g * sq * xc
        dxc = g * sq * i
        dla = da * a - dsq * (a * a) / sq
        dr = dla * (RG_C * ls)
        acc_row(11, dla * (RG_C * r) * _sigmoid(-lam))
        dgr = dr * r * (1.0 - r)
        dgi = di * i * (1.0 - i)
        acc_row(9, dgr)
        acc_row(10, dgi)
        hw = W // N_RG_HEADS
        parts = []
        for hd in range(N_RG_HEADS):
            sl = slice(hd * hw, (hd + 1) * hw)
            dwr_ref[hd] += _dot(xc[:, sl], dgr[:, sl], TN)
            dwi_ref[hd] += _dot(xc[:, sl], dgi[:, sl], TN)
            parts.append(_dot(dgr[:, sl], wr_ref[hd], NT) + _dot(dgi[:, sl], wi_ref[hd], NT))
        dxc = dxc + jnp.concatenate(parts, axis=1)
        ce = carry_dxc[...]
        e1, e2, e3 = _shift_up(dxc, 1, ce), _shift_up(dxc, 2, ce), _shift_up(dxc, 3, ce)
        dcx = ccw_ref[3:4, :] * dxc + ccw_ref[2:3, :] * e1 + ccw_ref[1:2, :] * e2 + ccw_ref[0:1, :] * e3
        acc_row(7, cx * dxc)
        acc_row(6, cx * e1)
        acc_row(5, cx * e2)
        acc_row(4, cx * e3)
        acc_row(8, dxc)
        carry_dxc[...] = dxc[0:8]
        dhc_ref[...] = jnp.concatenate([dcx, dcg], axis=1).astype(dhc_ref.dtype)

    const = lambda shape: pl.BlockSpec(shape, lambda s, t: (0,) * len(shape))
    return _pcall(
        body, name=name, grid=(T // S, nt), in_specs=in_specs,
        out_shape=(jax.ShapeDtypeStruct((T, 4 * W), BF16), jax.ShapeDtypeStruct((T, 2 * W), BF16),
                   jax.ShapeDtypeStruct((16, W), F32), jax.ShapeDtypeStruct(wr.shape, F32),
                   jax.ShapeDtypeStruct(wi.shape, F32)),
        out_specs=(row_blk(4 * W), row_blk(2 * W), const((16, W)), const(wr.shape), const(wi.shape)),
        scratch_shapes=[pltpu.VMEM((8, W), F32)] * 4 + [pltpu.VMEM((tt, W), F32)] * 3,
        compiler_params=_params(("arbitrary", "arbitrary")),
    )(*([h] * sum(counts)), cv, xc, yc, yc, dmix, dmix, caw, ccw, pv, wr, wi)


def _lo_mask():
    return lax.broadcasted_iota(jnp.int32, (1, 2 * HEAD_DIM), 1) < HEAD_DIM


def _dup(blk, odd, lo):
    rot = pltpu.roll(blk, HEAD_DIM, 1)
    return jnp.where(lo, rot, blk) if odd else jnp.where(lo, blk, rot)


def _stack_heads(x, hh, lo, masked):
    parts = []
    for g in range(KV_GROUP):
        jq = hh * KV_GROUP + g
        pb = x[:, (jq // 2) * 128:(jq // 2 + 1) * 128]
        if masked:
            pb = jnp.where(lo if jq % 2 == 0 else jnp.logical_not(lo), pb, 0.0)
        parts.append(pb)
    return jnp.concatenate(parts, axis=0)


def _unstack_pairs(st, lo):
    return [jnp.where(lo, st[(2 * pi) * BLOCK:(2 * pi + 1) * BLOCK], st[(2 * pi + 1) * BLOCK:(2 * pi + 2) * BLOCK])
            for pi in range(KV_GROUP // 2)]


def _window(ref, n):
    prev = ref[pl.ds(pl.multiple_of(jnp.maximum(n - 1, 0) * BLOCK, BLOCK), BLOCK), :]
    cur = ref[pl.ds(pl.multiple_of(n * BLOCK, BLOCK), BLOCK), :]
    return jnp.concatenate([prev, cur], axis=0)


def _valid_mask(n):
    rows = KV_GROUP * BLOCK
    qi = lax.broadcasted_iota(jnp.int32, (rows, 2 * BLOCK), 0) & (BLOCK - 1)
    kj = lax.broadcasted_iota(jnp.int32, (rows, 2 * BLOCK), 1)
    dist = qi + BLOCK - kj
    return (dist >= 0) & (dist < BLOCK) & ((n > 0) | (kj >= BLOCK))


def _sink_col(sinks_ref, layer, hh):
    return jnp.concatenate([jnp.full((BLOCK, 1), sinks_ref[layer, hh * KV_GROUP + g], F32) for g in range(KV_GROUP)],
                           axis=0)


def _softmax(qs, kdup, valid, sink):
    s = _dot(qs, kdup, NT) * (HEAD_DIM ** -0.5)
    s = jnp.where(valid, s, NEG_INF)
    m = jnp.maximum(jnp.max(s, axis=-1, keepdims=True), sink)
    e = jnp.exp(s - m)
    es = jnp.exp(sink - m)
    den = jnp.sum(e, axis=-1, keepdims=True) + es
    return e / den, es / den


def _attn_fwd(h, sinks, layer, *, S, D, name):
    T = h.shape[0]
    WB, KVW = D // 2, D // 16
    nb = S // BLOCK
    n_kv = KVW // HEAD_DIM

    def body(q_ref, k_ref, v_ref, sinks_ref, o_ref):
        n = pl.program_id(1)
        lo = _lo_mask()
        q = q_ref[...]
        kk, vv = _window(k_ref, n), _window(v_ref, n)
        valid = _valid_mask(n)
        blocks = []
        for hh in range(n_kv):
            cb = slice((hh // 2) * 128, (hh // 2 + 1) * 128)
            kdup, vdup = _dup(kk[:, cb], hh % 2, lo), _dup(vv[:, cb], hh % 2, lo)
            p, _ = _softmax(_stack_heads(q, hh, lo, True), kdup, valid, _sink_col(sinks_ref, layer, hh))
            blocks += _unstack_pairs(_dot(p, vdup, NN), lo)
        o_ref[...] = jnp.concatenate(blocks, axis=1)

    return _pcall(
        body, name=name, grid=(T // S, nb),
        in_specs=[pl.BlockSpec((BLOCK, WB), lambda s, n: (s * nb + n, D // WB)),
                  pl.BlockSpec((S, KVW), lambda s, n: (s, (D + WB) // KVW)),
                  pl.BlockSpec((S, KVW), lambda s, n: (s, (D + WB) // KVW + 1)),
                  pl.BlockSpec(memory_space=pltpu.SMEM)],
        out_shape=jax.ShapeDtypeStruct((T, WB), F32),
        out_specs=pl.BlockSpec((BLOCK, WB), lambda s, n: (s * nb + n, 0)),
        compiler_params=_params(("arbitrary", "arbitrary")),
    )(h, h, h, sinks)


def _attn_bwd(h, yb, dyb, sinks, layer, *, S, D, name):
    T = h.shape[0]
    WB, KVW = D // 2, D // 16
    nb = S // BLOCK
    n_kv = KVW // HEAD_DIM

    def body(q_ref, k_ref, v_ref, o_ref, do_ref, sinks_ref, dq_ref, dk_ref, dv_ref, dsink_ref, dk_acc, dv_acc):
        s_id, n = pl.program_id(0), pl.program_id(1)
        lo = _lo_mask()

        @pl.when(n == 0)
        def _():
            dk_acc[...] = jnp.zeros_like(dk_acc)
            dv_acc[...] = jnp.zeros_like(dv_acc)

        @pl.when((n == 0) & (s_id == 0))
        def _():
            dsink_ref[...] = jnp.zeros_like(dsink_ref)

        q, o, do = q_ref[...], o_ref[...], do_ref[...]
        kk, vv = _window(k_ref, n), _window(v_ref, n)
        valid = _valid_mask(n)
        lane = lax.broadcasted_iota(jnp.int32, dsink_ref.shape, 1)
        dq_blocks, dk_heads, dv_heads = [], [], []
        dsink = jnp.zeros(dsink_ref.shape, F32)
        for hh in range(n_kv):
            cb = slice((hh // 2) * 128, (hh // 2 + 1) * 128)
            kdup, vdup = _dup(kk[:, cb], hh % 2, lo), _dup(vv[:, cb], hh % 2, lo)
            qs = _stack_heads(q, hh, lo, True)
            dos = _stack_heads(do, hh, lo, True)
            delta = jnp.sum(dos * _stack_heads(o, hh, lo, False), axis=-1, keepdims=True)
            p, psink = _softmax(qs, kdup, valid, _sink_col(sinks_ref, layer, hh))
            dvr = _dot(p, dos, TN)
            dv_heads.append(dvr + pltpu.roll(dvr, HEAD_DIM, 1))
            ds = p * (_dot(dos, vdup, NT) - delta) * (HEAD_DIM ** -0.5)
            dq_blocks += _unstack_pairs(_dot(ds, kdup, NN), lo)
            dkr = _dot(ds, qs, TN)
            dk_heads.append(dkr + pltpu.roll(dkr, HEAD_DIM, 1))
            dsk = -psink * delta
            for g in range(KV_GROUP):
                tot = jnp.sum(dsk[g * BLOCK:(g + 1) * BLOCK], axis=0, keepdims=True)
                dsink = dsink + jnp.where(lane == hh * KV_GROUP + g, tot, 0.0)
        dsink_ref[...] += dsink
        dq_ref[...] = jnp.concatenate(dq_blocks, axis=1).astype(dq_ref.dtype)
        pair = lambda hs: jnp.concatenate([jnp.where(lo, hs[2 * m], hs[2 * m + 1]) for m in range(n_kv // 2)], axis=1)
        dkk, dvv = pair(dk_heads), pair(dv_heads)
        prev = pl.ds(pl.multiple_of(jnp.maximum(n - 1, 0) * BLOCK, BLOCK), BLOCK)
        cur = pl.ds(pl.multiple_of(n * BLOCK, BLOCK), BLOCK)
        dk_acc[prev, :] += dkk[:BLOCK]
        dk_acc[cur, :] += dkk[BLOCK:]
        dv_acc[prev, :] += dvv[:BLOCK]
        dv_acc[cur, :] += dvv[BLOCK:]

        @pl.when(n == nb - 1)
        def _():
            dk_ref[...] = dk_acc[...].astype(dk_ref.dtype)
            dv_ref[...] = dv_acc[...].astype(dv_ref.dtype)

    blk = lambda cb=0: pl.BlockSpec((BLOCK, WB), lambda s, n: (s * nb + n, cb))
    seq = lambda cb=0: pl.BlockSpec((S, KVW), lambda s, n: (s, cb))
    return _pcall(
        body, name=name, grid=(T // S, nb),
        in_specs=[blk(D // WB), seq((D + WB) // KVW), seq((D + WB) // KVW + 1), blk(), blk(),
                  pl.BlockSpec(memory_space=pltpu.SMEM)],
        out_shape=(jax.ShapeDtypeStruct((T, WB), BF16), jax.ShapeDtypeStruct((T, KVW), BF16),
                   jax.ShapeDtypeStruct((T, KVW), BF16), jax.ShapeDtypeStruct((8, 128), F32)),
        out_specs=(blk(), seq(), seq(), pl.BlockSpec((8, 128), lambda s, n: (0, 0))),
        scratch_shapes=[pltpu.VMEM((S, KVW), F32), pltpu.VMEM((S, KVW), F32)],
        compiler_params=_params(("arbitrary", "arbitrary")),
    )(h, h, h, yb, dyb, sinks)


def _bg_specs(D, tm):
    return _colspecs(D + D // 2 + 2 * (D // 16), D // 2, tm, lambda i: i)


def _mixb_fwd(yb, h, nb_g, *, D, tm, name):
    T, WB = yb.shape
    bg_specs, _ = _bg_specs(D, tm)

    def body(*refs):
        refs = list(refs)
        yb_ref = refs.pop(0)
        bg = _cat(_take(refs, len(bg_specs)))
        g_ref, o_ref = refs
        _, _, nrm = _rms(yb_ref[...], g_ref[...])
        o_ref[...] = (nrm * (bg * _sigmoid(bg))).astype(o_ref.dtype)

    row = pl.BlockSpec((tm, WB), lambda i: (i, 0))
    return _pcall(body, name=name, grid=(T // tm,),
                  in_specs=[row] + bg_specs + [pl.BlockSpec((1, WB), lambda i: (0, 0))],
                  out_shape=jax.ShapeDtypeStruct((T, WB), BF16), out_specs=row,
                  compiler_params=_params(("arbitrary",)))(yb, *([h] * len(bg_specs)), nb_g)


def _mixb_bwd(yb, h, dmix, nb_g, *, D, tm, name):
    T, WB = yb.shape
    W = D // 4
    bg_specs, _ = _bg_specs(D, tm)
    dm_specs, _ = _colspecs(W, WB, tm, lambda i: i)

    def body(*refs):
        refs = list(refs)
        yb_ref = refs.pop(0)
        bg = _cat(_take(refs, len(bg_specs)))
        dmix_b = _cat(_take(refs, len(dm_specs)))
        g_ref, dyb_ref, dbg_ref, dg_ref = refs

        @pl.when(pl.program_id(0) == 0)
        def _():
            dg_ref[...] = jnp.zeros_like(dg_ref)

        gamma = g_ref[...]
        xn, rstd, nrm = _rms(yb_ref[...], gamma)
        sg = _sigmoid(bg)
        dn = dmix_b * (bg * sg)
        dbg_ref[...] = (dmix_b * nrm * (sg * (1.0 + bg * (1.0 - sg)))).astype(dbg_ref.dtype)
        dg_ref[0:1, :] += jnp.sum(dn * xn, axis=0, keepdims=True)
        dyb_ref[...] = _rms_bwd(dn, xn, rstd, gamma)

    row = pl.BlockSpec((tm, WB), lambda i: (i, 0))
    return _pcall(body, name=name, grid=(T // tm,),
                  in_specs=[row] + bg_specs + dm_specs + [pl.BlockSpec((1, WB), lambda i: (0, 0))],
                  out_shape=(jax.ShapeDtypeStruct((T, WB), F32), jax.ShapeDtypeStruct((T, WB), BF16),
                             jax.ShapeDtypeStruct((8, WB), F32)),
                  out_specs=(row, row, pl.BlockSpec((8, WB), lambda i: (0, 0))),
                  compiler_params=_params(("arbitrary",)))(yb, *([h] * len(bg_specs)), *([dmix] * len(dm_specs)), nb_g)


def _concat_cols(parts, *, tm, name):
    T = parts[0].shape[0]
    total = sum(p.shape[1] for p in parts)

    def body(*refs):
        refs[-1][...] = jnp.concatenate([r[...] for r in refs[:-1]], axis=1)

    return _pcall(body, name=name, grid=(T // tm,),
                  in_specs=[pl.BlockSpec((tm, p.shape[1]), lambda i: (i, 0)) for p in parts],
                  out_shape=jax.ShapeDtypeStruct((T, total), parts[0].dtype),
                  out_specs=pl.BlockSpec((tm, total), lambda i: (i, 0)),
                  compiler_params=_params(("parallel",)))(*parts)


def _ln_fwd(z, g, b, *, tm, name):
    T, D = z.shape

    def body(z_ref, g_ref, b_ref, y_ref, yb_ref):
        zv = z_ref[...]
        mu = jnp.mean(zv, axis=-1, keepdims=True)
        zc = zv - mu
        var = jnp.mean(zc * zc, axis=-1, keepdims=True)
        y = zc * lax.rsqrt(var + LN_EPS) * g_ref[...] + b_ref[...]
        y_ref[...] = y
        yb_ref[...] = y.astype(BF16)

    row = pl.BlockSpec((tm, D), lambda i: (i, 0))
    vec = pl.BlockSpec((1, D), lambda i: (0, 0))
    return _pcall(body, name=name, grid=(T // tm,), in_specs=[row, vec, vec],
                  out_shape=(jax.ShapeDtypeStruct((T, D), F32), jax.ShapeDtypeStruct((T, D), BF16)),
                  out_specs=(row, row), compiler_params=_params(("parallel",)))(z, g, b)


def _ln_bwd(z, dy, g, *, tm, name):
    T, D = z.shape

    def body(z_ref, dy_ref, g_ref, dz_ref, dzb_ref, dgb_ref):
        @pl.when(pl.program_id(0) == 0)
        def _():
            dgb_ref[...] = jnp.zeros_like(dgb_ref)

        zv, dyv = z_ref[...], dy_ref[...]
        mu = jnp.mean(zv, axis=-1, keepdims=True)
        zc = zv - mu
        rstd = lax.rsqrt(jnp.mean(zc * zc, axis=-1, keepdims=True) + LN_EPS)
        xh = zc * rstd
        dxh = dyv * g_ref[...]
        dz = rstd * (dxh - jnp.mean(dxh, axis=-1, keepdims=True) - xh * jnp.mean(dxh * xh, axis=-1, keepdims=True))
        dz_ref[...] = dz
        dzb_ref[...] = dz.astype(BF16)
        dgb_ref[0:1, :] += jnp.sum(dyv * xh, axis=0, keepdims=True)
        dgb_ref[1:2, :] += jnp.sum(dyv, axis=0, keepdims=True)

    row = pl.BlockSpec((tm, D), lambda i: (i, 0))
    return _pcall(body, name=name, grid=(T // tm,), in_specs=[row, row, pl.BlockSpec((1, D), lambda i: (0, 0))],
                  out_shape=(jax.ShapeDtypeStruct((T, D), F32), jax.ShapeDtypeStruct((T, D), BF16),
                             jax.ShapeDtypeStruct((8, D), F32)),
                  out_specs=(row, row, pl.BlockSpec((8, D), lambda i: (0, 0))),
                  compiler_params=_params(("arbitrary",)))(z, dy, g)


def _loss_head(y, target, *, tm, name):
    T, D = y.shape

    def body(y_ref, t_ref, dy_ref, loss_ref):
        @pl.when(pl.program_id(0) == 0)
        def _():
            loss_ref[...] = jnp.zeros_like(loss_ref)

        err = y_ref[...] - t_ref[...]
        dy_ref[...] = err / D
        loss_ref[...] += 0.5 * jnp.sum(jnp.mean(err * err, axis=-1, keepdims=True), axis=0, keepdims=True)

    row = pl.BlockSpec((tm, D), lambda i: (i, 0))
    return _pcall(body, name=name, grid=(T // tm,), in_specs=[row, row],
                  out_shape=(jax.ShapeDtypeStruct((T, D), F32), jax.ShapeDtypeStruct((1, 1), F32)),
                  out_specs=(row, pl.BlockSpec((1, 1), lambda i: (0, 0))),
                  compiler_params=_params(("arbitrary",)))(y, target)


def _cast_bf16(w, layer, *, name):
    _, R, C = w.shape
    tr = _tile(R, 512, 8)

    def body(w_ref, o_ref):
        o_ref[...] = w_ref[...].astype(BF16)

    return _pcall(body, name=name, grid=(R // tr,), in_specs=[pl.BlockSpec((None, tr, C), lambda i: (layer, i, 0))],
                  out_shape=jax.ShapeDtypeStruct((R, C), BF16), out_specs=pl.BlockSpec((tr, C), lambda i: (i, 0)),
                  compiler_params=_params(("parallel",)))(w)


def _adamw_layer(g, w, m, v, layer, bufs, *, name):
    L, R, C = w.shape
    tr = _tile(R, max(8, (1 << 19) // C // 8 * 8), 8)
    if bufs is None:
        bufs = [lax.empty((L, R, C), F32) for _ in range(4)]

    def body(g_ref, w_ref, m_ref, v_ref, b0, b1, b2, b3, go_ref, d_ref, nm_ref, nv_ref):
        gv = g_ref[...]
        nm = ADAM_B1 * m_ref[...] + (1.0 - ADAM_B1) * gv
        nv = ADAM_B2 * v_ref[...] + (1.0 - ADAM_B2) * (gv * gv)
        m_hat = nm / (1.0 - ADAM_B1 ** ADAM_STEP)
        v_hat = nv / (1.0 - ADAM_B2 ** ADAM_STEP)
        go_ref[...] = gv
        d_ref[...] = -ADAM_LR * (m_hat / (jnp.sqrt(v_hat) + ADAM_EPS) + ADAM_WD * w_ref[...])
        nm_ref[...] = nm
        nv_ref[...] = nv

    lay = pl.BlockSpec((None, tr, C), lambda i: (layer, i, 0))
    shp = jax.ShapeDtypeStruct((L, R, C), F32)
    return list(_pcall(body, name=name, grid=(R // tr,),
                       in_specs=[pl.BlockSpec((tr, C), lambda i: (i, 0)), lay, lay, lay] + [ANY] * 4,
                       out_shape=(shp,) * 4, out_specs=(lay,) * 4, input_output_aliases={4 + k: k for k in range(4)},
                       compiler_params=_params(("parallel",)))(g, w, m, v, *bufs))


def _adamw(g, w, m, v, *, name):
    R, C = g.shape
    tr = _tile(R, max(8, (1 << 19) // C // 8 * 8), 8)

    def body(g_ref, w_ref, m_ref, v_ref, d_ref, nm_ref, nv_ref):
        gv = g_ref[...]
        nm = ADAM_B1 * m_ref[...] + (1.0 - ADAM_B1) * gv
        nv = ADAM_B2 * v_ref[...] + (1.0 - ADAM_B2) * (gv * gv)
        m_hat = nm / (1.0 - ADAM_B1 ** ADAM_STEP)
        v_hat = nv / (1.0 - ADAM_B2 ** ADAM_STEP)
        d_ref[...] = -ADAM_LR * (m_hat / (jnp.sqrt(v_hat) + ADAM_EPS) + ADAM_WD * w_ref[...])
        nm_ref[...] = nm
        nv_ref[...] = nv

    blk = pl.BlockSpec((tr, C), lambda i: (i, 0))
    shp = jax.ShapeDtypeStruct((R, C), F32)
    return _pcall(body, name=name, grid=(R // tr,), in_specs=[blk] * 4, out_shape=(shp, shp, shp),
                  out_specs=(blk, blk, blk), compiler_params=_params(("parallel",)))(g, w, m, v)


def _my_core():
    return lax.axis_index("c")


def _my_chip():
    return 2 * lax.axis_index("x") + lax.axis_index("y")


def _pair_sum(mine, theirs, *, half_axis, name):
    R, C = theirs.shape
    tr, tc = _tile(R, 512, 16), _tile(C, 2048)
    nrb, ncb = R // tr, C // tc

    def body(a_ref, b_ref, o_ref):
        o_ref[...] = (a_ref[...].astype(F32) + b_ref[...].astype(F32)).astype(BF16)

    if half_axis == 0:
        a_idx = lambda i, j: (_my_core() * nrb + i, j)
    else:
        a_idx = lambda i, j: (i, _my_core() * ncb + j)
    blk = pl.BlockSpec((tr, tc), lambda i, j: (i, j))
    return _pcall(body, name=name, grid=(nrb, ncb), in_specs=[pl.BlockSpec((tr, tc), a_idx), blk], out_specs=blk,
                  out_shape=jax.ShapeDtypeStruct(theirs.shape, BF16),
                  compiler_params=_params(("parallel", "parallel")))(mine, theirs)


def _final_sum(own, got, *, own_axis, out_shape, out_axis, name):
    _, R, C = got.shape
    tr, tc = _tile(R, 512, 16), _tile(C, 1024)
    nrb, ncb = R // tr, C // tc

    def body(a_ref, q_ref, o_ref):
        o_ref[...] = ((a_ref[...].astype(F32) + q_ref[0].astype(F32)) + q_ref[1].astype(F32)) + q_ref[2].astype(F32)

    if own_axis == 1:
        a_idx = lambda i, j: (i, _my_chip() * ncb + j)
    else:
        a_idx = lambda i, j: (_my_chip() * nrb + i, j)
    if out_axis == 0:
        o_idx = lambda i, j: (_my_core() * nrb + i, j)
    else:
        o_idx = lambda i, j: (i, _my_core() * ncb + j)
    return _pcall(body, name=name, grid=(nrb, ncb),
                  in_specs=[pl.BlockSpec((tr, tc), a_idx), pl.BlockSpec((3, tr, tc), lambda i, j: (0, i, j))],
                  out_specs=pl.BlockSpec((tr, tc), o_idx), out_shape=jax.ShapeDtypeStruct(out_shape, F32),
                  compiler_params=_params(("parallel", "parallel")))(own, got)


def _sum_devices(gathered, *, name):
    _, R, C = gathered.shape
    tr = _tile(R, 280, 8)

    def body(g_ref, o_ref):
        acc = g_ref[0]
        for d in range(1, N_DEV):
            acc = acc + g_ref[d]
        o_ref[...] = acc

    return _pcall(body, name=name, grid=(R // tr,), in_specs=[pl.BlockSpec((N_DEV, tr, C), lambda i: (0, i, 0))],
                  out_shape=jax.ShapeDtypeStruct((R, C), F32), out_specs=pl.BlockSpec((tr, C), lambda i: (i, 0)),
                  compiler_params=_params(("parallel",)))(gathered)


def _position():
    x, y, c = lax.axis_index("x"), lax.axis_index("y"), lax.axis_index("c")
    chips = [(1 - x, y), (x, 1 - y), (1 - x, 1 - y)]
    return x, y, c, chips


def _remote(src, dst, send_sems, recv_sems, k, to):
    return pltpu.make_async_remote_copy(src_ref=src, dst_ref=dst, send_sem=send_sems.at[k], recv_sem=recv_sems.at[k],
                                        device_id=to, device_id_type=MESH)


def _r(ref, start, n):
    return ref.at[pl.ds(pl.multiple_of(start, 16), n), :]


def _c(ref, start, n):
    return ref.at[:, pl.ds(pl.multiple_of(start, 128), n)]


class _part:
    def __init__(self, ins, outs, plan, n, n_local=0, aliased=False):
        self.ins, self.outs, self.plan, self.n, self.n_local, self.aliased = ins, outs, plan, n, n_local, aliased


def _comm_scratch(parts):
    if not parts:
        return []
    n, nl = sum(p.n for p in parts), sum(p.n_local for p in parts)
    return [pltpu.SemaphoreType.DMA((n,)), pltpu.SemaphoreType.DMA((n,)), pltpu.SemaphoreType.DMA((max(nl, 1),))]


def _comm_aliases(parts, in_base, out_base):
    aliases, ii, oi = {}, in_base, out_base
    for p in parts:
        if p.aliased:
            aliases.update({ii + k: oi + k for k in range(len(p.ins))})
        ii += len(p.ins)
        oi += len(p.outs)
    return aliases


def _comm_run(parts, phase, in_refs, out_refs, send_sems, recv_sems, local_sems):
    pos = _position()
    me = pos[:3]
    ii = oi = si = li = 0
    for p in parts:
        sends, recvs, locs = p.plan(in_refs[ii:ii + len(p.ins)], out_refs[oi:oi + len(p.outs)], pos)
        assert len(sends) == len(recvs) == p.n and len(locs) == p.n_local
        if phase == "start":
            for k, (src, dst) in enumerate(locs):
                pltpu.make_async_copy(src, dst, local_sems.at[li + k]).start()
            for k, (src, dst, to) in enumerate(sends):
                _remote(src, dst, send_sems, recv_sems, si + k, to).start()
        else:
            for k, dst in enumerate(recvs):
                _remote(dst, dst, send_sems, recv_sems, si + k, me).wait_recv()
            for k, (src, dst, to) in enumerate(sends):
                _remote(src, dst, send_sems, recv_sems, si + k, to).wait_send()
            for k, (src, dst) in enumerate(locs):
                pltpu.make_async_copy(src, dst, local_sems.at[li + k]).wait()
        ii, oi, si, li = ii + len(p.ins), oi + len(p.outs), si + p.n, li + p.n_local


def _comm_call(parts, *, name):
    n_in = sum(len(p.ins) for p in parts)
    n_out = sum(len(p.outs) for p in parts)

    def body(*refs):
        refs = list(refs)
        cin, cout = _take(refs, n_in), _take(refs, n_out)
        _comm_run(parts, "start", cin, cout, *refs)
        _comm_run(parts, "finish", cin, cout, *refs)

    return list(_pcall(body, name=name, in_specs=[ANY] * n_in, out_specs=[ANY] * n_out,
                       out_shape=[s for p in parts for s in p.outs], scratch_shapes=_comm_scratch(parts),
                       input_output_aliases=_comm_aliases(parts, 0, 0))(*[a for p in parts for a in p.ins]))


def _slab(wg, kind, chip, half):
    if kind == "in":
        d, ns = wg.shape[0], wg.shape[1] // N_CHIPS
        return _c(_r(wg, half * (d // 2), d // 2), chip * ns, ns)
    rs = wg.shape[0] // N_CHIPS
    return _r(wg, chip * rs + half * (rs // 2), rs // 2)


def _gather_ici(ws, kind):
    rows, cols = ws.shape
    full = (rows, N_CHIPS * cols) if kind == "in" else (N_CHIPS * rows, cols)

    def plan(ins, outs, pos):
        x, y, c, chips = pos
        j = 2 * x + y
        (ws_ref,), (wg,) = ins, outs
        mine = _r(ws_ref, c * (rows // 2), rows // 2)
        sends = [(mine, _slab(wg, kind, j, c), (*chip, c)) for chip in chips]
        recvs = [_slab(wg, kind, 2 * px + py, c) for px, py in chips]
        own = _c(wg, j * cols, cols) if kind == "in" else _r(wg, j * rows, rows)
        return sends, recvs, [(ws_ref, own)]

    return _part([ws], [jax.ShapeDtypeStruct(full, ws.dtype)], plan, 3, 1)


def _gather_d2d(wg, kind):
    def plan(ins, outs, pos):
        x, y, c, chips = pos
        (ref,) = outs
        sends = [(_slab(ref, kind, 2 * px + py, c), _slab(ref, kind, 2 * px + py, c), (x, y, 1 - c)) for px, py in chips]
        recvs = [_slab(ref, kind, 2 * px + py, 1 - c) for px, py in chips]
        return sends, recvs, []

    return _part([wg], [jax.ShapeDtypeStruct(wg.shape, wg.dtype)], plan, 3, aliased=True)


def _pair_send(gw, kind):
    rows, cols = gw.shape
    half = (rows // 2, cols) if kind == "in" else (rows, cols // 2)

    def plan(ins, outs, pos):
        x, y, c, _ = pos
        (src,), (rb,) = ins, outs
        theirs = _r(src, (1 - c) * half[0], half[0]) if kind == "in" else _c(src, (1 - c) * half[1], half[1])
        return [(theirs, rb, (x, y, 1 - c))], [rb], []

    return _part([gw], [jax.ShapeDtypeStruct(half, gw.dtype)], plan, 1)


def _chip_send(p, kind):
    rows, cols = p.shape
    shard = (rows, cols // N_CHIPS) if kind == "in" else (rows // N_CHIPS, cols)

    def plan(ins, outs, pos):
        x, y, c, chips = pos
        (src,), (q,) = ins, outs
        piece = lambda jk: _c(src, jk * shard[1], shard[1]) if kind == "in" else _r(src, jk * shard[0], shard[0])
        sends = [(piece(2 * px + py), q.at[kk], (px, py, c)) for kk, (px, py) in enumerate(chips)]
        return sends, [q.at[kk] for kk in range(3)], []

    return _part([p], [jax.ShapeDtypeStruct((3,) + shard, p.dtype)], plan, 3)


def _sibling_send(g, kind):
    rows, cols = g.shape

    def plan(ins, outs, pos):
        x, y, c, _ = pos
        (ref,) = outs
        half = (lambda h: _r(ref, h * (rows // 2), rows // 2)) if kind == "in" else (
            lambda h: _c(ref, h * (cols // 2), cols // 2))
        return [(half(c), half(c), (x, y, 1 - c))], [half(1 - c)], []

    return _part([g], [jax.ShapeDtypeStruct(g.shape, g.dtype)], plan, 1, aliased=True)


def _gather_small(block, *, name):
    R, C = block.shape

    def body(x_ref, out_ref, send_sems, recv_sems, local_sem):
        x, y, c, chips = _position()
        me, sib = (x, y, c), (x, y, 1 - c)
        slot = lambda px, py, pc: out_ref.at[4 * px + 2 * py + pc]
        mine = pltpu.make_async_copy(x_ref, slot(*me), local_sem)
        mine.start()
        first = [_remote(x_ref, slot(*me), send_sems, recv_sems, 0, sib)]
        first += [_remote(x_ref, slot(*me), send_sems, recv_sems, 1 + kk, (*chip, c)) for kk, chip in enumerate(chips)]
        for cp in first:
            cp.start()
        passed = []
        for kk, chip in enumerate(chips):
            _remote(slot(*chip, c), slot(*chip, c), send_sems, recv_sems, 1 + kk, sib).wait_recv()
            passed.append(_remote(slot(*chip, c), slot(*chip, c), send_sems, recv_sems, 4 + kk, sib))
            passed[-1].start()
        _remote(slot(*sib), slot(*sib), send_sems, recv_sems, 0, sib).wait_recv()
        for kk, chip in enumerate(chips):
            _remote(slot(*chip, 1 - c), slot(*chip, 1 - c), send_sems, recv_sems, 4 + kk, sib).wait_recv()
        for cp in first + passed:
            cp.wait_send()
        mine.wait()

    return _pcall(
        body, name=name, in_specs=[ANY], out_specs=ANY, out_shape=jax.ShapeDtypeStruct((N_DEV, R, C), F32),
        scratch_shapes=[pltpu.SemaphoreType.DMA((7,)), pltpu.SemaphoreType.DMA((7,)), pltpu.SemaphoreType.DMA],
    )(block)


_SMALL = ["gate_r_w", "gate_i_w", "conv_a_w", "conv_c_w", "sinks", "conv_c_b", "gate_r_b", "gate_i_b", "rg_lambda",
          "norm_a", "norm_b", "norm_c", "ln_g", "ln_b"]


def _pack_small(p):
    L = p["ln_g"].shape[0]
    rows = []
    for n in _SMALL:
        a = p[n]
        if n in ("gate_r_w", "gate_i_w", "norm_b", "ln_g", "ln_b"):
            a = a.reshape(L, -1, 1024)
        elif a.ndim == 2:
            a = a[:, None, :]
        if a.shape[-1] < 1024:
            a = jnp.pad(a, ((0, 0), (0, 0), (0, 1024 - a.shape[-1])))
        rows.append(a)
    out = jnp.concatenate(rows, axis=1)
    assert out.shape[1] == SMALL_ROWS
    return out.reshape(L * SMALL_ROWS, 1024)


def _unpack_small(flat, like):
    L = like["ln_g"].shape[0]
    a = flat.reshape(L, SMALL_ROWS, 1024)
    out, r = {}, 0
    for n in _SMALL:
        shp = like[n].shape
        nrows = max(1, math.prod(shp[1:]) // 1024) if n in ("gate_r_w", "gate_i_w", "norm_b", "ln_g", "ln_b") else (
            shp[1] if len(shp) == 3 else 1)
        blk = a[:, r:r + nrows, :]
        if n in ("gate_r_w", "gate_i_w", "norm_b", "ln_g", "ln_b"):
            out[n] = blk.reshape(shp)
        elif len(shp) == 3:
            out[n] = blk[:, :, :shp[2]]
        else:
            out[n] = blk[:, 0, :shp[1]]
        r += nrows
    return out


def kernel(x, w_in, conv_a_w, sinks, conv_c_w, conv_c_b, gate_r_w, gate_r_b, gate_i_w, gate_i_b, rg_lambda, norm_a, norm_b, norm_c, w_out, ln_g, ln_b, loss_target, m_w_in, m_conv_a_w, m_sinks, m_conv_c_w, m_conv_c_b, m_gate_r_w, m_gate_r_b, m_gate_i_w, m_gate_i_b, m_rg_lambda, m_norm_a, m_norm_b, m_norm_c, m_w_out, m_ln_g, m_ln_b, v_w_in, v_conv_a_w, v_sinks, v_conv_c_w, v_conv_c_b, v_gate_r_w, v_gate_r_b, v_gate_i_w, v_gate_i_b, v_rg_lambda, v_norm_a, v_norm_b, v_norm_c, v_w_out, v_ln_g, v_ln_b):
    names = ["w_in", "conv_a_w", "sinks", "conv_c_w", "conv_c_b", "gate_r_w", "gate_r_b", "gate_i_w", "gate_i_b",
             "rg_lambda", "norm_a", "norm_b", "norm_c", "w_out", "ln_g", "ln_b"]
    w = dict(zip(names, [w_in, conv_a_w, sinks, conv_c_w, conv_c_b, gate_r_w, gate_r_b, gate_i_w, gate_i_b, rg_lambda,
                         norm_a, norm_b, norm_c, w_out, ln_g, ln_b]))
    mom = dict(zip(names, [m_w_in, m_conv_a_w, m_sinks, m_conv_c_w, m_conv_c_b, m_gate_r_w, m_gate_r_b, m_gate_i_w,
                           m_gate_i_b, m_rg_lambda, m_norm_a, m_norm_b, m_norm_c, m_w_out, m_ln_g, m_ln_b]))
    vel = dict(zip(names, [v_w_in, v_conv_a_w, v_sinks, v_conv_c_w, v_conv_c_b, v_gate_r_w, v_gate_r_b, v_gate_i_w,
                           v_gate_i_b, v_rg_lambda, v_norm_a, v_norm_b, v_norm_c, v_w_out, v_ln_g, v_ln_b]))
    B, S, D = x.shape
    T = B * S
    L, _, NS = w_in.shape
    RS = w_out.shape[1]
    W = D // 4
    alpha = (2.0 * L) ** 0.25
    tt = _tile(S, 128, 8)
    tm_row = _tile(T, 256, 8)
    chip = _my_chip()

    ws_in = [_cast_bf16(w_in, l, name="cast_w_in") for l in range(L)]
    ws_out = [_cast_bf16(w_out, l, name="cast_w_out") for l in range(L)]
    wg_in, wg_out = [None] * L, [None] * L
    part_in, part_out = _comm_call([_gather_ici(ws_in[0], "in"), _gather_ici(ws_out[0], "out")], name="gather0_ici")
    wg_in[0], wg_out[0] = _comm_call([_gather_d2d(part_in, "in"), _gather_d2d(part_out, "out")], name="gather0_d2d")
    conv_local = jnp.concatenate([conv_a_w, conv_c_w], axis=1).reshape(L * 7, W // N_CHIPS)
    conv_local = jnp.pad(conv_local, ((0, (-L * 7) % 8), (0, 0)))
    conv_all = _gather_small(conv_local, name="gather_conv")
    conv_full = jnp.concatenate([conv_all[2 * jj][:L * 7] for jj in range(N_CHIPS)], axis=1).reshape(L, 7, W)
    caw_full, ccw_full = conv_full[:, :3], conv_full[:, 3:]

    xf = x.reshape(T, D)
    xb = _cast_bf16(xf[None], 0, name="cast_x")
    saved = []
    for l in range(L):
        comm = ([_gather_ici(ws_in[l + 1], "in")] if l + 1 < L else []) + ([_gather_d2d(part_out, "out")] if l else [])
        res = _mm(xb, wg_in[l], mode="nn", out_dtype=F32, name="proj_in", tm=1024, tn=1536, tk=512, comm=comm)
        h = res if not comm else res.pop(0)
        if l + 1 < L:
            part_in = res.pop(0)
        if l:
            wg_out[l] = res.pop(0)
        pv = jnp.stack([conv_c_b[l], gate_r_b[l], gate_i_b[l], rg_lambda[l], norm_a[l], norm_c[l]])
        mix_ac, cv, xc, yc = _ac_fwd(h, caw_full[l], ccw_full[l], pv, gate_r_w[l], gate_i_w[l], S=S, D=D, tt=tt,
                                     name="ac_fwd")
        yb = _attn_fwd(h, sinks, l, S=S, D=D, name="attn_fwd")
        mix_b = _mixb_fwd(yb, h, norm_b[l][None], D=D, tm=tm_row, name="mixb_fwd")
        mix = _concat_cols([mix_ac[:, :W], mix_b, mix_ac[:, W:]], tm=tm_row, name="concat_mix")
        comm = [_gather_ici(ws_out[l + 1], "out"), _gather_d2d(part_in, "in")] if l + 1 < L else []
        res = _mm(mix, wg_out[l], mode="nn", out_dtype=F32, name="proj_out", add=xf, add_scale=alpha, comm=comm)
        z = res if not comm else res.pop(0)
        if l + 1 < L:
            part_out, wg_in[l + 1] = res
        saved.append((xb, h, cv, xc, yc, yb, mix, z, pv))
        xf, xb = _ln_fwd(z, ln_g[l][None], ln_b[l][None], tm=tm_row, name="ln_fwd")
    dxn, loss_part = _loss_head(xf, loss_target.reshape(T, D), tm=tm_row, name="loss_head")
    loss = lax.psum(loss_part[0, 0], ("x", "y", "c"))

    def final_sums(p_in, q_in, p_out, q_out):
        return (_final_sum(p_in, q_in, own_axis=1, out_shape=(D, NS), out_axis=0, name="final_sum_in"),
                _final_sum(p_out, q_out, own_axis=0, out_shape=(RS, D), out_axis=1, name="final_sum_out"))

    bufs_in = bufs_out = None
    small_g = [None] * L
    p_in = p_out = None
    for l in reversed(range(L)):
        up = l + 1 < L
        xb_l, h, cv, xc, yc, yb, mix, z, pv = saved[l]
        dz, dzb, dgb = _ln_bwd(z, dxn, ln_g[l][None], tm=tm_row, name="ln_bwd")
        res = _mm(dzb, wg_out[l], mode="nt", out_dtype=F32, name="d_mix", comm=[_chip_send(p_out, "out")] if up else [])
        dmix, q_out = res if up else (res, None)
        gw_out = _mm(mix, dzb, mode="tn", out_dtype=BF16, name="d_w_out")
        dha, dhc, vec, dwr, dwi = _ac_bwd(h, cv, xc, yc, dmix, caw_full[l], ccw_full[l], pv, gate_r_w[l], gate_i_w[l],
                                          S=S, D=D, tt=tt, name="ac_bwd")
        dyb, dbg, dnb = _mixb_bwd(yb, h, dmix, norm_b[l][None], D=D, tm=tm_row, name="mixb_bwd")
        dq, dk, dv, dsk = _attn_bwd(h, yb, dyb, sinks, l, S=S, D=D, name="attn_bwd")
        dh = _concat_cols([dha, dq, dk, dv, dbg, dhc], tm=tm_row, name="concat_dh")
        res = _mm(xb_l, dh, mode="tn", out_dtype=BF16, name="d_w_in", tm=1024, tn=1536, tk=512,
                  comm=[_pair_send(gw_out, "out")] + ([_chip_send(p_in, "in")] if up else []))
        gw_in, rb_out = res[0], res[1]
        comm = [_pair_send(gw_in, "in")]
        if up:
            g_in_half, g_out_half = final_sums(p_in, res[2], p_out, q_out)
            comm += [_sibling_send(g_in_half, "in"), _sibling_send(g_out_half, "out")]
        res = _mm(dh, wg_in[l], mode="nt", out_dtype=F32, name="d_x", add=dz, add_scale=alpha, comm=comm)
        dxn, rb_in = res[0], res[1]
        if up:
            bufs_in = _adamw_layer(res[2], w_in, m_w_in, v_w_in, l + 1, bufs_in, name="adamw_w_in")
            bufs_out = _adamw_layer(res[3], w_out, m_w_out, v_w_out, l + 1, bufs_out, name="adamw_w_out")
        p_in = _pair_sum(gw_in, rb_in, half_axis=0, name="pair_sum_in")
        p_out = _pair_sum(gw_out, rb_out, half_axis=1, name="pair_sum_out")
        small_g[l] = dict(gate_r_w=dwr, gate_i_w=dwi, conv_a_w=vec[0:3], conv_c_w=vec[4:8], sinks=dsk[0, :2 * D // 256],
                          conv_c_b=vec[8], gate_r_b=vec[9], gate_i_b=vec[10], rg_lambda=vec[11], norm_a=vec[3],
                          norm_b=dnb[0], norm_c=vec[12], ln_g=dgb[0], ln_b=dgb[1])
    grad_x = dxn.reshape(B, S, D)
    q_in, q_out = _comm_call([_chip_send(p_in, "in"), _chip_send(p_out, "out")], name="chip_exchange0")
    g_in_half, g_out_half = final_sums(p_in, q_in, p_out, q_out)
    g_in0, g_out0 = _comm_call([_sibling_send(g_in_half, "in"), _sibling_send(g_out_half, "out")], name="sibling0")
    big = {"w_in": _adamw_layer(g_in0, w_in, m_w_in, v_w_in, 0, bufs_in, name="adamw_w_in"),
           "w_out": _adamw_layer(g_out0, w_out, m_w_out, v_w_out, 0, bufs_out, name="adamw_w_out")}

    like = {n: w[n] for n in _SMALL}
    like_full = dict(like, conv_a_w=caw_full, conv_c_w=ccw_full)
    part = _pack_small({n: jnp.stack([small_g[l][n] for l in range(L)]) for n in _SMALL})
    g_small = _unpack_small(_sum_devices(_gather_small(part, name="gather_small"), name="sum_small"), like_full)
    for n in ("conv_a_w", "conv_c_w"):
        g_small[n] = lax.dynamic_slice_in_dim(g_small[n], chip * (W // N_CHIPS), W // N_CHIPS, axis=2)

    d_s, m_s, v_s = _adamw(_pack_small(g_small), _pack_small(like), _pack_small({n: mom[n] for n in _SMALL}),
                           _pack_small({n: vel[n] for n in _SMALL}), name="adamw_small")
    grads = dict(g_small)
    delta, new_m, new_v = _unpack_small(d_s, like), _unpack_small(m_s, like), _unpack_small(v_s, like)
    for n in ("w_in", "w_out"):
        grads[n], delta[n], new_m[n], new_v[n] = big[n]

    return (loss, grad_x, *[grads[n] for n in names], *[delta[n] for n in names], *[new_m[n] for n in names],
            *[new_v[n] for n in names])
```

```python
import functools
import math

import jax
import jax.numpy as jnp
from jax import lax
from jax.experimental import pallas as pl
from jax.experimental.pallas import tpu as pltpu

F32 = jnp.float32
BF16 = jnp.bfloat16
_MXU_DTYPE = jnp.bfloat16

HEAD_DIM = 64
KV_GROUP = 8
BLOCK = 128
N_RG_HEADS = 8
RG_C = 8.0
LN_EPS = 1e-5
RMS_EPS = 1e-6
NEG_INF = -1e30
ADAM_LR, ADAM_B1, ADAM_B2, ADAM_EPS, ADAM_WD, ADAM_STEP = 0.001, 0.9, 0.999, 1e-08, 0.01, 10
N_CHIPS = 4
N_DEV = 8
SMALL_ROWS = 280
VMEM_LIMIT = 56 * 1024 * 1024

MESH = pl.DeviceIdType.MESH
ANY = pl.BlockSpec(memory_space=pl.ANY)


def _pcall(body, *, name, **kw):
    return pl.pallas_call(body, name=name, **kw)


def _params(sem=None):
    return pltpu.CompilerParams(dimension_semantics=sem, vmem_limit_bytes=VMEM_LIMIT)


def _tile(dim, pref, mult=128):
    best = None
    for t in range(mult, min(dim, pref) + 1, mult):
        if dim % t == 0:
            best = t
    return best if best is not None else dim


def _dot(a, b, dims):
    return lax.dot_general(a.astype(_MXU_DTYPE), b.astype(_MXU_DTYPE), (dims, ((), ())),
                           preferred_element_type=F32)


NN = ((1,), (0,))
NT = ((1,), (1,))
TN = ((0,), (0,))


def _mm(a, b, *, mode, out_dtype, name, tm=1024, tn=1024, tk=512, add=None, add_scale=1.0, comm=()):
    if mode == "nn":
        (M, K), N = a.shape, b.shape[1]
    elif mode == "nt":
        (M, K), N = a.shape, b.shape[0]
    else:
        (K, M), N = a.shape, b.shape[1]
    tm, tn, tk = _tile(M, tm), _tile(N, tn), _tile(K, tk)
    ni, nj, nk = M // tm, N // tn, K // tk
    dims = {"nn": NN, "nt": NT, "tn": TN}[mode]
    n_cin = sum(len(p.ins) for p in comm)
    n_cout = sum(len(p.outs) for p in comm)

    def body(*refs):
        refs = list(refs)
        a_ref, b_ref = _take(refs, 2)
        add_ref = refs.pop(0) if add is not None else None
        cin = _take(refs, n_cin)
        o_ref = refs.pop(0)
        cout = _take(refs, n_cout)
        acc = refs.pop(0) if nk > 1 else None
        i, j, k = pl.program_id(0), pl.program_id(1), pl.program_id(2)

        if comm:
            @pl.when((i == 0) & (j == 0) & (k == 0))
            def _():
                _comm_run(comm, "start", cin, cout, *refs)

        def finish(r):
            if add_ref is not None:
                r = r + add_scale * add_ref[...]
            o_ref[...] = r.astype(out_dtype)

        if nk == 1:
            finish(_dot(a_ref[...], b_ref[...], dims))
        else:
            @pl.when(k == 0)
            def _():
                acc[...] = jnp.zeros_like(acc)

            acc[...] += _dot(a_ref[...], b_ref[...], dims)

            @pl.when(k == nk - 1)
            def _():
                finish(acc[...])

        if comm:
            @pl.when((i == ni - 1) & (j == nj - 1) & (k == nk - 1))
            def _():
                _comm_run(comm, "finish", cin, cout, *refs)

    a_spec = {"nn": pl.BlockSpec((tm, tk), lambda i, j, k: (i, k)),
              "nt": pl.BlockSpec((tm, tk), lambda i, j, k: (i, k)),
              "tn": pl.BlockSpec((tk, tm), lambda i, j, k: (k, i))}[mode]
    b_spec = {"nn": pl.BlockSpec((tk, tn), lambda i, j, k: (k, j)),
              "nt": pl.BlockSpec((tn, tk), lambda i, j, k: (j, k)),
              "tn": pl.BlockSpec((tk, tn), lambda i, j, k: (k, j))}[mode]
    in_specs, operands = [a_spec, b_spec], [a, b]
    if add is not None:
        in_specs.append(pl.BlockSpec((tm, tn), lambda i, j, k: (i, j)))
        operands.append(add)
    aliases = _comm_aliases(comm, len(operands), 1)
    in_specs += [ANY] * n_cin
    operands += [arr for p in comm for arr in p.ins]
    out_shape = [jax.ShapeDtypeStruct((M, N), out_dtype)] + [s for p in comm for s in p.outs]
    out_specs = [pl.BlockSpec((tm, tn), lambda i, j, k: (i, j))] + [ANY] * n_cout
    sem = ("arbitrary",) * 3 if comm else ("parallel", "parallel", "arbitrary")
    res = _pcall(body, name=name, out_shape=out_shape, grid=(ni, nj, nk), in_specs=in_specs, out_specs=out_specs,
                 scratch_shapes=([pltpu.VMEM((tm, tn), F32)] if nk > 1 else []) + _comm_scratch(comm),
                 input_output_aliases=aliases,
                 compiler_params=_params(sem))(*operands)
    return list(res) if comm else res[0]


def _colspecs(off, width, rows, rowmap):
    bw = math.gcd(off, width) if off else width
    specs = [pl.BlockSpec((rows, bw), functools.partial(lambda cb, *g: (rowmap(*g), cb), off // bw + i))
             for i in range(width // bw)]
    return specs, bw


def _cat(refs):
    vals = [r[...] for r in refs]
    return vals[0] if len(vals) == 1 else jnp.concatenate(vals, axis=1)


def _take(refs, n):
    out = refs[:n]
    del refs[:n]
    return out


def _sigmoid(x):
    return 1.0 / (1.0 + jnp.exp(-x))


def _rms(y, gamma):
    rstd = lax.rsqrt(jnp.mean(y * y, axis=-1, keepdims=True) + RMS_EPS)
    xn = y * rstd
    return xn, rstd, xn * gamma


def _rms_bwd(dn, xn, rstd, gamma):
    dng = dn * gamma
    return rstd * (dng - xn * jnp.mean(dng * xn, axis=-1, keepdims=True))


def _shift_down(x, s, carry8):
    rolled = pltpu.roll(x, s, 0)
    cr = pltpu.roll(carry8, s, 0)
    row8 = lax.broadcasted_iota(jnp.int32, carry8.shape, 0)
    top = jnp.where(row8 < s, cr, rolled[0:8])
    return jnp.concatenate([top, rolled[8:]], axis=0)


def _shift_up(x, s, carry8):
    n = x.shape[0]
    rolled = pltpu.roll(x, n - s, 0)
    cr = pltpu.roll(carry8, 8 - s, 0)
    row8 = lax.broadcasted_iota(jnp.int32, carry8.shape, 0)
    bot = jnp.where(row8 >= 8 - s, cr, rolled[n - 8:])
    return jnp.concatenate([rolled[:n - 8], bot], axis=0)


def _chunk_scan(a, b):
    n = a.shape[0]
    r8 = lax.broadcasted_iota(jnp.int32, a.shape, 0) & 7
    for d in (1, 2, 4):
        ok = r8 >= d
        a_sh = jnp.where(ok, pltpu.roll(a, d, 0), 1.0)
        b_sh = jnp.where(ok, pltpu.roll(b, d, 0), 0.0)
        b = a * b_sh + b
        a = a * a_sh
    return a, b


def _chunk_scan_rev(c, b):
    n = c.shape[0]
    r8 = lax.broadcasted_iota(jnp.int32, c.shape, 0) & 7
    for d in (1, 2, 4):
        ok = r8 + d <= 7
        c_sh = jnp.where(ok, pltpu.roll(c, n - d, 0), 1.0)
        b_sh = jnp.where(ok, pltpu.roll(b, n - d, 0), 0.0)
        b = b + c * b_sh
        c = c * c_sh
    return c, b


def _log1p(x):
    w = 1.0 + x
    return jnp.where(w == 1.0, x, jnp.log(w) * (x / (w - 1.0)))


def _log_sigmoid(x):
    return jnp.minimum(x, 0.0) - _log1p(jnp.exp(-jnp.abs(x)))


def _expm1(x):
    u = jnp.exp(x)
    lu = jnp.log(u)
    small = jnp.where(u == 1.0, x, (u - 1.0) * (x / jnp.where(lu == 0.0, 1.0, lu)))
    return jnp.where(jnp.abs(x) < 0.5, small, u - 1.0)


def _gates(xc, wr_ref, wi_ref, br, bi, lam):
    hw = xc.shape[1] // N_RG_HEADS
    gr = jnp.concatenate([_dot(xc[:, h * hw:(h + 1) * hw], wr_ref[h], NN) for h in range(N_RG_HEADS)], axis=1) + br
    gi = jnp.concatenate([_dot(xc[:, h * hw:(h + 1) * hw], wi_ref[h], NN) for h in range(N_RG_HEADS)], axis=1) + bi
    r, i = _sigmoid(gr), _sigmoid(gi)
    ls = _log_sigmoid(lam)
    la = RG_C * r * ls
    a = jnp.exp(la)
    sq = jnp.sqrt(-_expm1(2.0 * la))
    return r, i, ls, a, sq


def _ac_fwd(h, caw, ccw, pv, wr, wi, *, S, D, tt, name):
    T = h.shape[0]
    W = D // 4
    nt = S // tt
    rowmap = lambda s, t: s * nt + t
    c_off = D + D // 2 + 2 * (D // 16) + D // 2
    offs = [0, W, 2 * W, 3 * W, c_off, c_off + W]
    in_specs, counts = [], []
    for off in offs:
        specs, _ = _colspecs(off, W, tt, rowmap)
        in_specs += specs
        counts.append(len(specs))
    full = lambda shape: pl.BlockSpec(shape, lambda s, t: (0,) * len(shape))
    in_specs += [full(caw.shape), full(ccw.shape), full(pv.shape), full(wr.shape), full(wi.shape)]

    def body(*refs):
        refs = list(refs)
        ab, ac, ax, ag, cx, cg = [_cat(_take(refs, n)) for n in counts]
        caw_ref, ccw_ref, pv_ref, wr_ref, wi_ref = _take(refs, 5)
        mixac_ref, cv_ref, xc_ref, yc_ref = _take(refs, 4)
        carry_p, carry_cx, carry_h, a_s, b_s = refs
        t = pl.program_id(1)

        @pl.when(t == 0)
        def _():
            carry_p[...] = jnp.zeros_like(carry_p)
            carry_cx[...] = jnp.zeros_like(carry_cx)
            carry_h[...] = jnp.zeros_like(carry_h)

        ccb, br, bi, lam, na, nc = [pv_ref[k:k + 1, :] for k in range(6)]
        p = ac * ax
        cp = carry_p[...]
        cv = caw_ref[2:3, :] * p + caw_ref[1:2, :] * _shift_down(p, 1, cp) + caw_ref[0:1, :] * _shift_down(p, 2, cp)
        carry_p[...] = p[tt - 8:tt]
        cv_ref[...] = cv
        _, _, n_a = _rms(ab * cv, na)
        mix_a = n_a * (ag * _sigmoid(ag))
        ccx = carry_cx[...]
        xc = (ccw_ref[3:4, :] * cx + ccw_ref[2:3, :] * _shift_down(cx, 1, ccx) + ccw_ref[1:2, :] * _shift_down(cx, 2, ccx)
              + ccw_ref[0:1, :] * _shift_down(cx, 3, ccx) + ccb)
        carry_cx[...] = cx[tt - 8:tt]
        xc_ref[...] = xc
        r, i, ls, a, sq = _gates(xc, wr_ref, wi_ref, br, bi, lam)
        u = sq * (i * xc)
        a_c, b_c = _chunk_scan(a, u)
        a_s[...] = a_c
        b_s[...] = b_c

        def step(k, hprev):
            rows = pl.ds(pl.multiple_of(k * 8, 8), 8)
            hc = a_s[rows, :] * hprev + b_s[rows, :]
            yc_ref[rows, :] = hc
            return hc[7:8, :]

        hlast = lax.fori_loop(0, tt // 8, step, carry_h[0:1, :])
        carry_h[...] = jnp.broadcast_to(hlast, carry_h.shape)
        _, _, n_c = _rms(yc_ref[...], nc)
        mix_c = n_c * (cg * _sigmoid(cg))
        mixac_ref[...] = jnp.concatenate([mix_a, mix_c], axis=1).astype(mixac_ref.dtype)

    row_blk = lambda w: pl.BlockSpec((tt, w), lambda s, t: (rowmap(s, t), 0))
    return _pcall(
        body, name=name, grid=(T // S, nt), in_specs=in_specs,
        out_shape=(jax.ShapeDtypeStruct((T, 2 * W), BF16), jax.ShapeDtypeStruct((T, W), F32),
                   jax.ShapeDtypeStruct((T, W), F32), jax.ShapeDtypeStruct((T, W), F32)),
        out_specs=(row_blk(2 * W), row_blk(W), row_blk(W), row_blk(W)),
        scratch_shapes=[pltpu.VMEM((8, W), F32), pltpu.VMEM((8, W), F32), pltpu.VMEM((8, W), F32),
                        pltpu.VMEM((tt, W), F32), pltpu.VMEM((tt, W), F32)],
        compiler_params=_params(("arbitrary", "arbitrary")),
    )(*([h] * sum(counts)), caw, ccw, pv, wr, wi)


def _ac_bwd(h, cv, xc, yc, dmix, caw, ccw, pv, wr, wi, *, S, D, tt, name):
    T = h.shape[0]
    W = D // 4
    nt = S // tt
    rowmap = lambda s, t: s * nt + (nt - 1 - t)
    c_off = D + D // 2 + 2 * (D // 16) + D // 2
    offs = [0, W, 2 * W, 3 * W, c_off, c_off + W]
    in_specs, counts = [], []
    for off in offs:
        specs, _ = _colspecs(off, W, tt, rowmap)
        in_specs += specs
        counts.append(len(specs))
    row_blk = lambda w, cb=0: pl.BlockSpec((tt, w), lambda s, t: (rowmap(s, t), cb))
    in_specs += [row_blk(W), row_blk(W), row_blk(W)]
    in_specs.append(pl.BlockSpec((8, W), lambda s, t: (jnp.maximum(rowmap(s, t) * (tt // 8) - 1, 0), 0)))
    in_specs += [row_blk(W, 0), row_blk(W, 3)]
    full = lambda shape: pl.BlockSpec(shape, lambda s, t: (0,) * len(shape))
    in_specs += [full(caw.shape), full(ccw.shape), full(pv.shape), full(wr.shape), full(wi.shape)]

    def body(*refs):
        refs = list(refs)
        ab, ac, ax, ag, cx, cg = [_cat(_take(refs, n)) for n in counts]
        cv_ref, xc_ref, yc_ref, halo_ref, dma_ref, dmc_ref, caw_ref, ccw_ref, pv_ref, wr_ref, wi_ref = _take(refs, 11)
        dha_ref, dhc_ref, vec_ref, dwr_ref, dwi_ref = _take(refs, 5)
        carry_dcv, carry_dxc, carry_a, carry_g, c_s, b_s, g_s = refs
        s_id, t = pl.program_id(0), pl.program_id(1)

        @pl.when(t == 0)
        def _():
            for cr in (carry_dcv, carry_dxc, carry_a, carry_g):
                cr[...] = jnp.zeros_like(cr)

        @pl.when((t == 0) & (s_id == 0))
        def _():
            vec_ref[...] = jnp.zeros_like(vec_ref)
            dwr_ref[...] = jnp.zeros_like(dwr_ref)
            dwi_ref[...] = jnp.zeros_like(dwi_ref)

        def acc_row(k, val):
            vec_ref[k:k + 1, :] += jnp.sum(val, axis=0, keepdims=True)

        ccb, br, bi, lam, na, nc = [pv_ref[k:k + 1, :] for k in range(6)]
        cv = cv_ref[...]
        dmix_a = dma_ref[...]
        p = ac * ax
        xn, rstd, n_a = _rms(ab * cv, na)
        sg = _sigmoid(ag)
        dn = dmix_a * (ag * sg)
        dag = dmix_a * n_a * (sg * (1.0 + ag * (1.0 - sg)))
        acc_row(3, dn * xn)
        dya = _rms_bwd(dn, xn, rstd, na)
        dab = dya * cv
        dcv = dya * ab
        cd = carry_dcv[...]
        d1, d2 = _shift_up(dcv, 1, cd), _shift_up(dcv, 2, cd)
        dp = caw_ref[2:3, :] * dcv + caw_ref[1:2, :] * d1 + caw_ref[0:1, :] * d2
        acc_row(2, p * dcv)
        acc_row(1, p * d1)
        acc_row(0, p * d2)
        carry_dcv[...] = dcv[0:8]
        dha_ref[...] = jnp.concatenate([dab, dp * ax, dp * ac, dag], axis=1).astype(dha_ref.dtype)
        xc = xc_ref[...]
        yc = yc_ref[...]
        dmix_c = dmc_ref[...]
        xn, rstd, n_c = _rms(yc, nc)
        sg = _sigmoid(cg)
        dn = dmix_c * (cg * sg)
        dcg = dmix_c * n_c * (sg * (1.0 + cg * (1.0 - sg)))
        acc_row(12, dn * xn)
        dyc = _rms_bwd(dn, xn, rstd, nc)
        r, i, ls, a, sq = _gates(xc, wr_ref, wi_ref, br, bi, lam)
        halo = jnp.where(t == nt - 1, 0.0, halo_ref[...])
        hprev = _shift_down(yc, 1, halo)
        c_c, b_c = _chunk_scan_rev(_shift_up(a, 1, carry_a[...]), dyc)
        c_s[...] = c_c
        b_s[...] = b_c

        def step(k, gnext):
            rows = pl.ds(pl.multiple_of((tt // 8 - 1 - k) * 8, 8), 8)
            gc = b_s[rows, :] + c_s[rows, :] * gnext
            g_s[rows, :] = gc
            return gc[0:1, :]

        lax.fori_loop(0, tt // 8, step, carry_g[0:1, :])
        g = g_s[...]
        carry_g[...] = g[0:8]
        carry_a[...] = a[0:8]
        da = g * hprev
        ixc = i * xc
        dsq = g * ixc
        di = g * sq * xc
        dxc = g * sq * i
        dla = da * a - dsq * (a * a) / sq
        dr = dla * (RG_C * ls)
        acc_row(11, dla * (RG_C * r) * _sigmoid(-lam))
        dgr = dr * r * (1.0 - r)
        dgi = di * i * (1.0 - i)
        acc_row(9, dgr)
        acc_row(10, dgi)
        hw = W // N_RG_HEADS
        parts = []
        for hd in range(N_RG_HEADS):
            sl = slice(hd * hw, (hd + 1) * hw)
            dwr_ref[hd] += _dot(xc[:, sl], dgr[:, sl], TN)
            dwi_ref[hd] += _dot(xc[:, sl], dgi[:, sl], TN)
            parts.append(_dot(dgr[:, sl], wr_ref[hd], NT) + _dot(dgi[:, sl], wi_ref[hd], NT))
        dxc = dxc + jnp.concatenate(parts, axis=1)
        ce = carry_dxc[...]
        e1, e2, e3 = _shift_up(dxc, 1, ce), _shift_up(dxc, 2, ce), _shift_up(dxc, 3, ce)
        dcx = ccw_ref[3:4, :] * dxc + ccw_ref[2:3, :] * e1 + ccw_ref[1:2, :] * e2 + ccw_ref[0:1, :] * e3
        acc_row(7, cx * dxc)
        acc_row(6, cx * e1)
        acc_row(5, cx * e2)
        acc_row(4, cx * e3)
        acc_row(8, dxc)
        carry_dxc[...] = dxc[0:8]
        dhc_ref[...] = jnp.concatenate([dcx, dcg], axis=1).astype(dhc_ref.dtype)

    const = lambda shape: pl.BlockSpec(shape, lambda s, t: (0,) * len(shape))
    return _pcall(
        body, name=name, grid=(T // S, nt), in_specs=in_specs,
        out_shape=(jax.ShapeDtypeStruct((T, 4 * W), BF16), jax.ShapeDtypeStruct((T, 2 * W), BF16),
                   jax.ShapeDtypeStruct((16, W), F32), jax.ShapeDtypeStruct(wr.shape, F32),
                   jax.ShapeDtypeStruct(wi.shape, F32)),
        out_specs=(row_blk(4 * W), row_blk(2 * W), const((16, W)), const(wr.shape), const(wi.shape)),
        scratch_shapes=[pltpu.VMEM((8, W), F32)] * 4 + [pltpu.VMEM((tt, W), F32)] * 3,
        compiler_params=_params(("arbitrary", "arbitrary")),
    )(*([h] * sum(counts)), cv, xc, yc, yc, dmix, dmix, caw, ccw, pv, wr, wi)


def _lo_mask():
    return lax.broadcasted_iota(jnp.int32, (1, 2 * HEAD_DIM), 1) < HEAD_DIM


def _dup(blk, odd, lo):
    rot = pltpu.roll(blk, HEAD_DIM, 1)
    return jnp.where(lo, rot, blk) if odd else jnp.where(lo, blk, rot)


def _stack_heads(x, hh, lo, masked):
    parts = []
    for g in range(KV_GROUP):
        jq = hh * KV_GROUP + g
        pb = x[:, (jq // 2) * 128:(jq // 2 + 1) * 128]
        if masked:
            pb = jnp.where(lo if jq % 2 == 0 else jnp.logical_not(lo), pb, 0.0)
        parts.append(pb)
    return jnp.concatenate(parts, axis=0)


def _unstack_pairs(st, lo):
    return [jnp.where(lo, st[(2 * pi) * BLOCK:(2 * pi + 1) * BLOCK], st[(2 * pi + 1) * BLOCK:(2 * pi + 2) * BLOCK])
            for pi in range(KV_GROUP // 2)]


def _window(ref, n):
    prev = ref[pl.ds(pl.multiple_of(jnp.maximum(n - 1, 0) * BLOCK, BLOCK), BLOCK), :]
    cur = ref[pl.ds(pl.multiple_of(n * BLOCK, BLOCK), BLOCK), :]
    return jnp.concatenate([prev, cur], axis=0)


def _valid_mask(n):
    rows = KV_GROUP * BLOCK
    qi = lax.broadcasted_iota(jnp.int32, (rows, 2 * BLOCK), 0) & (BLOCK - 1)
    kj = lax.broadcasted_iota(jnp.int32, (rows, 2 * BLOCK), 1)
    dist = qi + BLOCK - kj
    return (dist >= 0) & (dist < BLOCK) & ((n > 0) | (kj >= BLOCK))


def _sink_col(sinks_ref, layer, hh):
    return jnp.concatenate([jnp.full((BLOCK, 1), sinks_ref[layer, hh * KV_GROUP + g], F32) for g in range(KV_GROUP)],
                           axis=0)


def _softmax(qs, kdup, valid, sink):
    s = _dot(qs, kdup, NT) * (HEAD_DIM ** -0.5)
    s = jnp.where(valid, s, NEG_INF)
    m = jnp.maximum(jnp.max(s, axis=-1, keepdims=True), sink)
    e = jnp.exp(s - m)
    es = jnp.exp(sink - m)
    den = jnp.sum(e, axis=-1, keepdims=True) + es
    return e / den, es / den


def _attn_fwd(h, sinks, layer, *, S, D, name):
    T = h.shape[0]
    WB, KVW = D // 2, D // 16
    nb = S // BLOCK
    n_kv = KVW // HEAD_DIM

    def body(q_ref, k_ref, v_ref, sinks_ref, o_ref):
        n = pl.program_id(1)
        lo = _lo_mask()
        q = q_ref[...]
        kk, vv = _window(k_ref, n), _window(v_ref, n)
        valid = _valid_mask(n)
        blocks = []
        for hh in range(n_kv):
            cb = slice((hh // 2) * 128, (hh // 2 + 1) * 128)
            kdup, vdup = _dup(kk[:, cb], hh % 2, lo), _dup(vv[:, cb], hh % 2, lo)
            p, _ = _softmax(_stack_heads(q, hh, lo, True), kdup, valid, _sink_col(sinks_ref, layer, hh))
            blocks += _unstack_pairs(_dot(p, vdup, NN), lo)
        o_ref[...] = jnp.concatenate(blocks, axis=1)

    return _pcall(
        body, name=name, grid=(T // S, nb),
        in_specs=[pl.BlockSpec((BLOCK, WB), lambda s, n: (s * nb + n, D // WB)),
                  pl.BlockSpec((S, KVW), lambda s, n: (s, (D + WB) // KVW)),
                  pl.BlockSpec((S, KVW), lambda s, n: (s, (D + WB) // KVW + 1)),
                  pl.BlockSpec(memory_space=pltpu.SMEM)],
        out_shape=jax.ShapeDtypeStruct((T, WB), F32),
        out_specs=pl.BlockSpec((BLOCK, WB), lambda s, n: (s * nb + n, 0)),
        compiler_params=_params(("arbitrary", "arbitrary")),
    )(h, h, h, sinks)


def _attn_bwd(h, yb, dyb, sinks, layer, *, S, D, name):
    T = h.shape[0]
    WB, KVW = D // 2, D // 16
    nb = S // BLOCK
    n_kv = KVW // HEAD_DIM

    def body(q_ref, k_ref, v_ref, o_ref, do_ref, sinks_ref, dq_ref, dk_ref, dv_ref, dsink_ref, dk_acc, dv_acc):
        s_id, n = pl.program_id(0), pl.program_id(1)
        lo = _lo_mask()

        @pl.when(n == 0)
        def _():
            dk_acc[...] = jnp.zeros_like(dk_acc)
            dv_acc[...] = jnp.zeros_like(dv_acc)

        @pl.when((n == 0) & (s_id == 0))
        def _():
            dsink_ref[...] = jnp.zeros_like(dsink_ref)

        q, o, do = q_ref[...], o_ref[...], do_ref[...]
        kk, vv = _window(k_ref, n), _window(v_ref, n)
        valid = _valid_mask(n)
        lane = lax.broadcasted_iota(jnp.int32, dsink_ref.shape, 1)
        dq_blocks, dk_heads, dv_heads = [], [], []
        dsink = jnp.zeros(dsink_ref.shape, F32)
        for hh in range(n_kv):
            cb = slice((hh // 2) * 128, (hh // 2 + 1) * 128)
            kdup, vdup = _dup(kk[:, cb], hh % 2, lo), _dup(vv[:, cb], hh % 2, lo)
            qs = _stack_heads(q, hh, lo, True)
            dos = _stack_heads(do, hh, lo, True)
            delta = jnp.sum(dos * _stack_heads(o, hh, lo, False), axis=-1, keepdims=True)
            p, psink = _softmax(qs, kdup, valid, _sink_col(sinks_ref, layer, hh))
            dvr = _dot(p, dos, TN)
            dv_heads.append(dvr + pltpu.roll(dvr, HEAD_DIM, 1))
            ds = p * (_dot(dos, vdup, NT) - delta) * (HEAD_DIM ** -0.5)
            dq_blocks += _unstack_pairs(_dot(ds, kdup, NN), lo)
            dkr = _dot(ds, qs, TN)
            dk_heads.append(dkr + pltpu.roll(dkr, HEAD_DIM, 1))
            dsk = -psink * delta
            for g in range(KV_GROUP):
                tot = jnp.sum(dsk[g * BLOCK:(g + 1) * BLOCK], axis=0, keepdims=True)
                dsink = dsink + jnp.where(lane == hh * KV_GROUP + g, tot, 0.0)
        dsink_ref[...] += dsink
        dq_ref[...] = jnp.concatenate(dq_blocks, axis=1).astype(dq_ref.dtype)
        pair = lambda hs: jnp.concatenate([jnp.where(lo, hs[2 * m], hs[2 * m + 1]) for m in range(n_kv // 2)], axis=1)
        dkk, dvv = pair(dk_heads), pair(dv_heads)
        prev = pl.ds(pl.multiple_of(jnp.maximum(n - 1, 0) * BLOCK, BLOCK), BLOCK)
        cur = pl.ds(pl.multiple_of(n * BLOCK, BLOCK), BLOCK)
        dk_acc[prev, :] += dkk[:BLOCK]
        dk_acc[cur, :] += dkk[BLOCK:]
        dv_acc[prev, :] += dvv[:BLOCK]
        dv_acc[cur, :] += dvv[BLOCK:]

        @pl.when(n == nb - 1)
        def _():
            dk_ref[...] = dk_acc[...].astype(dk_ref.dtype)
            dv_ref[...] = dv_acc[...].astype(dv_ref.dtype)

    blk = lambda cb=0: pl.BlockSpec((BLOCK, WB), lambda s, n: (s * nb + n, cb))
    seq = lambda cb=0: pl.BlockSpec((S, KVW), lambda s, n: (s, cb))
    return _pcall(
        body, name=name, grid=(T // S, nb),
        in_specs=[blk(D // WB), seq((D + WB) // KVW), seq((D + WB) // KVW + 1), blk(), blk(),
                  pl.BlockSpec(memory_space=pltpu.SMEM)],
        out_shape=(jax.ShapeDtypeStruct((T, WB), BF16), jax.ShapeDtypeStruct((T, KVW), BF16),
                   jax.ShapeDtypeStruct((T, KVW), BF16), jax.ShapeDtypeStruct((8, 128), F32)),
        out_specs=(blk(), seq(), seq(), pl.BlockSpec((8, 128), lambda s, n: (0, 0))),
        scratch_shapes=[pltpu.VMEM((S, KVW), F32), pltpu.VMEM((S, KVW), F32)],
        compiler_params=_params(("arbitrary", "arbitrary")),
    )(h, h, h, yb, dyb, sinks)


def _bg_specs(D, tm):
    return _colspecs(D + D // 2 + 2 * (D // 16), D // 2, tm, lambda i: i)


def _mixb_fwd(yb, h, nb_g, *, D, tm, name):
    T, WB = yb.shape
    bg_specs, _ = _bg_specs(D, tm)

    def body(*refs):
        refs = list(refs)
        yb_ref = refs.pop(0)
        bg = _cat(_take(refs, len(bg_specs)))
        g_ref, o_ref = refs
        _, _, nrm = _rms(yb_ref[...], g_ref[...])
        o_ref[...] = (nrm * (bg * _sigmoid(bg))).astype(o_ref.dtype)

    row = pl.BlockSpec((tm, WB), lambda i: (i, 0))
    return _pcall(body, name=name, grid=(T // tm,),
                  in_specs=[row] + bg_specs + [pl.BlockSpec((1, WB), lambda i: (0, 0))],
                  out_shape=jax.ShapeDtypeStruct((T, WB), BF16), out_specs=row,
                  compiler_params=_params(("arbitrary",)))(yb, *([h] * len(bg_specs)), nb_g)


def _mixb_bwd(yb, h, dmix, nb_g, *, D, tm, name):
    T, WB = yb.shape
    W = D // 4
    bg_specs, _ = _bg_specs(D, tm)
    dm_specs, _ = _colspecs(W, WB, tm, lambda i: i)

    def body(*refs):
        refs = list(refs)
        yb_ref = refs.pop(0)
        bg = _cat(_take(refs, len(bg_specs)))
        dmix_b = _cat(_take(refs, len(dm_specs)))
        g_ref, dyb_ref, dbg_ref, dg_ref = refs

        @pl.when(pl.program_id(0) == 0)
        def _():
            dg_ref[...] = jnp.zeros_like(dg_ref)

        gamma = g_ref[...]
        xn, rstd, nrm = _rms(yb_ref[...], gamma)
        sg = _sigmoid(bg)
        dn = dmix_b * (bg * sg)
        dbg_ref[...] = (dmix_b * nrm * (sg * (1.0 + bg * (1.0 - sg)))).astype(dbg_ref.dtype)
        dg_ref[0:1, :] += jnp.sum(dn * xn, axis=0, keepdims=True)
        dyb_ref[...] = _rms_bwd(dn, xn, rstd, gamma)

    row = pl.BlockSpec((tm, WB), lambda i: (i, 0))
    return _pcall(body, name=name, grid=(T // tm,),
                  in_specs=[row] + bg_specs + dm_specs + [pl.BlockSpec((1, WB), lambda i: (0, 0))],
                  out_shape=(jax.ShapeDtypeStruct((T, WB), F32), jax.ShapeDtypeStruct((T, WB), BF16),
                             jax.ShapeDtypeStruct((8, WB), F32)),
                  out_specs=(row, row, pl.BlockSpec((8, WB), lambda i: (0, 0))),
                  compiler_params=_params(("arbitrary",)))(yb, *([h] * len(bg_specs)), *([dmix] * len(dm_specs)), nb_g)


def _concat_cols(parts, *, tm, name):
    parts = [p if isinstance(p, tuple) else (p, 0, p.shape[1]) for p in parts]
    T = parts[0][0].shape[0]
    total = sum(w for _, _, w in parts)

    def body(*refs):
        refs[-1][...] = jnp.concatenate([r[...] for r in refs[:-1]], axis=1)

    return _pcall(body, name=name, grid=(T // tm,),
                  in_specs=[pl.BlockSpec((tm, w), functools.partial(lambda cb, i: (i, cb), cb)) for _, cb, w in parts],
                  out_shape=jax.ShapeDtypeStruct((T, total), parts[0][0].dtype),
                  out_specs=pl.BlockSpec((tm, total), lambda i: (i, 0)),
                  compiler_params=_params(("parallel",)))(*[a for a, _, _ in parts])


def _ln_fwd(z, g, b, *, tm, name):
    T, D = z.shape

    def body(z_ref, g_ref, b_ref, y_ref, yb_ref):
        zv = z_ref[...]
        mu = jnp.mean(zv, axis=-1, keepdims=True)
        zc = zv - mu
        var = jnp.mean(zc * zc, axis=-1, keepdims=True)
        y = zc * lax.rsqrt(var + LN_EPS) * g_ref[...] + b_ref[...]
        y_ref[...] = y
        yb_ref[...] = y.astype(BF16)

    row = pl.BlockSpec((tm, D), lambda i: (i, 0))
    vec = pl.BlockSpec((1, D), lambda i: (0, 0))
    return _pcall(body, name=name, grid=(T // tm,), in_specs=[row, vec, vec],
                  out_shape=(jax.ShapeDtypeStruct((T, D), F32), jax.ShapeDtypeStruct((T, D), BF16)),
                  out_specs=(row, row), compiler_params=_params(("parallel",)))(z, g, b)


def _ln_bwd(z, dy, g, *, tm, name):
    T, D = z.shape

    def body(z_ref, dy_ref, g_ref, dz_ref, dzb_ref, dgb_ref):
        @pl.when(pl.program_id(0) == 0)
        def _():
            dgb_ref[...] = jnp.zeros_like(dgb_ref)

        zv, dyv = z_ref[...], dy_ref[...]
        mu = jnp.mean(zv, axis=-1, keepdims=True)
        zc = zv - mu
        rstd = lax.rsqrt(jnp.mean(zc * zc, axis=-1, keepdims=True) + LN_EPS)
        xh = zc * rstd
        dxh = dyv * g_ref[...]
        dz = rstd * (dxh - jnp.mean(dxh, axis=-1, keepdims=True) - xh * jnp.mean(dxh * xh, axis=-1, keepdims=True))
        dz_ref[...] = dz
        dzb_ref[...] = dz.astype(BF16)
        dgb_ref[0:1, :] += jnp.sum(dyv * xh, axis=0, keepdims=True)
        dgb_ref[1:2, :] += jnp.sum(dyv, axis=0, keepdims=True)

    row = pl.BlockSpec((tm, D), lambda i: (i, 0))
    return _pcall(body, name=name, grid=(T // tm,), in_specs=[row, row, pl.BlockSpec((1, D), lambda i: (0, 0))],
                  out_shape=(jax.ShapeDtypeStruct((T, D), F32), jax.ShapeDtypeStruct((T, D), BF16),
                             jax.ShapeDtypeStruct((8, D), F32)),
                  out_specs=(row, row, pl.BlockSpec((8, D), lambda i: (0, 0))),
                  compiler_params=_params(("arbitrary",)))(z, dy, g)


def _loss_head(y, target, *, tm, name):
    T, D = y.shape

    def body(y_ref, t_ref, dy_ref, loss_ref):
        @pl.when(pl.program_id(0) == 0)
        def _():
            loss_ref[...] = jnp.zeros_like(loss_ref)

        err = y_ref[...] - t_ref[...]
        dy_ref[...] = err / D
        loss_ref[...] += 0.5 * jnp.sum(jnp.mean(err * err, axis=-1, keepdims=True), axis=0, keepdims=True)

    row = pl.BlockSpec((tm, D), lambda i: (i, 0))
    return _pcall(body, name=name, grid=(T // tm,), in_specs=[row, row],
                  out_shape=(jax.ShapeDtypeStruct((T, D), F32), jax.ShapeDtypeStruct((1, 1), F32)),
                  out_specs=(row, pl.BlockSpec((1, 1), lambda i: (0, 0))),
                  compiler_params=_params(("arbitrary",)))(y, target)


def _cast_bf16(w, layer, *, name):
    _, R, C = w.shape
    tr = _tile(R, 512, 8)

    def body(w_ref, o_ref):
        o_ref[...] = w_ref[...].astype(BF16)

    return _pcall(body, name=name, grid=(R // tr,), in_specs=[pl.BlockSpec((None, tr, C), lambda i: (layer, i, 0))],
                  out_shape=jax.ShapeDtypeStruct((R, C), BF16), out_specs=pl.BlockSpec((tr, C), lambda i: (i, 0)),
                  compiler_params=_params(("parallel",)))(w)


def _adamw_layer(g, w, m, v, layer, bufs, *, name):
    L, R, C = w.shape
    tr = _tile(R, max(8, (1 << 19) // C // 8 * 8), 8)
    if bufs is None:
        bufs = [lax.empty((L, R, C), F32) for _ in range(4)]

    def body(g_ref, w_ref, m_ref, v_ref, b0, b1, b2, b3, go_ref, d_ref, nm_ref, nv_ref):
        gv = g_ref[...]
        nm = ADAM_B1 * m_ref[...] + (1.0 - ADAM_B1) * gv
        nv = ADAM_B2 * v_ref[...] + (1.0 - ADAM_B2) * (gv * gv)
        m_hat = nm / (1.0 - ADAM_B1 ** ADAM_STEP)
        v_hat = nv / (1.0 - ADAM_B2 ** ADAM_STEP)
        go_ref[...] = gv
        d_ref[...] = -ADAM_LR * (m_hat / (jnp.sqrt(v_hat) + ADAM_EPS) + ADAM_WD * w_ref[...])
        nm_ref[...] = nm
        nv_ref[...] = nv

    lay = pl.BlockSpec((None, tr, C), lambda i: (layer, i, 0))
    shp = jax.ShapeDtypeStruct((L, R, C), F32)
    return list(_pcall(body, name=name, grid=(R // tr,),
                       in_specs=[pl.BlockSpec((tr, C), lambda i: (i, 0)), lay, lay, lay] + [ANY] * 4,
                       out_shape=(shp,) * 4, out_specs=(lay,) * 4, input_output_aliases={4 + k: k for k in range(4)},
                       compiler_params=_params(("parallel",)))(g, w, m, v, *bufs))


def _adamw(g, w, m, v, *, name):
    R, C = g.shape
    tr = _tile(R, max(8, (1 << 19) // C // 8 * 8), 8)

    def body(g_ref, w_ref, m_ref, v_ref, d_ref, nm_ref, nv_ref):
        gv = g_ref[...]
        nm = ADAM_B1 * m_ref[...] + (1.0 - ADAM_B1) * gv
        nv = ADAM_B2 * v_ref[...] + (1.0 - ADAM_B2) * (gv * gv)
        m_hat = nm / (1.0 - ADAM_B1 ** ADAM_STEP)
        v_hat = nv / (1.0 - ADAM_B2 ** ADAM_STEP)
        d_ref[...] = -ADAM_LR * (m_hat / (jnp.sqrt(v_hat) + ADAM_EPS) + ADAM_WD * w_ref[...])
        nm_ref[...] = nm
        nv_ref[...] = nv

    blk = pl.BlockSpec((tr, C), lambda i: (i, 0))
    shp = jax.ShapeDtypeStruct((R, C), F32)
    return _pcall(body, name=name, grid=(R // tr,), in_specs=[blk] * 4, out_shape=(shp, shp, shp),
                  out_specs=(blk, blk, blk), compiler_params=_params(("parallel",)))(g, w, m, v)


def _my_core():
    return lax.axis_index("c")


def _my_chip():
    return 2 * lax.axis_index("x") + lax.axis_index("y")


def _pair_sum(mine, theirs, *, half_axis, name):
    R, C = theirs.shape
    tr, tc = _tile(R, 512, 16), _tile(C, 2048)
    nrb, ncb = R // tr, C // tc

    def body(a_ref, b_ref, o_ref):
        o_ref[...] = (a_ref[...].astype(F32) + b_ref[...].astype(F32)).astype(BF16)

    if half_axis == 0:
        a_idx = lambda i, j: (_my_core() * nrb + i, j)
    else:
        a_idx = lambda i, j: (i, _my_core() * ncb + j)
    blk = pl.BlockSpec((tr, tc), lambda i, j: (i, j))
    return _pcall(body, name=name, grid=(nrb, ncb), in_specs=[pl.BlockSpec((tr, tc), a_idx), blk], out_specs=blk,
                  out_shape=jax.ShapeDtypeStruct(theirs.shape, BF16),
                  compiler_params=_params(("parallel", "parallel")))(mine, theirs)


def _final_sum(own, got, *, own_axis, out_shape, out_axis, name):
    _, R, C = got.shape
    tr, tc = _tile(R, 512, 16), _tile(C, 1024)
    nrb, ncb = R // tr, C // tc

    def body(a_ref, q_ref, o_ref):
        o_ref[...] = ((a_ref[...].astype(F32) + q_ref[0].astype(F32)) + q_ref[1].astype(F32)) + q_ref[2].astype(F32)

    if own_axis == 1:
        a_idx = lambda i, j: (i, _my_chip() * ncb + j)
    else:
        a_idx = lambda i, j: (_my_chip() * nrb + i, j)
    if out_axis == 0:
        o_idx = lambda i, j: (_my_core() * nrb + i, j)
    else:
        o_idx = lambda i, j: (i, _my_core() * ncb + j)
    return _pcall(body, name=name, grid=(nrb, ncb),
                  in_specs=[pl.BlockSpec((tr, tc), a_idx), pl.BlockSpec((3, tr, tc), lambda i, j: (0, i, j))],
                  out_specs=pl.BlockSpec((tr, tc), o_idx), out_shape=jax.ShapeDtypeStruct(out_shape, F32),
                  compiler_params=_params(("parallel", "parallel")))(own, got)


def _sum_devices(gathered, *, name):
    _, R, C = gathered.shape
    tr = _tile(R, 280, 8)

    def body(g_ref, o_ref):
        acc = g_ref[0]
        for d in range(1, N_DEV):
            acc = acc + g_ref[d]
        o_ref[...] = acc

    return _pcall(body, name=name, grid=(R // tr,), in_specs=[pl.BlockSpec((N_DEV, tr, C), lambda i: (0, i, 0))],
                  out_shape=jax.ShapeDtypeStruct((R, C), F32), out_specs=pl.BlockSpec((tr, C), lambda i: (i, 0)),
                  compiler_params=_params(("parallel",)))(gathered)


def _position():
    x, y, c = lax.axis_index("x"), lax.axis_index("y"), lax.axis_index("c")
    chips = [(1 - x, y), (x, 1 - y), (1 - x, 1 - y)]
    return x, y, c, chips


def _remote(src, dst, send_sems, recv_sems, k, to):
    return pltpu.make_async_remote_copy(src_ref=src, dst_ref=dst, send_sem=send_sems.at[k], recv_sem=recv_sems.at[k],
                                        device_id=to, device_id_type=MESH)


def _r(ref, start, n):
    return ref.at[pl.ds(pl.multiple_of(start, 16), n), :]


def _c(ref, start, n):
    return ref.at[:, pl.ds(pl.multiple_of(start, 128), n)]


class _part:
    def __init__(self, ins, outs, plan, n, n_local=0, aliased=0):
        self.ins, self.outs, self.plan, self.n, self.n_local, self.aliased = ins, outs, plan, n, n_local, aliased


def _comm_scratch(parts):
    if not parts:
        return []
    n, nl = sum(p.n for p in parts), sum(p.n_local for p in parts)
    return [pltpu.SemaphoreType.DMA((n,)), pltpu.SemaphoreType.DMA((n,)), pltpu.SemaphoreType.DMA((max(nl, 1),))]


def _comm_aliases(parts, in_base, out_base):
    aliases, ii, oi = {}, in_base, out_base
    for p in parts:
        aliases.update({ii + k: oi + k for k in range(p.aliased)})
        ii += len(p.ins)
        oi += len(p.outs)
    return aliases


def _comm_run(parts, phase, in_refs, out_refs, send_sems, recv_sems, local_sems):
    pos = _position()
    me = pos[:3]
    ii = oi = si = li = 0
    for p in parts:
        sends, recvs, locs = p.plan(in_refs[ii:ii + len(p.ins)], out_refs[oi:oi + len(p.outs)], pos)
        assert len(sends) == len(recvs) == p.n and len(locs) == p.n_local
        if phase == "start":
            for k, (src, dst) in enumerate(locs):
                pltpu.make_async_copy(src, dst, local_sems.at[li + k]).start()
            for k, (src, dst, to) in enumerate(sends):
                _remote(src, dst, send_sems, recv_sems, si + k, to).start()
        else:
            for k, dst in enumerate(recvs):
                _remote(dst, dst, send_sems, recv_sems, si + k, me).wait_recv()
            for k, (src, dst, to) in enumerate(sends):
                _remote(src, dst, send_sems, recv_sems, si + k, to).wait_send()
            for k, (src, dst) in enumerate(locs):
                pltpu.make_async_copy(src, dst, local_sems.at[li + k]).wait()
        ii, oi, si, li = ii + len(p.ins), oi + len(p.outs), si + p.n, li + p.n_local


def _comm_call(parts, *, name):
    n_in = sum(len(p.ins) for p in parts)
    n_out = sum(len(p.outs) for p in parts)

    def body(*refs):
        refs = list(refs)
        cin, cout = _take(refs, n_in), _take(refs, n_out)
        _comm_run(parts, "start", cin, cout, *refs)
        _comm_run(parts, "finish", cin, cout, *refs)

    return list(_pcall(body, name=name, in_specs=[ANY] * n_in, out_specs=[ANY] * n_out,
                       out_shape=[s for p in parts for s in p.outs], scratch_shapes=_comm_scratch(parts),
                       input_output_aliases=_comm_aliases(parts, 0, 0))(*[a for p in parts for a in p.ins]))


def _slab(wg, kind, chip, half):
    if kind == "in":
        d, ns = wg.shape[0], wg.shape[1] // N_CHIPS
        return _c(_r(wg, half * (d // 2), d // 2), chip * ns, ns)
    rs = wg.shape[0] // N_CHIPS
    return _r(wg, chip * rs + half * (rs // 2), rs // 2)


def _gather_ici(ws, kind):
    rows, cols = ws.shape
    full = (rows, N_CHIPS * cols) if kind == "in" else (N_CHIPS * rows, cols)

    def plan(ins, outs, pos):
        x, y, c, chips = pos
        j = 2 * x + y
        (ws_ref,), (wg, land) = ins, outs
        mine = _r(ws_ref, c * (rows // 2), rows // 2)
        sends = [(mine, land.at[kk], (*chip, c)) for kk, chip in enumerate(chips)]
        own = _c(wg, j * cols, cols) if kind == "in" else _r(wg, j * rows, rows)
        return sends, [land.at[kk] for kk in range(3)], [(ws_ref, own)]

    return _part([ws], [jax.ShapeDtypeStruct(full, ws.dtype), jax.ShapeDtypeStruct((3, rows // 2, cols), ws.dtype)],
                 plan, 3, 1)


def _gather_d2d(wg, land, kind):
    def plan(ins, outs, pos):
        x, y, c, chips = pos
        (_, land_ref), (ref,) = ins, outs
        places = [_slab(ref, kind, 2 * px + py, c) for px, py in chips]
        sends = [(land_ref.at[kk], places[kk], (x, y, 1 - c)) for kk in range(3)]
        recvs = [_slab(ref, kind, 2 * px + py, 1 - c) for px, py in chips]
        return sends, recvs, [(land_ref.at[kk], places[kk]) for kk in range(3)]

    return _part([wg, land], [jax.ShapeDtypeStruct(wg.shape, wg.dtype)], plan, 3, 3, aliased=1)


def _pair_send(gw, kind):
    rows, cols = gw.shape
    half = (rows // 2, cols) if kind == "in" else (rows, cols // 2)

    def plan(ins, outs, pos):
        x, y, c, _ = pos
        (src,), (rb,) = ins, outs
        theirs = _r(src, (1 - c) * half[0], half[0]) if kind == "in" else _c(src, (1 - c) * half[1], half[1])
        return [(theirs, rb, (x, y, 1 - c))], [rb], []

    return _part([gw], [jax.ShapeDtypeStruct(half, gw.dtype)], plan, 1)


def _chip_send(p, kind):
    rows, cols = p.shape
    shard = (rows, cols // N_CHIPS) if kind == "in" else (rows // N_CHIPS, cols)

    def plan(ins, outs, pos):
        x, y, c, chips = pos
        (src,), (q,) = ins, outs
        piece = lambda jk: _c(src, jk * shard[1], shard[1]) if kind == "in" else _r(src, jk * shard[0], shard[0])
        sends = [(piece(2 * px + py), q.at[kk], (px, py, c)) for kk, (px, py) in enumerate(chips)]
        return sends, [q.at[kk] for kk in range(3)], []

    return _part([p], [jax.ShapeDtypeStruct((3,) + shard, p.dtype)], plan, 3)


def _sibling_send(g, kind):
    rows, cols = g.shape

    def plan(ins, outs, pos):
        x, y, c, _ = pos
        (ref,) = outs
        half = (lambda h: _r(ref, h * (rows // 2), rows // 2)) if kind == "in" else (
            lambda h: _c(ref, h * (cols // 2), cols // 2))
        return [(half(c), half(c), (x, y, 1 - c))], [half(1 - c)], []

    return _part([g], [jax.ShapeDtypeStruct(g.shape, g.dtype)], plan, 1, aliased=1)


def _small_ici(block):
    def plan(ins, outs, pos):
        x, y, c, chips = pos
        (src,), (out,) = ins, outs
        mine = out.at[4 * x + 2 * y + c]
        peers = [(x, y, 1 - c)] + [(px, py, c) for px, py in chips]
        return [(src, mine, p) for p in peers], [out.at[4 * px + 2 * py + pc] for px, py, pc in peers], [(src, mine)]

    return _part([block], [jax.ShapeDtypeStruct((N_DEV,) + block.shape, block.dtype)], plan, 4, 1)


def _small_d2d(gathered):
    def plan(ins, outs, pos):
        x, y, c, chips = pos
        (out,) = outs
        sends = [(out.at[4 * px + 2 * py + c], out.at[4 * px + 2 * py + c], (x, y, 1 - c)) for px, py in chips]
        return sends, [out.at[4 * px + 2 * py + (1 - c)] for px, py in chips], []

    return _part([gathered], [jax.ShapeDtypeStruct(gathered.shape, gathered.dtype)], plan, 3, aliased=1)


def _gather_small(block, *, name):
    R, C = block.shape

    def body(x_ref, out_ref, send_sems, recv_sems, local_sem):
        x, y, c, chips = _position()
        me, sib = (x, y, c), (x, y, 1 - c)
        slot = lambda px, py, pc: out_ref.at[4 * px + 2 * py + pc]
        mine = pltpu.make_async_copy(x_ref, slot(*me), local_sem)
        mine.start()
        first = [_remote(x_ref, slot(*me), send_sems, recv_sems, 0, sib)]
        first += [_remote(x_ref, slot(*me), send_sems, recv_sems, 1 + kk, (*chip, c)) for kk, chip in enumerate(chips)]
        for cp in first:
            cp.start()
        passed = []
        for kk, chip in enumerate(chips):
            _remote(slot(*chip, c), slot(*chip, c), send_sems, recv_sems, 1 + kk, sib).wait_recv()
            passed.append(_remote(slot(*chip, c), slot(*chip, c), send_sems, recv_sems, 4 + kk, sib))
            passed[-1].start()
        _remote(slot(*sib), slot(*sib), send_sems, recv_sems, 0, sib).wait_recv()
        for kk, chip in enumerate(chips):
            _remote(slot(*chip, 1 - c), slot(*chip, 1 - c), send_sems, recv_sems, 4 + kk, sib).wait_recv()
        for cp in first + passed:
            cp.wait_send()
        mine.wait()

    return _pcall(
        body, name=name, in_specs=[ANY], out_specs=ANY, out_shape=jax.ShapeDtypeStruct((N_DEV, R, C), F32),
        scratch_shapes=[pltpu.SemaphoreType.DMA((7,)), pltpu.SemaphoreType.DMA((7,)), pltpu.SemaphoreType.DMA],
    )(block)


_SMALL = ["gate_r_w", "gate_i_w", "conv_a_w", "conv_c_w", "sinks", "conv_c_b", "gate_r_b", "gate_i_b", "rg_lambda",
          "norm_a", "norm_b", "norm_c", "ln_g", "ln_b"]


def _pack_small(p):
    L = p["ln_g"].shape[0]
    rows = []
    for n in _SMALL:
        a = p[n]
        if n in ("gate_r_w", "gate_i_w", "norm_b", "ln_g", "ln_b"):
            a = a.reshape(L, -1, 1024)
        elif a.ndim == 2:
            a = a[:, None, :]
        if a.shape[-1] < 1024:
            a = jnp.pad(a, ((0, 0), (0, 0), (0, 1024 - a.shape[-1])))
        rows.append(a)
    out = jnp.concatenate(rows, axis=1)
    assert out.shape[1] == SMALL_ROWS
    return out.reshape(L * SMALL_ROWS, 1024)


def _unpack_small(flat, like):
    L = like["ln_g"].shape[0]
    a = flat.reshape(L, SMALL_ROWS, 1024)
    out, r = {}, 0
    for n in _SMALL:
        shp = like[n].shape
        nrows = max(1, math.prod(shp[1:]) // 1024) if n in ("gate_r_w", "gate_i_w", "norm_b", "ln_g", "ln_b") else (
            shp[1] if len(shp) == 3 else 1)
        blk = a[:, r:r + nrows, :]
        if n in ("gate_r_w", "gate_i_w", "norm_b", "ln_g", "ln_b"):
            out[n] = blk.reshape(shp)
        elif len(shp) == 3:
            out[n] = blk[:, :, :shp[2]]
        else:
            out[n] = blk[:, 0, :shp[1]]
        r += nrows
    return out


def kernel(x, w_in, conv_a_w, sinks, conv_c_w, conv_c_b, gate_r_w, gate_r_b, gate_i_w, gate_i_b, rg_lambda, norm_a, norm_b, norm_c, w_out, ln_g, ln_b, loss_target, m_w_in, m_conv_a_w, m_sinks, m_conv_c_w, m_conv_c_b, m_gate_r_w, m_gate_r_b, m_gate_i_w, m_gate_i_b, m_rg_lambda, m_norm_a, m_norm_b, m_norm_c, m_w_out, m_ln_g, m_ln_b, v_w_in, v_conv_a_w, v_sinks, v_conv_c_w, v_conv_c_b, v_gate_r_w, v_gate_r_b, v_gate_i_w, v_gate_i_b, v_rg_lambda, v_norm_a, v_norm_b, v_norm_c, v_w_out, v_ln_g, v_ln_b):
    names = ["w_in", "conv_a_w", "sinks", "conv_c_w", "conv_c_b", "gate_r_w", "gate_r_b", "gate_i_w", "gate_i_b",
             "rg_lambda", "norm_a", "norm_b", "norm_c", "w_out", "ln_g", "ln_b"]
    w = dict(zip(names, [w_in, conv_a_w, sinks, conv_c_w, conv_c_b, gate_r_w, gate_r_b, gate_i_w, gate_i_b, rg_lambda,
                         norm_a, norm_b, norm_c, w_out, ln_g, ln_b]))
    mom = dict(zip(names, [m_w_in, m_conv_a_w, m_sinks, m_conv_c_w, m_conv_c_b, m_gate_r_w, m_gate_r_b, m_gate_i_w,
                           m_gate_i_b, m_rg_lambda, m_norm_a, m_norm_b, m_norm_c, m_w_out, m_ln_g, m_ln_b]))
    vel = dict(zip(names, [v_w_in, v_conv_a_w, v_sinks, v_conv_c_w, v_conv_c_b, v_gate_r_w, v_gate_r_b, v_gate_i_w,
                           v_gate_i_b, v_rg_lambda, v_norm_a, v_norm_b, v_norm_c, v_w_out, v_ln_g, v_ln_b]))
    B, S, D = x.shape
    T = B * S
    L, _, NS = w_in.shape
    RS = w_out.shape[1]
    W = D // 4
    alpha = (2.0 * L) ** 0.25
    tt = _tile(S, 128, 8)
    tm_row = _tile(T, 256, 8)
    chip = _my_chip()

    ws_in = [_cast_bf16(w_in, l, name="cast_w_in") for l in range(L)]
    ws_out = [_cast_bf16(w_out, l, name="cast_w_out") for l in range(L)]
    wg_in, wg_out = [None] * L, [None] * L
    part_in, land_in, part_out, land_out = _comm_call([_gather_ici(ws_in[0], "in"), _gather_ici(ws_out[0], "out")],
                                                      name="gather0_ici")
    wg_in[0], wg_out[0] = _comm_call([_gather_d2d(part_in, land_in, "in"), _gather_d2d(part_out, land_out, "out")],
                                     name="gather0_d2d")
    conv_local = jnp.concatenate([conv_a_w, conv_c_w], axis=1).reshape(L * 7, W // N_CHIPS)
    conv_local = jnp.pad(conv_local, ((0, (-L * 7) % 8), (0, 0)))
    conv_all = _gather_small(conv_local, name="gather_conv")
    conv_full = jnp.concatenate([conv_all[2 * jj][:L * 7] for jj in range(N_CHIPS)], axis=1).reshape(L, 7, W)
    caw_full, ccw_full = conv_full[:, :3], conv_full[:, 3:]

    xf = x.reshape(T, D)
    xb = _cast_bf16(xf[None], 0, name="cast_x")
    saved = []
    for l in range(L):
        comm = ([_gather_ici(ws_in[l + 1], "in")] if l + 1 < L else []) + (
            [_gather_d2d(part_out, land_out, "out")] if l else [])
        res = _mm(xb, wg_in[l], mode="nn", out_dtype=F32, name="proj_in", tm=1024, tn=768, tk=4096, comm=comm)
        h = res if not comm else res.pop(0)
        if l + 1 < L:
            part_in, land_in = _take(res, 2)
        if l:
            wg_out[l] = res.pop(0)
        pv = jnp.stack([conv_c_b[l], gate_r_b[l], gate_i_b[l], rg_lambda[l], norm_a[l], norm_c[l]])
        mix_ac, cv, xc, yc = _ac_fwd(h, caw_full[l], ccw_full[l], pv, gate_r_w[l], gate_i_w[l], S=S, D=D, tt=tt,
                                     name="ac_fwd")
        yb = _attn_fwd(h, sinks, l, S=S, D=D, name="attn_fwd")
        mix_b = _mixb_fwd(yb, h, norm_b[l][None], D=D, tm=tm_row, name="mixb_fwd")
        mix = _concat_cols([(mix_ac, 0, W), mix_b, (mix_ac, 1, W)], tm=tm_row, name="concat_mix")
        comm = [_gather_ici(ws_out[l + 1], "out"), _gather_d2d(part_in, land_in, "in")] if l + 1 < L else []
        res = _mm(mix, wg_out[l], mode="nn", out_dtype=F32, name="proj_out", tn=512, tk=4096, add=xf, add_scale=alpha,
                  comm=comm)
        z = res if not comm else res.pop(0)
        if l + 1 < L:
            part_out, land_out, wg_in[l + 1] = res
        saved.append((xb, h, cv, xc, yc, yb, mix, z, pv))
        xf, xb = _ln_fwd(z, ln_g[l][None], ln_b[l][None], tm=tm_row, name="ln_fwd")
    dxn, loss_part = _loss_head(xf, loss_target.reshape(T, D), tm=tm_row, name="loss_head")
    loss = lax.psum(loss_part[0, 0], ("x", "y", "c"))

    def final_sums(p_in, q_in, p_out, q_out):
        return (_final_sum(p_in, q_in, own_axis=1, out_shape=(D, NS), out_axis=0, name="final_sum_in"),
                _final_sum(p_out, q_out, own_axis=0, out_shape=(RS, D), out_axis=1, name="final_sum_out"))

    bufs_in = bufs_out = None
    small_g = [None] * L
    p_in = p_out = None
    for l in reversed(range(L)):
        up, last = l + 1 < L, l == 0
        xb_l, h, cv, xc, yc, yb, mix, z, pv = saved[l]
        dz, dzb, dgb = _ln_bwd(z, dxn, ln_g[l][None], tm=tm_row, name="ln_bwd")
        res = _mm(dzb, wg_out[l], mode="nt", out_dtype=F32, name="d_mix", tk=4096,
                  comm=[_chip_send(p_out, "out")] if up else [])
        dmix, q_out = res if up else (res, None)
        gw_out = _mm(mix, dzb, mode="tn", out_dtype=BF16, name="d_w_out", tk=4096)
        dha, dhc, vec, dwr, dwi = _ac_bwd(h, cv, xc, yc, dmix, caw_full[l], ccw_full[l], pv, gate_r_w[l], gate_i_w[l],
                                          S=S, D=D, tt=tt, name="ac_bwd")
        dyb, dbg, dnb = _mixb_bwd(yb, h, dmix, norm_b[l][None], D=D, tm=tm_row, name="mixb_bwd")
        dq, dk, dv, dsk = _attn_bwd(h, yb, dyb, sinks, l, S=S, D=D, name="attn_bwd")
        dh = _concat_cols([dha, dq, dk, dv, dbg, dhc], tm=tm_row, name="concat_dh")
        small_g[l] = dict(gate_r_w=dwr, gate_i_w=dwi, conv_a_w=vec[0:3], conv_c_w=vec[4:8], sinks=dsk[0, :2 * D // 256],
                          conv_c_b=vec[8], gate_r_b=vec[9], gate_i_b=vec[10], rg_lambda=vec[11], norm_a=vec[3],
                          norm_b=dnb[0], norm_c=vec[12], ln_g=dgb[0], ln_b=dgb[1])
        comm = [_pair_send(gw_out, "out")] + ([_chip_send(p_in, "in")] if up else [])
        if last:
            comm.append(_small_ici(_pack_small({n: jnp.stack([small_g[k][n] for k in range(L)]) for n in _SMALL})))
        res = _mm(xb_l, dh, mode="tn", out_dtype=BF16, name="d_w_in", tm=1024, tn=768, tk=4096, comm=comm)
        gw_in, rb_out = _take(res, 2)
        p_out_l = _pair_sum(gw_out, rb_out, half_axis=1, name="pair_sum_out")
        d2d = [_pair_send(gw_in, "in")]
        if up:
            g_in_half, g_out_half = final_sums(p_in, res.pop(0), p_out, q_out)
            d2d += [_sibling_send(g_in_half, "in"), _sibling_send(g_out_half, "out")]
        if last:
            res = _comm_call(d2d + [_small_d2d(res.pop(0))], name="tail_d2d")
            small_all = res.pop()
        else:
            res = _mm(dh, wg_in[l], mode="nt", out_dtype=F32, name="d_x", tk=2688, add=dz, add_scale=alpha, comm=d2d)
            dxn = res.pop(0)
        p_in_l = _pair_sum(gw_in, res.pop(0), half_axis=0, name="pair_sum_in")
        if up:
            bufs_in = _adamw_layer(res[0], w_in, m_w_in, v_w_in, l + 1, bufs_in, name="adamw_w_in")
            bufs_out = _adamw_layer(res[1], w_out, m_w_out, v_w_out, l + 1, bufs_out, name="adamw_w_out")
        p_in, p_out = p_in_l, p_out_l
    dxn, q_in, q_out = _mm(dh, wg_in[0], mode="nt", out_dtype=F32, name="d_x", tk=2688, add=dz, add_scale=alpha,
                           comm=[_chip_send(p_in, "in"), _chip_send(p_out, "out")])
    grad_x = dxn.reshape(B, S, D)
    g_in_half, g_out_half = final_sums(p_in, q_in, p_out, q_out)
    g_in0, g_out0 = _comm_call([_sibling_send(g_in_half, "in"), _sibling_send(g_out_half, "out")], name="sibling0")
    big = {"w_in": _adamw_layer(g_in0, w_in, m_w_in, v_w_in, 0, bufs_in, name="adamw_w_in"),
           "w_out": _adamw_layer(g_out0, w_out, m_w_out, v_w_out, 0, bufs_out, name="adamw_w_out")}

    like = {n: w[n] for n in _SMALL}
    like_full = dict(like, conv_a_w=caw_full, conv_c_w=ccw_full)
    g_small = _unpack_small(_sum_devices(small_all, name="sum_small"), like_full)
    for n in ("conv_a_w", "conv_c_w"):
        g_small[n] = lax.dynamic_slice_in_dim(g_small[n], chip * (W // N_CHIPS), W // N_CHIPS, axis=2)

    d_s, m_s, v_s = _adamw(_pack_small(g_small), _pack_small(like), _pack_small({n: mom[n] for n in _SMALL}),
                           _pack_small({n: vel[n] for n in _SMALL}), name="adamw_small")
    grads = dict(g_small)
    delta, new_m, new_v = _unpack_small(d_s, like), _unpack_small(m_s, like), _unpack_small(v_s, like)
    for n in ("w_in", "w_out"):
        grads[n], delta[n], new_m[n], new_v[n] = big[n]

    return (loss, grad_x, *[grads[n] for n in names], *[delta[n] for n in names], *[new_m[n] for n in names],
            *[new_v[n] for n in names])
```

```python
import functools
import math

import jax
import jax.numpy as jnp
from jax import lax
from jax.experimental import pallas as pl
from jax.experimental.pallas import tpu as pltpu

F32 = jnp.float32
BF16 = jnp.bfloat16
_MXU_DTYPE = jnp.bfloat16

HEAD_DIM = 64
KV_GROUP = 8
BLOCK = 128
N_RG_HEADS = 8
RG_C = 8.0
LN_EPS = 1e-5
RMS_EPS = 1e-6
NEG_INF = -1e30
ADAM_LR, ADAM_B1, ADAM_B2, ADAM_EPS, ADAM_WD, ADAM_STEP = 0.001, 0.9, 0.999, 1e-08, 0.01, 10
N_CHIPS = 4
N_DEV = 8
SMALL_ROWS = 280
VMEM_LIMIT = 56 * 1024 * 1024

MESH = pl.DeviceIdType.MESH
ANY = pl.BlockSpec(memory_space=pl.ANY)


def _pcall(body, *, name, **kw):
    return pl.pallas_call(body, name=name, **kw)


def _params(sem=None):
    return pltpu.CompilerParams(dimension_semantics=sem, vmem_limit_bytes=VMEM_LIMIT)


def _tile(dim, pref, mult=128):
    best = None
    for t in range(mult, min(dim, pref) + 1, mult):
        if dim % t == 0:
            best = t
    return best if best is not None else dim


def _dot(a, b, dims):
    return lax.dot_general(a.astype(_MXU_DTYPE), b.astype(_MXU_DTYPE), (dims, ((), ())),
                           preferred_element_type=F32)


NN = ((1,), (0,))
NT = ((1,), (1,))
TN = ((0,), (0,))


def _mm(a, b, *, mode, out_dtype, name, tm=1024, tn=1024, tk=512, add=None, add_scale=1.0, comm=()):
    if mode == "nn":
        (M, K), N = a.shape, b.shape[1]
    elif mode == "nt":
        (M, K), N = a.shape, b.shape[0]
    else:
        (K, M), N = a.shape, b.shape[1]
    tm, tn, tk = _tile(M, tm), _tile(N, tn), _tile(K, tk)
    ni, nj, nk = M // tm, N // tn, K // tk
    dims = {"nn": NN, "nt": NT, "tn": TN}[mode]
    n_cin = sum(len(p.ins) for p in comm)
    n_cout = sum(len(p.outs) for p in comm)

    def body(*refs):
        refs = list(refs)
        a_ref, b_ref = _take(refs, 2)
        add_ref = refs.pop(0) if add is not None else None
        cin = _take(refs, n_cin)
        o_ref = refs.pop(0)
        cout = _take(refs, n_cout)
        acc = refs.pop(0) if nk > 1 else None
        i, j, k = pl.program_id(0), pl.program_id(1), pl.program_id(2)

        if comm:
            @pl.when((i == 0) & (j == 0) & (k == 0))
            def _():
                _comm_run(comm, "start", cin, cout, *refs)

        def finish(r):
            if add_ref is not None:
                r = r + add_scale * add_ref[...]
            o_ref[...] = r.astype(out_dtype)

        if nk == 1:
            finish(_dot(a_ref[...], b_ref[...], dims))
        else:
            @pl.when(k == 0)
            def _():
                acc[...] = jnp.zeros_like(acc)

            acc[...] += _dot(a_ref[...], b_ref[...], dims)

            @pl.when(k == nk - 1)
            def _():
                finish(acc[...])

        if comm:
            @pl.when((i == ni - 1) & (j == nj - 1) & (k == nk - 1))
            def _():
                _comm_run(comm, "finish", cin, cout, *refs)

    a_spec = {"nn": pl.BlockSpec((tm, tk), lambda i, j, k: (i, k)),
              "nt": pl.BlockSpec((tm, tk), lambda i, j, k: (i, k)),
              "tn": pl.BlockSpec((tk, tm), lambda i, j, k: (k, i))}[mode]
    b_spec = {"nn": pl.BlockSpec((tk, tn), lambda i, j, k: (k, j)),
              "nt": pl.BlockSpec((tn, tk), lambda i, j, k: (j, k)),
              "tn": pl.BlockSpec((tk, tn), lambda i, j, k: (k, j))}[mode]
    in_specs, operands = [a_spec, b_spec], [a, b]
    if add is not None:
        in_specs.append(pl.BlockSpec((tm, tn), lambda i, j, k: (i, j)))
        operands.append(add)
    aliases = _comm_aliases(comm, len(operands), 1)
    in_specs += [ANY] * n_cin
    operands += [arr for p in comm for arr in p.ins]
    out_shape = [jax.ShapeDtypeStruct((M, N), out_dtype)] + [s for p in comm for s in p.outs]
    out_specs = [pl.BlockSpec((tm, tn), lambda i, j, k: (i, j))] + [ANY] * n_cout
    sem = ("arbitrary",) * 3 if comm else ("parallel", "parallel", "arbitrary")
    res = _pcall(body, name=name, out_shape=out_shape, grid=(ni, nj, nk), in_specs=in_specs, out_specs=out_specs,
                 scratch_shapes=([pltpu.VMEM((tm, tn), F32)] if nk > 1 else []) + _comm_scratch(comm),
                 input_output_aliases=aliases,
                 compiler_params=_params(sem))(*operands)
    return list(res) if comm else res[0]


def _colspecs(off, width, rows, rowmap):
    bw = math.gcd(off, width) if off else width
    specs = [pl.BlockSpec((rows, bw), functools.partial(lambda cb, *g: (rowmap(*g), cb), off // bw + i))
             for i in range(width // bw)]
    return specs, bw


def _cat(refs):
    vals = [r[...] for r in refs]
    return vals[0] if len(vals) == 1 else jnp.concatenate(vals, axis=1)


def _take(refs, n):
    out = refs[:n]
    del refs[:n]
    return out


def _sigmoid(x):
    return 1.0 / (1.0 + jnp.exp(-x))


def _rms(y, gamma):
    rstd = lax.rsqrt(jnp.mean(y * y, axis=-1, keepdims=True) + RMS_EPS)
    xn = y * rstd
    return xn, rstd, xn * gamma


def _rms_bwd(dn, xn, rstd, gamma):
    dng = dn * gamma
    return rstd * (dng - xn * jnp.mean(dng * xn, axis=-1, keepdims=True))


def _shift_down(x, s, carry8):
    rolled = pltpu.roll(x, s, 0)
    cr = pltpu.roll(carry8, s, 0)
    row8 = lax.broadcasted_iota(jnp.int32, carry8.shape, 0)
    top = jnp.where(row8 < s, cr, rolled[0:8])
    return jnp.concatenate([top, rolled[8:]], axis=0)


def _shift_up(x, s, carry8):
    n = x.shape[0]
    rolled = pltpu.roll(x, n - s, 0)
    cr = pltpu.roll(carry8, 8 - s, 0)
    row8 = lax.broadcasted_iota(jnp.int32, carry8.shape, 0)
    bot = jnp.where(row8 >= 8 - s, cr, rolled[n - 8:])
    return jnp.concatenate([rolled[:n - 8], bot], axis=0)


def _chunk_scan(a, b):
    n = a.shape[0]
    r8 = lax.broadcasted_iota(jnp.int32, a.shape, 0) & 7
    for d in (1, 2, 4):
        ok = r8 >= d
        a_sh = jnp.where(ok, pltpu.roll(a, d, 0), 1.0)
        b_sh = jnp.where(ok, pltpu.roll(b, d, 0), 0.0)
        b = a * b_sh + b
        a = a * a_sh
    return a, b


def _chunk_scan_rev(c, b):
    n = c.shape[0]
    r8 = lax.broadcasted_iota(jnp.int32, c.shape, 0) & 7
    for d in (1, 2, 4):
        ok = r8 + d <= 7
        c_sh = jnp.where(ok, pltpu.roll(c, n - d, 0), 1.0)
        b_sh = jnp.where(ok, pltpu.roll(b, n - d, 0), 0.0)
        b = b + c * b_sh
        c = c * c_sh
    return c, b


def _log1p(x):
    w = 1.0 + x
    return jnp.where(w == 1.0, x, jnp.log(w) * (x / (w - 1.0)))


def _log_sigmoid(x):
    return jnp.minimum(x, 0.0) - _log1p(jnp.exp(-jnp.abs(x)))


def _expm1(x):
    u = jnp.exp(x)
    lu = jnp.log(u)
    small = jnp.where(u == 1.0, x, (u - 1.0) * (x / jnp.where(lu == 0.0, 1.0, lu)))
    return jnp.where(jnp.abs(x) < 0.5, small, u - 1.0)


def _gates(xc, wr_ref, wi_ref, br, bi, lam):
    hw = xc.shape[1] // N_RG_HEADS
    gr = jnp.concatenate([_dot(xc[:, h * hw:(h + 1) * hw], wr_ref[h], NN) for h in range(N_RG_HEADS)], axis=1) + br
    gi = jnp.concatenate([_dot(xc[:, h * hw:(h + 1) * hw], wi_ref[h], NN) for h in range(N_RG_HEADS)], axis=1) + bi
    r, i = _sigmoid(gr), _sigmoid(gi)
    ls = _log_sigmoid(lam)
    la = RG_C * r * ls
    a = jnp.exp(la)
    sq = jnp.sqrt(-_expm1(2.0 * la))
    return r, i, ls, a, sq


def _ac_fwd(h, caw, ccw, pv, wr, wi, *, S, D, tt, name):
    T = h.shape[0]
    W = D // 4
    nt = S // tt
    rowmap = lambda s, t: s * nt + t
    c_off = D + D // 2 + 2 * (D // 16) + D // 2
    offs = [0, W, 2 * W, 3 * W, c_off, c_off + W]
    in_specs, counts = [], []
    for off in offs:
        specs, _ = _colspecs(off, W, tt, rowmap)
        in_specs += specs
        counts.append(len(specs))
    full = lambda shape: pl.BlockSpec(shape, lambda s, t: (0,) * len(shape))
    in_specs += [full(caw.shape), full(ccw.shape), full(pv.shape), full(wr.shape), full(wi.shape)]

    def body(*refs):
        refs = list(refs)
        ab, ac, ax, ag, cx, cg = [_cat(_take(refs, n)) for n in counts]
        caw_ref, ccw_ref, pv_ref, wr_ref, wi_ref = _take(refs, 5)
        mixac_ref, cv_ref, xc_ref, yc_ref = _take(refs, 4)
        carry_p, carry_cx, carry_h, a_s, b_s = refs
        t = pl.program_id(1)

        @pl.when(t == 0)
        def _():
            carry_p[...] = jnp.zeros_like(carry_p)
            carry_cx[...] = jnp.zeros_like(carry_cx)
            carry_h[...] = jnp.zeros_like(carry_h)

        ccb, br, bi, lam, na, nc = [pv_ref[k:k + 1, :] for k in range(6)]
        p = ac * ax
        cp = carry_p[...]
        cv = caw_ref[2:3, :] * p + caw_ref[1:2, :] * _shift_down(p, 1, cp) + caw_ref[0:1, :] * _shift_down(p, 2, cp)
        carry_p[...] = p[tt - 8:tt]
        cv_ref[...] = cv
        _, _, n_a = _rms(ab * cv, na)
        mix_a = n_a * (ag * _sigmoid(ag))
        ccx = carry_cx[...]
        xc = (ccw_ref[3:4, :] * cx + ccw_ref[2:3, :] * _shift_down(cx, 1, ccx) + ccw_ref[1:2, :] * _shift_down(cx, 2, ccx)
              + ccw_ref[0:1, :] * _shift_down(cx, 3, ccx) + ccb)
        carry_cx[...] = cx[tt - 8:tt]
        xc_ref[...] = xc
        r, i, ls, a, sq = _gates(xc, wr_ref, wi_ref, br, bi, lam)
        u = sq * (i * xc)
        a_c, b_c = _chunk_scan(a, u)
        a_s[...] = a_c
        b_s[...] = b_c

        def step(k, hprev):
            rows = pl.ds(pl.multiple_of(k * 8, 8), 8)
            hc = a_s[rows, :] * hprev + b_s[rows, :]
            yc_ref[rows, :] = hc
            return hc[7:8, :]

        hlast = lax.fori_loop(0, tt // 8, step, carry_h[0:1, :])
        carry_h[...] = jnp.broadcast_to(hlast, carry_h.shape)
        _, _, n_c = _rms(yc_ref[...], nc)
        mix_c = n_c * (cg * _sigmoid(cg))
        mixac_ref[...] = jnp.concatenate([mix_a, mix_c], axis=1).astype(mixac_ref.dtype)

    row_blk = lambda w: pl.BlockSpec((tt, w), lambda s, t: (rowmap(s, t), 0))
    return _pcall(
        body, name=name, grid=(T // S, nt), in_specs=in_specs,
        out_shape=(jax.ShapeDtypeStruct((T, 2 * W), BF16), jax.ShapeDtypeStruct((T, W), F32),
                   jax.ShapeDtypeStruct((T, W), F32), jax.ShapeDtypeStruct((T, W), F32)),
        out_specs=(row_blk(2 * W), row_blk(W), row_blk(W), row_blk(W)),
        scratch_shapes=[pltpu.VMEM((8, W), F32), pltpu.VMEM((8, W), F32), pltpu.VMEM((8, W), F32),
                        pltpu.VMEM((tt, W), F32), pltpu.VMEM((tt, W), F32)],
        compiler_params=_params(("arbitrary", "arbitrary")),
    )(*([h] * sum(counts)), caw, ccw, pv, wr, wi)


def _ac_bwd(h, cv, xc, yc, dmix, caw, ccw, pv, wr, wi, *, S, D, tt, name):
    T = h.shape[0]
    W = D // 4
    nt = S // tt
    rowmap = lambda s, t: s * nt + (nt - 1 - t)
    c_off = D + D // 2 + 2 * (D // 16) + D // 2
    offs = [0, W, 2 * W, 3 * W, c_off, c_off + W]
    in_specs, counts = [], []
    for off in offs:
        specs, _ = _colspecs(off, W, tt, rowmap)
        in_specs += specs
        counts.append(len(specs))
    row_blk = lambda w, cb=0: pl.BlockSpec((tt, w), lambda s, t: (rowmap(s, t), cb))
    in_specs += [row_blk(W), row_blk(W), row_blk(W)]
    in_specs.append(pl.BlockSpec((8, W), lambda s, t: (jnp.maximum(rowmap(s, t) * (tt // 8) - 1, 0), 0)))
    in_specs += [row_blk(W, 0), row_blk(W, 3)]
    full = lambda shape: pl.BlockSpec(shape, lambda s, t: (0,) * len(shape))
    in_specs += [full(caw.shape), full(ccw.shape), full(pv.shape), full(wr.shape), full(wi.shape)]

    def body(*refs):
        refs = list(refs)
        ab, ac, ax, ag, cx, cg = [_cat(_take(refs, n)) for n in counts]
        cv_ref, xc_ref, yc_ref, halo_ref, dma_ref, dmc_ref, caw_ref, ccw_ref, pv_ref, wr_ref, wi_ref = _take(refs, 11)
        dha_ref, dhc_ref, vec_ref, dwr_ref, dwi_ref = _take(refs, 5)
        carry_dcv, carry_dxc, carry_a, carry_g, c_s, b_s, g_s = refs
        s_id, t = pl.program_id(0), pl.program_id(1)

        @pl.when(t == 0)
        def _():
            for cr in (carry_dcv, carry_dxc, carry_a, carry_g):
                cr[...] = jnp.zeros_like(cr)

        @pl.when((t == 0) & (s_id == 0))
        def _():
            vec_ref[...] = jnp.zeros_like(vec_ref)
            dwr_ref[...] = jnp.zeros_like(dwr_ref)
            dwi_ref[...] = jnp.zeros_like(dwi_ref)

        def acc_row(k, val):
            vec_ref[k:k + 1, :] += jnp.sum(val, axis=0, keepdims=True)

        ccb, br, bi, lam, na, nc = [pv_ref[k:k + 1, :] for k in range(6)]
        cv = cv_ref[...]
        dmix_a = dma_ref[...]
        p = ac * ax
        xn, rstd, n_a = _rms(ab * cv, na)
        sg = _sigmoid(ag)
        dn = dmix_a * (ag * sg)
        dag = dmix_a * n_a * (sg * (1.0 + ag * (1.0 - sg)))
        acc_row(3, dn * xn)
        dya = _rms_bwd(dn, xn, rstd, na)
        dab = dya * cv
        dcv = dya * ab
        cd = carry_dcv[...]
        d1, d2 = _shift_up(dcv, 1, cd), _shift_up(dcv, 2, cd)
        dp = caw_ref[2:3, :] * dcv + caw_ref[1:2, :] * d1 + caw_ref[0:1, :] * d2
        acc_row(2, p * dcv)
        acc_row(1, p * d1)
        acc_row(0, p * d2)
        carry_dcv[...] = dcv[0:8]
        dha_ref[...] = jnp.concatenate([dab, dp * ax, dp * ac, dag], axis=1).astype(dha_ref.dtype)
        xc = xc_ref[...]
        yc = yc_ref[...]
        dmix_c = dmc_ref[...]
        xn, rstd, n_c = _rms(yc, nc)
        sg = _sigmoid(cg)
        dn = dmix_c * (cg * sg)
        dcg = dmix_c * n_c * (sg * (1.0 + cg * (1.0 - sg)))
        acc_row(12, dn * xn)
        dyc = _rms_bwd(dn, xn, rstd, nc)
        r, i, ls, a, sq = _gates(xc, wr_ref, wi_ref, br, bi, lam)
        halo = jnp.where(t == nt - 1, 0.0, halo_ref[...])
        hprev = _shift_down(yc, 1, halo)
        c_c, b_c = _chunk_scan_rev(_shift_up(a, 1, carry_a[...]), dyc)
        c_s[...] = c_c
        b_s[...] = b_c

        def step(k, gnext):
            rows = pl.ds(pl.multiple_of((tt // 8 - 1 - k) * 8, 8), 8)
            gc = b_s[rows, :] + c_s[rows, :] * gnext
            g_s[rows, :] = gc
            return gc[0:1, :]

        lax.fori_loop(0, tt // 8, step, carry_g[0:1, :])
        g = g_s[...]
        carry_g[...] = g[0:8]
        carry_a[...] = a[0:8]
        da = g * hprev
        ixc = i * xc
        dsq = g * ixc
        di = g * sq * xc
        dxc = g * sq * i
        dla = da * a - dsq * (a * a) / sq
        dr = dla * (RG_C * ls)
        acc_row(11, dla * (RG_C * r) * _sigmoid(-lam))
        dgr = dr * r * (1.0 - r)
        dgi = di * i * (1.0 - i)
        acc_row(9, dgr)
        acc_row(10, dgi)
        hw = W // N_RG_HEADS
        parts = []
        for hd in range(N_RG_HEADS):
            sl = slice(hd * hw, (hd + 1) * hw)
            dwr_ref[hd] += _dot(xc[:, sl], dgr[:, sl], TN)
            dwi_ref[hd] += _dot(xc[:, sl], dgi[:, sl], TN)
            parts.append(_dot(dgr[:, sl], wr_ref[hd], NT) + _dot(dgi[:, sl], wi_ref[hd], NT))
        dxc = dxc + jnp.concatenate(parts, axis=1)
        ce = carry_dxc[...]
        e1, e2, e3 = _shift_up(dxc, 1, ce), _shift_up(dxc, 2, ce), _shift_up(dxc, 3, ce)
        dcx = ccw_ref[3:4, :] * dxc + ccw_ref[2:3, :] * e1 + ccw_ref[1:2, :] * e2 + ccw_ref[0:1, :] * e3
        acc_row(7, cx * dxc)
        acc_row(6, cx * e1)
        acc_row(5, cx * e2)
        acc_row(4, cx * e3)
        acc_row(8, dxc)
        carry_dxc[...] = dxc[0:8]
        dhc_ref[...] = jnp.concatenate([dcx, dcg], axis=1).astype(dhc_ref.dtype)

    const = lambda shape: pl.BlockSpec(shape, lambda s, t: (0,) * len(shape))
    return _pcall(
        body, name=name, grid=(T // S, nt), in_specs=in_specs,
        out_shape=(jax.ShapeDtypeStruct((T, 4 * W), BF16), jax.ShapeDtypeStruct((T, 2 * W), BF16),
                   jax.ShapeDtypeStruct((16, W), F32), jax.ShapeDtypeStruct(wr.shape, F32),
                   jax.ShapeDtypeStruct(wi.shape, F32)),
        out_specs=(row_blk(4 * W), row_blk(2 * W), const((16, W)), const(wr.shape), const(wi.shape)),
        scratch_shapes=[pltpu.VMEM((8, W), F32)] * 4 + [pltpu.VMEM((tt, W), F32)] * 3,
        compiler_params=_params(("arbitrary", "arbitrary")),
    )(*([h] * sum(counts)), cv, xc, yc, yc, dmix, dmix, caw, ccw, pv, wr, wi)


def _lo_mask():
    return lax.broadcasted_iota(jnp.int32, (1, 2 * HEAD_DIM), 1) < HEAD_DIM


def _dup(blk, odd, lo):
    rot = pltpu.roll(blk, HEAD_DIM, 1)
    return jnp.where(lo, rot, blk) if odd else jnp.where(lo, blk, rot)


def _stack_heads(x, hh, lo, masked):
    parts = []
    for g in range(KV_GROUP):
        jq = hh * KV_GROUP + g
        pb = x[:, (jq // 2) * 128:(jq // 2 + 1) * 128]
        if masked:
            pb = jnp.where(lo if jq % 2 == 0 else jnp.logical_not(lo), pb, 0.0)
        parts.append(pb)
    return jnp.concatenate(parts, axis=0)


def _unstack_pairs(st, lo):
    return [jnp.where(lo, st[(2 * pi) * BLOCK:(2 * pi + 1) * BLOCK], st[(2 * pi + 1) * BLOCK:(2 * pi + 2) * BLOCK])
            for pi in range(KV_GROUP // 2)]


def _window(ref, n):
    prev = ref[pl.ds(pl.multiple_of(jnp.maximum(n - 1, 0) * BLOCK, BLOCK), BLOCK), :]
    cur = ref[pl.ds(pl.multiple_of(n * BLOCK, BLOCK), BLOCK), :]
    return jnp.concatenate([prev, cur], axis=0)


def _valid_mask(n):
    rows = KV_GROUP * BLOCK
    qi = lax.broadcasted_iota(jnp.int32, (rows, 2 * BLOCK), 0) & (BLOCK - 1)
    kj = lax.broadcasted_iota(jnp.int32, (rows, 2 * BLOCK), 1)
    dist = qi + BLOCK - kj
    return (dist >= 0) & (dist < BLOCK) & ((n > 0) | (kj >= BLOCK))


def _sink_col(sinks_ref, layer, hh):
    return jnp.concatenate([jnp.full((BLOCK, 1), sinks_ref[layer, hh * KV_GROUP + g], F32) for g in range(KV_GROUP)],
                           axis=0)


def _softmax(qs, kdup, valid, sink):
    s = _dot(qs, kdup, NT) * (HEAD_DIM ** -0.5)
    s = jnp.where(valid, s, NEG_INF)
    m = jnp.maximum(jnp.max(s, axis=-1, keepdims=True), sink)
    e = jnp.exp(s - m)
    es = jnp.exp(sink - m)
    den = jnp.sum(e, axis=-1, keepdims=True) + es
    return e / den, es / den


def _attn_fwd(h, sinks, layer, *, S, D, name):
    T = h.shape[0]
    WB, KVW = D // 2, D // 16
    nb = S // BLOCK
    n_kv = KVW // HEAD_DIM

    def body(q_ref, k_ref, v_ref, sinks_ref, o_ref):
        n = pl.program_id(1)
        lo = _lo_mask()
        q = q_ref[...]
        kk, vv = _window(k_ref, n), _window(v_ref, n)
        valid = _valid_mask(n)
        blocks = []
        for hh in range(n_kv):
            cb = slice((hh // 2) * 128, (hh // 2 + 1) * 128)
            kdup, vdup = _dup(kk[:, cb], hh % 2, lo), _dup(vv[:, cb], hh % 2, lo)
            p, _ = _softmax(_stack_heads(q, hh, lo, True), kdup, valid, _sink_col(sinks_ref, layer, hh))
            blocks += _unstack_pairs(_dot(p, vdup, NN), lo)
        o_ref[...] = jnp.concatenate(blocks, axis=1)

    return _pcall(
        body, name=name, grid=(T // S, nb),
        in_specs=[pl.BlockSpec((BLOCK, WB), lambda s, n: (s * nb + n, D // WB)),
                  pl.BlockSpec((S, KVW), lambda s, n: (s, (D + WB) // KVW)),
                  pl.BlockSpec((S, KVW), lambda s, n: (s, (D + WB) // KVW + 1)),
                  pl.BlockSpec(memory_space=pltpu.SMEM)],
        out_shape=jax.ShapeDtypeStruct((T, WB), F32),
        out_specs=pl.BlockSpec((BLOCK, WB), lambda s, n: (s * nb + n, 0)),
        compiler_params=_params(("arbitrary", "arbitrary")),
    )(h, h, h, sinks)


def _attn_bwd(h, yb, dyb, sinks, layer, *, S, D, name):
    T = h.shape[0]
    WB, KVW = D // 2, D // 16
    nb = S // BLOCK
    n_kv = KVW // HEAD_DIM

    def body(q_ref, k_ref, v_ref, o_ref, do_ref, sinks_ref, dq_ref, dk_ref, dv_ref, dsink_ref, dk_acc, dv_acc):
        s_id, n = pl.program_id(0), pl.program_id(1)
        lo = _lo_mask()

        @pl.when(n == 0)
        def _():
            dk_acc[...] = jnp.zeros_like(dk_acc)
            dv_acc[...] = jnp.zeros_like(dv_acc)

        @pl.when((n == 0) & (s_id == 0))
        def _():
            dsink_ref[...] = jnp.zeros_like(dsink_ref)

        q, o, do = q_ref[...], o_ref[...], do_ref[...]
        kk, vv = _window(k_ref, n), _window(v_ref, n)
        valid = _valid_mask(n)
        lane = lax.broadcasted_iota(jnp.int32, dsink_ref.shape, 1)
        dq_blocks, dk_heads, dv_heads = [], [], []
        dsink = jnp.zeros(dsink_ref.shape, F32)
        for hh in range(n_kv):
            cb = slice((hh // 2) * 128, (hh // 2 + 1) * 128)
            kdup, vdup = _dup(kk[:, cb], hh % 2, lo), _dup(vv[:, cb], hh % 2, lo)
            qs = _stack_heads(q, hh, lo, True)
            dos = _stack_heads(do, hh, lo, True)
            delta = jnp.sum(dos * _stack_heads(o, hh, lo, False), axis=-1, keepdims=True)
            p, psink = _softmax(qs, kdup, valid, _sink_col(sinks_ref, layer, hh))
            dvr = _dot(p, dos, TN)
            dv_heads.append(dvr + pltpu.roll(dvr, HEAD_DIM, 1))
            ds = p * (_dot(dos, vdup, NT) - delta) * (HEAD_DIM ** -0.5)
            dq_blocks += _unstack_pairs(_dot(ds, kdup, NN), lo)
            dkr = _dot(ds, qs, TN)
            dk_heads.append(dkr + pltpu.roll(dkr, HEAD_DIM, 1))
            dsk = -psink * delta
            for g in range(KV_GROUP):
                tot = jnp.sum(dsk[g * BLOCK:(g + 1) * BLOCK], axis=0, keepdims=True)
                dsink = dsink + jnp.where(lane == hh * KV_GROUP + g, tot, 0.0)
        dsink_ref[...] += dsink
        dq_ref[...] = jnp.concatenate(dq_blocks, axis=1).astype(dq_ref.dtype)
        pair = lambda hs: jnp.concatenate([jnp.where(lo, hs[2 * m], hs[2 * m + 1]) for m in range(n_kv // 2)], axis=1)
        dkk, dvv = pair(dk_heads), pair(dv_heads)
        prev = pl.ds(pl.multiple_of(jnp.maximum(n - 1, 0) * BLOCK, BLOCK), BLOCK)
        cur = pl.ds(pl.multiple_of(n * BLOCK, BLOCK), BLOCK)
        dk_acc[prev, :] += dkk[:BLOCK]
        dk_acc[cur, :] += dkk[BLOCK:]
        dv_acc[prev, :] += dvv[:BLOCK]
        dv_acc[cur, :] += dvv[BLOCK:]

        @pl.when(n == nb - 1)
        def _():
            dk_ref[...] = dk_acc[...].astype(dk_ref.dtype)
            dv_ref[...] = dv_acc[...].astype(dv_ref.dtype)

    blk = lambda cb=0: pl.BlockSpec((BLOCK, WB), lambda s, n: (s * nb + n, cb))
    seq = lambda cb=0: pl.BlockSpec((S, KVW), lambda s, n: (s, cb))
    return _pcall(
        body, name=name, grid=(T // S, nb),
        in_specs=[blk(D // WB), seq((D + WB) // KVW), seq((D + WB) // KVW + 1), blk(), blk(),
                  pl.BlockSpec(memory_space=pltpu.SMEM)],
        out_shape=(jax.ShapeDtypeStruct((T, WB), BF16), jax.ShapeDtypeStruct((T, KVW), BF16),
                   jax.ShapeDtypeStruct((T, KVW), BF16), jax.ShapeDtypeStruct((8, 128), F32)),
        out_specs=(blk(), seq(), seq(), pl.BlockSpec((8, 128), lambda s, n: (0, 0))),
        scratch_shapes=[pltpu.VMEM((S, KVW), F32), pltpu.VMEM((S, KVW), F32)],
        compiler_params=_params(("arbitrary", "arbitrary")),
    )(h, h, h, yb, dyb, sinks)


def _bg_specs(D, tm):
    return _colspecs(D + D // 2 + 2 * (D // 16), D // 2, tm, lambda i: i)


def _mixb_fwd(yb, h, nb_g, *, D, tm, name):
    T, WB = yb.shape
    bg_specs, _ = _bg_specs(D, tm)

    def body(*refs):
        refs = list(refs)
        yb_ref = refs.pop(0)
        bg = _cat(_take(refs, len(bg_specs)))
        g_ref, o_ref = refs
        _, _, nrm = _rms(yb_ref[...], g_ref[...])
        o_ref[...] = (nrm * (bg * _sigmoid(bg))).astype(o_ref.dtype)

    row = pl.BlockSpec((tm, WB), lambda i: (i, 0))
    return _pcall(body, name=name, grid=(T // tm,),
                  in_specs=[row] + bg_specs + [pl.BlockSpec((1, WB), lambda i: (0, 0))],
                  out_shape=jax.ShapeDtypeStruct((T, WB), BF16), out_specs=row,
                  compiler_params=_params(("arbitrary",)))(yb, *([h] * len(bg_specs)), nb_g)


def _mixb_bwd(yb, h, dmix, nb_g, *, D, tm, name):
    T, WB = yb.shape
    W = D // 4
    bg_specs, _ = _bg_specs(D, tm)
    dm_specs, _ = _colspecs(W, WB, tm, lambda i: i)

    def body(*refs):
        refs = list(refs)
        yb_ref = refs.pop(0)
        bg = _cat(_take(refs, len(bg_specs)))
        dmix_b = _cat(_take(refs, len(dm_specs)))
        g_ref, dyb_ref, dbg_ref, dg_ref = refs

        @pl.when(pl.program_id(0) == 0)
        def _():
            dg_ref[...] = jnp.zeros_like(dg_ref)

        gamma = g_ref[...]
        xn, rstd, nrm = _rms(yb_ref[...], gamma)
        sg = _sigmoid(bg)
        dn = dmix_b * (bg * sg)
        dbg_ref[...] = (dmix_b * nrm * (sg * (1.0 + bg * (1.0 - sg)))).astype(dbg_ref.dtype)
        dg_ref[0:1, :] += jnp.sum(dn * xn, axis=0, keepdims=True)
        dyb_ref[...] = _rms_bwd(dn, xn, rstd, gamma)

    row = pl.BlockSpec((tm, WB), lambda i: (i, 0))
    return _pcall(body, name=name, grid=(T // tm,),
                  in_specs=[row] + bg_specs + dm_specs + [pl.BlockSpec((1, WB), lambda i: (0, 0))],
                  out_shape=(jax.ShapeDtypeStruct((T, WB), F32), jax.ShapeDtypeStruct((T, WB), BF16),
                             jax.ShapeDtypeStruct((8, WB), F32)),
                  out_specs=(row, row, pl.BlockSpec((8, WB), lambda i: (0, 0))),
                  compiler_params=_params(("arbitrary",)))(yb, *([h] * len(bg_specs)), *([dmix] * len(dm_specs)), nb_g)


def _concat_cols(parts, *, tm, name):
    parts = [p if isinstance(p, tuple) else (p, 0, p.shape[1]) for p in parts]
    T = parts[0][0].shape[0]
    total = sum(w for _, _, w in parts)

    def body(*refs):
        refs[-1][...] = jnp.concatenate([r[...] for r in refs[:-1]], axis=1)

    return _pcall(body, name=name, grid=(T // tm,),
                  in_specs=[pl.BlockSpec((tm, w), functools.partial(lambda cb, i: (i, cb), cb)) for _, cb, w in parts],
                  out_shape=jax.ShapeDtypeStruct((T, total), parts[0][0].dtype),
                  out_specs=pl.BlockSpec((tm, total), lambda i: (i, 0)),
                  compiler_params=_params(("parallel",)))(*[a for a, _, _ in parts])


def _ln_fwd(z, g, b, *, tm, name):
    T, D = z.shape

    def body(z_ref, g_ref, b_ref, y_ref, yb_ref):
        zv = z_ref[...]
        mu = jnp.mean(zv, axis=-1, keepdims=True)
        zc = zv - mu
        var = jnp.mean(zc * zc, axis=-1, keepdims=True)
        y = zc * lax.rsqrt(var + LN_EPS) * g_ref[...] + b_ref[...]
        y_ref[...] = y
        yb_ref[...] = y.astype(BF16)

    row = pl.BlockSpec((tm, D), lambda i: (i, 0))
    vec = pl.BlockSpec((1, D), lambda i: (0, 0))
    return _pcall(body, name=name, grid=(T // tm,), in_specs=[row, vec, vec],
                  out_shape=(jax.ShapeDtypeStruct((T, D), F32), jax.ShapeDtypeStruct((T, D), BF16)),
                  out_specs=(row, row), compiler_params=_params(("parallel",)))(z, g, b)


def _ln_bwd(z, dy, g, *, tm, name):
    T, D = z.shape

    def body(z_ref, dy_ref, g_ref, dz_ref, dzb_ref, dgb_ref):
        @pl.when(pl.program_id(0) == 0)
        def _():
            dgb_ref[...] = jnp.zeros_like(dgb_ref)

        zv, dyv = z_ref[...], dy_ref[...]
        mu = jnp.mean(zv, axis=-1, keepdims=True)
        zc = zv - mu
        rstd = lax.rsqrt(jnp.mean(zc * zc, axis=-1, keepdims=True) + LN_EPS)
        xh = zc * rstd
        dxh = dyv * g_ref[...]
        dz = rstd * (dxh - jnp.mean(dxh, axis=-1, keepdims=True) - xh * jnp.mean(dxh * xh, axis=-1, keepdims=True))
        dz_ref[...] = dz
        dzb_ref[...] = dz.astype(BF16)
        dgb_ref[0:1, :] += jnp.sum(dyv * xh, axis=0, keepdims=True)
        dgb_ref[1:2, :] += jnp.sum(dyv, axis=0, keepdims=True)

    row = pl.BlockSpec((tm, D), lambda i: (i, 0))
    return _pcall(body, name=name, grid=(T // tm,), in_specs=[row, row, pl.BlockSpec((1, D), lambda i: (0, 0))],
                  out_shape=(jax.ShapeDtypeStruct((T, D), F32), jax.ShapeDtypeStruct((T, D), BF16),
                             jax.ShapeDtypeStruct((8, D), F32)),
                  out_specs=(row, row, pl.BlockSpec((8, D), lambda i: (0, 0))),
                  compiler_params=_params(("arbitrary",)))(z, dy, g)


def _loss_head(y, target, *, tm, name):
    T, D = y.shape

    def body(y_ref, t_ref, dy_ref, loss_ref):
        @pl.when(pl.program_id(0) == 0)
        def _():
            loss_ref[...] = jnp.zeros_like(loss_ref)

        err = y_ref[...] - t_ref[...]
        dy_ref[...] = err / D
        loss_ref[...] += 0.5 * jnp.sum(jnp.mean(err * err, axis=-1, keepdims=True), axis=0, keepdims=True)

    row = pl.BlockSpec((tm, D), lambda i: (i, 0))
    return _pcall(body, name=name, grid=(T // tm,), in_specs=[row, row],
                  out_shape=(jax.ShapeDtypeStruct((T, D), F32), jax.ShapeDtypeStruct((1, 1), F32)),
                  out_specs=(row, pl.BlockSpec((1, 1), lambda i: (0, 0))),
                  compiler_params=_params(("arbitrary",)))(y, target)


def _cast_bf16(w, layer, *, name):
    _, R, C = w.shape
    tr = _tile(R, 512, 8)

    def body(w_ref, o_ref):
        o_ref[...] = w_ref[...].astype(BF16)

    return _pcall(body, name=name, grid=(R // tr,), in_specs=[pl.BlockSpec((None, tr, C), lambda i: (layer, i, 0))],
                  out_shape=jax.ShapeDtypeStruct((R, C), BF16), out_specs=pl.BlockSpec((tr, C), lambda i: (i, 0)),
                  compiler_params=_params(("parallel",)))(w)


def _cast_shard(w, layer, kind, *, name):
    _, R, C = w.shape
    tr = _tile(R, 512, 16)
    nrb = R // tr
    if kind == "in":
        full, o_idx = (R, N_CHIPS * C), lambda i: (i, _my_chip())
    else:
        full, o_idx = (N_CHIPS * R, C), lambda i: (_my_chip() * nrb + i, 0)

    def body(w_ref, o_ref):
        o_ref[...] = w_ref[...].astype(BF16)

    return _pcall(body, name=name, grid=(nrb,), in_specs=[pl.BlockSpec((None, tr, C), lambda i: (layer, i, 0))],
                  out_shape=jax.ShapeDtypeStruct(full, BF16), out_specs=pl.BlockSpec((tr, C), o_idx),
                  compiler_params=_params(("parallel",)))(w)


def _adamw_layer(g, w, m, v, layer, bufs, *, name):
    L, R, C = w.shape
    tr = _tile(R, max(8, (1 << 19) // C // 8 * 8), 8)
    if bufs is None:
        bufs = [lax.empty((L, R, C), F32) for _ in range(4)]

    def body(g_ref, w_ref, m_ref, v_ref, b0, b1, b2, b3, go_ref, d_ref, nm_ref, nv_ref):
        gv = g_ref[...]
        nm = ADAM_B1 * m_ref[...] + (1.0 - ADAM_B1) * gv
        nv = ADAM_B2 * v_ref[...] + (1.0 - ADAM_B2) * (gv * gv)
        m_hat = nm / (1.0 - ADAM_B1 ** ADAM_STEP)
        v_hat = nv / (1.0 - ADAM_B2 ** ADAM_STEP)
        go_ref[...] = gv
        d_ref[...] = -ADAM_LR * (m_hat / (jnp.sqrt(v_hat) + ADAM_EPS) + ADAM_WD * w_ref[...])
        nm_ref[...] = nm
        nv_ref[...] = nv

    lay = pl.BlockSpec((None, tr, C), lambda i: (layer, i, 0))
    shp = jax.ShapeDtypeStruct((L, R, C), F32)
    return list(_pcall(body, name=name, grid=(R // tr,),
                       in_specs=[pl.BlockSpec((tr, C), lambda i: (i, 0)), lay, lay, lay] + [ANY] * 4,
                       out_shape=(shp,) * 4, out_specs=(lay,) * 4, input_output_aliases={4 + k: k for k in range(4)},
                       compiler_params=_params(("parallel",)))(g, w, m, v, *bufs))


def _adamw(g, w, m, v, *, name):
    R, C = g.shape
    tr = _tile(R, max(8, (1 << 19) // C // 8 * 8), 8)

    def body(g_ref, w_ref, m_ref, v_ref, d_ref, nm_ref, nv_ref):
        gv = g_ref[...]
        nm = ADAM_B1 * m_ref[...] + (1.0 - ADAM_B1) * gv
        nv = ADAM_B2 * v_ref[...] + (1.0 - ADAM_B2) * (gv * gv)
        m_hat = nm / (1.0 - ADAM_B1 ** ADAM_STEP)
        v_hat = nv / (1.0 - ADAM_B2 ** ADAM_STEP)
        d_ref[...] = -ADAM_LR * (m_hat / (jnp.sqrt(v_hat) + ADAM_EPS) + ADAM_WD * w_ref[...])
        nm_ref[...] = nm
        nv_ref[...] = nv

    blk = pl.BlockSpec((tr, C), lambda i: (i, 0))
    shp = jax.ShapeDtypeStruct((R, C), F32)
    return _pcall(body, name=name, grid=(R // tr,), in_specs=[blk] * 4, out_shape=(shp, shp, shp),
                  out_specs=(blk, blk, blk), compiler_params=_params(("parallel",)))(g, w, m, v)


def _my_core():
    return lax.axis_index("c")


def _my_chip():
    return 2 * lax.axis_index("x") + lax.axis_index("y")


def _pair_sum(mine, theirs, *, half_axis, name):
    R, C = theirs.shape
    tr, tc = _tile(R, 512, 16), _tile(C, 2048)
    nrb, ncb = R // tr, C // tc

    def body(a_ref, b_ref, o_ref):
        o_ref[...] = (a_ref[...].astype(F32) + b_ref[...].astype(F32)).astype(BF16)

    if half_axis == 0:
        a_idx = lambda i, j: (_my_core() * nrb + i, j)
    else:
        a_idx = lambda i, j: (i, _my_core() * ncb + j)
    blk = pl.BlockSpec((tr, tc), lambda i, j: (i, j))
    return _pcall(body, name=name, grid=(nrb, ncb), in_specs=[pl.BlockSpec((tr, tc), a_idx), blk], out_specs=blk,
                  out_shape=jax.ShapeDtypeStruct(theirs.shape, BF16),
                  compiler_params=_params(("parallel", "parallel")))(mine, theirs)


def _final_sum(own, got, *, own_axis, out_shape, out_axis, name):
    _, R, C = got.shape
    tr, tc = _tile(R, 512, 16), _tile(C, 1024)
    nrb, ncb = R // tr, C // tc

    def body(a_ref, q_ref, o_ref):
        o_ref[...] = ((a_ref[...].astype(F32) + q_ref[0].astype(F32)) + q_ref[1].astype(F32)) + q_ref[2].astype(F32)

    if own_axis == 1:
        a_idx = lambda i, j: (i, _my_chip() * ncb + j)
    else:
        a_idx = lambda i, j: (_my_chip() * nrb + i, j)
    if out_axis == 0:
        o_idx = lambda i, j: (_my_core() * nrb + i, j)
    else:
        o_idx = lambda i, j: (i, _my_core() * ncb + j)
    return _pcall(body, name=name, grid=(nrb, ncb),
                  in_specs=[pl.BlockSpec((tr, tc), a_idx), pl.BlockSpec((3, tr, tc), lambda i, j: (0, i, j))],
                  out_specs=pl.BlockSpec((tr, tc), o_idx), out_shape=jax.ShapeDtypeStruct(out_shape, F32),
                  compiler_params=_params(("parallel", "parallel")))(own, got)


def _sum_devices(gathered, *, name):
    _, R, C = gathered.shape
    tr = _tile(R, 280, 8)

    def body(g_ref, o_ref):
        acc = g_ref[0]
        for d in range(1, N_DEV):
            acc = acc + g_ref[d]
        o_ref[...] = acc

    return _pcall(body, name=name, grid=(R // tr,), in_specs=[pl.BlockSpec((N_DEV, tr, C), lambda i: (0, i, 0))],
                  out_shape=jax.ShapeDtypeStruct((R, C), F32), out_specs=pl.BlockSpec((tr, C), lambda i: (i, 0)),
                  compiler_params=_params(("parallel",)))(gathered)


def _position():
    x, y, c = lax.axis_index("x"), lax.axis_index("y"), lax.axis_index("c")
    chips = [(1 - x, y), (x, 1 - y), (1 - x, 1 - y)]
    return x, y, c, chips


def _remote(src, dst, send_sems, recv_sems, k, to):
    return pltpu.make_async_remote_copy(src_ref=src, dst_ref=dst, send_sem=send_sems.at[k], recv_sem=recv_sems.at[k],
                                        device_id=to, device_id_type=MESH)


def _r(ref, start, n):
    return ref.at[pl.ds(pl.multiple_of(start, 16), n), :]


def _c(ref, start, n):
    return ref.at[:, pl.ds(pl.multiple_of(start, 128), n)]


class _part:
    def __init__(self, ins, outs, plan, n, n_local=0, aliased=0):
        self.ins, self.outs, self.plan, self.n, self.n_local, self.aliased = ins, outs, plan, n, n_local, aliased


def _comm_scratch(parts):
    if not parts:
        return []
    n, nl = sum(p.n for p in parts), sum(p.n_local for p in parts)
    return [pltpu.SemaphoreType.DMA((n,)), pltpu.SemaphoreType.DMA((n,)), pltpu.SemaphoreType.DMA((max(nl, 1),))]


def _comm_aliases(parts, in_base, out_base):
    aliases, ii, oi = {}, in_base, out_base
    for p in parts:
        aliases.update({ii + k: oi + k for k in range(p.aliased)})
        ii += len(p.ins)
        oi += len(p.outs)
    return aliases


def _comm_run(parts, phase, in_refs, out_refs, send_sems, recv_sems, local_sems):
    pos = _position()
    me = pos[:3]
    ii = oi = si = li = 0
    for p in parts:
        sends, recvs, locs = p.plan(in_refs[ii:ii + len(p.ins)], out_refs[oi:oi + len(p.outs)], pos)
        assert len(sends) == len(recvs) == p.n and len(locs) == p.n_local
        if phase == "start":
            for k, (src, dst) in enumerate(locs):
                pltpu.make_async_copy(src, dst, local_sems.at[li + k]).start()
            for k, (src, dst, to) in enumerate(sends):
                _remote(src, dst, send_sems, recv_sems, si + k, to).start()
        else:
            for k, dst in enumerate(recvs):
                _remote(dst, dst, send_sems, recv_sems, si + k, me).wait_recv()
            for k, (src, dst, to) in enumerate(sends):
                _remote(src, dst, send_sems, recv_sems, si + k, to).wait_send()
            for k, (src, dst) in enumerate(locs):
                pltpu.make_async_copy(src, dst, local_sems.at[li + k]).wait()
        ii, oi, si, li = ii + len(p.ins), oi + len(p.outs), si + p.n, li + p.n_local


def _comm_call(parts, *, name):
    n_in = sum(len(p.ins) for p in parts)
    n_out = sum(len(p.outs) for p in parts)

    def body(*refs):
        refs = list(refs)
        cin, cout = _take(refs, n_in), _take(refs, n_out)
        _comm_run(parts, "start", cin, cout, *refs)
        _comm_run(parts, "finish", cin, cout, *refs)

    return list(_pcall(body, name=name, in_specs=[ANY] * n_in, out_specs=[ANY] * n_out,
                       out_shape=[s for p in parts for s in p.outs], scratch_shapes=_comm_scratch(parts),
                       input_output_aliases=_comm_aliases(parts, 0, 0))(*[a for p in parts for a in p.ins]))


def _slab(wg, kind, chip, half):
    if kind == "in":
        d, ns = wg.shape[0], wg.shape[1] // N_CHIPS
        return _c(_r(wg, half * (d // 2), d // 2), chip * ns, ns)
    rs = wg.shape[0] // N_CHIPS
    return _r(wg, chip * rs + half * (rs // 2), rs // 2)


def _gather_ici(wg, kind):
    def plan(ins, outs, pos):
        x, y, c, chips = pos
        (ref,) = outs
        mine = _slab(ref, kind, 2 * x + y, c)
        return [(mine, mine, (*chip, c)) for chip in chips], [_slab(ref, kind, 2 * px + py, c) for px, py in chips], []

    return _part([wg], [jax.ShapeDtypeStruct(wg.shape, wg.dtype)], plan, 3, aliased=1)


def _gather_d2d(wg, kind):
    def plan(ins, outs, pos):
        x, y, c, chips = pos
        (ref,) = outs
        sends = [(_slab(ref, kind, 2 * px + py, c), _slab(ref, kind, 2 * px + py, c), (x, y, 1 - c)) for px, py in chips]
        return sends, [_slab(ref, kind, 2 * px + py, 1 - c) for px, py in chips], []

    return _part([wg], [jax.ShapeDtypeStruct(wg.shape, wg.dtype)], plan, 3, aliased=1)


def _pair_send(gw, kind):
    rows, cols = gw.shape
    half = (rows // 2, cols) if kind == "in" else (rows, cols // 2)

    def plan(ins, outs, pos):
        x, y, c, _ = pos
        (src,), (rb,) = ins, outs
        theirs = _r(src, (1 - c) * half[0], half[0]) if kind == "in" else _c(src, (1 - c) * half[1], half[1])
        return [(theirs, rb, (x, y, 1 - c))], [rb], []

    return _part([gw], [jax.ShapeDtypeStruct(half, gw.dtype)], plan, 1)


def _chip_send(p, kind):
    rows, cols = p.shape
    shard = (rows, cols // N_CHIPS) if kind == "in" else (rows // N_CHIPS, cols)

    def plan(ins, outs, pos):
        x, y, c, chips = pos
        (src,), (q,) = ins, outs
        piece = lambda jk: _c(src, jk * shard[1], shard[1]) if kind == "in" else _r(src, jk * shard[0], shard[0])
        sends = [(piece(2 * px + py), q.at[kk], (px, py, c)) for kk, (px, py) in enumerate(chips)]
        return sends, [q.at[kk] for kk in range(3)], []

    return _part([p], [jax.ShapeDtypeStruct((3,) + shard, p.dtype)], plan, 3)


def _sibling_send(g, kind):
    rows, cols = g.shape

    def plan(ins, outs, pos):
        x, y, c, _ = pos
        (ref,) = outs
        half = (lambda h: _r(ref, h * (rows // 2), rows // 2)) if kind == "in" else (
            lambda h: _c(ref, h * (cols // 2), cols // 2))
        return [(half(c), half(c), (x, y, 1 - c))], [half(1 - c)], []

    return _part([g], [jax.ShapeDtypeStruct(g.shape, g.dtype)], plan, 1, aliased=1)


def _small_ici(block):
    def plan(ins, outs, pos):
        x, y, c, chips = pos
        (src,), (out,) = ins, outs
        mine = out.at[4 * x + 2 * y + c]
        peers = [(x, y, 1 - c)] + [(px, py, c) for px, py in chips]
        return [(src, mine, p) for p in peers], [out.at[4 * px + 2 * py + pc] for px, py, pc in peers], [(src, mine)]

    return _part([block], [jax.ShapeDtypeStruct((N_DEV,) + block.shape, block.dtype)], plan, 4, 1)


def _small_d2d(gathered):
    def plan(ins, outs, pos):
        x, y, c, chips = pos
        (out,) = outs
        sends = [(out.at[4 * px + 2 * py + c], out.at[4 * px + 2 * py + c], (x, y, 1 - c)) for px, py in chips]
        return sends, [out.at[4 * px + 2 * py + (1 - c)] for px, py in chips], []

    return _part([gathered], [jax.ShapeDtypeStruct(gathered.shape, gathered.dtype)], plan, 3, aliased=1)


def _gather_small(block, *, name):
    R, C = block.shape

    def body(x_ref, out_ref, send_sems, recv_sems, local_sem):
        x, y, c, chips = _position()
        me, sib = (x, y, c), (x, y, 1 - c)
        slot = lambda px, py, pc: out_ref.at[4 * px + 2 * py + pc]
        mine = pltpu.make_async_copy(x_ref, slot(*me), local_sem)
        mine.start()
        first = [_remote(x_ref, slot(*me), send_sems, recv_sems, 0, sib)]
        first += [_remote(x_ref, slot(*me), send_sems, recv_sems, 1 + kk, (*chip, c)) for kk, chip in enumerate(chips)]
        for cp in first:
            cp.start()
        passed = []
        for kk, chip in enumerate(chips):
            _remote(slot(*chip, c), slot(*chip, c), send_sems, recv_sems, 1 + kk, sib).wait_recv()
            passed.append(_remote(slot(*chip, c), slot(*chip, c), send_sems, recv_sems, 4 + kk, sib))
            passed[-1].start()
        _remote(slot(*sib), slot(*sib), send_sems, recv_sems, 0, sib).wait_recv()
        for kk, chip in enumerate(chips):
            _remote(slot(*chip, 1 - c), slot(*chip, 1 - c), send_sems, recv_sems, 4 + kk, sib).wait_recv()
        for cp in first + passed:
            cp.wait_send()
        mine.wait()

    return _pcall(
        body, name=name, in_specs=[ANY], out_specs=ANY, out_shape=jax.ShapeDtypeStruct((N_DEV, R, C), F32),
        scratch_shapes=[pltpu.SemaphoreType.DMA((7,)), pltpu.SemaphoreType.DMA((7,)), pltpu.SemaphoreType.DMA],
    )(block)


_SMALL = ["gate_r_w", "gate_i_w", "conv_a_w", "conv_c_w", "sinks", "conv_c_b", "gate_r_b", "gate_i_b", "rg_lambda",
          "norm_a", "norm_b", "norm_c", "ln_g", "ln_b"]


def _pack_small(p):
    L = p["ln_g"].shape[0]
    rows = []
    for n in _SMALL:
        a = p[n]
        if n in ("gate_r_w", "gate_i_w", "norm_b", "ln_g", "ln_b"):
            a = a.reshape(L, -1, 1024)
        elif a.ndim == 2:
            a = a[:, None, :]
        if a.shape[-1] < 1024:
            a = jnp.pad(a, ((0, 0), (0, 0), (0, 1024 - a.shape[-1])))
        rows.append(a)
    out = jnp.concatenate(rows, axis=1)
    assert out.shape[1] == SMALL_ROWS
    return out.reshape(L * SMALL_ROWS, 1024)


def _unpack_small(flat, like):
    L = like["ln_g"].shape[0]
    a = flat.reshape(L, SMALL_ROWS, 1024)
    out, r = {}, 0
    for n in _SMALL:
        shp = like[n].shape
        nrows = max(1, math.prod(shp[1:]) // 1024) if n in ("gate_r_w", "gate_i_w", "norm_b", "ln_g", "ln_b") else (
            shp[1] if len(shp) == 3 else 1)
        blk = a[:, r:r + nrows, :]
        if n in ("gate_r_w", "gate_i_w", "norm_b", "ln_g", "ln_b"):
            out[n] = blk.reshape(shp)
        elif len(shp) == 3:
            out[n] = blk[:, :, :shp[2]]
        else:
            out[n] = blk[:, 0, :shp[1]]
        r += nrows
    return out


def kernel(x, w_in, conv_a_w, sinks, conv_c_w, conv_c_b, gate_r_w, gate_r_b, gate_i_w, gate_i_b, rg_lambda, norm_a, norm_b, norm_c, w_out, ln_g, ln_b, loss_target, m_w_in, m_conv_a_w, m_sinks, m_conv_c_w, m_conv_c_b, m_gate_r_w, m_gate_r_b, m_gate_i_w, m_gate_i_b, m_rg_lambda, m_norm_a, m_norm_b, m_norm_c, m_w_out, m_ln_g, m_ln_b, v_w_in, v_conv_a_w, v_sinks, v_conv_c_w, v_conv_c_b, v_gate_r_w, v_gate_r_b, v_gate_i_w, v_gate_i_b, v_rg_lambda, v_norm_a, v_norm_b, v_norm_c, v_w_out, v_ln_g, v_ln_b):
    names = ["w_in", "conv_a_w", "sinks", "conv_c_w", "conv_c_b", "gate_r_w", "gate_r_b", "gate_i_w", "gate_i_b",
             "rg_lambda", "norm_a", "norm_b", "norm_c", "w_out", "ln_g", "ln_b"]
    w = dict(zip(names, [w_in, conv_a_w, sinks, conv_c_w, conv_c_b, gate_r_w, gate_r_b, gate_i_w, gate_i_b, rg_lambda,
                         norm_a, norm_b, norm_c, w_out, ln_g, ln_b]))
    mom = dict(zip(names, [m_w_in, m_conv_a_w, m_sinks, m_conv_c_w, m_conv_c_b, m_gate_r_w, m_gate_r_b, m_gate_i_w,
                           m_gate_i_b, m_rg_lambda, m_norm_a, m_norm_b, m_norm_c, m_w_out, m_ln_g, m_ln_b]))
    vel = dict(zip(names, [v_w_in, v_conv_a_w, v_sinks, v_conv_c_w, v_conv_c_b, v_gate_r_w, v_gate_r_b, v_gate_i_w,
                           v_gate_i_b, v_rg_lambda, v_norm_a, v_norm_b, v_norm_c, v_w_out, v_ln_g, v_ln_b]))
    B, S, D = x.shape
    T = B * S
    L, _, NS = w_in.shape
    RS = w_out.shape[1]
    W = D // 4
    alpha = (2.0 * L) ** 0.25
    tt = _tile(S, 128, 8)
    tm_row = _tile(T, 256, 8)
    chip = _my_chip()

    ws_in = [_cast_shard(w_in, l, "in", name="cast_w_in") for l in range(L)]
    ws_out = [_cast_shard(w_out, l, "out", name="cast_w_out") for l in range(L)]
    wg_in, wg_out = [None] * L, [None] * L
    part_in, part_out = _comm_call([_gather_ici(ws_in[0], "in"), _gather_ici(ws_out[0], "out")], name="gather0_ici")
    wg_in[0], wg_out[0] = _comm_call([_gather_d2d(part_in, "in"), _gather_d2d(part_out, "out")], name="gather0_d2d")
    conv_local = jnp.concatenate([conv_a_w, conv_c_w], axis=1).reshape(L * 7, W // N_CHIPS)
    conv_local = jnp.pad(conv_local, ((0, (-L * 7) % 8), (0, 0)))
    conv_all = _gather_small(conv_local, name="gather_conv")
    conv_full = jnp.concatenate([conv_all[2 * jj][:L * 7] for jj in range(N_CHIPS)], axis=1).reshape(L, 7, W)
    caw_full, ccw_full = conv_full[:, :3], conv_full[:, 3:]

    xf = x.reshape(T, D)
    xb = _cast_bf16(xf[None], 0, name="cast_x")
    saved = []
    for l in range(L):
        comm = ([_gather_ici(ws_in[l + 1], "in")] if l + 1 < L else []) + ([_gather_d2d(part_out, "out")] if l else [])
        res = _mm(xb, wg_in[l], mode="nn", out_dtype=F32, name="proj_in", tm=1024, tn=768, tk=4096, comm=comm)
        h = res if not comm else res.pop(0)
        if l + 1 < L:
            part_in = res.pop(0)
        if l:
            wg_out[l] = res.pop(0)
        pv = jnp.stack([conv_c_b[l], gate_r_b[l], gate_i_b[l], rg_lambda[l], norm_a[l], norm_c[l]])
        mix_ac, cv, xc, yc = _ac_fwd(h, caw_full[l], ccw_full[l], pv, gate_r_w[l], gate_i_w[l], S=S, D=D, tt=tt,
                                     name="ac_fwd")
        yb = _attn_fwd(h, sinks, l, S=S, D=D, name="attn_fwd")
        mix_b = _mixb_fwd(yb, h, norm_b[l][None], D=D, tm=tm_row, name="mixb_fwd")
        mix = _concat_cols([(mix_ac, 0, W), mix_b, (mix_ac, 1, W)], tm=tm_row, name="concat_mix")
        comm = [_gather_ici(ws_out[l + 1], "out"), _gather_d2d(part_in, "in")] if l + 1 < L else []
        res = _mm(mix, wg_out[l], mode="nn", out_dtype=F32, name="proj_out", tn=512, tk=4096, add=xf, add_scale=alpha,
                  comm=comm)
        z = res if not comm else res.pop(0)
        if l + 1 < L:
            part_out, wg_in[l + 1] = res
        saved.append((xb, h, cv, xc, yc, yb, mix, z, pv))
        xf, xb = _ln_fwd(z, ln_g[l][None], ln_b[l][None], tm=tm_row, name="ln_fwd")
    dxn, loss_part = _loss_head(xf, loss_target.reshape(T, D), tm=tm_row, name="loss_head")
    loss = lax.psum(loss_part[0, 0], ("x", "y", "c"))

    def final_sums(p_in, q_in, p_out, q_out):
        return (_final_sum(p_in, q_in, own_axis=1, out_shape=(D, NS), out_axis=0, name="final_sum_in"),
                _final_sum(p_out, q_out, own_axis=0, out_shape=(RS, D), out_axis=1, name="final_sum_out"))

    bufs_in = bufs_out = None
    small_g = [None] * L
    p_in = p_out = None
    for l in reversed(range(L)):
        up, last = l + 1 < L, l == 0
        xb_l, h, cv, xc, yc, yb, mix, z, pv = saved[l]
        dz, dzb, dgb = _ln_bwd(z, dxn, ln_g[l][None], tm=tm_row, name="ln_bwd")
        res = _mm(dzb, wg_out[l], mode="nt", out_dtype=F32, name="d_mix", tk=4096,
                  comm=[_chip_send(p_out, "out")] if up else [])
        dmix, q_out = res if up else (res, None)
        gw_out = _mm(mix, dzb, mode="tn", out_dtype=BF16, name="d_w_out", tk=4096)
        dha, dhc, vec, dwr, dwi = _ac_bwd(h, cv, xc, yc, dmix, caw_full[l], ccw_full[l], pv, gate_r_w[l], gate_i_w[l],
                                          S=S, D=D, tt=tt, name="ac_bwd")
        dyb, dbg, dnb = _mixb_bwd(yb, h, dmix, norm_b[l][None], D=D, tm=tm_row, name="mixb_bwd")
        dq, dk, dv, dsk = _attn_bwd(h, yb, dyb, sinks, l, S=S, D=D, name="attn_bwd")
        dh = _concat_cols([dha, dq, dk, dv, dbg, dhc], tm=tm_row, name="concat_dh")
        small_g[l] = dict(gate_r_w=dwr, gate_i_w=dwi, conv_a_w=vec[0:3], conv_c_w=vec[4:8], sinks=dsk[0, :2 * D // 256],
                          conv_c_b=vec[8], gate_r_b=vec[9], gate_i_b=vec[10], rg_lambda=vec[11], norm_a=vec[3],
                          norm_b=dnb[0], norm_c=vec[12], ln_g=dgb[0], ln_b=dgb[1])
        comm = [_pair_send(gw_out, "out")] + ([_chip_send(p_in, "in")] if up else [])
        if last:
            comm.append(_small_ici(_pack_small({n: jnp.stack([small_g[k][n] for k in range(L)]) for n in _SMALL})))
        res = _mm(xb_l, dh, mode="tn", out_dtype=BF16, name="d_w_in", tm=1024, tn=768, tk=4096, comm=comm)
        gw_in, rb_out = _take(res, 2)
        p_out_l = _pair_sum(gw_out, rb_out, half_axis=1, name="pair_sum_out")
        d2d = [_pair_send(gw_in, "in")]
        if up:
            g_in_half, g_out_half = final_sums(p_in, res.pop(0), p_out, q_out)
            d2d += [_sibling_send(g_in_half, "in"), _sibling_send(g_out_half, "out")]
        if last:
            res = _comm_call(d2d + [_small_d2d(res.pop(0))], name="tail_d2d")
            small_all = res.pop()
        else:
            res = _mm(dh, wg_in[l], mode="nt", out_dtype=F32, name="d_x", tk=2688, add=dz, add_scale=alpha, comm=d2d)
            dxn = res.pop(0)
        p_in_l = _pair_sum(gw_in, res.pop(0), half_axis=0, name="pair_sum_in")
        if up:
            bufs_in = _adamw_layer(res[0], w_in, m_w_in, v_w_in, l + 1, bufs_in, name="adamw_w_in")
            bufs_out = _adamw_layer(res[1], w_out, m_w_out, v_w_out, l + 1, bufs_out, name="adamw_w_out")
        p_in, p_out = p_in_l, p_out_l
    dxn, q_in, q_out = _mm(dh, wg_in[0], mode="nt", out_dtype=F32, name="d_x", tk=2688, add=dz, add_scale=alpha,
                           comm=[_chip_send(p_in, "in"), _chip_send(p_out, "out")])
    grad_x = dxn.reshape(B, S, D)
    g_in_half, g_out_half = final_sums(p_in, q_in, p_out, q_out)
    g_in0, g_out0 = _comm_call([_sibling_send(g_in_half, "in"), _sibling_send(g_out_half, "out")], name="sibling0")
    big = {"w_in": _adamw_layer(g_in0, w_in, m_w_in, v_w_in, 0, bufs_in, name="adamw_w_in"),
           "w_out": _adamw_layer(g_out0, w_out, m_w_out, v_w_out, 0, bufs_out, name="adamw_w_out")}

    like = {n: w[n] for n in _SMALL}
    like_full = dict(like, conv_a_w=caw_full, conv_c_w=ccw_full)
    g_small = _unpack_small(_sum_devices(small_all, name="sum_small"), like_full)
    for n in ("conv_a_w", "conv_c_w"):
        g_small[n] = lax.dynamic_slice_in_dim(g_small[n], chip * (W // N_CHIPS), W // N_CHIPS, axis=2)

    d_s, m_s, v_s = _adamw(_pack_small(g_small), _pack_small(like), _pack_small({n: mom[n] for n in _SMALL}),
                           _pack_small({n: vel[n] for n in _SMALL}), name="adamw_small")
    grads = dict(g_small)
    delta, new_m, new_v = _unpack_small(d_s, like), _unpack_small(m_s, like), _unpack_small(v_s, like)
    for n in ("w_in", "w_out"):
        grads[n], delta[n], new_m[n], new_v[n] = big[n]

    return (loss, grad_x, *[grads[n] for n in names], *[delta[n] for n in names], *[new_m[n] for n in names],
            *[new_v[n] for n in names])
```

```python
import functools
import math

import jax
import jax.numpy as jnp
from jax import lax
from jax.experimental import pallas as pl
from jax.experimental.pallas import tpu as pltpu

F32 = jnp.float32
BF16 = jnp.bfloat16
_MXU_DTYPE = jnp.bfloat16

HEAD_DIM = 64
KV_GROUP = 8
BLOCK = 128
N_RG_HEADS = 8
RG_C = 8.0
LN_EPS = 1e-5
RMS_EPS = 1e-6
NEG_INF = -1e30
ADAM_LR, ADAM_B1, ADAM_B2, ADAM_EPS, ADAM_WD, ADAM_STEP = 0.001, 0.9, 0.999, 1e-08, 0.01, 10
N_CHIPS = 4
N_DEV = 8
SMALL_ROWS = 280
VMEM_LIMIT = 56 * 1024 * 1024

MESH = pl.DeviceIdType.MESH
ANY = pl.BlockSpec(memory_space=pl.ANY)


def _pcall(body, *, name, **kw):
    return pl.pallas_call(body, name=name, **kw)


def _params(sem=None):
    return pltpu.CompilerParams(dimension_semantics=sem, vmem_limit_bytes=VMEM_LIMIT)


def _tile(dim, pref, mult=128):
    best = None
    for t in range(mult, min(dim, pref) + 1, mult):
        if dim % t == 0:
            best = t
    return best if best is not None else dim


def _dot(a, b, dims):
    return lax.dot_general(a.astype(_MXU_DTYPE), b.astype(_MXU_DTYPE), (dims, ((), ())),
                           preferred_element_type=F32)


NN = ((1,), (0,))
NT = ((1,), (1,))
TN = ((0,), (0,))


def _mm(a, b, *, mode, out_dtype, name, tm=1024, tn=1024, tk=512, add=None, add_scale=1.0, comm=()):
    if mode == "nn":
        (M, K), N = a.shape, b.shape[1]
    elif mode == "nt":
        (M, K), N = a.shape, b.shape[0]
    else:
        (K, M), N = a.shape, b.shape[1]
    tm, tn, tk = _tile(M, tm), _tile(N, tn), _tile(K, tk)
    ni, nj, nk = M // tm, N // tn, K // tk
    dims = {"nn": NN, "nt": NT, "tn": TN}[mode]
    n_cin = sum(len(p.ins) for p in comm)
    n_cout = sum(len(p.outs) for p in comm)

    def body(*refs):
        refs = list(refs)
        a_ref, b_ref = _take(refs, 2)
        add_ref = refs.pop(0) if add is not None else None
        cin = _take(refs, n_cin)
        o_ref = refs.pop(0)
        cout = _take(refs, n_cout)
        acc = refs.pop(0) if nk > 1 else None
        i, j, k = pl.program_id(0), pl.program_id(1), pl.program_id(2)

        if comm:
            @pl.when((i == 0) & (j == 0) & (k == 0))
            def _():
                _comm_run(comm, "start", cin, cout, *refs)

        def finish(r):
            if add_ref is not None:
                r = r + add_scale * add_ref[...]
            o_ref[...] = r.astype(out_dtype)

        if nk == 1:
            finish(_dot(a_ref[...], b_ref[...], dims))
        else:
            @pl.when(k == 0)
            def _():
                acc[...] = jnp.zeros_like(acc)

            acc[...] += _dot(a_ref[...], b_ref[...], dims)

            @pl.when(k == nk - 1)
            def _():
                finish(acc[...])

        if comm:
            @pl.when((i == ni - 1) & (j == nj - 1) & (k == nk - 1))
            def _():
                _comm_run(comm, "finish", cin, cout, *refs)

    a_spec = {"nn": pl.BlockSpec((tm, tk), lambda i, j, k: (i, k)),
              "nt": pl.BlockSpec((tm, tk), lambda i, j, k: (i, k)),
              "tn": pl.BlockSpec((tk, tm), lambda i, j, k: (k, i))}[mode]
    b_spec = {"nn": pl.BlockSpec((tk, tn), lambda i, j, k: (k, j)),
              "nt": pl.BlockSpec((tn, tk), lambda i, j, k: (j, k)),
              "tn": pl.BlockSpec((tk, tn), lambda i, j, k: (k, j))}[mode]
    in_specs, operands = [a_spec, b_spec], [a, b]
    if add is not None:
        in_specs.append(pl.BlockSpec((tm, tn), lambda i, j, k: (i, j)))
        operands.append(add)
    aliases = _comm_aliases(comm, len(operands), 1)
    in_specs += [ANY] * n_cin
    operands += [arr for p in comm for arr in p.ins]
    out_shape = [jax.ShapeDtypeStruct((M, N), out_dtype)] + [s for p in comm for s in p.outs]
    out_specs = [pl.BlockSpec((tm, tn), lambda i, j, k: (i, j))] + [ANY] * n_cout
    sem = ("arbitrary",) * 3 if comm else ("parallel", "parallel", "arbitrary")
    res = _pcall(body, name=name, out_shape=out_shape, grid=(ni, nj, nk), in_specs=in_specs, out_specs=out_specs,
                 scratch_shapes=([pltpu.VMEM((tm, tn), F32)] if nk > 1 else []) + _comm_scratch(comm),
                 input_output_aliases=aliases,
                 compiler_params=_params(sem))(*operands)
    return list(res) if comm else res[0]


def _colspecs(off, width, rows, rowmap):
    bw = math.gcd(off, width) if off else width
    specs = [pl.BlockSpec((rows, bw), functools.partial(lambda cb, *g: (rowmap(*g), cb), off // bw + i))
             for i in range(width // bw)]
    return specs, bw


def _cat(refs):
    vals = [r[...] for r in refs]
    return vals[0] if len(vals) == 1 else jnp.concatenate(vals, axis=1)


def _take(refs, n):
    out = refs[:n]
    del refs[:n]
    return out


def _sigmoid(x):
    return 1.0 / (1.0 + jnp.exp(-x))


def _rms(y, gamma):
    rstd = lax.rsqrt(jnp.mean(y * y, axis=-1, keepdims=True) + RMS_EPS)
    xn = y * rstd
    return xn, rstd, xn * gamma


def _rms_bwd(dn, xn, rstd, gamma):
    dng = dn * gamma
    return rstd * (dng - xn * jnp.mean(dng * xn, axis=-1, keepdims=True))


def _shift_down(x, s, carry8):
    rolled = pltpu.roll(x, s, 0)
    cr = pltpu.roll(carry8, s, 0)
    row8 = lax.broadcasted_iota(jnp.int32, carry8.shape, 0)
    top = jnp.where(row8 < s, cr, rolled[0:8])
    return jnp.concatenate([top, rolled[8:]], axis=0)


def _shift_up(x, s, carry8):
    n = x.shape[0]
    rolled = pltpu.roll(x, n - s, 0)
    cr = pltpu.roll(carry8, 8 - s, 0)
    row8 = lax.broadcasted_iota(jnp.int32, carry8.shape, 0)
    bot = jnp.where(row8 >= 8 - s, cr, rolled[n - 8:])
    return jnp.concatenate([rolled[:n - 8], bot], axis=0)


def _chunk_scan(a, b):
    n = a.shape[0]
    r8 = lax.broadcasted_iota(jnp.int32, a.shape, 0) & 7
    for d in (1, 2, 4):
        ok = r8 >= d
        a_sh = jnp.where(ok, pltpu.roll(a, d, 0), 1.0)
        b_sh = jnp.where(ok, pltpu.roll(b, d, 0), 0.0)
        b = a * b_sh + b
        a = a * a_sh
    return a, b


def _chunk_scan_rev(c, b):
    n = c.shape[0]
    r8 = lax.broadcasted_iota(jnp.int32, c.shape, 0) & 7
    for d in (1, 2, 4):
        ok = r8 + d <= 7
        c_sh = jnp.where(ok, pltpu.roll(c, n - d, 0), 1.0)
        b_sh = jnp.where(ok, pltpu.roll(b, n - d, 0), 0.0)
        b = b + c * b_sh
        c = c * c_sh
    return c, b


def _log1p(x):
    w = 1.0 + x
    return jnp.where(w == 1.0, x, jnp.log(w) * (x / (w - 1.0)))


def _log_sigmoid(x):
    return jnp.minimum(x, 0.0) - _log1p(jnp.exp(-jnp.abs(x)))


def _expm1(x):
    u = jnp.exp(x)
    lu = jnp.log(u)
    small = jnp.where(u == 1.0, x, (u - 1.0) * (x / jnp.where(lu == 0.0, 1.0, lu)))
    return jnp.where(jnp.abs(x) < 0.5, small, u - 1.0)


def _gates(xc, wr_ref, wi_ref, br, bi, lam):
    hw = xc.shape[1] // N_RG_HEADS
    gr = jnp.concatenate([_dot(xc[:, h * hw:(h + 1) * hw], wr_ref[h], NN) for h in range(N_RG_HEADS)], axis=1) + br
    gi = jnp.concatenate([_dot(xc[:, h * hw:(h + 1) * hw], wi_ref[h], NN) for h in range(N_RG_HEADS)], axis=1) + bi
    r, i = _sigmoid(gr), _sigmoid(gi)
    ls = _log_sigmoid(lam)
    la = RG_C * r * ls
    a = jnp.exp(la)
    sq = jnp.sqrt(-_expm1(2.0 * la))
    return r, i, ls, a, sq


def _ac_fwd(h, caw, ccw, pv, wr, wi, *, S, D, tt, name):
    T = h.shape[0]
    W = D // 4
    nt = S // tt
    rowmap = lambda s, t: s * nt + t
    c_off = D + D // 2 + 2 * (D // 16) + D // 2
    offs = [0, W, 2 * W, 3 * W, c_off, c_off + W]
    in_specs, counts = [], []
    for off in offs:
        specs, _ = _colspecs(off, W, tt, rowmap)
        in_specs += specs
        counts.append(len(specs))
    full = lambda shape: pl.BlockSpec(shape, lambda s, t: (0,) * len(shape))
    in_specs += [full(caw.shape), full(ccw.shape), full(pv.shape), full(wr.shape), full(wi.shape)]

    def body(*refs):
        refs = list(refs)
        ab, ac, ax, ag, cx, cg = [_cat(_take(refs, n)) for n in counts]
        caw_ref, ccw_ref, pv_ref, wr_ref, wi_ref = _take(refs, 5)
        mixac_ref, cv_ref, xc_ref, yc_ref = _take(refs, 4)
        carry_p, carry_cx, carry_h, a_s, b_s = refs
        t = pl.program_id(1)

        @pl.when(t == 0)
        def _():
            carry_p[...] = jnp.zeros_like(carry_p)
            carry_cx[...] = jnp.zeros_like(carry_cx)
            carry_h[...] = jnp.zeros_like(carry_h)

        ccb, br, bi, lam, na, nc = [pv_ref[k:k + 1, :] for k in range(6)]
        p = ac * ax
        cp = carry_p[...]
        cv = caw_ref[2:3, :] * p + caw_ref[1:2, :] * _shift_down(p, 1, cp) + caw_ref[0:1, :] * _shift_down(p, 2, cp)
        carry_p[...] = p[tt - 8:tt]
        cv_ref[...] = cv
        _, _, n_a = _rms(ab * cv, na)
        mix_a = n_a * (ag * _sigmoid(ag))
        ccx = carry_cx[...]
        xc = (ccw_ref[3:4, :] * cx + ccw_ref[2:3, :] * _shift_down(cx, 1, ccx) + ccw_ref[1:2, :] * _shift_down(cx, 2, ccx)
              + ccw_ref[0:1, :] * _shift_down(cx, 3, ccx) + ccb)
        carry_cx[...] = cx[tt - 8:tt]
        xc_ref[...] = xc
        r, i, ls, a, sq = _gates(xc, wr_ref, wi_ref, br, bi, lam)
        u = sq * (i * xc)
        a_c, b_c = _chunk_scan(a, u)
        a_s[...] = a_c
        b_s[...] = b_c

        def step(k, hprev):
            rows = pl.ds(pl.multiple_of(k * 8, 8), 8)
            hc = a_s[rows, :] * hprev + b_s[rows, :]
            yc_ref[rows, :] = hc
            return hc[7:8, :]

        hlast = lax.fori_loop(0, tt // 8, step, carry_h[0:1, :])
        carry_h[...] = jnp.broadcast_to(hlast, carry_h.shape)
        _, _, n_c = _rms(yc_ref[...], nc)
        mix_c = n_c * (cg * _sigmoid(cg))
        mixac_ref[...] = jnp.concatenate([mix_a, mix_c], axis=1).astype(mixac_ref.dtype)

    row_blk = lambda w: pl.BlockSpec((tt, w), lambda s, t: (rowmap(s, t), 0))
    return _pcall(
        body, name=name, grid=(T // S, nt), in_specs=in_specs,
        out_shape=(jax.ShapeDtypeStruct((T, 2 * W), BF16), jax.ShapeDtypeStruct((T, W), F32),
                   jax.ShapeDtypeStruct((T, W), F32), jax.ShapeDtypeStruct((T, W), F32)),
        out_specs=(row_blk(2 * W), row_blk(W), row_blk(W), row_blk(W)),
        scratch_shapes=[pltpu.VMEM((8, W), F32), pltpu.VMEM((8, W), F32), pltpu.VMEM((8, W), F32),
                        pltpu.VMEM((tt, W), F32), pltpu.VMEM((tt, W), F32)],
        compiler_params=_params(("arbitrary", "arbitrary")),
    )(*([h] * sum(counts)), caw, ccw, pv, wr, wi)


def _ac_bwd(h, cv, xc, yc, dmix, caw, ccw, pv, wr, wi, *, S, D, tt, name):
    T = h.shape[0]
    W = D // 4
    nt = S // tt
    rowmap = lambda s, t: s * nt + (nt - 1 - t)
    c_off = D + D // 2 + 2 * (D // 16) + D // 2
    offs = [0, W, 2 * W, 3 * W, c_off, c_off + W]
    in_specs, counts = [], []
    for off in offs:
        specs, _ = _colspecs(off, W, tt, rowmap)
        in_specs += specs
        counts.append(len(specs))
    row_blk = lambda w, cb=0: pl.BlockSpec((tt, w), lambda s, t: (rowmap(s, t), cb))
    in_specs += [row_blk(W), row_blk(W), row_blk(W)]
    in_specs.append(pl.BlockSpec((8, W), lambda s, t: (jnp.maximum(rowmap(s, t) * (tt // 8) - 1, 0), 0)))
    in_specs += [row_blk(W, 0), row_blk(W, 3)]
    full = lambda shape: pl.BlockSpec(shape, lambda s, t: (0,) * len(shape))
    in_specs += [full(caw.shape), full(ccw.shape), full(pv.shape), full(wr.shape), full(wi.shape)]

    def body(*refs):
        refs = list(refs)
        ab, ac, ax, ag, cx, cg = [_cat(_take(refs, n)) for n in counts]
        cv_ref, xc_ref, yc_ref, halo_ref, dma_ref, dmc_ref, caw_ref, ccw_ref, pv_ref, wr_ref, wi_ref = _take(refs, 11)
        dha_ref, dhc_ref, vec_ref, dwr_ref, dwi_ref = _take(refs, 5)
        carry_dcv, carry_dxc, carry_a, carry_g, c_s, b_s, g_s = refs
        s_id, t = pl.program_id(0), pl.program_id(1)

        @pl.when(t == 0)
        def _():
            for cr in (carry_dcv, carry_dxc, carry_a, carry_g):
                cr[...] = jnp.zeros_like(cr)

        @pl.when((t == 0) & (s_id == 0))
        def _():
            vec_ref[...] = jnp.zeros_like(vec_ref)
            dwr_ref[...] = jnp.zeros_like(dwr_ref)
            dwi_ref[...] = jnp.zeros_like(dwi_ref)

        def acc_row(k, val):
            vec_ref[k:k + 1, :] += jnp.sum(val, axis=0, keepdims=True)

        ccb, br, bi, lam, na, nc = [pv_ref[k:k + 1, :] for k in range(6)]
        cv = cv_ref[...]
        dmix_a = dma_ref[...]
        p = ac * ax
        xn, rstd, n_a = _rms(ab * cv, na)
        sg = _sigmoid(ag)
        dn = dmix_a * (ag * sg)
        dag = dmix_a * n_a * (sg * (1.0 + ag * (1.0 - sg)))
        acc_row(3, dn * xn)
        dya = _rms_bwd(dn, xn, rstd, na)
        dab = dya * cv
        dcv = dya * ab
        cd = carry_dcv[...]
        d1, d2 = _shift_up(dcv, 1, cd), _shift_up(dcv, 2, cd)
        dp = caw_ref[2:3, :] * dcv + caw_ref[1:2, :] * d1 + caw_ref[0:1, :] * d2
        acc_row(2, p * dcv)
        acc_row(1, p * d1)
        acc_row(0, p * d2)
        carry_dcv[...] = dcv[0:8]
        dha_ref[...] = jnp.concatenate([dab, dp * ax, dp * ac, dag], axis=1).astype(dha_ref.dtype)
        xc = xc_ref[...]
        yc = yc_ref[...]
        dmix_c = dmc_ref[...]
        xn, rstd, n_c = _rms(yc, nc)
        sg = _sigmoid(cg)
        dn = dmix_c * (cg * sg)
        dcg = dmix_c * n_c * (sg * (1.0 + cg * (1.0 - sg)))
        acc_row(12, dn * xn)
        dyc = _rms_bwd(dn, xn, rstd, nc)
        r, i, ls, a, sq = _gates(xc, wr_ref, wi_ref, br, bi, lam)
        halo = jnp.where(t == nt - 1, 0.0, halo_ref[...])
        hprev = _shift_down(yc, 1, halo)
        c_c, b_c = _chunk_scan_rev(_shift_up(a, 1, carry_a[...]), dyc)
        c_s[...] = c_c
        b_s[...] = b_c

        def step(k, gnext):
            rows = pl.ds(pl.multiple_of((tt // 8 - 1 - k) * 8, 8), 8)
            gc = b_s[rows, :] + c_s[rows, :] * gnext
            g_s[rows, :] = gc
            return gc[0:1, :]

        lax.fori_loop(0, tt // 8, step, carry_g[0:1, :])
        g = g_s[...]
        carry_g[...] = g[0:8]
        carry_a[...] = a[0:8]
        da = g * hprev
        ixc = i * xc
        dsq = g * ixc
        di = g * sq * xc
        dxc = g * sq * i
        dla = da * a - dsq * (a * a) / sq
        dr = dla * (RG_C * ls)
        acc_row(11, dla * (RG_C * r) * _sigmoid(-lam))
        dgr = dr * r * (1.0 - r)
        dgi = di * i * (1.0 - i)
        acc_row(9, dgr)
        acc_row(10, dgi)
        hw = W // N_RG_HEADS
        parts = []
        for hd in range(N_RG_HEADS):
            sl = slice(hd * hw, (hd + 1) * hw)
            dwr_ref[hd] += _dot(xc[:, sl], dgr[:, sl], TN)
            dwi_ref[hd] += _dot(xc[:, sl], dgi[:, sl], TN)
            parts.append(_dot(dgr[:, sl], wr_ref[hd], NT) + _dot(dgi[:, sl], wi_ref[hd], NT))
        dxc = dxc + jnp.concatenate(parts, axis=1)
        ce = carry_dxc[...]
        e1, e2, e3 = _shift_up(dxc, 1, ce), _shift_up(dxc, 2, ce), _shift_up(dxc, 3, ce)
        dcx = ccw_ref[3:4, :] * dxc + ccw_ref[2:3, :] * e1 + ccw_ref[1:2, :] * e2 + ccw_ref[0:1, :] * e3
        acc_row(7, cx * dxc)
        acc_row(6, cx * e1)
        acc_row(5, cx * e2)
        acc_row(4, cx * e3)
        acc_row(8, dxc)
        carry_dxc[...] = dxc[0:8]
        dhc_ref[...] = jnp.concatenate([dcx, dcg], axis=1).astype(dhc_ref.dtype)

    const = lambda shape: pl.BlockSpec(shape, lambda s, t: (0,) * len(shape))
    return _pcall(
        body, name=name, grid=(T // S, nt), in_specs=in_specs,
        out_shape=(jax.ShapeDtypeStruct((T, 4 * W), BF16), jax.ShapeDtypeStruct((T, 2 * W), BF16),
                   jax.ShapeDtypeStruct((16, W), F32), jax.ShapeDtypeStruct(wr.shape, F32),
                   jax.ShapeDtypeStruct(wi.shape, F32)),
        out_specs=(row_blk(4 * W), row_blk(2 * W), const((16, W)), const(wr.shape), const(wi.shape)),
        scratch_shapes=[pltpu.VMEM((8, W), F32)] * 4 + [pltpu.VMEM((tt, W), F32)] * 3,
        compiler_params=_params(("arbitrary", "arbitrary")),
    )(*([h] * sum(counts)), cv, xc, yc, yc, dmix, dmix, caw, ccw, pv, wr, wi)


def _lo_mask():
    return lax.broadcasted_iota(jnp.int32, (1, 2 * HEAD_DIM), 1) < HEAD_DIM


def _dup(blk, odd, lo):
    rot = pltpu.roll(blk, HEAD_DIM, 1)
    return jnp.where(lo, rot, blk) if odd else jnp.where(lo, blk, rot)


def _stack_heads(x, hh, lo, masked):
    parts = []
    for g in range(KV_GROUP):
        jq = hh * KV_GROUP + g
        pb = x[:, (jq // 2) * 128:(jq // 2 + 1) * 128]
        if masked:
            pb = jnp.where(lo if jq % 2 == 0 else jnp.logical_not(lo), pb, 0.0)
        parts.append(pb)
    return jnp.concatenate(parts, axis=0)


def _unstack_pairs_t(st_t):
    hi = lax.broadcasted_iota(jnp.int32, (2 * HEAD_DIM, BLOCK), 0) >= HEAD_DIM
    return [jnp.where(hi, st_t[:, (2 * pi + 1) * BLOCK:(2 * pi + 2) * BLOCK], st_t[:, (2 * pi) * BLOCK:(2 * pi + 1) * BLOCK]).T
            for pi in range(KV_GROUP // 2)]


def _window(ref, n):
    prev = ref[pl.ds(pl.multiple_of(jnp.maximum(n - 1, 0) * BLOCK, BLOCK), BLOCK), :]
    cur = ref[pl.ds(pl.multiple_of(n * BLOCK, BLOCK), BLOCK), :]
    return jnp.concatenate([prev, cur], axis=0)


def _valid_mask_t(n):
    cols = KV_GROUP * BLOCK
    kj = lax.broadcasted_iota(jnp.int32, (2 * BLOCK, cols), 0)
    qi = lax.broadcasted_iota(jnp.int32, (2 * BLOCK, cols), 1) & (BLOCK - 1)
    dist = qi + BLOCK - kj
    return (dist >= 0) & (dist < BLOCK) & ((n > 0) | (kj >= BLOCK))


def _sink_row(sinks_ref, layer, hh):
    return jnp.concatenate([jnp.full((1, BLOCK), sinks_ref[layer, hh * KV_GROUP + g], F32) for g in range(KV_GROUP)],
                           axis=1)


def _softmax_t(qs, kdup, valid, sink):
    s = jnp.where(valid, _dot(kdup, qs, NT), NEG_INF)
    m = jnp.maximum(jnp.max(s, axis=0, keepdims=True), sink)
    e = jnp.exp(s - m)
    es = jnp.exp(sink - m)
    r = 1.0 / (jnp.sum(e, axis=0, keepdims=True) + es)
    return e * r, es * r


def _lane_sums_row(x):
    hi = x.astype(BF16)
    lo = (x - hi.astype(F32)).astype(BF16)
    ones = jnp.ones((8, x.shape[1]), BF16)
    dims = (NT, ((), ()))
    return (lax.dot_general(ones, hi, dims, preferred_element_type=F32)
            + lax.dot_general(ones, lo, dims, preferred_element_type=F32))[0:1]


def _attn_fwd(h, sinks, layer, *, S, D, name):
    T = h.shape[0]
    WB, KVW = D // 2, D // 16
    nb = S // BLOCK
    n_kv = KVW // HEAD_DIM

    def body(q_ref, k_ref, v_ref, sinks_ref, o_ref):
        n = pl.program_id(1)
        lo = _lo_mask()
        q = q_ref[...] * (HEAD_DIM ** -0.5)
        kk, vv = _window(k_ref, n), _window(v_ref, n)
        valid = _valid_mask_t(n)
        blocks = []
        for hh in range(n_kv):
            cb = slice((hh // 2) * 128, (hh // 2 + 1) * 128)
            kdup, vdup = _dup(kk[:, cb], hh % 2, lo), _dup(vv[:, cb], hh % 2, lo)
            p_t, _ = _softmax_t(_stack_heads(q, hh, lo, True), kdup, valid, _sink_row(sinks_ref, layer, hh))
            blocks += _unstack_pairs_t(_dot(vdup, p_t, TN))
        o_ref[...] = jnp.concatenate(blocks, axis=1)

    return _pcall(
        body, name=name, grid=(T // S, nb),
        in_specs=[pl.BlockSpec((BLOCK, WB), lambda s, n: (s * nb + n, D // WB)),
                  pl.BlockSpec((S, KVW), lambda s, n: (s, (D + WB) // KVW)),
                  pl.BlockSpec((S, KVW), lambda s, n: (s, (D + WB) // KVW + 1)),
                  pl.BlockSpec(memory_space=pltpu.SMEM)],
        out_shape=jax.ShapeDtypeStruct((T, WB), F32),
        out_specs=pl.BlockSpec((BLOCK, WB), lambda s, n: (s * nb + n, 0)),
        compiler_params=_params(("arbitrary", "arbitrary")),
    )(h, h, h, sinks)


def _attn_bwd(h, yb, dyb, sinks, layer, *, S, D, name):
    T = h.shape[0]
    WB, KVW = D // 2, D // 16
    nb = S // BLOCK
    n_kv = KVW // HEAD_DIM

    def body(q_ref, k_ref, v_ref, o_ref, do_ref, sinks_ref, dq_ref, dk_ref, dv_ref, dsink_ref, dk_acc, dv_acc):
        s_id, n = pl.program_id(0), pl.program_id(1)
        lo = _lo_mask()

        @pl.when(n == 0)
        def _():
            dk_acc[...] = jnp.zeros_like(dk_acc)
            dv_acc[...] = jnp.zeros_like(dv_acc)

        @pl.when((n == 0) & (s_id == 0))
        def _():
            dsink_ref[...] = jnp.zeros_like(dsink_ref)

        scale = HEAD_DIM ** -0.5
        q, o, do = q_ref[...] * scale, o_ref[...], do_ref[...]
        kk, vv = _window(k_ref, n), _window(v_ref, n)
        valid = _valid_mask_t(n)
        lane = lax.broadcasted_iota(jnp.int32, dsink_ref.shape, 1)
        dq_blocks, dk_heads, dv_heads = [], [], []
        dsink = jnp.zeros(dsink_ref.shape, F32)
        for hh in range(n_kv):
            cb = slice((hh // 2) * 128, (hh // 2 + 1) * 128)
            kdup, vdup = _dup(kk[:, cb], hh % 2, lo), _dup(vv[:, cb], hh % 2, lo)
            qs = _stack_heads(q, hh, lo, True)
            dos = _stack_heads(do, hh, lo, True)
            delta = _lane_sums_row(dos * _stack_heads(o, hh, lo, False))
            p_t, psink = _softmax_t(qs, kdup, valid, _sink_row(sinks_ref, layer, hh))
            dvr = _dot(p_t, dos, NN)
            dv_heads.append(dvr + pltpu.roll(dvr, HEAD_DIM, 1))
            ds_t = p_t * (_dot(vdup, dos, NT) - delta)
            dq_blocks += [b * scale for b in _unstack_pairs_t(_dot(kdup, ds_t, TN))]
            dkr = _dot(ds_t, qs, NN)
            dk_heads.append(dkr + pltpu.roll(dkr, HEAD_DIM, 1))
            dsk = -psink * delta
            for g in range(KV_GROUP):
                tot = jnp.sum(dsk[:, g * BLOCK:(g + 1) * BLOCK], axis=1, keepdims=True)
                dsink = dsink + jnp.where(lane == hh * KV_GROUP + g, tot, 0.0)
        dsink_ref[...] += dsink
        dq_ref[...] = jnp.concatenate(dq_blocks, axis=1).astype(dq_ref.dtype)
        pair = lambda hs: jnp.concatenate([jnp.where(lo, hs[2 * m], hs[2 * m + 1]) for m in range(n_kv // 2)], axis=1)
        dkk, dvv = pair(dk_heads), pair(dv_heads)
        prev = pl.ds(pl.multiple_of(jnp.maximum(n - 1, 0) * BLOCK, BLOCK), BLOCK)
        cur = pl.ds(pl.multiple_of(n * BLOCK, BLOCK), BLOCK)
        dk_acc[prev, :] += dkk[:BLOCK]
        dk_acc[cur, :] += dkk[BLOCK:]
        dv_acc[prev, :] += dvv[:BLOCK]
        dv_acc[cur, :] += dvv[BLOCK:]

        @pl.when(n == nb - 1)
        def _():
            dk_ref[...] = dk_acc[...].astype(dk_ref.dtype)
            dv_ref[...] = dv_acc[...].astype(dv_ref.dtype)

    blk = lambda cb=0: pl.BlockSpec((BLOCK, WB), lambda s, n: (s * nb + n, cb))
    seq = lambda cb=0: pl.BlockSpec((S, KVW), lambda s, n: (s, cb))
    return _pcall(
        body, name=name, grid=(T // S, nb),
        in_specs=[blk(D // WB), seq((D + WB) // KVW), seq((D + WB) // KVW + 1), blk(), blk(),
                  pl.BlockSpec(memory_space=pltpu.SMEM)],
        out_shape=(jax.ShapeDtypeStruct((T, WB), BF16), jax.ShapeDtypeStruct((T, KVW), BF16),
                   jax.ShapeDtypeStruct((T, KVW), BF16), jax.ShapeDtypeStruct((8, 128), F32)),
        out_specs=(blk(), seq(), seq(), pl.BlockSpec((8, 128), lambda s, n: (0, 0))),
        scratch_shapes=[pltpu.VMEM((S, KVW), F32), pltpu.VMEM((S, KVW), F32)],
        compiler_params=_params(("arbitrary", "arbitrary")),
    )(h, h, h, yb, dyb, sinks)


def _bg_specs(D, tm):
    return _colspecs(D + D // 2 + 2 * (D // 16), D // 2, tm, lambda i: i)


def _mixb_fwd(yb, h, nb_g, *, D, tm, name):
    T, WB = yb.shape
    bg_specs, _ = _bg_specs(D, tm)

    def body(*refs):
        refs = list(refs)
        yb_ref = refs.pop(0)
        bg = _cat(_take(refs, len(bg_specs)))
        g_ref, o_ref = refs
        _, _, nrm = _rms(yb_ref[...], g_ref[...])
        o_ref[...] = (nrm * (bg * _sigmoid(bg))).astype(o_ref.dtype)

    row = pl.BlockSpec((tm, WB), lambda i: (i, 0))
    return _pcall(body, name=name, grid=(T // tm,),
                  in_specs=[row] + bg_specs + [pl.BlockSpec((1, WB), lambda i: (0, 0))],
                  out_shape=jax.ShapeDtypeStruct((T, WB), BF16), out_specs=row,
                  compiler_params=_params(("arbitrary",)))(yb, *([h] * len(bg_specs)), nb_g)


def _mixb_bwd(yb, h, dmix, nb_g, *, D, tm, name):
    T, WB = yb.shape
    W = D // 4
    bg_specs, _ = _bg_specs(D, tm)
    dm_specs, _ = _colspecs(W, WB, tm, lambda i: i)

    def body(*refs):
        refs = list(refs)
        yb_ref = refs.pop(0)
        bg = _cat(_take(refs, len(bg_specs)))
        dmix_b = _cat(_take(refs, len(dm_specs)))
        g_ref, dyb_ref, dbg_ref, dg_ref = refs

        @pl.when(pl.program_id(0) == 0)
        def _():
            dg_ref[...] = jnp.zeros_like(dg_ref)

        gamma = g_ref[...]
        xn, rstd, nrm = _rms(yb_ref[...], gamma)
        sg = _sigmoid(bg)
        dn = dmix_b * (bg * sg)
        dbg_ref[...] = (dmix_b * nrm * (sg * (1.0 + bg * (1.0 - sg)))).astype(dbg_ref.dtype)
        dg_ref[0:1, :] += jnp.sum(dn * xn, axis=0, keepdims=True)
        dyb_ref[...] = _rms_bwd(dn, xn, rstd, gamma)

    row = pl.BlockSpec((tm, WB), lambda i: (i, 0))
    return _pcall(body, name=name, grid=(T // tm,),
                  in_specs=[row] + bg_specs + dm_specs + [pl.BlockSpec((1, WB), lambda i: (0, 0))],
                  out_shape=(jax.ShapeDtypeStruct((T, WB), F32), jax.ShapeDtypeStruct((T, WB), BF16),
                             jax.ShapeDtypeStruct((8, WB), F32)),
                  out_specs=(row, row, pl.BlockSpec((8, WB), lambda i: (0, 0))),
                  compiler_params=_params(("arbitrary",)))(yb, *([h] * len(bg_specs)), *([dmix] * len(dm_specs)), nb_g)


def _concat_cols(parts, *, tm, name):
    parts = [p if isinstance(p, tuple) else (p, 0, p.shape[1]) for p in parts]
    T = parts[0][0].shape[0]
    total = sum(w for _, _, w in parts)

    def body(*refs):
        refs[-1][...] = jnp.concatenate([r[...] for r in refs[:-1]], axis=1)

    return _pcall(body, name=name, grid=(T // tm,),
                  in_specs=[pl.BlockSpec((tm, w), functools.partial(lambda cb, i: (i, cb), cb)) for _, cb, w in parts],
                  out_shape=jax.ShapeDtypeStruct((T, total), parts[0][0].dtype),
                  out_specs=pl.BlockSpec((tm, total), lambda i: (i, 0)),
                  compiler_params=_params(("parallel",)))(*[a for a, _, _ in parts])


def _ln_fwd(z, g, b, *, tm, name):
    T, D = z.shape

    def body(z_ref, g_ref, b_ref, y_ref, yb_ref):
        zv = z_ref[...]
        mu = jnp.mean(zv, axis=-1, keepdims=True)
        zc = zv - mu
        var = jnp.mean(zc * zc, axis=-1, keepdims=True)
        y = zc * lax.rsqrt(var + LN_EPS) * g_ref[...] + b_ref[...]
        y_ref[...] = y
        yb_ref[...] = y.astype(BF16)

    row = pl.BlockSpec((tm, D), lambda i: (i, 0))
    vec = pl.BlockSpec((1, D), lambda i: (0, 0))
    return _pcall(body, name=name, grid=(T // tm,), in_specs=[row, vec, vec],
                  out_shape=(jax.ShapeDtypeStruct((T, D), F32), jax.ShapeDtypeStruct((T, D), BF16)),
                  out_specs=(row, row), compiler_params=_params(("parallel",)))(z, g, b)


def _ln_bwd(z, dy, g, *, tm, name):
    T, D = z.shape

    def body(z_ref, dy_ref, g_ref, dz_ref, dzb_ref, dgb_ref):
        @pl.when(pl.program_id(0) == 0)
        def _():
            dgb_ref[...] = jnp.zeros_like(dgb_ref)

        zv, dyv = z_ref[...], dy_ref[...]
        mu = jnp.mean(zv, axis=-1, keepdims=True)
        zc = zv - mu
        rstd = lax.rsqrt(jnp.mean(zc * zc, axis=-1, keepdims=True) + LN_EPS)
        xh = zc * rstd
        dxh = dyv * g_ref[...]
        dz = rstd * (dxh - jnp.mean(dxh, axis=-1, keepdims=True) - xh * jnp.mean(dxh * xh, axis=-1, keepdims=True))
        dz_ref[...] = dz
        dzb_ref[...] = dz.astype(BF16)
        dgb_ref[0:1, :] += jnp.sum(dyv * xh, axis=0, keepdims=True)
        dgb_ref[1:2, :] += jnp.sum(dyv, axis=0, keepdims=True)

    row = pl.BlockSpec((tm, D), lambda i: (i, 0))
    return _pcall(body, name=name, grid=(T // tm,), in_specs=[row, row, pl.BlockSpec((1, D), lambda i: (0, 0))],
                  out_shape=(jax.ShapeDtypeStruct((T, D), F32), jax.ShapeDtypeStruct((T, D), BF16),
                             jax.ShapeDtypeStruct((8, D), F32)),
                  out_specs=(row, row, pl.BlockSpec((8, D), lambda i: (0, 0))),
                  compiler_params=_params(("arbitrary",)))(z, dy, g)


def _loss_head(y, target, *, tm, name):
    T, D = y.shape

    def body(y_ref, t_ref, dy_ref, loss_ref):
        @pl.when(pl.program_id(0) == 0)
        def _():
            loss_ref[...] = jnp.zeros_like(loss_ref)

        err = y_ref[...] - t_ref[...]
        dy_ref[...] = err / D
        loss_ref[...] += 0.5 * jnp.sum(jnp.mean(err * err, axis=-1, keepdims=True), axis=0, keepdims=True)

    row = pl.BlockSpec((tm, D), lambda i: (i, 0))
    return _pcall(body, name=name, grid=(T // tm,), in_specs=[row, row],
                  out_shape=(jax.ShapeDtypeStruct((T, D), F32), jax.ShapeDtypeStruct((1, 1), F32)),
                  out_specs=(row, pl.BlockSpec((1, 1), lambda i: (0, 0))),
                  compiler_params=_params(("arbitrary",)))(y, target)


def _cast_bf16(w, layer, *, name):
    _, R, C = w.shape
    tr = _tile(R, 512, 8)

    def body(w_ref, o_ref):
        o_ref[...] = w_ref[...].astype(BF16)

    return _pcall(body, name=name, grid=(R // tr,), in_specs=[pl.BlockSpec((None, tr, C), lambda i: (layer, i, 0))],
                  out_shape=jax.ShapeDtypeStruct((R, C), BF16), out_specs=pl.BlockSpec((tr, C), lambda i: (i, 0)),
                  compiler_params=_params(("parallel",)))(w)


def _cast_shard(w, layer, kind, *, name):
    _, R, C = w.shape
    tr = _tile(R, 512, 16)
    nrb = R // tr
    if kind == "in":
        full, o_idx = (R, N_CHIPS * C), lambda i: (i, _my_chip())
    else:
        full, o_idx = (N_CHIPS * R, C), lambda i: (_my_chip() * nrb + i, 0)

    def body(w_ref, o_ref):
        o_ref[...] = w_ref[...].astype(BF16)

    return _pcall(body, name=name, grid=(nrb,), in_specs=[pl.BlockSpec((None, tr, C), lambda i: (layer, i, 0))],
                  out_shape=jax.ShapeDtypeStruct(full, BF16), out_specs=pl.BlockSpec((tr, C), o_idx),
                  compiler_params=_params(("parallel",)))(w)


def _adamw_layer(g, w, m, v, layer, bufs, *, name):
    L, R, C = w.shape
    tr = _tile(R, max(8, (1 << 19) // C // 8 * 8), 8)
    if bufs is None:
        bufs = [lax.empty((L, R, C), F32) for _ in range(4)]

    def body(g_ref, w_ref, m_ref, v_ref, b0, b1, b2, b3, go_ref, d_ref, nm_ref, nv_ref):
        gv = g_ref[...]
        nm = ADAM_B1 * m_ref[...] + (1.0 - ADAM_B1) * gv
        nv = ADAM_B2 * v_ref[...] + (1.0 - ADAM_B2) * (gv * gv)
        m_hat = nm / (1.0 - ADAM_B1 ** ADAM_STEP)
        v_hat = nv / (1.0 - ADAM_B2 ** ADAM_STEP)
        go_ref[...] = gv
        d_ref[...] = -ADAM_LR * (m_hat / (jnp.sqrt(v_hat) + ADAM_EPS) + ADAM_WD * w_ref[...])
        nm_ref[...] = nm
        nv_ref[...] = nv

    lay = pl.BlockSpec((None, tr, C), lambda i: (layer, i, 0))
    shp = jax.ShapeDtypeStruct((L, R, C), F32)
    return list(_pcall(body, name=name, grid=(R // tr,),
                       in_specs=[pl.BlockSpec((tr, C), lambda i: (i, 0)), lay, lay, lay] + [ANY] * 4,
                       out_shape=(shp,) * 4, out_specs=(lay,) * 4, input_output_aliases={4 + k: k for k in range(4)},
                       compiler_params=_params(("parallel",)))(g, w, m, v, *bufs))


def _adamw(g, w, m, v, *, name):
    R, C = g.shape
    tr = _tile(R, max(8, (1 << 19) // C // 8 * 8), 8)

    def body(g_ref, w_ref, m_ref, v_ref, d_ref, nm_ref, nv_ref):
        gv = g_ref[...]
        nm = ADAM_B1 * m_ref[...] + (1.0 - ADAM_B1) * gv
        nv = ADAM_B2 * v_ref[...] + (1.0 - ADAM_B2) * (gv * gv)
        m_hat = nm / (1.0 - ADAM_B1 ** ADAM_STEP)
        v_hat = nv / (1.0 - ADAM_B2 ** ADAM_STEP)
        d_ref[...] = -ADAM_LR * (m_hat / (jnp.sqrt(v_hat) + ADAM_EPS) + ADAM_WD * w_ref[...])
        nm_ref[...] = nm
        nv_ref[...] = nv

    blk = pl.BlockSpec((tr, C), lambda i: (i, 0))
    shp = jax.ShapeDtypeStruct((R, C), F32)
    return _pcall(body, name=name, grid=(R // tr,), in_specs=[blk] * 4, out_shape=(shp, shp, shp),
                  out_specs=(blk, blk, blk), compiler_params=_params(("parallel",)))(g, w, m, v)


def _my_core():
    return lax.axis_index("c")


def _my_chip():
    return 2 * lax.axis_index("x") + lax.axis_index("y")


def _pair_sum(mine, theirs, *, half_axis, name):
    R, C = theirs.shape
    tr, tc = _tile(R, 512, 16), _tile(C, 2048)
    nrb, ncb = R // tr, C // tc

    def body(a_ref, b_ref, o_ref):
        o_ref[...] = (a_ref[...].astype(F32) + b_ref[...].astype(F32)).astype(BF16)

    if half_axis == 0:
        a_idx = lambda i, j: (_my_core() * nrb + i, j)
    else:
        a_idx = lambda i, j: (i, _my_core() * ncb + j)
    blk = pl.BlockSpec((tr, tc), lambda i, j: (i, j))
    return _pcall(body, name=name, grid=(nrb, ncb), in_specs=[pl.BlockSpec((tr, tc), a_idx), blk], out_specs=blk,
                  out_shape=jax.ShapeDtypeStruct(theirs.shape, BF16),
                  compiler_params=_params(("parallel", "parallel")))(mine, theirs)


def _final_sum(own, got, *, own_axis, out_shape, out_axis, name):
    _, R, C = got.shape
    tr, tc = _tile(R, 512, 16), _tile(C, 1024)
    nrb, ncb = R // tr, C // tc

    def body(a_ref, q_ref, o_ref):
        o_ref[...] = ((a_ref[...].astype(F32) + q_ref[0].astype(F32)) + q_ref[1].astype(F32)) + q_ref[2].astype(F32)

    if own_axis == 1:
        a_idx = lambda i, j: (i, _my_chip() * ncb + j)
    else:
        a_idx = lambda i, j: (_my_chip() * nrb + i, j)
    if out_axis == 0:
        o_idx = lambda i, j: (_my_core() * nrb + i, j)
    else:
        o_idx = lambda i, j: (i, _my_core() * ncb + j)
    return _pcall(body, name=name, grid=(nrb, ncb),
                  in_specs=[pl.BlockSpec((tr, tc), a_idx), pl.BlockSpec((3, tr, tc), lambda i, j: (0, i, j))],
                  out_specs=pl.BlockSpec((tr, tc), o_idx), out_shape=jax.ShapeDtypeStruct(out_shape, F32),
                  compiler_params=_params(("parallel", "parallel")))(own, got)


def _sum_devices(gathered, *, name):
    _, R, C = gathered.shape
    tr = _tile(R, 280, 8)

    def body(g_ref, o_ref):
        acc = g_ref[0]
        for d in range(1, N_DEV):
            acc = acc + g_ref[d]
        o_ref[...] = acc

    return _pcall(body, name=name, grid=(R // tr,), in_specs=[pl.BlockSpec((N_DEV, tr, C), lambda i: (0, i, 0))],
                  out_shape=jax.ShapeDtypeStruct((R, C), F32), out_specs=pl.BlockSpec((tr, C), lambda i: (i, 0)),
                  compiler_params=_params(("parallel",)))(gathered)


def _position():
    x, y, c = lax.axis_index("x"), lax.axis_index("y"), lax.axis_index("c")
    chips = [(1 - x, y), (x, 1 - y), (1 - x, 1 - y)]
    return x, y, c, chips


def _remote(src, dst, send_sems, recv_sems, k, to):
    return pltpu.make_async_remote_copy(src_ref=src, dst_ref=dst, send_sem=send_sems.at[k], recv_sem=recv_sems.at[k],
                                        device_id=to, device_id_type=MESH)


def _r(ref, start, n):
    return ref.at[pl.ds(pl.multiple_of(start, 16), n), :]


def _c(ref, start, n):
    return ref.at[:, pl.ds(pl.multiple_of(start, 128), n)]


class _part:
    def __init__(self, ins, outs, plan, n, n_local=0, aliased=0):
        self.ins, self.outs, self.plan, self.n, self.n_local, self.aliased = ins, outs, plan, n, n_local, aliased


def _comm_scratch(parts):
    if not parts:
        return []
    n, nl = sum(p.n for p in parts), sum(p.n_local for p in parts)
    return [pltpu.SemaphoreType.DMA((n,)), pltpu.SemaphoreType.DMA((n,)), pltpu.SemaphoreType.DMA((max(nl, 1),))]


def _comm_aliases(parts, in_base, out_base):
    aliases, ii, oi = {}, in_base, out_base
    for p in parts:
        aliases.update({ii + k: oi + k for k in range(p.aliased)})
        ii += len(p.ins)
        oi += len(p.outs)
    return aliases


def _comm_run(parts, phase, in_refs, out_refs, send_sems, recv_sems, local_sems):
    pos = _position()
    me = pos[:3]
    ii = oi = si = li = 0
    for p in parts:
        sends, recvs, locs = p.plan(in_refs[ii:ii + len(p.ins)], out_refs[oi:oi + len(p.outs)], pos)
        assert len(sends) == len(recvs) == p.n and len(locs) == p.n_local
        if phase == "start":
            for k, (src, dst) in enumerate(locs):
                pltpu.make_async_copy(src, dst, local_sems.at[li + k]).start()
            for k, (src, dst, to) in enumerate(sends):
                _remote(src, dst, send_sems, recv_sems, si + k, to).start()
        else:
            for k, dst in enumerate(recvs):
                _remote(dst, dst, send_sems, recv_sems, si + k, me).wait_recv()
            for k, (src, dst, to) in enumerate(sends):
                _remote(src, dst, send_sems, recv_sems, si + k, to).wait_send()
            for k, (src, dst) in enumerate(locs):
                pltpu.make_async_copy(src, dst, local_sems.at[li + k]).wait()
        ii, oi, si, li = ii + len(p.ins), oi + len(p.outs), si + p.n, li + p.n_local


def _comm_call(parts, *, name):
    n_in = sum(len(p.ins) for p in parts)
    n_out = sum(len(p.outs) for p in parts)

    def body(*refs):
        refs = list(refs)
        cin, cout = _take(refs, n_in), _take(refs, n_out)
        _comm_run(parts, "start", cin, cout, *refs)
        _comm_run(parts, "finish", cin, cout, *refs)

    return list(_pcall(body, name=name, in_specs=[ANY] * n_in, out_specs=[ANY] * n_out,
                       out_shape=[s for p in parts for s in p.outs], scratch_shapes=_comm_scratch(parts),
                       input_output_aliases=_comm_aliases(parts, 0, 0))(*[a for p in parts for a in p.ins]))


def _slab(wg, kind, chip, half):
    if kind == "in":
        d, ns = wg.shape[0], wg.shape[1] // N_CHIPS
        return _c(_r(wg, half * (d // 2), d // 2), chip * ns, ns)
    rs = wg.shape[0] // N_CHIPS
    return _r(wg, chip * rs + half * (rs // 2), rs // 2)


def _gather_ici(wg, kind):
    def plan(ins, outs, pos):
        x, y, c, chips = pos
        (ref,) = outs
        mine = _slab(ref, kind, 2 * x + y, c)
        return [(mine, mine, (*chip, c)) for chip in chips], [_slab(ref, kind, 2 * px + py, c) for px, py in chips], []

    return _part([wg], [jax.ShapeDtypeStruct(wg.shape, wg.dtype)], plan, 3, aliased=1)


def _gather_d2d(wg, kind):
    def plan(ins, outs, pos):
        x, y, c, chips = pos
        (ref,) = outs
        sends = [(_slab(ref, kind, 2 * px + py, c), _slab(ref, kind, 2 * px + py, c), (x, y, 1 - c)) for px, py in chips]
        return sends, [_slab(ref, kind, 2 * px + py, 1 - c) for px, py in chips], []

    return _part([wg], [jax.ShapeDtypeStruct(wg.shape, wg.dtype)], plan, 3, aliased=1)


def _pair_send(gw, kind):
    rows, cols = gw.shape
    half = (rows // 2, cols) if kind == "in" else (rows, cols // 2)

    def plan(ins, outs, pos):
        x, y, c, _ = pos
        (src,), (rb,) = ins, outs
        theirs = _r(src, (1 - c) * half[0], half[0]) if kind == "in" else _c(src, (1 - c) * half[1], half[1])
        return [(theirs, rb, (x, y, 1 - c))], [rb], []

    return _part([gw], [jax.ShapeDtypeStruct(half, gw.dtype)], plan, 1)


def _chip_send(p, kind):
    rows, cols = p.shape
    shard = (rows, cols // N_CHIPS) if kind == "in" else (rows // N_CHIPS, cols)

    def plan(ins, outs, pos):
        x, y, c, chips = pos
        (src,), (q,) = ins, outs
        piece = lambda jk: _c(src, jk * shard[1], shard[1]) if kind == "in" else _r(src, jk * shard[0], shard[0])
        sends = [(piece(2 * px + py), q.at[kk], (px, py, c)) for kk, (px, py) in enumerate(chips)]
        return sends, [q.at[kk] for kk in range(3)], []

    return _part([p], [jax.ShapeDtypeStruct((3,) + shard, p.dtype)], plan, 3)


def _sibling_send(g, kind):
    rows, cols = g.shape

    def plan(ins, outs, pos):
        x, y, c, _ = pos
        (ref,) = outs
        half = (lambda h: _r(ref, h * (rows // 2), rows // 2)) if kind == "in" else (
            lambda h: _c(ref, h * (cols // 2), cols // 2))
        return [(half(c), half(c), (x, y, 1 - c))], [half(1 - c)], []

    return _part([g], [jax.ShapeDtypeStruct(g.shape, g.dtype)], plan, 1, aliased=1)


def _small_ici(block):
    def plan(ins, outs, pos):
        x, y, c, chips = pos
        (src,), (out,) = ins, outs
        mine = out.at[4 * x + 2 * y + c]
        peers = [(x, y, 1 - c)] + [(px, py, c) for px, py in chips]
        return [(src, mine, p) for p in peers], [out.at[4 * px + 2 * py + pc] for px, py, pc in peers], [(src, mine)]

    return _part([block], [jax.ShapeDtypeStruct((N_DEV,) + block.shape, block.dtype)], plan, 4, 1)


def _small_d2d(gathered):
    def plan(ins, outs, pos):
        x, y, c, chips = pos
        (out,) = outs
        sends = [(out.at[4 * px + 2 * py + c], out.at[4 * px + 2 * py + c], (x, y, 1 - c)) for px, py in chips]
        return sends, [out.at[4 * px + 2 * py + (1 - c)] for px, py in chips], []

    return _part([gathered], [jax.ShapeDtypeStruct(gathered.shape, gathered.dtype)], plan, 3, aliased=1)


def _gather_small(block, *, name):
    R, C = block.shape

    def body(x_ref, out_ref, send_sems, recv_sems, local_sem):
        x, y, c, chips = _position()
        me, sib = (x, y, c), (x, y, 1 - c)
        slot = lambda px, py, pc: out_ref.at[4 * px + 2 * py + pc]
        mine = pltpu.make_async_copy(x_ref, slot(*me), local_sem)
        mine.start()
        first = [_remote(x_ref, slot(*me), send_sems, recv_sems, 0, sib)]
        first += [_remote(x_ref, slot(*me), send_sems, recv_sems, 1 + kk, (*chip, c)) for kk, chip in enumerate(chips)]
        for cp in first:
            cp.start()
        passed = []
        for kk, chip in enumerate(chips):
            _remote(slot(*chip, c), slot(*chip, c), send_sems, recv_sems, 1 + kk, sib).wait_recv()
            passed.append(_remote(slot(*chip, c), slot(*chip, c), send_sems, recv_sems, 4 + kk, sib))
            passed[-1].start()
        _remote(slot(*sib), slot(*sib), send_sems, recv_sems, 0, sib).wait_recv()
        for kk, chip in enumerate(chips):
            _remote(slot(*chip, 1 - c), slot(*chip, 1 - c), send_sems, recv_sems, 4 + kk, sib).wait_recv()
        for cp in first + passed:
            cp.wait_send()
        mine.wait()

    return _pcall(
        body, name=name, in_specs=[ANY], out_specs=ANY, out_shape=jax.ShapeDtypeStruct((N_DEV, R, C), F32),
        scratch_shapes=[pltpu.SemaphoreType.DMA((7,)), pltpu.SemaphoreType.DMA((7,)), pltpu.SemaphoreType.DMA],
    )(block)


_SMALL = ["gate_r_w", "gate_i_w", "conv_a_w", "conv_c_w", "sinks", "conv_c_b", "gate_r_b", "gate_i_b", "rg_lambda",
          "norm_a", "norm_b", "norm_c", "ln_g", "ln_b"]


def _pack_small(p):
    L = p["ln_g"].shape[0]
    rows = []
    for n in _SMALL:
        a = p[n]
        if n in ("gate_r_w", "gate_i_w", "norm_b", "ln_g", "ln_b"):
            a = a.reshape(L, -1, 1024)
        elif a.ndim == 2:
            a = a[:, None, :]
        if a.shape[-1] < 1024:
            a = jnp.pad(a, ((0, 0), (0, 0), (0, 1024 - a.shape[-1])))
        rows.append(a)
    out = jnp.concatenate(rows, axis=1)
    assert out.shape[1] == SMALL_ROWS
    return out.reshape(L * SMALL_ROWS, 1024)


def _unpack_small(flat, like):
    L = like["ln_g"].shape[0]
    a = flat.reshape(L, SMALL_ROWS, 1024)
    out, r = {}, 0
    for n in _SMALL:
        shp = like[n].shape
        nrows = max(1, math.prod(shp[1:]) // 1024) if n in ("gate_r_w", "gate_i_w", "norm_b", "ln_g", "ln_b") else (
            shp[1] if len(shp) == 3 else 1)
        blk = a[:, r:r + nrows, :]
        if n in ("gate_r_w", "gate_i_w", "norm_b", "ln_g", "ln_b"):
            out[n] = blk.reshape(shp)
        elif len(shp) == 3:
            out[n] = blk[:, :, :shp[2]]
        else:
            out[n] = blk[:, 0, :shp[1]]
        r += nrows
    return out


def kernel(x, w_in, conv_a_w, sinks, conv_c_w, conv_c_b, gate_r_w, gate_r_b, gate_i_w, gate_i_b, rg_lambda, norm_a, norm_b, norm_c, w_out, ln_g, ln_b, loss_target, m_w_in, m_conv_a_w, m_sinks, m_conv_c_w, m_conv_c_b, m_gate_r_w, m_gate_r_b, m_gate_i_w, m_gate_i_b, m_rg_lambda, m_norm_a, m_norm_b, m_norm_c, m_w_out, m_ln_g, m_ln_b, v_w_in, v_conv_a_w, v_sinks, v_conv_c_w, v_conv_c_b, v_gate_r_w, v_gate_r_b, v_gate_i_w, v_gate_i_b, v_rg_lambda, v_norm_a, v_norm_b, v_norm_c, v_w_out, v_ln_g, v_ln_b):
    names = ["w_in", "conv_a_w", "sinks", "conv_c_w", "conv_c_b", "gate_r_w", "gate_r_b", "gate_i_w", "gate_i_b",
             "rg_lambda", "norm_a", "norm_b", "norm_c", "w_out", "ln_g", "ln_b"]
    w = dict(zip(names, [w_in, conv_a_w, sinks, conv_c_w, conv_c_b, gate_r_w, gate_r_b, gate_i_w, gate_i_b, rg_lambda,
                         norm_a, norm_b, norm_c, w_out, ln_g, ln_b]))
    mom = dict(zip(names, [m_w_in, m_conv_a_w, m_sinks, m_conv_c_w, m_conv_c_b, m_gate_r_w, m_gate_r_b, m_gate_i_w,
                           m_gate_i_b, m_rg_lambda, m_norm_a, m_norm_b, m_norm_c, m_w_out, m_ln_g, m_ln_b]))
    vel = dict(zip(names, [v_w_in, v_conv_a_w, v_sinks, v_conv_c_w, v_conv_c_b, v_gate_r_w, v_gate_r_b, v_gate_i_w,
                           v_gate_i_b, v_rg_lambda, v_norm_a, v_norm_b, v_norm_c, v_w_out, v_ln_g, v_ln_b]))
    B, S, D = x.shape
    T = B * S
    L, _, NS = w_in.shape
    RS = w_out.shape[1]
    W = D // 4
    alpha = (2.0 * L) ** 0.25
    tt = _tile(S, 128, 8)
    tm_row = _tile(T, 256, 8)
    chip = _my_chip()

    ws_in = [_cast_shard(w_in, l, "in", name="cast_w_in") for l in range(L)]
    ws_out = [_cast_shard(w_out, l, "out", name="cast_w_out") for l in range(L)]
    wg_in, wg_out = [None] * L, [None] * L
    part_in, part_out = _comm_call([_gather_ici(ws_in[0], "in"), _gather_ici(ws_out[0], "out")], name="gather0_ici")
    wg_in[0], wg_out[0] = _comm_call([_gather_d2d(part_in, "in"), _gather_d2d(part_out, "out")], name="gather0_d2d")
    conv_local = jnp.concatenate([conv_a_w, conv_c_w], axis=1).reshape(L * 7, W // N_CHIPS)
    conv_local = jnp.pad(conv_local, ((0, (-L * 7) % 8), (0, 0)))
    conv_all = _gather_small(conv_local, name="gather_conv")
    conv_full = jnp.concatenate([conv_all[2 * jj][:L * 7] for jj in range(N_CHIPS)], axis=1).reshape(L, 7, W)
    caw_full, ccw_full = conv_full[:, :3], conv_full[:, 3:]

    xf = x.reshape(T, D)
    xb = _cast_bf16(xf[None], 0, name="cast_x")
    saved = []
    for l in range(L):
        comm = ([_gather_ici(ws_in[l + 1], "in")] if l + 1 < L else []) + ([_gather_d2d(part_out, "out")] if l else [])
        res = _mm(xb, wg_in[l], mode="nn", out_dtype=F32, name="proj_in", tm=1024, tn=768, tk=4096, comm=comm)
        h = res if not comm else res.pop(0)
        if l + 1 < L:
            part_in = res.pop(0)
        if l:
            wg_out[l] = res.pop(0)
        pv = jnp.stack([conv_c_b[l], gate_r_b[l], gate_i_b[l], rg_lambda[l], norm_a[l], norm_c[l]])
        mix_ac, cv, xc, yc = _ac_fwd(h, caw_full[l], ccw_full[l], pv, gate_r_w[l], gate_i_w[l], S=S, D=D, tt=tt,
                                     name="ac_fwd")
        yb = _attn_fwd(h, sinks, l, S=S, D=D, name="attn_fwd")
        mix_b = _mixb_fwd(yb, h, norm_b[l][None], D=D, tm=tm_row, name="mixb_fwd")
        mix = _concat_cols([(mix_ac, 0, W), mix_b, (mix_ac, 1, W)], tm=tm_row, name="concat_mix")
        comm = [_gather_ici(ws_out[l + 1], "out"), _gather_d2d(part_in, "in")] if l + 1 < L else []
        res = _mm(mix, wg_out[l], mode="nn", out_dtype=F32, name="proj_out", tn=512, tk=4096, add=xf, add_scale=alpha,
                  comm=comm)
        z = res if not comm else res.pop(0)
        if l + 1 < L:
            part_out, wg_in[l + 1] = res
        saved.append((xb, h, cv, xc, yc, yb, mix, z, pv))
        xf, xb = _ln_fwd(z, ln_g[l][None], ln_b[l][None], tm=tm_row, name="ln_fwd")
    dxn, loss_part = _loss_head(xf, loss_target.reshape(T, D), tm=tm_row, name="loss_head")
    loss = lax.psum(loss_part[0, 0], ("x", "y", "c"))

    def final_sums(p_in, q_in, p_out, q_out):
        return (_final_sum(p_in, q_in, own_axis=1, out_shape=(D, NS), out_axis=0, name="final_sum_in"),
                _final_sum(p_out, q_out, own_axis=0, out_shape=(RS, D), out_axis=1, name="final_sum_out"))

    bufs_in = bufs_out = None
    small_g = [None] * L
    p_in = p_out = None
    for l in reversed(range(L)):
        up, last = l + 1 < L, l == 0
        xb_l, h, cv, xc, yc, yb, mix, z, pv = saved[l]
        dz, dzb, dgb = _ln_bwd(z, dxn, ln_g[l][None], tm=tm_row, name="ln_bwd")
        res = _mm(dzb, wg_out[l], mode="nt", out_dtype=F32, name="d_mix", tk=4096,
                  comm=[_chip_send(p_out, "out")] if up else [])
        dmix, q_out = res if up else (res, None)
        gw_out = _mm(mix, dzb, mode="tn", out_dtype=BF16, name="d_w_out", tk=4096)
        dha, dhc, vec, dwr, dwi = _ac_bwd(h, cv, xc, yc, dmix, caw_full[l], ccw_full[l], pv, gate_r_w[l], gate_i_w[l],
                                          S=S, D=D, tt=tt, name="ac_bwd")
        dyb, dbg, dnb = _mixb_bwd(yb, h, dmix, norm_b[l][None], D=D, tm=tm_row, name="mixb_bwd")
        dq, dk, dv, dsk = _attn_bwd(h, yb, dyb, sinks, l, S=S, D=D, name="attn_bwd")
        dh = _concat_cols([dha, dq, dk, dv, dbg, dhc], tm=tm_row, name="concat_dh")
        small_g[l] = dict(gate_r_w=dwr, gate_i_w=dwi, conv_a_w=vec[0:3], conv_c_w=vec[4:8], sinks=dsk[0, :2 * D // 256],
                          conv_c_b=vec[8], gate_r_b=vec[9], gate_i_b=vec[10], rg_lambda=vec[11], norm_a=vec[3],
                          norm_b=dnb[0], norm_c=vec[12], ln_g=dgb[0], ln_b=dgb[1])
        comm = [_pair_send(gw_out, "out")] + ([_chip_send(p_in, "in")] if up else [])
        if last:
            comm.append(_small_ici(_pack_small({n: jnp.stack([small_g[k][n] for k in range(L)]) for n in _SMALL})))
        res = _mm(xb_l, dh, mode="tn", out_dtype=BF16, name="d_w_in", tm=1024, tn=768, tk=4096, comm=comm)
        gw_in, rb_out = _take(res, 2)
        p_out_l = _pair_sum(gw_out, rb_out, half_axis=1, name="pair_sum_out")
        d2d = [_pair_send(gw_in, "in")]
        if up:
            g_in_half, g_out_half = final_sums(p_in, res.pop(0), p_out, q_out)
            d2d += [_sibling_send(g_in_half, "in"), _sibling_send(g_out_half, "out")]
        if last:
            res = _comm_call(d2d + [_small_d2d(res.pop(0))], name="tail_d2d")
            small_all = res.pop()
        else:
            res = _mm(dh, wg_in[l], mode="nt", out_dtype=F32, name="d_x", tk=2688, add=dz, add_scale=alpha, comm=d2d)
            dxn = res.pop(0)
        p_in_l = _pair_sum(gw_in, res.pop(0), half_axis=0, name="pair_sum_in")
        if up:
            bufs_in = _adamw_layer(res[0], w_in, m_w_in, v_w_in, l + 1, bufs_in, name="adamw_w_in")
            bufs_out = _adamw_layer(res[1], w_out, m_w_out, v_w_out, l + 1, bufs_out, name="adamw_w_out")
        p_in, p_out = p_in_l, p_out_l
    dxn, q_in, q_out = _mm(dh, wg_in[0], mode="nt", out_dtype=F32, name="d_x", tk=2688, add=dz, add_scale=alpha,
                           comm=[_chip_send(p_in, "in"), _chip_send(p_out, "out")])
    grad_x = dxn.reshape(B, S, D)
    g_in_half, g_out_half = final_sums(p_in, q_in, p_out, q_out)
    g_in0, g_out0 = _comm_call([_sibling_send(g_in_half, "in"), _sibling_send(g_out_half, "out")], name="sibling0")
    big = {"w_in": _adamw_layer(g_in0, w_in, m_w_in, v_w_in, 0, bufs_in, name="adamw_w_in"),
           "w_out": _adamw_layer(g_out0, w_out, m_w_out, v_w_out, 0, bufs_out, name="adamw_w_out")}

    like = {n: w[n] for n in _SMALL}
    like_full = dict(like, conv_a_w=caw_full, conv_c_w=ccw_full)
    g_small = _unpack_small(_sum_devices(small_all, name="sum_small"), like_full)
    for n in ("conv_a_w", "conv_c_w"):
        g_small[n] = lax.dynamic_slice_in_dim(g_small[n], chip * (W // N_CHIPS), W // N_CHIPS, axis=2)

    d_s, m_s, v_s = _adamw(_pack_small(g_small), _pack_small(like), _pack_small({n: mom[n] for n in _SMALL}),
                           _pack_small({n: vel[n] for n in _SMALL}), name="adamw_small")
    grads = dict(g_small)
    delta, new_m, new_v = _unpack_small(d_s, like), _unpack_small(m_s, like), _unpack_small(v_s, like)
    for n in ("w_in", "w_out"):
        grads[n], delta[n], new_m[n], new_v[n] = big[n]

    return (loss, grad_x, *[grads[n] for n in names], *[delta[n] for n in names], *[new_m[n] for n in names],
            *[new_v[n] for n in names])
```

```python
import functools
import math

import jax
import jax.numpy as jnp
from jax import lax
from jax.experimental import pallas as pl
from jax.experimental.pallas import tpu as pltpu

F32 = jnp.float32
BF16 = jnp.bfloat16
_MXU_DTYPE = jnp.bfloat16

HEAD_DIM = 64
KV_GROUP = 8
BLOCK = 128
N_RG_HEADS = 8
RG_C = 8.0
LN_EPS = 1e-5
RMS_EPS = 1e-6
NEG_INF = -1e30
ADAM_LR, ADAM_B1, ADAM_B2, ADAM_EPS, ADAM_WD, ADAM_STEP = 0.001, 0.9, 0.999, 1e-08, 0.01, 10
N_CHIPS = 4
N_DEV = 8
SMALL_ROWS = 280
VMEM_LIMIT = 56 * 1024 * 1024

MESH = pl.DeviceIdType.MESH
ANY = pl.BlockSpec(memory_space=pl.ANY)


def _pcall(body, *, name, **kw):
    return pl.pallas_call(body, name=name, **kw)


def _params(sem=None):
    return pltpu.CompilerParams(dimension_semantics=sem, vmem_limit_bytes=VMEM_LIMIT)


def _tile(dim, pref, mult=128):
    best = None
    for t in range(mult, min(dim, pref) + 1, mult):
        if dim % t == 0:
            best = t
    return best if best is not None else dim


def _dot(a, b, dims):
    return lax.dot_general(a.astype(_MXU_DTYPE), b.astype(_MXU_DTYPE), (dims, ((), ())),
                           preferred_element_type=F32)


NN = ((1,), (0,))
NT = ((1,), (1,))
TN = ((0,), (0,))


def _mm(a, b, *, mode, out_dtype, name, tm=1024, tn=1024, tk=512, add=None, add_scale=1.0, comm=(), deps=()):
    if mode == "nn":
        (M, K), N = a.shape, b.shape[1]
    elif mode == "nt":
        (M, K), N = a.shape, b.shape[0]
    else:
        (K, M), N = a.shape, b.shape[1]
    tm, tn, tk = _tile(M, tm), _tile(N, tn), _tile(K, tk)
    ni, nj, nk = M // tm, N // tn, K // tk
    dims = {"nn": NN, "nt": NT, "tn": TN}[mode]
    n_cin = sum(len(p.ins) for p in comm)
    n_cout = sum(len(p.outs) for p in comm)

    def body(*refs):
        refs = list(refs)
        a_ref, b_ref = _take(refs, 2)
        add_ref = refs.pop(0) if add is not None else None
        _take(refs, len(deps))
        cin = _take(refs, n_cin)
        o_ref = refs.pop(0)
        cout = _take(refs, n_cout)
        acc = refs.pop(0) if nk > 1 else None
        i, j, k = pl.program_id(0), pl.program_id(1), pl.program_id(2)

        if comm:
            @pl.when((i == 0) & (j == 0) & (k == 0))
            def _():
                _comm_run(comm, "start", cin, cout, *refs)

        def finish(r):
            if add_ref is not None:
                r = r + add_scale * add_ref[...]
            o_ref[...] = r.astype(out_dtype)

        if nk == 1:
            finish(_dot(a_ref[...], b_ref[...], dims))
        else:
            @pl.when(k == 0)
            def _():
                acc[...] = jnp.zeros_like(acc)

            acc[...] += _dot(a_ref[...], b_ref[...], dims)

            @pl.when(k == nk - 1)
            def _():
                finish(acc[...])

        if comm:
            @pl.when((i == ni - 1) & (j == nj - 1) & (k == nk - 1))
            def _():
                _comm_run(comm, "finish", cin, cout, *refs)

    a_spec = {"nn": pl.BlockSpec((tm, tk), lambda i, j, k: (i, k)),
              "nt": pl.BlockSpec((tm, tk), lambda i, j, k: (i, k)),
              "tn": pl.BlockSpec((tk, tm), lambda i, j, k: (k, i))}[mode]
    b_spec = {"nn": pl.BlockSpec((tk, tn), lambda i, j, k: (k, j)),
              "nt": pl.BlockSpec((tn, tk), lambda i, j, k: (j, k)),
              "tn": pl.BlockSpec((tk, tn), lambda i, j, k: (k, j))}[mode]
    in_specs, operands = [a_spec, b_spec], [a, b]
    if add is not None:
        in_specs.append(pl.BlockSpec((tm, tn), lambda i, j, k: (i, j)))
        operands.append(add)
    in_specs += [ANY] * len(deps)
    operands += list(deps)
    aliases = _comm_aliases(comm, len(operands), 1)
    in_specs += [ANY] * n_cin
    operands += [arr for p in comm for arr in p.ins]
    out_shape = [jax.ShapeDtypeStruct((M, N), out_dtype)] + [s for p in comm for s in p.outs]
    out_specs = [pl.BlockSpec((tm, tn), lambda i, j, k: (i, j))] + [ANY] * n_cout
    sem = ("arbitrary",) * 3 if comm else ("parallel", "parallel", "arbitrary")
    res = _pcall(body, name=name, out_shape=out_shape, grid=(ni, nj, nk), in_specs=in_specs, out_specs=out_specs,
                 scratch_shapes=([pltpu.VMEM((tm, tn), F32)] if nk > 1 else []) + _comm_scratch(comm),
                 input_output_aliases=aliases,
                 compiler_params=_params(sem))(*operands)
    return list(res) if comm else res[0]


def _colspecs(off, width, rows, rowmap):
    bw = math.gcd(off, width) if off else width
    specs = [pl.BlockSpec((rows, bw), functools.partial(lambda cb, *g: (rowmap(*g), cb), off // bw + i))
             for i in range(width // bw)]
    return specs, bw


def _cat(refs):
    vals = [r[...] for r in refs]
    return vals[0] if len(vals) == 1 else jnp.concatenate(vals, axis=1)


def _take(refs, n):
    out = refs[:n]
    del refs[:n]
    return out


def _sigmoid(x):
    return 1.0 / (1.0 + jnp.exp(-x))


def _rms(y, gamma):
    rstd = lax.rsqrt(jnp.mean(y * y, axis=-1, keepdims=True) + RMS_EPS)
    xn = y * rstd
    return xn, rstd, xn * gamma


def _rms_bwd(dn, xn, rstd, gamma):
    dng = dn * gamma
    return rstd * (dng - xn * jnp.mean(dng * xn, axis=-1, keepdims=True))


def _shift_down(x, s, carry8):
    rolled = pltpu.roll(x, s, 0)
    cr = pltpu.roll(carry8, s, 0)
    row8 = lax.broadcasted_iota(jnp.int32, carry8.shape, 0)
    top = jnp.where(row8 < s, cr, rolled[0:8])
    return jnp.concatenate([top, rolled[8:]], axis=0)


def _shift_up(x, s, carry8):
    n = x.shape[0]
    rolled = pltpu.roll(x, n - s, 0)
    cr = pltpu.roll(carry8, 8 - s, 0)
    row8 = lax.broadcasted_iota(jnp.int32, carry8.shape, 0)
    bot = jnp.where(row8 >= 8 - s, cr, rolled[n - 8:])
    return jnp.concatenate([rolled[:n - 8], bot], axis=0)


def _chunk_scan(a, b):
    n = a.shape[0]
    r8 = lax.broadcasted_iota(jnp.int32, a.shape, 0) & 7
    for d in (1, 2, 4):
        ok = r8 >= d
        a_sh = jnp.where(ok, pltpu.roll(a, d, 0), 1.0)
        b_sh = jnp.where(ok, pltpu.roll(b, d, 0), 0.0)
        b = a * b_sh + b
        a = a * a_sh
    return a, b


def _chunk_scan_rev(c, b):
    n = c.shape[0]
    r8 = lax.broadcasted_iota(jnp.int32, c.shape, 0) & 7
    for d in (1, 2, 4):
        ok = r8 + d <= 7
        c_sh = jnp.where(ok, pltpu.roll(c, n - d, 0), 1.0)
        b_sh = jnp.where(ok, pltpu.roll(b, n - d, 0), 0.0)
        b = b + c * b_sh
        c = c * c_sh
    return c, b


def _log1p(x):
    w = 1.0 + x
    return jnp.where(w == 1.0, x, jnp.log(w) * (x / (w - 1.0)))


def _log_sigmoid(x):
    return jnp.minimum(x, 0.0) - _log1p(jnp.exp(-jnp.abs(x)))


def _expm1(x):
    u = jnp.exp(x)
    lu = jnp.log(u)
    small = jnp.where(u == 1.0, x, (u - 1.0) * (x / jnp.where(lu == 0.0, 1.0, lu)))
    return jnp.where(jnp.abs(x) < 0.5, small, u - 1.0)


def _gates(xc, wr_ref, wi_ref, br, bi, lam):
    hw = xc.shape[1] // N_RG_HEADS
    gr = jnp.concatenate([_dot(xc[:, h * hw:(h + 1) * hw], wr_ref[h], NN) for h in range(N_RG_HEADS)], axis=1) + br
    gi = jnp.concatenate([_dot(xc[:, h * hw:(h + 1) * hw], wi_ref[h], NN) for h in range(N_RG_HEADS)], axis=1) + bi
    r, i = _sigmoid(gr), _sigmoid(gi)
    ls = _log_sigmoid(lam)
    la = RG_C * r * ls
    a = jnp.exp(la)
    sq = jnp.sqrt(-_expm1(2.0 * la))
    return r, i, ls, a, sq


def _ac_fwd(h, caw, ccw, pv, wr, wi, *, S, D, tt, name):
    T = h.shape[0]
    W = D // 4
    nt = S // tt
    rowmap = lambda s, t: s * nt + t
    c_off = D + D // 2 + 2 * (D // 16) + D // 2
    offs = [0, W, 2 * W, 3 * W, c_off, c_off + W]
    in_specs, counts = [], []
    for off in offs:
        specs, _ = _colspecs(off, W, tt, rowmap)
        in_specs += specs
        counts.append(len(specs))
    full = lambda shape: pl.BlockSpec(shape, lambda s, t: (0,) * len(shape))
    in_specs += [full(caw.shape), full(ccw.shape), full(pv.shape), full(wr.shape), full(wi.shape)]

    def body(*refs):
        refs = list(refs)
        ab, ac, ax, ag, cx, cg = [_cat(_take(refs, n)) for n in counts]
        caw_ref, ccw_ref, pv_ref, wr_ref, wi_ref = _take(refs, 5)
        mixac_ref, cv_ref, xc_ref, yc_ref = _take(refs, 4)
        carry_p, carry_cx, carry_h, a_s, b_s = refs
        t = pl.program_id(1)

        @pl.when(t == 0)
        def _():
            carry_p[...] = jnp.zeros_like(carry_p)
            carry_cx[...] = jnp.zeros_like(carry_cx)
            carry_h[...] = jnp.zeros_like(carry_h)

        ccb, br, bi, lam, na, nc = [pv_ref[k:k + 1, :] for k in range(6)]
        p = ac * ax
        cp = carry_p[...]
        cv = caw_ref[2:3, :] * p + caw_ref[1:2, :] * _shift_down(p, 1, cp) + caw_ref[0:1, :] * _shift_down(p, 2, cp)
        carry_p[...] = p[tt - 8:tt]
        cv_ref[...] = cv
        _, _, n_a = _rms(ab * cv, na)
        mix_a = n_a * (ag * _sigmoid(ag))
        ccx = carry_cx[...]
        xc = (ccw_ref[3:4, :] * cx + ccw_ref[2:3, :] * _shift_down(cx, 1, ccx) + ccw_ref[1:2, :] * _shift_down(cx, 2, ccx)
              + ccw_ref[0:1, :] * _shift_down(cx, 3, ccx) + ccb)
        carry_cx[...] = cx[tt - 8:tt]
        xc_ref[...] = xc
        r, i, ls, a, sq = _gates(xc, wr_ref, wi_ref, br, bi, lam)
        u = sq * (i * xc)
        a_c, b_c = _chunk_scan(a, u)
        a_s[...] = a_c
        b_s[...] = b_c

        def step(k, hprev):
            rows = pl.ds(pl.multiple_of(k * 8, 8), 8)
            hc = a_s[rows, :] * hprev + b_s[rows, :]
            yc_ref[rows, :] = hc
            return hc[7:8, :]

        hlast = lax.fori_loop(0, tt // 8, step, carry_h[0:1, :])
        carry_h[...] = jnp.broadcast_to(hlast, carry_h.shape)
        _, _, n_c = _rms(yc_ref[...], nc)
        mix_c = n_c * (cg * _sigmoid(cg))
        mixac_ref[...] = jnp.concatenate([mix_a, mix_c], axis=1).astype(mixac_ref.dtype)

    row_blk = lambda w: pl.BlockSpec((tt, w), lambda s, t: (rowmap(s, t), 0))
    return _pcall(
        body, name=name, grid=(T // S, nt), in_specs=in_specs,
        out_shape=(jax.ShapeDtypeStruct((T, 2 * W), BF16), jax.ShapeDtypeStruct((T, W), F32),
                   jax.ShapeDtypeStruct((T, W), F32), jax.ShapeDtypeStruct((T, W), F32)),
        out_specs=(row_blk(2 * W), row_blk(W), row_blk(W), row_blk(W)),
        scratch_shapes=[pltpu.VMEM((8, W), F32), pltpu.VMEM((8, W), F32), pltpu.VMEM((8, W), F32),
                        pltpu.VMEM((tt, W), F32), pltpu.VMEM((tt, W), F32)],
        compiler_params=_params(("arbitrary", "arbitrary")),
    )(*([h] * sum(counts)), caw, ccw, pv, wr, wi)


def _ac_bwd(h, cv, xc, yc, dmix, caw, ccw, pv, wr, wi, *, S, D, tt, name):
    T = h.shape[0]
    W = D // 4
    nt = S // tt
    rowmap = lambda s, t: s * nt + (nt - 1 - t)
    c_off = D + D // 2 + 2 * (D // 16) + D // 2
    offs = [0, W, 2 * W, 3 * W, c_off, c_off + W]
    in_specs, counts = [], []
    for off in offs:
        specs, _ = _colspecs(off, W, tt, rowmap)
        in_specs += specs
        counts.append(len(specs))
    row_blk = lambda w, cb=0: pl.BlockSpec((tt, w), lambda s, t: (rowmap(s, t), cb))
    in_specs += [row_blk(W), row_blk(W), row_blk(W)]
    in_specs.append(pl.BlockSpec((8, W), lambda s, t: (jnp.maximum(rowmap(s, t) * (tt // 8) - 1, 0), 0)))
    in_specs += [row_blk(W, 0), row_blk(W, 3)]
    full = lambda shape: pl.BlockSpec(shape, lambda s, t: (0,) * len(shape))
    in_specs += [full(caw.shape), full(ccw.shape), full(pv.shape), full(wr.shape), full(wi.shape)]

    def body(*refs):
        refs = list(refs)
        ab, ac, ax, ag, cx, cg = [_cat(_take(refs, n)) for n in counts]
        cv_ref, xc_ref, yc_ref, halo_ref, dma_ref, dmc_ref, caw_ref, ccw_ref, pv_ref, wr_ref, wi_ref = _take(refs, 11)
        dha_ref, dhc_ref, vec_ref, dwr_ref, dwi_ref = _take(refs, 5)
        carry_dcv, carry_dxc, carry_a, carry_g, c_s, b_s, g_s = refs
        s_id, t = pl.program_id(0), pl.program_id(1)

        @pl.when(t == 0)
        def _():
            for cr in (carry_dcv, carry_dxc, carry_a, carry_g):
                cr[...] = jnp.zeros_like(cr)

        @pl.when((t == 0) & (s_id == 0))
        def _():
            vec_ref[...] = jnp.zeros_like(vec_ref)
            dwr_ref[...] = jnp.zeros_like(dwr_ref)
            dwi_ref[...] = jnp.zeros_like(dwi_ref)

        def acc_row(k, val):
            vec_ref[k:k + 1, :] += jnp.sum(val, axis=0, keepdims=True)

        ccb, br, bi, lam, na, nc = [pv_ref[k:k + 1, :] for k in range(6)]
        cv = cv_ref[...]
        dmix_a = dma_ref[...]
        p = ac * ax
        xn, rstd, n_a = _rms(ab * cv, na)
        sg = _sigmoid(ag)
        dn = dmix_a * (ag * sg)
        dag = dmix_a * n_a * (sg * (1.0 + ag * (1.0 - sg)))
        acc_row(3, dn * xn)
        dya = _rms_bwd(dn, xn, rstd, na)
        dab = dya * cv
        dcv = dya * ab
        cd = carry_dcv[...]
        d1, d2 = _shift_up(dcv, 1, cd), _shift_up(dcv, 2, cd)
        dp = caw_ref[2:3, :] * dcv + caw_ref[1:2, :] * d1 + caw_ref[0:1, :] * d2
        acc_row(2, p * dcv)
        acc_row(1, p * d1)
        acc_row(0, p * d2)
        carry_dcv[...] = dcv[0:8]
        dha_ref[...] = jnp.concatenate([dab, dp * ax, dp * ac, dag], axis=1).astype(dha_ref.dtype)
        xc = xc_ref[...]
        yc = yc_ref[...]
        dmix_c = dmc_ref[...]
        xn, rstd, n_c = _rms(yc, nc)
        sg = _sigmoid(cg)
        dn = dmix_c * (cg * sg)
        dcg = dmix_c * n_c * (sg * (1.0 + cg * (1.0 - sg)))
        acc_row(12, dn * xn)
        dyc = _rms_bwd(dn, xn, rstd, nc)
        r, i, ls, a, sq = _gates(xc, wr_ref, wi_ref, br, bi, lam)
        halo = jnp.where(t == nt - 1, 0.0, halo_ref[...])
        hprev = _shift_down(yc, 1, halo)
        c_c, b_c = _chunk_scan_rev(_shift_up(a, 1, carry_a[...]), dyc)
        c_s[...] = c_c
        b_s[...] = b_c

        def step(k, gnext):
            rows = pl.ds(pl.multiple_of((tt // 8 - 1 - k) * 8, 8), 8)
            gc = b_s[rows, :] + c_s[rows, :] * gnext
            g_s[rows, :] = gc
            return gc[0:1, :]

        lax.fori_loop(0, tt // 8, step, carry_g[0:1, :])
        g = g_s[...]
        carry_g[...] = g[0:8]
        carry_a[...] = a[0:8]
        da = g * hprev
        ixc = i * xc
        dsq = g * ixc
        di = g * sq * xc
        dxc = g * sq * i
        dla = da * a - dsq * (a * a) / sq
        dr = dla * (RG_C * ls)
        acc_row(11, dla * (RG_C * r) * _sigmoid(-lam))
        dgr = dr * r * (1.0 - r)
        dgi = di * i * (1.0 - i)
        acc_row(9, dgr)
        acc_row(10, dgi)
        hw = W // N_RG_HEADS
        parts = []
        for hd in range(N_RG_HEADS):
            sl = slice(hd * hw, (hd + 1) * hw)
            dwr_ref[hd] += _dot(xc[:, sl], dgr[:, sl], TN)
            dwi_ref[hd] += _dot(xc[:, sl], dgi[:, sl], TN)
            parts.append(_dot(dgr[:, sl], wr_ref[hd], NT) + _dot(dgi[:, sl], wi_ref[hd], NT))
        dxc = dxc + jnp.concatenate(parts, axis=1)
        ce = carry_dxc[...]
        e1, e2, e3 = _shift_up(dxc, 1, ce), _shift_up(dxc, 2, ce), _shift_up(dxc, 3, ce)
        dcx = ccw_ref[3:4, :] * dxc + ccw_ref[2:3, :] * e1 + ccw_ref[1:2, :] * e2 + ccw_ref[0:1, :] * e3
        acc_row(7, cx * dxc)
        acc_row(6, cx * e1)
        acc_row(5, cx * e2)
        acc_row(4, cx * e3)
        acc_row(8, dxc)
        carry_dxc[...] = dxc[0:8]
        dhc_ref[...] = jnp.concatenate([dcx, dcg], axis=1).astype(dhc_ref.dtype)

    const = lambda shape: pl.BlockSpec(shape, lambda s, t: (0,) * len(shape))
    return _pcall(
        body, name=name, grid=(T // S, nt), in_specs=in_specs,
        out_shape=(jax.ShapeDtypeStruct((T, 4 * W), BF16), jax.ShapeDtypeStruct((T, 2 * W), BF16),
                   jax.ShapeDtypeStruct((16, W), F32), jax.ShapeDtypeStruct(wr.shape, F32),
                   jax.ShapeDtypeStruct(wi.shape, F32)),
        out_specs=(row_blk(4 * W), row_blk(2 * W), const((16, W)), const(wr.shape), const(wi.shape)),
        scratch_shapes=[pltpu.VMEM((8, W), F32)] * 4 + [pltpu.VMEM((tt, W), F32)] * 3,
        compiler_params=_params(("arbitrary", "arbitrary")),
    )(*([h] * sum(counts)), cv, xc, yc, yc, dmix, dmix, caw, ccw, pv, wr, wi)


def _lo_mask():
    return lax.broadcasted_iota(jnp.int32, (1, 2 * HEAD_DIM), 1) < HEAD_DIM


def _dup(blk, odd, lo):
    rot = pltpu.roll(blk, HEAD_DIM, 1)
    return jnp.where(lo, rot, blk) if odd else jnp.where(lo, blk, rot)


def _stack_heads(x, hh, lo, masked):
    parts = []
    for g in range(KV_GROUP):
        jq = hh * KV_GROUP + g
        pb = x[:, (jq // 2) * 128:(jq // 2 + 1) * 128]
        if masked:
            pb = jnp.where(lo if jq % 2 == 0 else jnp.logical_not(lo), pb, 0.0)
        parts.append(pb)
    return jnp.concatenate(parts, axis=0)


def _unstack_pairs_t(st_t):
    hi = lax.broadcasted_iota(jnp.int32, (2 * HEAD_DIM, BLOCK), 0) >= HEAD_DIM
    return [jnp.where(hi, st_t[:, (2 * pi + 1) * BLOCK:(2 * pi + 2) * BLOCK], st_t[:, (2 * pi) * BLOCK:(2 * pi + 1) * BLOCK]).T
            for pi in range(KV_GROUP // 2)]


def _window(ref, n):
    prev = ref[pl.ds(pl.multiple_of(jnp.maximum(n - 1, 0) * BLOCK, BLOCK), BLOCK), :]
    cur = ref[pl.ds(pl.multiple_of(n * BLOCK, BLOCK), BLOCK), :]
    return jnp.concatenate([prev, cur], axis=0)


def _valid_mask_t(n):
    cols = KV_GROUP * BLOCK
    kj = lax.broadcasted_iota(jnp.int32, (2 * BLOCK, cols), 0)
    qi = lax.broadcasted_iota(jnp.int32, (2 * BLOCK, cols), 1) & (BLOCK - 1)
    dist = qi + BLOCK - kj
    return (dist >= 0) & (dist < BLOCK) & ((n > 0) | (kj >= BLOCK))


def _sink_row(sinks_ref, layer, hh):
    return jnp.concatenate([jnp.full((1, BLOCK), sinks_ref[layer, hh * KV_GROUP + g], F32) for g in range(KV_GROUP)],
                           axis=1)


def _softmax_t(qs, kdup, valid, sink):
    s = jnp.where(valid, _dot(kdup, qs, NT), NEG_INF)
    m = jnp.maximum(jnp.max(s, axis=0, keepdims=True), sink)
    e = jnp.exp(s - m)
    es = jnp.exp(sink - m)
    r = 1.0 / (jnp.sum(e, axis=0, keepdims=True) + es)
    return e * r, es * r


def _lane_sums_row(x):
    hi = x.astype(BF16)
    lo = (x - hi.astype(F32)).astype(BF16)
    ones = jnp.ones((8, x.shape[1]), BF16)
    dims = (NT, ((), ()))
    return (lax.dot_general(ones, hi, dims, preferred_element_type=F32)
            + lax.dot_general(ones, lo, dims, preferred_element_type=F32))[0:1]


def _attn_fwd(h, sinks, layer, *, S, D, name):
    T = h.shape[0]
    WB, KVW = D // 2, D // 16
    nb = S // BLOCK
    n_kv = KVW // HEAD_DIM

    def body(q_ref, k_ref, v_ref, sinks_ref, o_ref):
        n = pl.program_id(1)
        lo = _lo_mask()
        q = q_ref[...] * (HEAD_DIM ** -0.5)
        kk, vv = _window(k_ref, n), _window(v_ref, n)
        valid = _valid_mask_t(n)
        blocks = []
        for hh in range(n_kv):
            cb = slice((hh // 2) * 128, (hh // 2 + 1) * 128)
            kdup, vdup = _dup(kk[:, cb], hh % 2, lo), _dup(vv[:, cb], hh % 2, lo)
            p_t, _ = _softmax_t(_stack_heads(q, hh, lo, True), kdup, valid, _sink_row(sinks_ref, layer, hh))
            blocks += _unstack_pairs_t(_dot(vdup, p_t, TN))
        o_ref[...] = jnp.concatenate(blocks, axis=1)

    return _pcall(
        body, name=name, grid=(T // S, nb),
        in_specs=[pl.BlockSpec((BLOCK, WB), lambda s, n: (s * nb + n, D // WB)),
                  pl.BlockSpec((S, KVW), lambda s, n: (s, (D + WB) // KVW)),
                  pl.BlockSpec((S, KVW), lambda s, n: (s, (D + WB) // KVW + 1)),
                  pl.BlockSpec(memory_space=pltpu.SMEM)],
        out_shape=jax.ShapeDtypeStruct((T, WB), F32),
        out_specs=pl.BlockSpec((BLOCK, WB), lambda s, n: (s * nb + n, 0)),
        compiler_params=_params(("arbitrary", "arbitrary")),
    )(h, h, h, sinks)


def _attn_bwd(h, yb, dyb, sinks, layer, *, S, D, name):
    T = h.shape[0]
    WB, KVW = D // 2, D // 16
    nb = S // BLOCK
    n_kv = KVW // HEAD_DIM

    def body(q_ref, k_ref, v_ref, o_ref, do_ref, sinks_ref, dq_ref, dk_ref, dv_ref, dsink_ref, dk_acc, dv_acc):
        s_id, n = pl.program_id(0), pl.program_id(1)
        lo = _lo_mask()

        @pl.when(n == 0)
        def _():
            dk_acc[...] = jnp.zeros_like(dk_acc)
            dv_acc[...] = jnp.zeros_like(dv_acc)

        @pl.when((n == 0) & (s_id == 0))
        def _():
            dsink_ref[...] = jnp.zeros_like(dsink_ref)

        scale = HEAD_DIM ** -0.5
        q, o, do = q_ref[...] * scale, o_ref[...], do_ref[...]
        kk, vv = _window(k_ref, n), _window(v_ref, n)
        valid = _valid_mask_t(n)
        lane = lax.broadcasted_iota(jnp.int32, dsink_ref.shape, 1)
        dq_blocks, dk_heads, dv_heads = [], [], []
        dsink = jnp.zeros(dsink_ref.shape, F32)
        for hh in range(n_kv):
            cb = slice((hh // 2) * 128, (hh // 2 + 1) * 128)
            kdup, vdup = _dup(kk[:, cb], hh % 2, lo), _dup(vv[:, cb], hh % 2, lo)
            qs = _stack_heads(q, hh, lo, True)
            dos = _stack_heads(do, hh, lo, True)
            delta = _lane_sums_row(dos * _stack_heads(o, hh, lo, False))
            p_t, psink = _softmax_t(qs, kdup, valid, _sink_row(sinks_ref, layer, hh))
            dvr = _dot(p_t, dos, NN)
            dv_heads.append(dvr + pltpu.roll(dvr, HEAD_DIM, 1))
            ds_t = p_t * (_dot(vdup, dos, NT) - delta)
            dq_blocks += [b * scale for b in _unstack_pairs_t(_dot(kdup, ds_t, TN))]
            dkr = _dot(ds_t, qs, NN)
            dk_heads.append(dkr + pltpu.roll(dkr, HEAD_DIM, 1))
            dsk = -psink * delta
            for g in range(KV_GROUP):
                tot = jnp.sum(dsk[:, g * BLOCK:(g + 1) * BLOCK], axis=1, keepdims=True)
                dsink = dsink + jnp.where(lane == hh * KV_GROUP + g, tot, 0.0)
        dsink_ref[...] += dsink
        dq_ref[...] = jnp.concatenate(dq_blocks, axis=1).astype(dq_ref.dtype)
        pair = lambda hs: jnp.concatenate([jnp.where(lo, hs[2 * m], hs[2 * m + 1]) for m in range(n_kv // 2)], axis=1)
        dkk, dvv = pair(dk_heads), pair(dv_heads)
        prev = pl.ds(pl.multiple_of(jnp.maximum(n - 1, 0) * BLOCK, BLOCK), BLOCK)
        cur = pl.ds(pl.multiple_of(n * BLOCK, BLOCK), BLOCK)
        dk_acc[prev, :] += dkk[:BLOCK]
        dk_acc[cur, :] += dkk[BLOCK:]
        dv_acc[prev, :] += dvv[:BLOCK]
        dv_acc[cur, :] += dvv[BLOCK:]

        @pl.when(n == nb - 1)
        def _():
            dk_ref[...] = dk_acc[...].astype(dk_ref.dtype)
            dv_ref[...] = dv_acc[...].astype(dv_ref.dtype)

    blk = lambda cb=0: pl.BlockSpec((BLOCK, WB), lambda s, n: (s * nb + n, cb))
    seq = lambda cb=0: pl.BlockSpec((S, KVW), lambda s, n: (s, cb))
    return _pcall(
        body, name=name, grid=(T // S, nb),
        in_specs=[blk(D // WB), seq((D + WB) // KVW), seq((D + WB) // KVW + 1), blk(), blk(),
                  pl.BlockSpec(memory_space=pltpu.SMEM)],
        out_shape=(jax.ShapeDtypeStruct((T, WB), BF16), jax.ShapeDtypeStruct((T, KVW), BF16),
                   jax.ShapeDtypeStruct((T, KVW), BF16), jax.ShapeDtypeStruct((8, 128), F32)),
        out_specs=(blk(), seq(), seq(), pl.BlockSpec((8, 128), lambda s, n: (0, 0))),
        scratch_shapes=[pltpu.VMEM((S, KVW), F32), pltpu.VMEM((S, KVW), F32)],
        compiler_params=_params(("arbitrary", "arbitrary")),
    )(h, h, h, yb, dyb, sinks)


def _bg_specs(D, tm):
    return _colspecs(D + D // 2 + 2 * (D // 16), D // 2, tm, lambda i: i)


def _mixb_fwd(yb, h, nb_g, *, D, tm, name):
    T, WB = yb.shape
    bg_specs, _ = _bg_specs(D, tm)

    def body(*refs):
        refs = list(refs)
        yb_ref = refs.pop(0)
        bg = _cat(_take(refs, len(bg_specs)))
        g_ref, o_ref = refs
        _, _, nrm = _rms(yb_ref[...], g_ref[...])
        o_ref[...] = (nrm * (bg * _sigmoid(bg))).astype(o_ref.dtype)

    row = pl.BlockSpec((tm, WB), lambda i: (i, 0))
    return _pcall(body, name=name, grid=(T // tm,),
                  in_specs=[row] + bg_specs + [pl.BlockSpec((1, WB), lambda i: (0, 0))],
                  out_shape=jax.ShapeDtypeStruct((T, WB), BF16), out_specs=row,
                  compiler_params=_params(("arbitrary",)))(yb, *([h] * len(bg_specs)), nb_g)


def _mixb_bwd(yb, h, dmix, nb_g, *, D, tm, name):
    T, WB = yb.shape
    W = D // 4
    bg_specs, _ = _bg_specs(D, tm)
    dm_specs, _ = _colspecs(W, WB, tm, lambda i: i)

    def body(*refs):
        refs = list(refs)
        yb_ref = refs.pop(0)
        bg = _cat(_take(refs, len(bg_specs)))
        dmix_b = _cat(_take(refs, len(dm_specs)))
        g_ref, dyb_ref, dbg_ref, dg_ref = refs

        @pl.when(pl.program_id(0) == 0)
        def _():
            dg_ref[...] = jnp.zeros_like(dg_ref)

        gamma = g_ref[...]
        xn, rstd, nrm = _rms(yb_ref[...], gamma)
        sg = _sigmoid(bg)
        dn = dmix_b * (bg * sg)
        dbg_ref[...] = (dmix_b * nrm * (sg * (1.0 + bg * (1.0 - sg)))).astype(dbg_ref.dtype)
        dg_ref[0:1, :] += jnp.sum(dn * xn, axis=0, keepdims=True)
        dyb_ref[...] = _rms_bwd(dn, xn, rstd, gamma)

    row = pl.BlockSpec((tm, WB), lambda i: (i, 0))
    return _pcall(body, name=name, grid=(T // tm,),
                  in_specs=[row] + bg_specs + dm_specs + [pl.BlockSpec((1, WB), lambda i: (0, 0))],
                  out_shape=(jax.ShapeDtypeStruct((T, WB), F32), jax.ShapeDtypeStruct((T, WB), BF16),
                             jax.ShapeDtypeStruct((8, WB), F32)),
                  out_specs=(row, row, pl.BlockSpec((8, WB), lambda i: (0, 0))),
                  compiler_params=_params(("arbitrary",)))(yb, *([h] * len(bg_specs)), *([dmix] * len(dm_specs)), nb_g)


def _concat_cols(parts, *, tm, name):
    parts = [p if isinstance(p, tuple) else (p, 0, p.shape[1]) for p in parts]
    T = parts[0][0].shape[0]
    total = sum(w for _, _, w in parts)

    def body(*refs):
        refs[-1][...] = jnp.concatenate([r[...] for r in refs[:-1]], axis=1)

    return _pcall(body, name=name, grid=(T // tm,),
                  in_specs=[pl.BlockSpec((tm, w), functools.partial(lambda cb, i: (i, cb), cb)) for _, cb, w in parts],
                  out_shape=jax.ShapeDtypeStruct((T, total), parts[0][0].dtype),
                  out_specs=pl.BlockSpec((tm, total), lambda i: (i, 0)),
                  compiler_params=_params(("parallel",)))(*[a for a, _, _ in parts])


def _ln_fwd(z, g, b, *, tm, name):
    T, D = z.shape

    def body(z_ref, g_ref, b_ref, y_ref, yb_ref):
        zv = z_ref[...]
        mu = jnp.mean(zv, axis=-1, keepdims=True)
        zc = zv - mu
        var = jnp.mean(zc * zc, axis=-1, keepdims=True)
        y = zc * lax.rsqrt(var + LN_EPS) * g_ref[...] + b_ref[...]
        y_ref[...] = y
        yb_ref[...] = y.astype(BF16)

    row = pl.BlockSpec((tm, D), lambda i: (i, 0))
    vec = pl.BlockSpec((1, D), lambda i: (0, 0))
    return _pcall(body, name=name, grid=(T // tm,), in_specs=[row, vec, vec],
                  out_shape=(jax.ShapeDtypeStruct((T, D), F32), jax.ShapeDtypeStruct((T, D), BF16)),
                  out_specs=(row, row), compiler_params=_params(("parallel",)))(z, g, b)


def _ln_bwd(z, dy, g, *, tm, name, deps=()):
    T, D = z.shape

    def body(z_ref, dy_ref, g_ref, *rest):
        dz_ref, dzb_ref, dgb_ref = rest[len(deps):]

        @pl.when(pl.program_id(0) == 0)
        def _():
            dgb_ref[...] = jnp.zeros_like(dgb_ref)

        zv, dyv = z_ref[...], dy_ref[...]
        mu = jnp.mean(zv, axis=-1, keepdims=True)
        zc = zv - mu
        rstd = lax.rsqrt(jnp.mean(zc * zc, axis=-1, keepdims=True) + LN_EPS)
        xh = zc * rstd
        dxh = dyv * g_ref[...]
        dz = rstd * (dxh - jnp.mean(dxh, axis=-1, keepdims=True) - xh * jnp.mean(dxh * xh, axis=-1, keepdims=True))
        dz_ref[...] = dz
        dzb_ref[...] = dz.astype(BF16)
        dgb_ref[0:1, :] += jnp.sum(dyv * xh, axis=0, keepdims=True)
        dgb_ref[1:2, :] += jnp.sum(dyv, axis=0, keepdims=True)

    row = pl.BlockSpec((tm, D), lambda i: (i, 0))
    return _pcall(body, name=name, grid=(T // tm,),
                  in_specs=[row, row, pl.BlockSpec((1, D), lambda i: (0, 0))] + [ANY] * len(deps),
                  out_shape=(jax.ShapeDtypeStruct((T, D), F32), jax.ShapeDtypeStruct((T, D), BF16),
                             jax.ShapeDtypeStruct((8, D), F32)),
                  out_specs=(row, row, pl.BlockSpec((8, D), lambda i: (0, 0))),
                  compiler_params=_params(("arbitrary",)))(z, dy, g, *deps)


def _loss_head(y, target, *, tm, name):
    T, D = y.shape

    def body(y_ref, t_ref, dy_ref, loss_ref):
        @pl.when(pl.program_id(0) == 0)
        def _():
            loss_ref[...] = jnp.zeros_like(loss_ref)

        err = y_ref[...] - t_ref[...]
        dy_ref[...] = err / D
        loss_ref[...] += 0.5 * jnp.sum(jnp.mean(err * err, axis=-1, keepdims=True), axis=0, keepdims=True)

    row = pl.BlockSpec((tm, D), lambda i: (i, 0))
    return _pcall(body, name=name, grid=(T // tm,), in_specs=[row, row],
                  out_shape=(jax.ShapeDtypeStruct((T, D), F32), jax.ShapeDtypeStruct((1, 1), F32)),
                  out_specs=(row, pl.BlockSpec((1, 1), lambda i: (0, 0))),
                  compiler_params=_params(("arbitrary",)))(y, target)


def _cast_bf16(w, layer, *, name):
    _, R, C = w.shape
    tr = _tile(R, 512, 8)

    def body(w_ref, o_ref):
        o_ref[...] = w_ref[...].astype(BF16)

    return _pcall(body, name=name, grid=(R // tr,), in_specs=[pl.BlockSpec((None, tr, C), lambda i: (layer, i, 0))],
                  out_shape=jax.ShapeDtypeStruct((R, C), BF16), out_specs=pl.BlockSpec((tr, C), lambda i: (i, 0)),
                  compiler_params=_params(("parallel",)))(w)


def _cast_shard(w, layer, kind, *, name):
    _, R, C = w.shape
    tr = _tile(R, 512, 16)
    nrb = R // tr
    if kind == "in":
        full, o_idx = (R, N_CHIPS * C), lambda i: (i, _my_chip())
    else:
        full, o_idx = (N_CHIPS * R, C), lambda i: (_my_chip() * nrb + i, 0)

    def body(w_ref, o_ref):
        o_ref[...] = w_ref[...].astype(BF16)

    return _pcall(body, name=name, grid=(nrb,), in_specs=[pl.BlockSpec((None, tr, C), lambda i: (layer, i, 0))],
                  out_shape=jax.ShapeDtypeStruct(full, BF16), out_specs=pl.BlockSpec((tr, C), o_idx),
                  compiler_params=_params(("parallel",)))(w)


def _adamw_layer(g, w, m, v, layer, bufs, *, name):
    L, R, C = w.shape
    tr = _tile(R, max(8, (1 << 19) // C // 8 * 8), 8)
    if bufs is None:
        bufs = [lax.empty((L, R, C), F32) for _ in range(4)]

    def body(g_ref, w_ref, m_ref, v_ref, b0, b1, b2, b3, go_ref, d_ref, nm_ref, nv_ref):
        gv = g_ref[...]
        nm = ADAM_B1 * m_ref[...] + (1.0 - ADAM_B1) * gv
        nv = ADAM_B2 * v_ref[...] + (1.0 - ADAM_B2) * (gv * gv)
        m_hat = nm / (1.0 - ADAM_B1 ** ADAM_STEP)
        v_hat = nv / (1.0 - ADAM_B2 ** ADAM_STEP)
        go_ref[...] = gv
        d_ref[...] = -ADAM_LR * (m_hat / (jnp.sqrt(v_hat) + ADAM_EPS) + ADAM_WD * w_ref[...])
        nm_ref[...] = nm
        nv_ref[...] = nv

    lay = pl.BlockSpec((None, tr, C), lambda i: (layer, i, 0))
    shp = jax.ShapeDtypeStruct((L, R, C), F32)
    return list(_pcall(body, name=name, grid=(R // tr,),
                       in_specs=[pl.BlockSpec((tr, C), lambda i: (i, 0)), lay, lay, lay] + [ANY] * 4,
                       out_shape=(shp,) * 4, out_specs=(lay,) * 4, input_output_aliases={4 + k: k for k in range(4)},
                       compiler_params=_params(("parallel",)))(g, w, m, v, *bufs))


def _adamw(g, w, m, v, *, name):
    R, C = g.shape
    tr = _tile(R, max(8, (1 << 19) // C // 8 * 8), 8)

    def body(g_ref, w_ref, m_ref, v_ref, d_ref, nm_ref, nv_ref):
        gv = g_ref[...]
        nm = ADAM_B1 * m_ref[...] + (1.0 - ADAM_B1) * gv
        nv = ADAM_B2 * v_ref[...] + (1.0 - ADAM_B2) * (gv * gv)
        m_hat = nm / (1.0 - ADAM_B1 ** ADAM_STEP)
        v_hat = nv / (1.0 - ADAM_B2 ** ADAM_STEP)
        d_ref[...] = -ADAM_LR * (m_hat / (jnp.sqrt(v_hat) + ADAM_EPS) + ADAM_WD * w_ref[...])
        nm_ref[...] = nm
        nv_ref[...] = nv

    blk = pl.BlockSpec((tr, C), lambda i: (i, 0))
    shp = jax.ShapeDtypeStruct((R, C), F32)
    return _pcall(body, name=name, grid=(R // tr,), in_specs=[blk] * 4, out_shape=(shp, shp, shp),
                  out_specs=(blk, blk, blk), compiler_params=_params(("parallel",)))(g, w, m, v)


def _my_core():
    return lax.axis_index("c")


def _my_chip():
    return 2 * lax.axis_index("x") + lax.axis_index("y")


def _pair_sum(mine, theirs, *, half_axis, name):
    R, C = theirs.shape
    tr, tc = _tile(R, 512, 16), _tile(C, 2048)
    nrb, ncb = R // tr, C // tc

    def body(a_ref, b_ref, o_ref):
        o_ref[...] = (a_ref[...].astype(F32) + b_ref[...].astype(F32)).astype(BF16)

    if half_axis == 0:
        a_idx = lambda i, j: (_my_core() * nrb + i, j)
    else:
        a_idx = lambda i, j: (i, _my_core() * ncb + j)
    blk = pl.BlockSpec((tr, tc), lambda i, j: (i, j))
    return _pcall(body, name=name, grid=(nrb, ncb), in_specs=[pl.BlockSpec((tr, tc), a_idx), blk], out_specs=blk,
                  out_shape=jax.ShapeDtypeStruct(theirs.shape, BF16),
                  compiler_params=_params(("parallel", "parallel")))(mine, theirs)


def _final_sum(own, got, *, own_axis, out_shape, out_axis, name):
    _, R, C = got.shape
    tr, tc = _tile(R, 512, 16), _tile(C, 1024)
    nrb, ncb = R // tr, C // tc

    def body(a_ref, q_ref, o_ref):
        o_ref[...] = ((a_ref[...].astype(F32) + q_ref[0].astype(F32)) + q_ref[1].astype(F32)) + q_ref[2].astype(F32)

    if own_axis == 1:
        a_idx = lambda i, j: (i, _my_chip() * ncb + j)
    else:
        a_idx = lambda i, j: (_my_chip() * nrb + i, j)
    if out_axis == 0:
        o_idx = lambda i, j: (_my_core() * nrb + i, j)
    else:
        o_idx = lambda i, j: (i, _my_core() * ncb + j)
    return _pcall(body, name=name, grid=(nrb, ncb),
                  in_specs=[pl.BlockSpec((tr, tc), a_idx), pl.BlockSpec((3, tr, tc), lambda i, j: (0, i, j))],
                  out_specs=pl.BlockSpec((tr, tc), o_idx), out_shape=jax.ShapeDtypeStruct(out_shape, F32),
                  compiler_params=_params(("parallel", "parallel")))(own, got)


def _sum_devices(gathered, *, name):
    _, R, C = gathered.shape
    tr = _tile(R, 280, 8)

    def body(g_ref, o_ref):
        acc = g_ref[0]
        for d in range(1, N_DEV):
            acc = acc + g_ref[d]
        o_ref[...] = acc

    return _pcall(body, name=name, grid=(R // tr,), in_specs=[pl.BlockSpec((N_DEV, tr, C), lambda i: (0, i, 0))],
                  out_shape=jax.ShapeDtypeStruct((R, C), F32), out_specs=pl.BlockSpec((tr, C), lambda i: (i, 0)),
                  compiler_params=_params(("parallel",)))(gathered)


def _position():
    x, y, c = lax.axis_index("x"), lax.axis_index("y"), lax.axis_index("c")
    chips = [(1 - x, y), (x, 1 - y), (1 - x, 1 - y)]
    return x, y, c, chips


def _remote(src, dst, send_sems, recv_sems, k, to):
    return pltpu.make_async_remote_copy(src_ref=src, dst_ref=dst, send_sem=send_sems.at[k], recv_sem=recv_sems.at[k],
                                        device_id=to, device_id_type=MESH)


def _r(ref, start, n):
    return ref.at[pl.ds(pl.multiple_of(start, 16), n), :]


def _c(ref, start, n):
    return ref.at[:, pl.ds(pl.multiple_of(start, 128), n)]


class _part:
    def __init__(self, ins, outs, plan, n, n_local=0, aliased=0):
        self.ins, self.outs, self.plan, self.n, self.n_local, self.aliased = ins, outs, plan, n, n_local, aliased


def _comm_scratch(parts):
    if not parts:
        return []
    n, nl = sum(p.n for p in parts), sum(p.n_local for p in parts)
    return [pltpu.SemaphoreType.DMA((n,)), pltpu.SemaphoreType.DMA((n,)), pltpu.SemaphoreType.DMA((max(nl, 1),))]


def _comm_aliases(parts, in_base, out_base):
    aliases, ii, oi = {}, in_base, out_base
    for p in parts:
        aliases.update({ii + k: oi + k for k in range(p.aliased)})
        ii += len(p.ins)
        oi += len(p.outs)
    return aliases


def _comm_run(parts, phase, in_refs, out_refs, send_sems, recv_sems, local_sems):
    pos = _position()
    me = pos[:3]
    ii = oi = si = li = 0
    for p in parts:
        sends, recvs, locs = p.plan(in_refs[ii:ii + len(p.ins)], out_refs[oi:oi + len(p.outs)], pos)
        assert len(sends) == len(recvs) == p.n and len(locs) == p.n_local
        if phase == "start":
            for k, (src, dst) in enumerate(locs):
                pltpu.make_async_copy(src, dst, local_sems.at[li + k]).start()
            for k, (src, dst, to) in enumerate(sends):
                _remote(src, dst, send_sems, recv_sems, si + k, to).start()
        else:
            for k, dst in enumerate(recvs):
                _remote(dst, dst, send_sems, recv_sems, si + k, me).wait_recv()
            for k, (src, dst, to) in enumerate(sends):
                _remote(src, dst, send_sems, recv_sems, si + k, to).wait_send()
            for k, (src, dst) in enumerate(locs):
                pltpu.make_async_copy(src, dst, local_sems.at[li + k]).wait()
        ii, oi, si, li = ii + len(p.ins), oi + len(p.outs), si + p.n, li + p.n_local


def _comm_call(parts, *, name):
    n_in = sum(len(p.ins) for p in parts)
    n_out = sum(len(p.outs) for p in parts)

    def body(*refs):
        refs = list(refs)
        cin, cout = _take(refs, n_in), _take(refs, n_out)
        _comm_run(parts, "start", cin, cout, *refs)
        _comm_run(parts, "finish", cin, cout, *refs)

    return list(_pcall(body, name=name, in_specs=[ANY] * n_in, out_specs=[ANY] * n_out,
                       out_shape=[s for p in parts for s in p.outs], scratch_shapes=_comm_scratch(parts),
                       input_output_aliases=_comm_aliases(parts, 0, 0))(*[a for p in parts for a in p.ins]))


HBM = pl.BlockSpec(memory_space=pltpu.HBM)
SEM = pl.BlockSpec(memory_space=pltpu.SEMAPHORE)
EFFECT = pltpu.SideEffectType.DATAFLOW_SIDE_EFFECTING


def _split_refs(parts, arr):
    out, i = [], 0
    for p in parts:
        ins = arr[i:i + len(p.ins)]
        i += len(p.ins)
        lands = arr[i:i + len(p.outs) - p.aliased]
        i += len(lands)
        out.append((ins, list(ins[:p.aliased]) + list(lands)))
    return out


def _split_start(parts, *, name):
    assert all(p.n_local == 0 for p in parts)
    arrays = []
    for p in parts:
        arrays += list(p.ins) + [lax.empty(s.shape, s.dtype) for s in p.outs[p.aliased:]]
    n, na = sum(p.n for p in parts), len(arrays)

    def body(*refs):
        refs = list(refs)
        arr = _take(refs, na)
        sems = _take(refs, 2 * n)
        token = refs[na]
        pos = _position()
        k = 0
        for p, (ins, outs) in zip(parts, _split_refs(parts, arr)):
            sends, _, _ = p.plan(ins, outs, pos)
            for src, dst, to in sends:
                pltpu.make_async_remote_copy(src_ref=src, dst_ref=dst, send_sem=sems[k], recv_sem=sems[n + k],
                                             device_id=to, device_id_type=MESH).start()
                k += 1
        token[...] = jnp.zeros_like(token)

    res = _pcall(
        body, name=name,
        out_shape=[pltpu.SemaphoreType.DMA(())] * (2 * n) + [pltpu.HBM(a.shape, a.dtype) for a in arrays]
        + [jax.ShapeDtypeStruct((8, 128), F32)],
        in_specs=[HBM] * na, out_specs=[SEM] * (2 * n) + [HBM] * na + [pl.BlockSpec(memory_space=pltpu.VMEM)],
        input_output_aliases={i: 2 * n + i for i in range(na)},
        compiler_params=pltpu.CompilerParams(has_side_effects=EFFECT),
    )(*[pltpu.with_memory_space_constraint(a, pltpu.HBM) for a in arrays])
    return (list(res[:2 * n]), list(res[2 * n:2 * n + na])), res[-1]


def _split_wait(parts, state, after, *, name):
    sems, arrays = state
    n, na = len(sems) // 2, len(arrays)

    def body(*refs):
        refs = list(refs)
        arr = _take(refs, na)
        sm = _take(refs, 2 * n)
        pos = _position()
        me = pos[:3]
        k = 0
        for p, (ins, outs) in zip(parts, _split_refs(parts, arr)):
            sends, recvs, _ = p.plan(ins, outs, pos)
            for (src, dst, to), land in zip(sends, recvs):
                pltpu.make_async_remote_copy(src_ref=src, dst_ref=dst, send_sem=sm[k], recv_sem=sm[n + k],
                                             device_id=to, device_id_type=MESH).wait_send()
                pltpu.make_async_remote_copy(src_ref=land, dst_ref=land, send_sem=sm[k], recv_sem=sm[n + k],
                                             device_id=me, device_id_type=MESH).wait_recv()
                k += 1

    res = _pcall(
        body, name=name, out_shape=[pltpu.HBM(a.shape, a.dtype) for a in arrays],
        in_specs=[HBM] * na + [SEM] * (2 * n) + [ANY] * len(after), out_specs=[HBM] * na,
        input_output_aliases={i: i for i in range(na)},
        compiler_params=pltpu.CompilerParams(has_side_effects=EFFECT),
    )(*arrays, *sems, *after)
    return _split_refs(parts, list(res))


def _slab(wg, kind, chip, half):
    if kind == "in":
        d, ns = wg.shape[0], wg.shape[1] // N_CHIPS
        return _c(_r(wg, half * (d // 2), d // 2), chip * ns, ns)
    rs = wg.shape[0] // N_CHIPS
    return _r(wg, chip * rs + half * (rs // 2), rs // 2)


def _gather_ici(wg, kind):
    def plan(ins, outs, pos):
        x, y, c, chips = pos
        (ref,) = outs
        mine = _slab(ref, kind, 2 * x + y, c)
        return [(mine, mine, (*chip, c)) for chip in chips], [_slab(ref, kind, 2 * px + py, c) for px, py in chips], []

    return _part([wg], [jax.ShapeDtypeStruct(wg.shape, wg.dtype)], plan, 3, aliased=1)


def _gather_d2d(wg, kind):
    def plan(ins, outs, pos):
        x, y, c, chips = pos
        (ref,) = outs
        sends = [(_slab(ref, kind, 2 * px + py, c), _slab(ref, kind, 2 * px + py, c), (x, y, 1 - c)) for px, py in chips]
        return sends, [_slab(ref, kind, 2 * px + py, 1 - c) for px, py in chips], []

    return _part([wg], [jax.ShapeDtypeStruct(wg.shape, wg.dtype)], plan, 3, aliased=1)


def _pair_send(gw, kind):
    rows, cols = gw.shape
    half = (rows // 2, cols) if kind == "in" else (rows, cols // 2)

    def plan(ins, outs, pos):
        x, y, c, _ = pos
        (src,), (rb,) = ins, outs
        theirs = _r(src, (1 - c) * half[0], half[0]) if kind == "in" else _c(src, (1 - c) * half[1], half[1])
        return [(theirs, rb, (x, y, 1 - c))], [rb], []

    return _part([gw], [jax.ShapeDtypeStruct(half, gw.dtype)], plan, 1)


def _chip_send(p, kind):
    rows, cols = p.shape
    shard = (rows, cols // N_CHIPS) if kind == "in" else (rows // N_CHIPS, cols)

    def plan(ins, outs, pos):
        x, y, c, chips = pos
        (src,), (q,) = ins, outs
        piece = lambda jk: _c(src, jk * shard[1], shard[1]) if kind == "in" else _r(src, jk * shard[0], shard[0])
        sends = [(piece(2 * px + py), q.at[kk], (px, py, c)) for kk, (px, py) in enumerate(chips)]
        return sends, [q.at[kk] for kk in range(3)], []

    return _part([p], [jax.ShapeDtypeStruct((3,) + shard, p.dtype)], plan, 3)


def _sibling_send(g, kind):
    rows, cols = g.shape

    def plan(ins, outs, pos):
        x, y, c, _ = pos
        (ref,) = outs
        half = (lambda h: _r(ref, h * (rows // 2), rows // 2)) if kind == "in" else (
            lambda h: _c(ref, h * (cols // 2), cols // 2))
        return [(half(c), half(c), (x, y, 1 - c))], [half(1 - c)], []

    return _part([g], [jax.ShapeDtypeStruct(g.shape, g.dtype)], plan, 1, aliased=1)


def _small_ici(block):
    def plan(ins, outs, pos):
        x, y, c, chips = pos
        (src,), (out,) = ins, outs
        mine = out.at[4 * x + 2 * y + c]
        peers = [(x, y, 1 - c)] + [(px, py, c) for px, py in chips]
        return [(src, mine, p) for p in peers], [out.at[4 * px + 2 * py + pc] for px, py, pc in peers], [(src, mine)]

    return _part([block], [jax.ShapeDtypeStruct((N_DEV,) + block.shape, block.dtype)], plan, 4, 1)


def _small_d2d(gathered):
    def plan(ins, outs, pos):
        x, y, c, chips = pos
        (out,) = outs
        sends = [(out.at[4 * px + 2 * py + c], out.at[4 * px + 2 * py + c], (x, y, 1 - c)) for px, py in chips]
        return sends, [out.at[4 * px + 2 * py + (1 - c)] for px, py in chips], []

    return _part([gathered], [jax.ShapeDtypeStruct(gathered.shape, gathered.dtype)], plan, 3, aliased=1)


def _gather_small(block, *, name):
    R, C = block.shape

    def body(x_ref, out_ref, send_sems, recv_sems, local_sem):
        x, y, c, chips = _position()
        me, sib = (x, y, c), (x, y, 1 - c)
        slot = lambda px, py, pc: out_ref.at[4 * px + 2 * py + pc]
        mine = pltpu.make_async_copy(x_ref, slot(*me), local_sem)
        mine.start()
        first = [_remote(x_ref, slot(*me), send_sems, recv_sems, 0, sib)]
        first += [_remote(x_ref, slot(*me), send_sems, recv_sems, 1 + kk, (*chip, c)) for kk, chip in enumerate(chips)]
        for cp in first:
            cp.start()
        passed = []
        for kk, chip in enumerate(chips):
            _remote(slot(*chip, c), slot(*chip, c), send_sems, recv_sems, 1 + kk, sib).wait_recv()
            passed.append(_remote(slot(*chip, c), slot(*chip, c), send_sems, recv_sems, 4 + kk, sib))
            passed[-1].start()
        _remote(slot(*sib), slot(*sib), send_sems, recv_sems, 0, sib).wait_recv()
        for kk, chip in enumerate(chips):
            _remote(slot(*chip, 1 - c), slot(*chip, 1 - c), send_sems, recv_sems, 4 + kk, sib).wait_recv()
        for cp in first + passed:
            cp.wait_send()
        mine.wait()

    return _pcall(
        body, name=name, in_specs=[ANY], out_specs=ANY, out_shape=jax.ShapeDtypeStruct((N_DEV, R, C), F32),
        scratch_shapes=[pltpu.SemaphoreType.DMA((7,)), pltpu.SemaphoreType.DMA((7,)), pltpu.SemaphoreType.DMA],
    )(block)


_SMALL = ["gate_r_w", "gate_i_w", "conv_a_w", "conv_c_w", "sinks", "conv_c_b", "gate_r_b", "gate_i_b", "rg_lambda",
          "norm_a", "norm_b", "norm_c", "ln_g", "ln_b"]


def _pack_small(p):
    L = p["ln_g"].shape[0]
    rows = []
    for n in _SMALL:
        a = p[n]
        if n in ("gate_r_w", "gate_i_w", "norm_b", "ln_g", "ln_b"):
            a = a.reshape(L, -1, 1024)
        elif a.ndim == 2:
            a = a[:, None, :]
        if a.shape[-1] < 1024:
            a = jnp.pad(a, ((0, 0), (0, 0), (0, 1024 - a.shape[-1])))
        rows.append(a)
    out = jnp.concatenate(rows, axis=1)
    assert out.shape[1] == SMALL_ROWS
    return out.reshape(L * SMALL_ROWS, 1024)


def _unpack_small(flat, like):
    L = like["ln_g"].shape[0]
    a = flat.reshape(L, SMALL_ROWS, 1024)
    out, r = {}, 0
    for n in _SMALL:
        shp = like[n].shape
        nrows = max(1, math.prod(shp[1:]) // 1024) if n in ("gate_r_w", "gate_i_w", "norm_b", "ln_g", "ln_b") else (
            shp[1] if len(shp) == 3 else 1)
        blk = a[:, r:r + nrows, :]
        if n in ("gate_r_w", "gate_i_w", "norm_b", "ln_g", "ln_b"):
            out[n] = blk.reshape(shp)
        elif len(shp) == 3:
            out[n] = blk[:, :, :shp[2]]
        else:
            out[n] = blk[:, 0, :shp[1]]
        r += nrows
    return out


def kernel(x, w_in, conv_a_w, sinks, conv_c_w, conv_c_b, gate_r_w, gate_r_b, gate_i_w, gate_i_b, rg_lambda, norm_a, norm_b, norm_c, w_out, ln_g, ln_b, loss_target, m_w_in, m_conv_a_w, m_sinks, m_conv_c_w, m_conv_c_b, m_gate_r_w, m_gate_r_b, m_gate_i_w, m_gate_i_b, m_rg_lambda, m_norm_a, m_norm_b, m_norm_c, m_w_out, m_ln_g, m_ln_b, v_w_in, v_conv_a_w, v_sinks, v_conv_c_w, v_conv_c_b, v_gate_r_w, v_gate_r_b, v_gate_i_w, v_gate_i_b, v_rg_lambda, v_norm_a, v_norm_b, v_norm_c, v_w_out, v_ln_g, v_ln_b):
    names = ["w_in", "conv_a_w", "sinks", "conv_c_w", "conv_c_b", "gate_r_w", "gate_r_b", "gate_i_w", "gate_i_b",
             "rg_lambda", "norm_a", "norm_b", "norm_c", "w_out", "ln_g", "ln_b"]
    w = dict(zip(names, [w_in, conv_a_w, sinks, conv_c_w, conv_c_b, gate_r_w, gate_r_b, gate_i_w, gate_i_b, rg_lambda,
                         norm_a, norm_b, norm_c, w_out, ln_g, ln_b]))
    mom = dict(zip(names, [m_w_in, m_conv_a_w, m_sinks, m_conv_c_w, m_conv_c_b, m_gate_r_w, m_gate_r_b, m_gate_i_w,
                           m_gate_i_b, m_rg_lambda, m_norm_a, m_norm_b, m_norm_c, m_w_out, m_ln_g, m_ln_b]))
    vel = dict(zip(names, [v_w_in, v_conv_a_w, v_sinks, v_conv_c_w, v_conv_c_b, v_gate_r_w, v_gate_r_b, v_gate_i_w,
                           v_gate_i_b, v_rg_lambda, v_norm_a, v_norm_b, v_norm_c, v_w_out, v_ln_g, v_ln_b]))
    B, S, D = x.shape
    T = B * S
    L, _, NS = w_in.shape
    RS = w_out.shape[1]
    W = D // 4
    alpha = (2.0 * L) ** 0.25
    tt = _tile(S, 128, 8)
    tm_row = _tile(T, 256, 8)
    chip = _my_chip()

    ws_in = [_cast_shard(w_in, l, "in", name="cast_w_in") for l in range(L)]
    ws_out = [_cast_shard(w_out, l, "out", name="cast_w_out") for l in range(L)]
    wg_in, wg_out = [None] * L, [None] * L
    part_in, part_out = _comm_call([_gather_ici(ws_in[0], "in"), _gather_ici(ws_out[0], "out")], name="gather0_ici")
    wg_in[0], wg_out[0] = _comm_call([_gather_d2d(part_in, "in"), _gather_d2d(part_out, "out")], name="gather0_d2d")
    conv_local = jnp.concatenate([conv_a_w, conv_c_w], axis=1).reshape(L * 7, W // N_CHIPS)
    conv_local = jnp.pad(conv_local, ((0, (-L * 7) % 8), (0, 0)))
    conv_all = _gather_small(conv_local, name="gather_conv")
    conv_full = jnp.concatenate([conv_all[2 * jj][:L * 7] for jj in range(N_CHIPS)], axis=1).reshape(L, 7, W)
    caw_full, ccw_full = conv_full[:, :3], conv_full[:, 3:]

    xf = x.reshape(T, D)
    xb = _cast_bf16(xf[None], 0, name="cast_x")
    saved = []
    for l in range(L):
        nxt = l + 1 < L
        deps = ()
        if nxt:
            g_parts = [_gather_ici(ws_in[l + 1], "in"), _gather_ici(ws_out[l + 1], "out")]
            g_state, g_token = _split_start(g_parts, name=f"gather_start{l + 1}")
            deps = (g_token,)
        comm = [_gather_d2d(part_out, "out")] if l else []
        res = _mm(xb, wg_in[l], mode="nn", out_dtype=F32, name="proj_in", tm=1024, tn=768, tk=4096, comm=comm, deps=deps)
        h = res if not comm else res.pop(0)
        if l:
            wg_out[l] = res.pop(0)
        pv = jnp.stack([conv_c_b[l], gate_r_b[l], gate_i_b[l], rg_lambda[l], norm_a[l], norm_c[l]])
        mix_ac, cv, xc, yc = _ac_fwd(h, caw_full[l], ccw_full[l], pv, gate_r_w[l], gate_i_w[l], S=S, D=D, tt=tt,
                                     name="ac_fwd")
        yb = _attn_fwd(h, sinks, l, S=S, D=D, name="attn_fwd")
        mix_b = _mixb_fwd(yb, h, norm_b[l][None], D=D, tm=tm_row, name="mixb_fwd")
        mix = _concat_cols([(mix_ac, 0, W), mix_b, (mix_ac, 1, W)], tm=tm_row, name="concat_mix")
        comm = []
        if nxt:
            (_, (part_in,)), (_, (part_out,)) = _split_wait(g_parts, g_state, [mix], name=f"gather_wait{l + 1}")
            comm = [_gather_d2d(part_in, "in")]
        res = _mm(mix, wg_out[l], mode="nn", out_dtype=F32, name="proj_out", tn=512, tk=4096, add=xf, add_scale=alpha,
                  comm=comm)
        z = res if not comm else res.pop(0)
        if nxt:
            wg_in[l + 1] = res.pop(0)
        saved.append((xb, h, cv, xc, yc, yb, mix, z, pv))
        xf, xb = _ln_fwd(z, ln_g[l][None], ln_b[l][None], tm=tm_row, name="ln_fwd")
    dxn, loss_part = _loss_head(xf, loss_target.reshape(T, D), tm=tm_row, name="loss_head")
    loss = lax.psum(loss_part[0, 0], ("x", "y", "c"))

    def final_sums(p_in, q_in, p_out, q_out):
        return (_final_sum(p_in, q_in, own_axis=1, out_shape=(D, NS), out_axis=0, name="final_sum_in"),
                _final_sum(p_out, q_out, own_axis=0, out_shape=(RS, D), out_axis=1, name="final_sum_out"))

    bufs_in = bufs_out = None
    small_g = [None] * L
    ce = None
    for l in reversed(range(L)):
        up, last = l + 1 < L, l == 0
        xb_l, h, cv, xc, yc, yb, mix, z, pv = saved[l]
        dz, dzb, dgb = _ln_bwd(z, dxn, ln_g[l][None], tm=tm_row, name="ln_bwd", deps=(ce[2],) if up else ())
        dmix = _mm(dzb, wg_out[l], mode="nt", out_dtype=F32, name="d_mix", tk=4096)
        gw_out = _mm(mix, dzb, mode="tn", out_dtype=BF16, name="d_w_out", tk=4096)
        dha, dhc, vec, dwr, dwi = _ac_bwd(h, cv, xc, yc, dmix, caw_full[l], ccw_full[l], pv, gate_r_w[l], gate_i_w[l],
                                          S=S, D=D, tt=tt, name="ac_bwd")
        dyb, dbg, dnb = _mixb_bwd(yb, h, dmix, norm_b[l][None], D=D, tm=tm_row, name="mixb_bwd")
        dq, dk, dv, dsk = _attn_bwd(h, yb, dyb, sinks, l, S=S, D=D, name="attn_bwd")
        dh = _concat_cols([dha, dq, dk, dv, dbg, dhc], tm=tm_row, name="concat_dh")
        small_g[l] = dict(gate_r_w=dwr, gate_i_w=dwi, conv_a_w=vec[0:3], conv_c_w=vec[4:8], sinks=dsk[0, :2 * D // 256],
                          conv_c_b=vec[8], gate_r_b=vec[9], gate_i_b=vec[10], rg_lambda=vec[11], norm_a=vec[3],
                          norm_b=dnb[0], norm_c=vec[12], ln_g=dgb[0], ln_b=dgb[1])
        if up:
            ((p_in,), (q_in,)), ((p_out,), (q_out,)) = _split_wait(ce[0], ce[1], [dh], name=f"chip_wait{l + 1}")
            g_in_half, g_out_half = final_sums(p_in, q_in, p_out, q_out)
        comm = [_pair_send(gw_out, "out")]
        if last:
            comm.append(_small_ici(_pack_small({n: jnp.stack([small_g[k][n] for k in range(L)]) for n in _SMALL})))
        res = _mm(xb_l, dh, mode="tn", out_dtype=BF16, name="d_w_in", tm=1024, tn=768, tk=4096, comm=comm)
        gw_in, rb_out = _take(res, 2)
        p_out_l = _pair_sum(gw_out, rb_out, half_axis=1, name="pair_sum_out")
        d2d = [_pair_send(gw_in, "in")]
        if up:
            d2d += [_sibling_send(g_in_half, "in"), _sibling_send(g_out_half, "out")]
        if last:
            res = _comm_call(d2d + [_small_d2d(res.pop(0))], name="tail_d2d")
            small_all = res.pop()
        else:
            res = _mm(dh, wg_in[l], mode="nt", out_dtype=F32, name="d_x", tk=2688, add=dz, add_scale=alpha, comm=d2d)
            dxn = res.pop(0)
        p_in_l = _pair_sum(gw_in, res.pop(0), half_axis=0, name="pair_sum_in")
        if up:
            bufs_in = _adamw_layer(res[0], w_in, m_w_in, v_w_in, l + 1, bufs_in, name="adamw_w_in")
            bufs_out = _adamw_layer(res[1], w_out, m_w_out, v_w_out, l + 1, bufs_out, name="adamw_w_out")
        ce_parts = [_chip_send(p_in_l, "in"), _chip_send(p_out_l, "out")]
        ce = (ce_parts, *_split_start(ce_parts, name=f"chip_start{l}"))
    dxn = _mm(dh, wg_in[0], mode="nt", out_dtype=F32, name="d_x", tk=2688, add=dz, add_scale=alpha, deps=(ce[2],))
    grad_x = dxn.reshape(B, S, D)
    ((p_in,), (q_in,)), ((p_out,), (q_out,)) = _split_wait(ce[0], ce[1], [dxn] + (bufs_in or []) + (bufs_out or []),
                                                            name="chip_wait0")
    g_in_half, g_out_half = final_sums(p_in, q_in, p_out, q_out)
    g_in0, g_out0 = _comm_call([_sibling_send(g_in_half, "in"), _sibling_send(g_out_half, "out")], name="sibling0")
    big = {"w_in": _adamw_layer(g_in0, w_in, m_w_in, v_w_in, 0, bufs_in, name="adamw_w_in"),
           "w_out": _adamw_layer(g_out0, w_out, m_w_out, v_w_out, 0, bufs_out, name="adamw_w_out")}

    like = {n: w[n] for n in _SMALL}
    like_full = dict(like, conv_a_w=caw_full, conv_c_w=ccw_full)
    g_small = _unpack_small(_sum_devices(small_all, name="sum_small"), like_full)
    for n in ("conv_a_w", "conv_c_w"):
        g_small[n] = lax.dynamic_slice_in_dim(g_small[n], chip * (W // N_CHIPS), W // N_CHIPS, axis=2)

    d_s, m_s, v_s = _adamw(_pack_small(g_small), _pack_small(like), _pack_small({n: mom[n] for n in _SMALL}),
                           _pack_small({n: vel[n] for n in _SMALL}), name="adamw_small")
    grads = dict(g_small)
    delta, new_m, new_v = _unpack_small(d_s, like), _unpack_small(m_s, like), _unpack_small(v_s, like)
    for n in ("w_in", "w_out"):
        grads[n], delta[n], new_m[n], new_v[n] = big[n]

    return (loss, grad_x, *[grads[n] for n in names], *[delta[n] for n in names], *[new_m[n] for n in names],
            *[new_v[n] for n in names])
```

```python
import functools
import math

import jax
import jax.numpy as jnp
from jax import lax
from jax.experimental import pallas as pl
from jax.experimental.pallas import tpu as pltpu

F32 = jnp.float32
BF16 = jnp.bfloat16
_MXU_DTYPE = jnp.bfloat16

HEAD_DIM = 64
KV_GROUP = 8
BLOCK = 128
N_RG_HEADS = 8
RG_C = 8.0
LN_EPS = 1e-5
RMS_EPS = 1e-6
NEG_INF = -1e30
ADAM_LR, ADAM_B1, ADAM_B2, ADAM_EPS, ADAM_WD, ADAM_STEP = 0.001, 0.9, 0.999, 1e-08, 0.01, 10
N_CHIPS = 4
N_DEV = 8
SMALL_ROWS = 280
VMEM_LIMIT = 56 * 1024 * 1024

MESH = pl.DeviceIdType.MESH
ANY = pl.BlockSpec(memory_space=pl.ANY)


def _pcall(body, *, name, **kw):
    return pl.pallas_call(body, name=name, **kw)


def _params(sem=None):
    return pltpu.CompilerParams(dimension_semantics=sem, vmem_limit_bytes=VMEM_LIMIT)


def _tile(dim, pref, mult=128):
    best = None
    for t in range(mult, min(dim, pref) + 1, mult):
        if dim % t == 0:
            best = t
    return best if best is not None else dim


def _dot(a, b, dims):
    return lax.dot_general(a.astype(_MXU_DTYPE), b.astype(_MXU_DTYPE), (dims, ((), ())),
                           preferred_element_type=F32)


NN = ((1,), (0,))
NT = ((1,), (1,))
TN = ((0,), (0,))


def _mm(a, b, *, mode, out_dtype, name, tm=1024, tn=1024, tk=512, add=None, add_scale=1.0, comm=(), deps=()):
    if mode == "nn":
        (M, K), N = a.shape, b.shape[1]
    elif mode == "nt":
        (M, K), N = a.shape, b.shape[0]
    else:
        (K, M), N = a.shape, b.shape[1]
    tm, tn, tk = _tile(M, tm), _tile(N, tn), _tile(K, tk)
    ni, nj, nk = M // tm, N // tn, K // tk
    dims = {"nn": NN, "nt": NT, "tn": TN}[mode]
    n_cin = sum(len(p.ins) for p in comm)
    n_cout = sum(len(p.outs) for p in comm)

    def body(*refs):
        refs = list(refs)
        a_ref, b_ref = _take(refs, 2)
        add_ref = refs.pop(0) if add is not None else None
        _take(refs, len(deps))
        cin = _take(refs, n_cin)
        o_ref = refs.pop(0)
        cout = _take(refs, n_cout)
        acc = refs.pop(0) if nk > 1 else None
        i, j, k = pl.program_id(0), pl.program_id(1), pl.program_id(2)

        if comm:
            @pl.when((i == 0) & (j == 0) & (k == 0))
            def _():
                _comm_run(comm, "start", cin, cout, *refs)

        def finish(r):
            if add_ref is not None:
                r = r + add_scale * add_ref[...]
            o_ref[...] = r.astype(out_dtype)

        if nk == 1:
            finish(_dot(a_ref[...], b_ref[...], dims))
        else:
            @pl.when(k == 0)
            def _():
                acc[...] = jnp.zeros_like(acc)

            acc[...] += _dot(a_ref[...], b_ref[...], dims)

            @pl.when(k == nk - 1)
            def _():
                finish(acc[...])

        if comm:
            @pl.when((i == ni - 1) & (j == nj - 1) & (k == nk - 1))
            def _():
                _comm_run(comm, "finish", cin, cout, *refs)

    a_spec = {"nn": pl.BlockSpec((tm, tk), lambda i, j, k: (i, k)),
              "nt": pl.BlockSpec((tm, tk), lambda i, j, k: (i, k)),
              "tn": pl.BlockSpec((tk, tm), lambda i, j, k: (k, i))}[mode]
    b_spec = {"nn": pl.BlockSpec((tk, tn), lambda i, j, k: (k, j)),
              "nt": pl.BlockSpec((tn, tk), lambda i, j, k: (j, k)),
              "tn": pl.BlockSpec((tk, tn), lambda i, j, k: (k, j))}[mode]
    in_specs, operands = [a_spec, b_spec], [a, b]
    if add is not None:
        in_specs.append(pl.BlockSpec((tm, tn), lambda i, j, k: (i, j)))
        operands.append(add)
    in_specs += [ANY] * len(deps)
    operands += list(deps)
    aliases = _comm_aliases(comm, len(operands), 1)
    in_specs += [ANY] * n_cin
    operands += [arr for p in comm for arr in p.ins]
    out_shape = [jax.ShapeDtypeStruct((M, N), out_dtype)] + [s for p in comm for s in p.outs]
    out_specs = [pl.BlockSpec((tm, tn), lambda i, j, k: (i, j))] + [ANY] * n_cout
    sem = ("arbitrary",) * 3 if comm else ("parallel", "parallel", "arbitrary")
    res = _pcall(body, name=name, out_shape=out_shape, grid=(ni, nj, nk), in_specs=in_specs, out_specs=out_specs,
                 scratch_shapes=([pltpu.VMEM((tm, tn), F32)] if nk > 1 else []) + _comm_scratch(comm),
                 input_output_aliases=aliases,
                 compiler_params=_params(sem))(*operands)
    return list(res) if comm else res[0]


def _colspecs(off, width, rows, rowmap):
    bw = math.gcd(off, width) if off else width
    specs = [pl.BlockSpec((rows, bw), functools.partial(lambda cb, *g: (rowmap(*g), cb), off // bw + i))
             for i in range(width // bw)]
    return specs, bw


def _cat(refs):
    vals = [r[...] for r in refs]
    return vals[0] if len(vals) == 1 else jnp.concatenate(vals, axis=1)


def _take(refs, n):
    out = refs[:n]
    del refs[:n]
    return out


def _sigmoid(x):
    return 1.0 / (1.0 + jnp.exp(-x))


def _rms(y, gamma):
    rstd = lax.rsqrt(jnp.mean(y * y, axis=-1, keepdims=True) + RMS_EPS)
    xn = y * rstd
    return xn, rstd, xn * gamma


def _rms_bwd(dn, xn, rstd, gamma):
    dng = dn * gamma
    return rstd * (dng - xn * jnp.mean(dng * xn, axis=-1, keepdims=True))


def _shift_down(x, s, carry8):
    rolled = pltpu.roll(x, s, 0)
    cr = pltpu.roll(carry8, s, 0)
    row8 = lax.broadcasted_iota(jnp.int32, carry8.shape, 0)
    top = jnp.where(row8 < s, cr, rolled[0:8])
    return jnp.concatenate([top, rolled[8:]], axis=0)


def _shift_up(x, s, carry8):
    n = x.shape[0]
    rolled = pltpu.roll(x, n - s, 0)
    cr = pltpu.roll(carry8, 8 - s, 0)
    row8 = lax.broadcasted_iota(jnp.int32, carry8.shape, 0)
    bot = jnp.where(row8 >= 8 - s, cr, rolled[n - 8:])
    return jnp.concatenate([rolled[:n - 8], bot], axis=0)


def _chunk_scan(a, b):
    n = a.shape[0]
    r8 = lax.broadcasted_iota(jnp.int32, a.shape, 0) & 7
    for d in (1, 2, 4):
        ok = r8 >= d
        a_sh = jnp.where(ok, pltpu.roll(a, d, 0), 1.0)
        b_sh = jnp.where(ok, pltpu.roll(b, d, 0), 0.0)
        b = a * b_sh + b
        a = a * a_sh
    return a, b


def _chunk_scan_rev(c, b):
    n = c.shape[0]
    r8 = lax.broadcasted_iota(jnp.int32, c.shape, 0) & 7
    for d in (1, 2, 4):
        ok = r8 + d <= 7
        c_sh = jnp.where(ok, pltpu.roll(c, n - d, 0), 1.0)
        b_sh = jnp.where(ok, pltpu.roll(b, n - d, 0), 0.0)
        b = b + c * b_sh
        c = c * c_sh
    return c, b


def _log1p(x):
    w = 1.0 + x
    return jnp.where(w == 1.0, x, jnp.log(w) * (x / (w - 1.0)))


def _log_sigmoid(x):
    return jnp.minimum(x, 0.0) - _log1p(jnp.exp(-jnp.abs(x)))


def _expm1(x):
    u = jnp.exp(x)
    lu = jnp.log(u)
    small = jnp.where(u == 1.0, x, (u - 1.0) * (x / jnp.where(lu == 0.0, 1.0, lu)))
    return jnp.where(jnp.abs(x) < 0.5, small, u - 1.0)


def _gates(xc, wr_ref, wi_ref, br, bi, lam):
    hw = xc.shape[1] // N_RG_HEADS
    gr = jnp.concatenate([_dot(xc[:, h * hw:(h + 1) * hw], wr_ref[h], NN) for h in range(N_RG_HEADS)], axis=1) + br
    gi = jnp.concatenate([_dot(xc[:, h * hw:(h + 1) * hw], wi_ref[h], NN) for h in range(N_RG_HEADS)], axis=1) + bi
    r, i = _sigmoid(gr), _sigmoid(gi)
    ls = _log_sigmoid(lam)
    la = RG_C * r * ls
    a = jnp.exp(la)
    sq = jnp.sqrt(-_expm1(2.0 * la))
    return r, i, ls, a, sq


def _ac_fwd(h, caw, ccw, pv, wr, wi, *, S, D, tt, name):
    T = h.shape[0]
    W = D // 4
    nt = S // tt
    rowmap = lambda s, t: s * nt + t
    c_off = D + D // 2 + 2 * (D // 16) + D // 2
    offs = [0, W, 2 * W, 3 * W, c_off, c_off + W]
    in_specs, counts = [], []
    for off in offs:
        specs, _ = _colspecs(off, W, tt, rowmap)
        in_specs += specs
        counts.append(len(specs))
    full = lambda shape: pl.BlockSpec(shape, lambda s, t: (0,) * len(shape))
    in_specs += [full(caw.shape), full(ccw.shape), full(pv.shape), full(wr.shape), full(wi.shape)]

    def body(*refs):
        refs = list(refs)
        ab, ac, ax, ag, cx, cg = [_cat(_take(refs, n)) for n in counts]
        caw_ref, ccw_ref, pv_ref, wr_ref, wi_ref = _take(refs, 5)
        mixac_ref, cv_ref, xc_ref, yc_ref = _take(refs, 4)
        carry_p, carry_cx, carry_h, a_s, b_s = refs
        t = pl.program_id(1)

        @pl.when(t == 0)
        def _():
            carry_p[...] = jnp.zeros_like(carry_p)
            carry_cx[...] = jnp.zeros_like(carry_cx)
            carry_h[...] = jnp.zeros_like(carry_h)

        ccb, br, bi, lam, na, nc = [pv_ref[k:k + 1, :] for k in range(6)]
        p = ac * ax
        cp = carry_p[...]
        cv = caw_ref[2:3, :] * p + caw_ref[1:2, :] * _shift_down(p, 1, cp) + caw_ref[0:1, :] * _shift_down(p, 2, cp)
        carry_p[...] = p[tt - 8:tt]
        cv_ref[...] = cv
        _, _, n_a = _rms(ab * cv, na)
        mix_a = n_a * (ag * _sigmoid(ag))
        ccx = carry_cx[...]
        xc = (ccw_ref[3:4, :] * cx + ccw_ref[2:3, :] * _shift_down(cx, 1, ccx) + ccw_ref[1:2, :] * _shift_down(cx, 2, ccx)
              + ccw_ref[0:1, :] * _shift_down(cx, 3, ccx) + ccb)
        carry_cx[...] = cx[tt - 8:tt]
        xc_ref[...] = xc
        r, i, ls, a, sq = _gates(xc, wr_ref, wi_ref, br, bi, lam)
        u = sq * (i * xc)
        a_c, b_c = _chunk_scan(a, u)
        a_s[...] = a_c
        b_s[...] = b_c

        def step(k, hprev):
            rows = pl.ds(pl.multiple_of(k * 8, 8), 8)
            hc = a_s[rows, :] * hprev + b_s[rows, :]
            yc_ref[rows, :] = hc
            return hc[7:8, :]

        hlast = lax.fori_loop(0, tt // 8, step, carry_h[0:1, :])
        carry_h[...] = jnp.broadcast_to(hlast, carry_h.shape)
        _, _, n_c = _rms(yc_ref[...], nc)
        mix_c = n_c * (cg * _sigmoid(cg))
        mixac_ref[...] = jnp.concatenate([mix_a, mix_c], axis=1).astype(mixac_ref.dtype)

    row_blk = lambda w: pl.BlockSpec((tt, w), lambda s, t: (rowmap(s, t), 0))
    return _pcall(
        body, name=name, grid=(T // S, nt), in_specs=in_specs,
        out_shape=(jax.ShapeDtypeStruct((T, 4 * W), BF16), jax.ShapeDtypeStruct((T, W), F32),
                   jax.ShapeDtypeStruct((T, W), F32), jax.ShapeDtypeStruct((T, W), F32)),
        out_specs=(pl.BlockSpec((tt, 2 * W), lambda s, t: (rowmap(s, t), 1)), row_blk(W), row_blk(W), row_blk(W)),
        scratch_shapes=[pltpu.VMEM((8, W), F32), pltpu.VMEM((8, W), F32), pltpu.VMEM((8, W), F32),
                        pltpu.VMEM((tt, W), F32), pltpu.VMEM((tt, W), F32)],
        compiler_params=_params(("arbitrary", "arbitrary")),
    )(*([h] * sum(counts)), caw, ccw, pv, wr, wi)


def _ac_bwd(h, cv, xc, yc, dmix, caw, ccw, pv, wr, wi, *, S, D, tt, name):
    T = h.shape[0]
    W = D // 4
    nt = S // tt
    rowmap = lambda s, t: s * nt + (nt - 1 - t)
    c_off = D + D // 2 + 2 * (D // 16) + D // 2
    offs = [0, W, 2 * W, 3 * W, c_off, c_off + W]
    in_specs, counts = [], []
    for off in offs:
        specs, _ = _colspecs(off, W, tt, rowmap)
        in_specs += specs
        counts.append(len(specs))
    row_blk = lambda w, cb=0: pl.BlockSpec((tt, w), lambda s, t: (rowmap(s, t), cb))
    in_specs += [row_blk(W), row_blk(W), row_blk(W)]
    in_specs.append(pl.BlockSpec((8, W), lambda s, t: (jnp.maximum(rowmap(s, t) * (tt // 8) - 1, 0), 0)))
    in_specs += [row_blk(W, 2), row_blk(W, 3)]
    full = lambda shape: pl.BlockSpec(shape, lambda s, t: (0,) * len(shape))
    in_specs += [full(caw.shape), full(ccw.shape), full(pv.shape), full(wr.shape), full(wi.shape)]

    def body(*refs):
        refs = list(refs)
        ab, ac, ax, ag, cx, cg = [_cat(_take(refs, n)) for n in counts]
        cv_ref, xc_ref, yc_ref, halo_ref, dma_ref, dmc_ref, caw_ref, ccw_ref, pv_ref, wr_ref, wi_ref = _take(refs, 11)
        dha_ref, dhc_ref, vec_ref, dwr_ref, dwi_ref = _take(refs, 5)
        carry_dcv, carry_dxc, carry_a, carry_g, c_s, b_s, g_s = refs
        s_id, t = pl.program_id(0), pl.program_id(1)

        @pl.when(t == 0)
        def _():
            for cr in (carry_dcv, carry_dxc, carry_a, carry_g):
                cr[...] = jnp.zeros_like(cr)

        @pl.when((t == 0) & (s_id == 0))
        def _():
            vec_ref[...] = jnp.zeros_like(vec_ref)
            dwr_ref[...] = jnp.zeros_like(dwr_ref)
            dwi_ref[...] = jnp.zeros_like(dwi_ref)

        def acc_row(k, val):
            vec_ref[k:k + 1, :] += jnp.sum(val, axis=0, keepdims=True)

        ccb, br, bi, lam, na, nc = [pv_ref[k:k + 1, :] for k in range(6)]
        cv = cv_ref[...]
        dmix_a = dma_ref[...]
        p = ac * ax
        xn, rstd, n_a = _rms(ab * cv, na)
        sg = _sigmoid(ag)
        dn = dmix_a * (ag * sg)
        dag = dmix_a * n_a * (sg * (1.0 + ag * (1.0 - sg)))
        acc_row(3, dn * xn)
        dya = _rms_bwd(dn, xn, rstd, na)
        dab = dya * cv
        dcv = dya * ab
        cd = carry_dcv[...]
        d1, d2 = _shift_up(dcv, 1, cd), _shift_up(dcv, 2, cd)
        dp = caw_ref[2:3, :] * dcv + caw_ref[1:2, :] * d1 + caw_ref[0:1, :] * d2
        acc_row(2, p * dcv)
        acc_row(1, p * d1)
        acc_row(0, p * d2)
        carry_dcv[...] = dcv[0:8]
        dha_ref[...] = jnp.concatenate([dab, dp * ax, dp * ac, dag], axis=1).astype(dha_ref.dtype)
        xc = xc_ref[...]
        yc = yc_ref[...]
        dmix_c = dmc_ref[...]
        xn, rstd, n_c = _rms(yc, nc)
        sg = _sigmoid(cg)
        dn = dmix_c * (cg * sg)
        dcg = dmix_c * n_c * (sg * (1.0 + cg * (1.0 - sg)))
        acc_row(12, dn * xn)
        dyc = _rms_bwd(dn, xn, rstd, nc)
        r, i, ls, a, sq = _gates(xc, wr_ref, wi_ref, br, bi, lam)
        halo = jnp.where(t == nt - 1, 0.0, halo_ref[...])
        hprev = _shift_down(yc, 1, halo)
        c_c, b_c = _chunk_scan_rev(_shift_up(a, 1, carry_a[...]), dyc)
        c_s[...] = c_c
        b_s[...] = b_c

        def step(k, gnext):
            rows = pl.ds(pl.multiple_of((tt // 8 - 1 - k) * 8, 8), 8)
            gc = b_s[rows, :] + c_s[rows, :] * gnext
            g_s[rows, :] = gc
            return gc[0:1, :]

        lax.fori_loop(0, tt // 8, step, carry_g[0:1, :])
        g = g_s[...]
        carry_g[...] = g[0:8]
        carry_a[...] = a[0:8]
        da = g * hprev
        ixc = i * xc
        dsq = g * ixc
        di = g * sq * xc
        dxc = g * sq * i
        dla = da * a - dsq * (a * a) / sq
        dr = dla * (RG_C * ls)
        acc_row(11, dla * (RG_C * r) * _sigmoid(-lam))
        dgr = dr * r * (1.0 - r)
        dgi = di * i * (1.0 - i)
        acc_row(9, dgr)
        acc_row(10, dgi)
        hw = W // N_RG_HEADS
        parts = []
        for hd in range(N_RG_HEADS):
            sl = slice(hd * hw, (hd + 1) * hw)
            dwr_ref[hd] += _dot(xc[:, sl], dgr[:, sl], TN)
            dwi_ref[hd] += _dot(xc[:, sl], dgi[:, sl], TN)
            parts.append(_dot(dgr[:, sl], wr_ref[hd], NT) + _dot(dgi[:, sl], wi_ref[hd], NT))
        dxc = dxc + jnp.concatenate(parts, axis=1)
        ce = carry_dxc[...]
        e1, e2, e3 = _shift_up(dxc, 1, ce), _shift_up(dxc, 2, ce), _shift_up(dxc, 3, ce)
        dcx = ccw_ref[3:4, :] * dxc + ccw_ref[2:3, :] * e1 + ccw_ref[1:2, :] * e2 + ccw_ref[0:1, :] * e3
        acc_row(7, cx * dxc)
        acc_row(6, cx * e1)
        acc_row(5, cx * e2)
        acc_row(4, cx * e3)
        acc_row(8, dxc)
        carry_dxc[...] = dxc[0:8]
        dhc_ref[...] = jnp.concatenate([dcx, dcg], axis=1).astype(dhc_ref.dtype)

    const = lambda shape: pl.BlockSpec(shape, lambda s, t: (0,) * len(shape))
    return _pcall(
        body, name=name, grid=(T // S, nt), in_specs=in_specs,
        out_shape=(jax.ShapeDtypeStruct((T, 4 * W), BF16), jax.ShapeDtypeStruct((T, 2 * W), BF16),
                   jax.ShapeDtypeStruct((16, W), F32), jax.ShapeDtypeStruct(wr.shape, F32),
                   jax.ShapeDtypeStruct(wi.shape, F32)),
        out_specs=(row_blk(4 * W), row_blk(2 * W), const((16, W)), const(wr.shape), const(wi.shape)),
        scratch_shapes=[pltpu.VMEM((8, W), F32)] * 4 + [pltpu.VMEM((tt, W), F32)] * 3,
        compiler_params=_params(("arbitrary", "arbitrary")),
    )(*([h] * sum(counts)), cv, xc, yc, yc, dmix, dmix, caw, ccw, pv, wr, wi)


def _lo_mask():
    return lax.broadcasted_iota(jnp.int32, (1, 2 * HEAD_DIM), 1) < HEAD_DIM


def _dup(blk, odd, lo):
    rot = pltpu.roll(blk, HEAD_DIM, 1)
    return jnp.where(lo, rot, blk) if odd else jnp.where(lo, blk, rot)


def _stack_heads(x, hh, lo, masked):
    parts = []
    for g in range(KV_GROUP):
        jq = hh * KV_GROUP + g
        pb = x[:, (jq // 2) * 128:(jq // 2 + 1) * 128]
        if masked:
            pb = jnp.where(lo if jq % 2 == 0 else jnp.logical_not(lo), pb, 0.0)
        parts.append(pb)
    return jnp.concatenate(parts, axis=0)


def _unstack_pairs_t(st_t):
    hi = lax.broadcasted_iota(jnp.int32, (2 * HEAD_DIM, BLOCK), 0) >= HEAD_DIM
    return [jnp.where(hi, st_t[:, (2 * pi + 1) * BLOCK:(2 * pi + 2) * BLOCK], st_t[:, (2 * pi) * BLOCK:(2 * pi + 1) * BLOCK]).T
            for pi in range(KV_GROUP // 2)]


def _window(ref, n):
    prev = ref[pl.ds(pl.multiple_of(jnp.maximum(n - 1, 0) * BLOCK, BLOCK), BLOCK), :]
    cur = ref[pl.ds(pl.multiple_of(n * BLOCK, BLOCK), BLOCK), :]
    return jnp.concatenate([prev, cur], axis=0)


def _valid_mask_t(n):
    cols = KV_GROUP * BLOCK
    kj = lax.broadcasted_iota(jnp.int32, (2 * BLOCK, cols), 0)
    qi = lax.broadcasted_iota(jnp.int32, (2 * BLOCK, cols), 1) & (BLOCK - 1)
    dist = qi + BLOCK - kj
    return (dist >= 0) & (dist < BLOCK) & ((n > 0) | (kj >= BLOCK))


def _sink_row(sinks_ref, layer, hh):
    return jnp.concatenate([jnp.full((1, BLOCK), sinks_ref[layer, hh * KV_GROUP + g], F32) for g in range(KV_GROUP)],
                           axis=1)


def _softmax_t(qs, kdup, valid, sink):
    s = jnp.where(valid, _dot(kdup, qs, NT), NEG_INF)
    m = jnp.maximum(jnp.max(s, axis=0, keepdims=True), sink)
    e = jnp.exp(s - m)
    es = jnp.exp(sink - m)
    r = 1.0 / (jnp.sum(e, axis=0, keepdims=True) + es)
    return e * r, es * r


def _lane_sums_row(x):
    hi = x.astype(BF16)
    lo = (x - hi.astype(F32)).astype(BF16)
    ones = jnp.ones((8, x.shape[1]), BF16)
    dims = (NT, ((), ()))
    return (lax.dot_general(ones, hi, dims, preferred_element_type=F32)
            + lax.dot_general(ones, lo, dims, preferred_element_type=F32))[0:1]


def _attn_fwd(h, sinks, layer, *, S, D, name):
    T = h.shape[0]
    WB, KVW = D // 2, D // 16
    nb = S // BLOCK
    n_kv = KVW // HEAD_DIM

    def body(q_ref, k_ref, v_ref, sinks_ref, o_ref):
        n = pl.program_id(1)
        lo = _lo_mask()
        q = q_ref[...] * (HEAD_DIM ** -0.5)
        kk, vv = _window(k_ref, n), _window(v_ref, n)
        valid = _valid_mask_t(n)
        blocks = []
        for hh in range(n_kv):
            cb = slice((hh // 2) * 128, (hh // 2 + 1) * 128)
            kdup, vdup = _dup(kk[:, cb], hh % 2, lo), _dup(vv[:, cb], hh % 2, lo)
            p_t, _ = _softmax_t(_stack_heads(q, hh, lo, True), kdup, valid, _sink_row(sinks_ref, layer, hh))
            blocks += _unstack_pairs_t(_dot(vdup, p_t, TN))
        o_ref[...] = jnp.concatenate(blocks, axis=1)

    return _pcall(
        body, name=name, grid=(T // S, nb),
        in_specs=[pl.BlockSpec((BLOCK, WB), lambda s, n: (s * nb + n, D // WB)),
                  pl.BlockSpec((S, KVW), lambda s, n: (s, (D + WB) // KVW)),
                  pl.BlockSpec((S, KVW), lambda s, n: (s, (D + WB) // KVW + 1)),
                  pl.BlockSpec(memory_space=pltpu.SMEM)],
        out_shape=jax.ShapeDtypeStruct((T, WB), F32),
        out_specs=pl.BlockSpec((BLOCK, WB), lambda s, n: (s * nb + n, 0)),
        compiler_params=_params(("arbitrary", "arbitrary")),
    )(h, h, h, sinks)


def _attn_bwd(h, yb, dyb, sinks, layer, *, S, D, name):
    T = h.shape[0]
    WB, KVW = D // 2, D // 16
    nb = S // BLOCK
    n_kv = KVW // HEAD_DIM

    def body(q_ref, k_ref, v_ref, o_ref, do_ref, sinks_ref, dq_ref, dk_ref, dv_ref, dsink_ref, dk_acc, dv_acc):
        s_id, n = pl.program_id(0), pl.program_id(1)
        lo = _lo_mask()

        @pl.when(n == 0)
        def _():
            dk_acc[...] = jnp.zeros_like(dk_acc)
            dv_acc[...] = jnp.zeros_like(dv_acc)

        @pl.when((n == 0) & (s_id == 0))
        def _():
            dsink_ref[...] = jnp.zeros_like(dsink_ref)

        scale = HEAD_DIM ** -0.5
        q, o, do = q_ref[...] * scale, o_ref[...], do_ref[...]
        kk, vv = _window(k_ref, n), _window(v_ref, n)
        valid = _valid_mask_t(n)
        lane = lax.broadcasted_iota(jnp.int32, dsink_ref.shape, 1)
        dq_blocks, dk_heads, dv_heads = [], [], []
        dsink = jnp.zeros(dsink_ref.shape, F32)
        for hh in range(n_kv):
            cb = slice((hh // 2) * 128, (hh // 2 + 1) * 128)
            kdup, vdup = _dup(kk[:, cb], hh % 2, lo), _dup(vv[:, cb], hh % 2, lo)
            qs = _stack_heads(q, hh, lo, True)
            dos = _stack_heads(do, hh, lo, True)
            delta = _lane_sums_row(dos * _stack_heads(o, hh, lo, False))
            p_t, psink = _softmax_t(qs, kdup, valid, _sink_row(sinks_ref, layer, hh))
            dvr = _dot(p_t, dos, NN)
            dv_heads.append(dvr + pltpu.roll(dvr, HEAD_DIM, 1))
            ds_t = p_t * (_dot(vdup, dos, NT) - delta)
            dq_blocks += [b * scale for b in _unstack_pairs_t(_dot(kdup, ds_t, TN))]
            dkr = _dot(ds_t, qs, NN)
            dk_heads.append(dkr + pltpu.roll(dkr, HEAD_DIM, 1))
            dsk = -psink * delta
            for g in range(KV_GROUP):
                tot = jnp.sum(dsk[:, g * BLOCK:(g + 1) * BLOCK], axis=1, keepdims=True)
                dsink = dsink + jnp.where(lane == hh * KV_GROUP + g, tot, 0.0)
        dsink_ref[...] += dsink
        dq_ref[...] = jnp.concatenate(dq_blocks, axis=1).astype(dq_ref.dtype)
        pair = lambda hs: jnp.concatenate([jnp.where(lo, hs[2 * m], hs[2 * m + 1]) for m in range(n_kv // 2)], axis=1)
        dkk, dvv = pair(dk_heads), pair(dv_heads)
        prev = pl.ds(pl.multiple_of(jnp.maximum(n - 1, 0) * BLOCK, BLOCK), BLOCK)
        cur = pl.ds(pl.multiple_of(n * BLOCK, BLOCK), BLOCK)
        dk_acc[prev, :] += dkk[:BLOCK]
        dk_acc[cur, :] += dkk[BLOCK:]
        dv_acc[prev, :] += dvv[:BLOCK]
        dv_acc[cur, :] += dvv[BLOCK:]

        @pl.when(n == nb - 1)
        def _():
            dk_ref[...] = dk_acc[...].astype(dk_ref.dtype)
            dv_ref[...] = dv_acc[...].astype(dv_ref.dtype)

    blk = lambda cb=0: pl.BlockSpec((BLOCK, WB), lambda s, n: (s * nb + n, cb))
    seq = lambda cb=0: pl.BlockSpec((S, KVW), lambda s, n: (s, cb))
    return _pcall(
        body, name=name, grid=(T // S, nb),
        in_specs=[blk(D // WB), seq((D + WB) // KVW), seq((D + WB) // KVW + 1), blk(), blk(),
                  pl.BlockSpec(memory_space=pltpu.SMEM)],
        out_shape=(jax.ShapeDtypeStruct((T, WB), BF16), jax.ShapeDtypeStruct((T, KVW), BF16),
                   jax.ShapeDtypeStruct((T, KVW), BF16), jax.ShapeDtypeStruct((8, 128), F32)),
        out_specs=(blk(), seq(), seq(), pl.BlockSpec((8, 128), lambda s, n: (0, 0))),
        scratch_shapes=[pltpu.VMEM((S, KVW), F32), pltpu.VMEM((S, KVW), F32)],
        compiler_params=_params(("arbitrary", "arbitrary")),
    )(h, h, h, yb, dyb, sinks)


def _bg_specs(D, tm):
    return _colspecs(D + D // 2 + 2 * (D // 16), D // 2, tm, lambda i: i)


def _mixb_fwd(yb, h, nb_g, mix, *, D, tm, name):
    T, WB = yb.shape
    bg_specs, _ = _bg_specs(D, tm)

    def body(*refs):
        refs = list(refs)
        yb_ref = refs.pop(0)
        bg = _cat(_take(refs, len(bg_specs)))
        g_ref, _, o_ref = refs
        _, _, nrm = _rms(yb_ref[...], g_ref[...])
        o_ref[...] = (nrm * (bg * _sigmoid(bg))).astype(o_ref.dtype)

    row = pl.BlockSpec((tm, WB), lambda i: (i, 0))
    return _pcall(body, name=name, grid=(T // tm,),
                  in_specs=[row] + bg_specs + [pl.BlockSpec((1, WB), lambda i: (0, 0)), ANY],
                  out_shape=jax.ShapeDtypeStruct(mix.shape, mix.dtype), out_specs=row,
                  input_output_aliases={len(bg_specs) + 2: 0},
                  compiler_params=_params(("arbitrary",)))(yb, *([h] * len(bg_specs)), nb_g, mix)


def _mixb_bwd(yb, h, dmix, nb_g, *, D, tm, name):
    T, WB = yb.shape
    bg_specs, _ = _bg_specs(D, tm)
    dm_specs, _ = _colspecs(0, WB, tm, lambda i: i)

    def body(*refs):
        refs = list(refs)
        yb_ref = refs.pop(0)
        bg = _cat(_take(refs, len(bg_specs)))
        dmix_b = _cat(_take(refs, len(dm_specs)))
        g_ref, dyb_ref, dbg_ref, dg_ref = refs

        @pl.when(pl.program_id(0) == 0)
        def _():
            dg_ref[...] = jnp.zeros_like(dg_ref)

        gamma = g_ref[...]
        xn, rstd, nrm = _rms(yb_ref[...], gamma)
        sg = _sigmoid(bg)
        dn = dmix_b * (bg * sg)
        dbg_ref[...] = (dmix_b * nrm * (sg * (1.0 + bg * (1.0 - sg)))).astype(dbg_ref.dtype)
        dg_ref[0:1, :] += jnp.sum(dn * xn, axis=0, keepdims=True)
        dyb_ref[...] = _rms_bwd(dn, xn, rstd, gamma)

    row = pl.BlockSpec((tm, WB), lambda i: (i, 0))
    return _pcall(body, name=name, grid=(T // tm,),
                  in_specs=[row] + bg_specs + dm_specs + [pl.BlockSpec((1, WB), lambda i: (0, 0))],
                  out_shape=(jax.ShapeDtypeStruct((T, WB), F32), jax.ShapeDtypeStruct((T, WB), BF16),
                             jax.ShapeDtypeStruct((8, WB), F32)),
                  out_specs=(row, row, pl.BlockSpec((8, WB), lambda i: (0, 0))),
                  compiler_params=_params(("arbitrary",)))(yb, *([h] * len(bg_specs)), *([dmix] * len(dm_specs)), nb_g)


def _concat_cols(parts, *, tm, name):
    parts = [p if isinstance(p, tuple) else (p, 0, p.shape[1]) for p in parts]
    T = parts[0][0].shape[0]
    total = sum(w for _, _, w in parts)

    def body(*refs):
        refs[-1][...] = jnp.concatenate([r[...] for r in refs[:-1]], axis=1)

    return _pcall(body, name=name, grid=(T // tm,),
                  in_specs=[pl.BlockSpec((tm, w), functools.partial(lambda cb, i: (i, cb), cb)) for _, cb, w in parts],
                  out_shape=jax.ShapeDtypeStruct((T, total), parts[0][0].dtype),
                  out_specs=pl.BlockSpec((tm, total), lambda i: (i, 0)),
                  compiler_params=_params(("parallel",)))(*[a for a, _, _ in parts])


def _ln_fwd(z, g, b, *, tm, name):
    T, D = z.shape

    def body(z_ref, g_ref, b_ref, y_ref, yb_ref):
        zv = z_ref[...]
        mu = jnp.mean(zv, axis=-1, keepdims=True)
        zc = zv - mu
        var = jnp.mean(zc * zc, axis=-1, keepdims=True)
        y = zc * lax.rsqrt(var + LN_EPS) * g_ref[...] + b_ref[...]
        y_ref[...] = y
        yb_ref[...] = y.astype(BF16)

    row = pl.BlockSpec((tm, D), lambda i: (i, 0))
    vec = pl.BlockSpec((1, D), lambda i: (0, 0))
    return _pcall(body, name=name, grid=(T // tm,), in_specs=[row, vec, vec],
                  out_shape=(jax.ShapeDtypeStruct((T, D), F32), jax.ShapeDtypeStruct((T, D), BF16)),
                  out_specs=(row, row), compiler_params=_params(("parallel",)))(z, g, b)


def _ln_bwd(z, dy, g, *, tm, name, deps=()):
    T, D = z.shape

    def body(z_ref, dy_ref, g_ref, *rest):
        dz_ref, dzb_ref, dgb_ref = rest[len(deps):]

        @pl.when(pl.program_id(0) == 0)
        def _():
            dgb_ref[...] = jnp.zeros_like(dgb_ref)

        zv, dyv = z_ref[...], dy_ref[...]
        mu = jnp.mean(zv, axis=-1, keepdims=True)
        zc = zv - mu
        rstd = lax.rsqrt(jnp.mean(zc * zc, axis=-1, keepdims=True) + LN_EPS)
        xh = zc * rstd
        dxh = dyv * g_ref[...]
        dz = rstd * (dxh - jnp.mean(dxh, axis=-1, keepdims=True) - xh * jnp.mean(dxh * xh, axis=-1, keepdims=True))
        dz_ref[...] = dz
        dzb_ref[...] = dz.astype(BF16)
        dgb_ref[0:1, :] += jnp.sum(dyv * xh, axis=0, keepdims=True)
        dgb_ref[1:2, :] += jnp.sum(dyv, axis=0, keepdims=True)

    row = pl.BlockSpec((tm, D), lambda i: (i, 0))
    return _pcall(body, name=name, grid=(T // tm,),
                  in_specs=[row, row, pl.BlockSpec((1, D), lambda i: (0, 0))] + [ANY] * len(deps),
                  out_shape=(jax.ShapeDtypeStruct((T, D), F32), jax.ShapeDtypeStruct((T, D), BF16),
                             jax.ShapeDtypeStruct((8, D), F32)),
                  out_specs=(row, row, pl.BlockSpec((8, D), lambda i: (0, 0))),
                  compiler_params=_params(("arbitrary",)))(z, dy, g, *deps)


def _loss_head(y, target, *, tm, name):
    T, D = y.shape

    def body(y_ref, t_ref, dy_ref, loss_ref):
        @pl.when(pl.program_id(0) == 0)
        def _():
            loss_ref[...] = jnp.zeros_like(loss_ref)

        err = y_ref[...] - t_ref[...]
        dy_ref[...] = err / D
        loss_ref[...] += 0.5 * jnp.sum(jnp.mean(err * err, axis=-1, keepdims=True), axis=0, keepdims=True)

    row = pl.BlockSpec((tm, D), lambda i: (i, 0))
    return _pcall(body, name=name, grid=(T // tm,), in_specs=[row, row],
                  out_shape=(jax.ShapeDtypeStruct((T, D), F32), jax.ShapeDtypeStruct((1, 1), F32)),
                  out_specs=(row, pl.BlockSpec((1, 1), lambda i: (0, 0))),
                  compiler_params=_params(("arbitrary",)))(y, target)


def _cast_bf16(w, layer, *, name):
    _, R, C = w.shape
    tr = _tile(R, 512, 8)

    def body(w_ref, o_ref):
        o_ref[...] = w_ref[...].astype(BF16)

    return _pcall(body, name=name, grid=(R // tr,), in_specs=[pl.BlockSpec((None, tr, C), lambda i: (layer, i, 0))],
                  out_shape=jax.ShapeDtypeStruct((R, C), BF16), out_specs=pl.BlockSpec((tr, C), lambda i: (i, 0)),
                  compiler_params=_params(("parallel",)))(w)


def _cast_shard(w, layer, kind, *, name):
    _, R, C = w.shape
    tr = _tile(R, 512, 16)
    nrb = R // tr
    if kind == "in":
        full, o_idx = (R, N_CHIPS * C), lambda i: (i, _my_chip())
    else:
        full, o_idx = (N_CHIPS * R, C), lambda i: (_out_pos(_my_chip()) * nrb + i, 0)

    def body(w_ref, o_ref):
        o_ref[...] = w_ref[...].astype(BF16)

    return _pcall(body, name=name, grid=(nrb,), in_specs=[pl.BlockSpec((None, tr, C), lambda i: (layer, i, 0))],
                  out_shape=jax.ShapeDtypeStruct(full, BF16), out_specs=pl.BlockSpec((tr, C), o_idx),
                  compiler_params=_params(("parallel",)))(w)


def _adamw_layer(g, w, m, v, layer, bufs, *, name):
    L, R, C = w.shape
    tr = _tile(R, max(8, (1 << 19) // C // 8 * 8), 8)
    if bufs is None:
        bufs = [lax.empty((L, R, C), F32) for _ in range(4)]

    def body(g_ref, w_ref, m_ref, v_ref, b0, b1, b2, b3, go_ref, d_ref, nm_ref, nv_ref):
        gv = g_ref[...]
        nm = ADAM_B1 * m_ref[...] + (1.0 - ADAM_B1) * gv
        nv = ADAM_B2 * v_ref[...] + (1.0 - ADAM_B2) * (gv * gv)
        m_hat = nm / (1.0 - ADAM_B1 ** ADAM_STEP)
        v_hat = nv / (1.0 - ADAM_B2 ** ADAM_STEP)
        go_ref[...] = gv
        d_ref[...] = -ADAM_LR * (m_hat / (jnp.sqrt(v_hat) + ADAM_EPS) + ADAM_WD * w_ref[...])
        nm_ref[...] = nm
        nv_ref[...] = nv

    lay = pl.BlockSpec((None, tr, C), lambda i: (layer, i, 0))
    shp = jax.ShapeDtypeStruct((L, R, C), F32)
    return list(_pcall(body, name=name, grid=(R // tr,),
                       in_specs=[pl.BlockSpec((tr, C), lambda i: (i, 0)), lay, lay, lay] + [ANY] * 4,
                       out_shape=(shp,) * 4, out_specs=(lay,) * 4, input_output_aliases={4 + k: k for k in range(4)},
                       compiler_params=_params(("parallel",)))(g, w, m, v, *bufs))


def _adamw(g, w, m, v, *, name):
    R, C = g.shape
    tr = _tile(R, max(8, (1 << 19) // C // 8 * 8), 8)

    def body(g_ref, w_ref, m_ref, v_ref, d_ref, nm_ref, nv_ref):
        gv = g_ref[...]
        nm = ADAM_B1 * m_ref[...] + (1.0 - ADAM_B1) * gv
        nv = ADAM_B2 * v_ref[...] + (1.0 - ADAM_B2) * (gv * gv)
        m_hat = nm / (1.0 - ADAM_B1 ** ADAM_STEP)
        v_hat = nv / (1.0 - ADAM_B2 ** ADAM_STEP)
        d_ref[...] = -ADAM_LR * (m_hat / (jnp.sqrt(v_hat) + ADAM_EPS) + ADAM_WD * w_ref[...])
        nm_ref[...] = nm
        nv_ref[...] = nv

    blk = pl.BlockSpec((tr, C), lambda i: (i, 0))
    shp = jax.ShapeDtypeStruct((R, C), F32)
    return _pcall(body, name=name, grid=(R // tr,), in_specs=[blk] * 4, out_shape=(shp, shp, shp),
                  out_specs=(blk, blk, blk), compiler_params=_params(("parallel",)))(g, w, m, v)


def _my_core():
    return lax.axis_index("c")


def _my_chip():
    return 2 * lax.axis_index("x") + lax.axis_index("y")


def _out_pos(chip):
    assert N_CHIPS == 4
    return jnp.where(chip == 3, 3, (chip + 2) % 3)


def _pair_sum(mine, theirs, *, half_axis, name):
    R, C = theirs.shape
    tr, tc = _tile(R, 512, 16), _tile(C, 2048)
    nrb, ncb = R // tr, C // tc

    def body(a_ref, b_ref, o_ref):
        o_ref[...] = (a_ref[...].astype(F32) + b_ref[...].astype(F32)).astype(BF16)

    if half_axis == 0:
        a_idx = lambda i, j: (_my_core() * nrb + i, j)
    else:
        a_idx = lambda i, j: (i, _my_core() * ncb + j)
    blk = pl.BlockSpec((tr, tc), lambda i, j: (i, j))
    return _pcall(body, name=name, grid=(nrb, ncb), in_specs=[pl.BlockSpec((tr, tc), a_idx), blk], out_specs=blk,
                  out_shape=jax.ShapeDtypeStruct(theirs.shape, BF16),
                  compiler_params=_params(("parallel", "parallel")))(mine, theirs)


def _final_sum(own, got, *, own_axis, out_shape, out_axis, name):
    _, R, C = got.shape
    tr, tc = _tile(R, 512, 16), _tile(C, 1024)
    nrb, ncb = R // tr, C // tc

    def body(a_ref, q_ref, o_ref):
        o_ref[...] = ((a_ref[...].astype(F32) + q_ref[0].astype(F32)) + q_ref[1].astype(F32)) + q_ref[2].astype(F32)

    if own_axis == 1:
        a_idx = lambda i, j: (i, _my_chip() * ncb + j)
    else:
        a_idx = lambda i, j: (_out_pos(_my_chip()) * nrb + i, j)
    if out_axis == 0:
        o_idx = lambda i, j: (_my_core() * nrb + i, j)
    else:
        o_idx = lambda i, j: (i, _my_core() * ncb + j)
    return _pcall(body, name=name, grid=(nrb, ncb),
                  in_specs=[pl.BlockSpec((tr, tc), a_idx), pl.BlockSpec((3, tr, tc), lambda i, j: (0, i, j))],
                  out_specs=pl.BlockSpec((tr, tc), o_idx), out_shape=jax.ShapeDtypeStruct(out_shape, F32),
                  compiler_params=_params(("parallel", "parallel")))(own, got)


def _sum_devices(gathered, *, name):
    _, R, C = gathered.shape
    tr = _tile(R, 280, 8)

    def body(g_ref, o_ref):
        acc = g_ref[0]
        for d in range(1, N_DEV):
            acc = acc + g_ref[d]
        o_ref[...] = acc

    return _pcall(body, name=name, grid=(R // tr,), in_specs=[pl.BlockSpec((N_DEV, tr, C), lambda i: (0, i, 0))],
                  out_shape=jax.ShapeDtypeStruct((R, C), F32), out_specs=pl.BlockSpec((tr, C), lambda i: (i, 0)),
                  compiler_params=_params(("parallel",)))(gathered)


def _position():
    x, y, c = lax.axis_index("x"), lax.axis_index("y"), lax.axis_index("c")
    chips = [(1 - x, y), (x, 1 - y), (1 - x, 1 - y)]
    return x, y, c, chips


def _remote(src, dst, send_sems, recv_sems, k, to):
    return pltpu.make_async_remote_copy(src_ref=src, dst_ref=dst, send_sem=send_sems.at[k], recv_sem=recv_sems.at[k],
                                        device_id=to, device_id_type=MESH)


def _r(ref, start, n):
    return ref.at[pl.ds(pl.multiple_of(start, 16), n), :]


def _c(ref, start, n):
    return ref.at[:, pl.ds(pl.multiple_of(start, 128), n)]


class _part:
    def __init__(self, ins, outs, plan, n, n_local=0, aliased=0):
        self.ins, self.outs, self.plan, self.n, self.n_local, self.aliased = ins, outs, plan, n, n_local, aliased


def _comm_scratch(parts):
    if not parts:
        return []
    n, nl = sum(p.n for p in parts), sum(p.n_local for p in parts)
    return [pltpu.SemaphoreType.DMA((n,)), pltpu.SemaphoreType.DMA((n,)), pltpu.SemaphoreType.DMA((max(nl, 1),))]


def _comm_aliases(parts, in_base, out_base):
    aliases, ii, oi = {}, in_base, out_base
    for p in parts:
        aliases.update({ii + k: oi + k for k in range(p.aliased)})
        ii += len(p.ins)
        oi += len(p.outs)
    return aliases


def _comm_run(parts, phase, in_refs, out_refs, send_sems, recv_sems, local_sems):
    pos = _position()
    me = pos[:3]
    ii = oi = si = li = 0
    for p in parts:
        sends, recvs, locs = p.plan(in_refs[ii:ii + len(p.ins)], out_refs[oi:oi + len(p.outs)], pos)
        assert len(sends) == len(recvs) == p.n and len(locs) == p.n_local
        if phase == "start":
            for k, (src, dst) in enumerate(locs):
                pltpu.make_async_copy(src, dst, local_sems.at[li + k]).start()
            for k, (src, dst, to) in enumerate(sends):
                _remote(src, dst, send_sems, recv_sems, si + k, to).start()
        else:
            for k, dst in enumerate(recvs):
                _remote(dst, dst, send_sems, recv_sems, si + k, me).wait_recv()
            for k, (src, dst, to) in enumerate(sends):
                _remote(src, dst, send_sems, recv_sems, si + k, to).wait_send()
            for k, (src, dst) in enumerate(locs):
                pltpu.make_async_copy(src, dst, local_sems.at[li + k]).wait()
        ii, oi, si, li = ii + len(p.ins), oi + len(p.outs), si + p.n, li + p.n_local


def _comm_call(parts, *, name):
    n_in = sum(len(p.ins) for p in parts)
    n_out = sum(len(p.outs) for p in parts)

    def body(*refs):
        refs = list(refs)
        cin, cout = _take(refs, n_in), _take(refs, n_out)
        _comm_run(parts, "start", cin, cout, *refs)
        _comm_run(parts, "finish", cin, cout, *refs)

    return list(_pcall(body, name=name, in_specs=[ANY] * n_in, out_specs=[ANY] * n_out,
                       out_shape=[s for p in parts for s in p.outs], scratch_shapes=_comm_scratch(parts),
                       input_output_aliases=_comm_aliases(parts, 0, 0))(*[a for p in parts for a in p.ins]))


HBM = pl.BlockSpec(memory_space=pltpu.HBM)
SEM = pl.BlockSpec(memory_space=pltpu.SEMAPHORE)
EFFECT = pltpu.SideEffectType.DATAFLOW_SIDE_EFFECTING


def _split_refs(parts, arr):
    out, i = [], 0
    for p in parts:
        ins = arr[i:i + len(p.ins)]
        i += len(p.ins)
        lands = arr[i:i + len(p.outs) - p.aliased]
        i += len(lands)
        out.append((ins, list(ins[:p.aliased]) + list(lands)))
    return out


def _split_start(parts, *, name):
    assert all(p.n_local == 0 for p in parts)
    arrays = []
    for p in parts:
        arrays += list(p.ins) + [lax.empty(s.shape, s.dtype) for s in p.outs[p.aliased:]]
    n, na = sum(p.n for p in parts), len(arrays)

    def body(*refs):
        refs = list(refs)
        arr = _take(refs, na)
        sems = _take(refs, 2 * n)
        token = refs[na]
        pos = _position()
        k = 0
        for p, (ins, outs) in zip(parts, _split_refs(parts, arr)):
            sends, _, _ = p.plan(ins, outs, pos)
            for src, dst, to in sends:
                pltpu.make_async_remote_copy(src_ref=src, dst_ref=dst, send_sem=sems[k], recv_sem=sems[n + k],
                                             device_id=to, device_id_type=MESH).start()
                k += 1
        token[...] = jnp.zeros_like(token)

    res = _pcall(
        body, name=name,
        out_shape=[pltpu.SemaphoreType.DMA(())] * (2 * n) + [pltpu.HBM(a.shape, a.dtype) for a in arrays]
        + [jax.ShapeDtypeStruct((8, 128), F32)],
        in_specs=[HBM] * na, out_specs=[SEM] * (2 * n) + [HBM] * na + [pl.BlockSpec(memory_space=pltpu.VMEM)],
        input_output_aliases={i: 2 * n + i for i in range(na)},
        compiler_params=pltpu.CompilerParams(has_side_effects=EFFECT),
    )(*[pltpu.with_memory_space_constraint(a, pltpu.HBM) for a in arrays])
    return (list(res[:2 * n]), list(res[2 * n:2 * n + na])), res[-1]


def _split_wait(parts, state, after, *, name):
    sems, arrays = state
    n, na = len(sems) // 2, len(arrays)

    def body(*refs):
        refs = list(refs)
        arr = _take(refs, na)
        sm = _take(refs, 2 * n)
        pos = _position()
        me = pos[:3]
        k = 0
        for p, (ins, outs) in zip(parts, _split_refs(parts, arr)):
            sends, recvs, _ = p.plan(ins, outs, pos)
            for (src, dst, to), land in zip(sends, recvs):
                pltpu.make_async_remote_copy(src_ref=src, dst_ref=dst, send_sem=sm[k], recv_sem=sm[n + k],
                                             device_id=to, device_id_type=MESH).wait_send()
                pltpu.make_async_remote_copy(src_ref=land, dst_ref=land, send_sem=sm[k], recv_sem=sm[n + k],
                                             device_id=me, device_id_type=MESH).wait_recv()
                k += 1

    res = _pcall(
        body, name=name, out_shape=[pltpu.HBM(a.shape, a.dtype) for a in arrays],
        in_specs=[HBM] * na + [SEM] * (2 * n) + [ANY] * len(after), out_specs=[HBM] * na,
        input_output_aliases={i: i for i in range(na)},
        compiler_params=pltpu.CompilerParams(has_side_effects=EFFECT),
    )(*arrays, *sems, *after)
    return _split_refs(parts, list(res))


def _slab(wg, kind, chip, half):
    if kind == "in":
        d, ns = wg.shape[0], wg.shape[1] // N_CHIPS
        return _c(_r(wg, half * (d // 2), d // 2), chip * ns, ns)
    rs = wg.shape[0] // N_CHIPS
    return _r(wg, _out_pos(chip) * rs + half * (rs // 2), rs // 2)


def _gather_ici(wg, kind):
    def plan(ins, outs, pos):
        x, y, c, chips = pos
        (ref,) = outs
        mine = _slab(ref, kind, 2 * x + y, c)
        return [(mine, mine, (*chip, c)) for chip in chips], [_slab(ref, kind, 2 * px + py, c) for px, py in chips], []

    return _part([wg], [jax.ShapeDtypeStruct(wg.shape, wg.dtype)], plan, 3, aliased=1)


def _gather_d2d(wg, kind):
    def plan(ins, outs, pos):
        x, y, c, chips = pos
        (ref,) = outs
        sends = [(_slab(ref, kind, 2 * px + py, c), _slab(ref, kind, 2 * px + py, c), (x, y, 1 - c)) for px, py in chips]
        return sends, [_slab(ref, kind, 2 * px + py, 1 - c) for px, py in chips], []

    return _part([wg], [jax.ShapeDtypeStruct(wg.shape, wg.dtype)], plan, 3, aliased=1)


def _pair_send(gw, kind):
    rows, cols = gw.shape
    half = (rows // 2, cols) if kind == "in" else (rows, cols // 2)

    def plan(ins, outs, pos):
        x, y, c, _ = pos
        (src,), (rb,) = ins, outs
        theirs = _r(src, (1 - c) * half[0], half[0]) if kind == "in" else _c(src, (1 - c) * half[1], half[1])
        return [(theirs, rb, (x, y, 1 - c))], [rb], []

    return _part([gw], [jax.ShapeDtypeStruct(half, gw.dtype)], plan, 1)


def _chip_send(p, kind):
    rows, cols = p.shape
    shard = (rows, cols // N_CHIPS) if kind == "in" else (rows // N_CHIPS, cols)

    def plan(ins, outs, pos):
        x, y, c, chips = pos
        (src,), (q,) = ins, outs
        piece = lambda jk: (_c(src, jk * shard[1], shard[1]) if kind == "in"
                            else _r(src, _out_pos(jk) * shard[0], shard[0]))
        sends = [(piece(2 * px + py), q.at[kk], (px, py, c)) for kk, (px, py) in enumerate(chips)]
        return sends, [q.at[kk] for kk in range(3)], []

    return _part([p], [jax.ShapeDtypeStruct((3,) + shard, p.dtype)], plan, 3)


def _sibling_send(g, kind):
    rows, cols = g.shape

    def plan(ins, outs, pos):
        x, y, c, _ = pos
        (ref,) = outs
        half = (lambda h: _r(ref, h * (rows // 2), rows // 2)) if kind == "in" else (
            lambda h: _c(ref, h * (cols // 2), cols // 2))
        return [(half(c), half(c), (x, y, 1 - c))], [half(1 - c)], []

    return _part([g], [jax.ShapeDtypeStruct(g.shape, g.dtype)], plan, 1, aliased=1)


def _small_ici(block):
    def plan(ins, outs, pos):
        x, y, c, chips = pos
        (src,), (out,) = ins, outs
        mine = out.at[4 * x + 2 * y + c]
        peers = [(x, y, 1 - c)] + [(px, py, c) for px, py in chips]
        return [(src, mine, p) for p in peers], [out.at[4 * px + 2 * py + pc] for px, py, pc in peers], [(src, mine)]

    return _part([block], [jax.ShapeDtypeStruct((N_DEV,) + block.shape, block.dtype)], plan, 4, 1)


def _small_d2d(gathered):
    def plan(ins, outs, pos):
        x, y, c, chips = pos
        (out,) = outs
        sends = [(out.at[4 * px + 2 * py + c], out.at[4 * px + 2 * py + c], (x, y, 1 - c)) for px, py in chips]
        return sends, [out.at[4 * px + 2 * py + (1 - c)] for px, py in chips], []

    return _part([gathered], [jax.ShapeDtypeStruct(gathered.shape, gathered.dtype)], plan, 3, aliased=1)


_SMALL = ["gate_r_w", "gate_i_w", "conv_a_w", "conv_c_w", "sinks", "conv_c_b", "gate_r_b", "gate_i_b", "rg_lambda",
          "norm_a", "norm_b", "norm_c", "ln_g", "ln_b"]


def _pack_small(p):
    L = p["ln_g"].shape[0]
    rows = []
    for n in _SMALL:
        a = p[n]
        if n in ("gate_r_w", "gate_i_w", "norm_b", "ln_g", "ln_b"):
            a = a.reshape(L, -1, 1024)
        elif a.ndim == 2:
            a = a[:, None, :]
        if a.shape[-1] < 1024:
            a = jnp.pad(a, ((0, 0), (0, 0), (0, 1024 - a.shape[-1])))
        rows.append(a)
    out = jnp.concatenate(rows, axis=1)
    assert out.shape[1] == SMALL_ROWS
    return out.reshape(L * SMALL_ROWS, 1024)


def _unpack_small(flat, like):
    L = like["ln_g"].shape[0]
    a = flat.reshape(L, SMALL_ROWS, 1024)
    out, r = {}, 0
    for n in _SMALL:
        shp = like[n].shape
        nrows = max(1, math.prod(shp[1:]) // 1024) if n in ("gate_r_w", "gate_i_w", "norm_b", "ln_g", "ln_b") else (
            shp[1] if len(shp) == 3 else 1)
        blk = a[:, r:r + nrows, :]
        if n in ("gate_r_w", "gate_i_w", "norm_b", "ln_g", "ln_b"):
            out[n] = blk.reshape(shp)
        elif len(shp) == 3:
            out[n] = blk[:, :, :shp[2]]
        else:
            out[n] = blk[:, 0, :shp[1]]
        r += nrows
    return out


def kernel(x, w_in, conv_a_w, sinks, conv_c_w, conv_c_b, gate_r_w, gate_r_b, gate_i_w, gate_i_b, rg_lambda, norm_a, norm_b, norm_c, w_out, ln_g, ln_b, loss_target, m_w_in, m_conv_a_w, m_sinks, m_conv_c_w, m_conv_c_b, m_gate_r_w, m_gate_r_b, m_gate_i_w, m_gate_i_b, m_rg_lambda, m_norm_a, m_norm_b, m_norm_c, m_w_out, m_ln_g, m_ln_b, v_w_in, v_conv_a_w, v_sinks, v_conv_c_w, v_conv_c_b, v_gate_r_w, v_gate_r_b, v_gate_i_w, v_gate_i_b, v_rg_lambda, v_norm_a, v_norm_b, v_norm_c, v_w_out, v_ln_g, v_ln_b):
    names = ["w_in", "conv_a_w", "sinks", "conv_c_w", "conv_c_b", "gate_r_w", "gate_r_b", "gate_i_w", "gate_i_b",
             "rg_lambda", "norm_a", "norm_b", "norm_c", "w_out", "ln_g", "ln_b"]
    w = dict(zip(names, [w_in, conv_a_w, sinks, conv_c_w, conv_c_b, gate_r_w, gate_r_b, gate_i_w, gate_i_b, rg_lambda,
                         norm_a, norm_b, norm_c, w_out, ln_g, ln_b]))
    mom = dict(zip(names, [m_w_in, m_conv_a_w, m_sinks, m_conv_c_w, m_conv_c_b, m_gate_r_w, m_gate_r_b, m_gate_i_w,
                           m_gate_i_b, m_rg_lambda, m_norm_a, m_norm_b, m_norm_c, m_w_out, m_ln_g, m_ln_b]))
    vel = dict(zip(names, [v_w_in, v_conv_a_w, v_sinks, v_conv_c_w, v_conv_c_b, v_gate_r_w, v_gate_r_b, v_gate_i_w,
                           v_gate_i_b, v_rg_lambda, v_norm_a, v_norm_b, v_norm_c, v_w_out, v_ln_g, v_ln_b]))
    B, S, D = x.shape
    T = B * S
    L, _, NS = w_in.shape
    RS = w_out.shape[1]
    W = D // 4
    alpha = (2.0 * L) ** 0.25
    tt = _tile(S, 128, 8)
    tm_row = _tile(T, 256, 8)
    chip = _my_chip()

    ws_in = [_cast_shard(w_in, l, "in", name="cast_w_in") for l in range(L)]
    ws_out = [_cast_shard(w_out, l, "out", name="cast_w_out") for l in range(L)]
    wg_in, wg_out = [None] * L, [None] * L
    conv_local = jnp.concatenate([conv_a_w, conv_c_w], axis=1).reshape(L * 7, W // N_CHIPS)
    conv_local = jnp.pad(conv_local, ((0, (-L * 7) % 8), (0, 0)))
    part_in, part_out, conv_all = _comm_call(
        [_gather_ici(ws_in[0], "in"), _gather_ici(ws_out[0], "out"), _small_ici(conv_local)], name="gather0_ici")
    wg_in[0], wg_out[0], conv_all = _comm_call(
        [_gather_d2d(part_in, "in"), _gather_d2d(part_out, "out"), _small_d2d(conv_all)], name="gather0_d2d")
    conv_full = jnp.concatenate([conv_all[2 * jj][:L * 7] for jj in range(N_CHIPS)], axis=1).reshape(L, 7, W)
    caw_full, ccw_full = conv_full[:, :3], conv_full[:, 3:]

    xf = x.reshape(T, D)
    xb = _cast_bf16(xf[None], 0, name="cast_x")
    saved = []
    for l in range(L):
        nxt = l + 1 < L
        deps = ()
        if nxt:
            g_parts = [_gather_ici(ws_in[l + 1], "in"), _gather_ici(ws_out[l + 1], "out")]
            g_state, g_token = _split_start(g_parts, name=f"gather_start{l + 1}")
            deps = (g_token,)
        comm = [_gather_d2d(part_out, "out")] if l else []
        res = _mm(xb, wg_in[l], mode="nn", out_dtype=F32, name="proj_in", tm=1024, tn=768, tk=4096, comm=comm, deps=deps)
        h = res if not comm else res.pop(0)
        if l:
            wg_out[l] = res.pop(0)
        pv = jnp.stack([conv_c_b[l], gate_r_b[l], gate_i_b[l], rg_lambda[l], norm_a[l], norm_c[l]])
        mix, cv, xc, yc = _ac_fwd(h, caw_full[l], ccw_full[l], pv, gate_r_w[l], gate_i_w[l], S=S, D=D, tt=tt,
                                  name="ac_fwd")
        yb = _attn_fwd(h, sinks, l, S=S, D=D, name="attn_fwd")
        mix = _mixb_fwd(yb, h, norm_b[l][None], mix, D=D, tm=tm_row, name="mixb_fwd")
        comm = []
        if nxt:
            (_, (part_in,)), (_, (part_out,)) = _split_wait(g_parts, g_state, [mix], name=f"gather_wait{l + 1}")
            comm = [_gather_d2d(part_in, "in")]
        res = _mm(mix, wg_out[l], mode="nn", out_dtype=F32, name="proj_out", tn=512, tk=4096, add=xf, add_scale=alpha,
                  comm=comm)
        z = res if not comm else res.pop(0)
        if nxt:
            wg_in[l + 1] = res.pop(0)
        saved.append((xb, h, cv, xc, yc, yb, mix, z, pv))
        xf, xb = _ln_fwd(z, ln_g[l][None], ln_b[l][None], tm=tm_row, name="ln_fwd")
    dxn, loss_part = _loss_head(xf, loss_target.reshape(T, D), tm=tm_row, name="loss_head")
    loss = lax.psum(loss_part[0, 0], ("x", "y", "c"))

    def final_sums(p_in, q_in, p_out, q_out):
        return (_final_sum(p_in, q_in, own_axis=1, out_shape=(D, NS), out_axis=0, name="final_sum_in"),
                _final_sum(p_out, q_out, own_axis=0, out_shape=(RS, D), out_axis=1, name="final_sum_out"))

    bufs_in = bufs_out = None
    small_g = [None] * L
    ce = None
    for l in reversed(range(L)):
        up, last = l + 1 < L, l == 0
        xb_l, h, cv, xc, yc, yb, mix, z, pv = saved[l]
        dz, dzb, dgb = _ln_bwd(z, dxn, ln_g[l][None], tm=tm_row, name="ln_bwd", deps=(ce[2],) if up else ())
        dmix = _mm(dzb, wg_out[l], mode="nt", out_dtype=F32, name="d_mix", tk=4096)
        gw_out = _mm(mix, dzb, mode="tn", out_dtype=BF16, name="d_w_out", tk=4096)
        dha, dhc, vec, dwr, dwi = _ac_bwd(h, cv, xc, yc, dmix, caw_full[l], ccw_full[l], pv, gate_r_w[l], gate_i_w[l],
                                          S=S, D=D, tt=tt, name="ac_bwd")
        dyb, dbg, dnb = _mixb_bwd(yb, h, dmix, norm_b[l][None], D=D, tm=tm_row, name="mixb_bwd")
        dq, dk, dv, dsk = _attn_bwd(h, yb, dyb, sinks, l, S=S, D=D, name="attn_bwd")
        dh = _concat_cols([dha, dq, dk, dv, dbg, dhc], tm=tm_row, name="concat_dh")
        small_g[l] = dict(gate_r_w=dwr, gate_i_w=dwi, conv_a_w=vec[0:3], conv_c_w=vec[4:8], sinks=dsk[0, :2 * D // 256],
                          conv_c_b=vec[8], gate_r_b=vec[9], gate_i_b=vec[10], rg_lambda=vec[11], norm_a=vec[3],
                          norm_b=dnb[0], norm_c=vec[12], ln_g=dgb[0], ln_b=dgb[1])
        if up:
            ((p_in,), (q_in,)), ((p_out,), (q_out,)) = _split_wait(ce[0], ce[1], [dh], name=f"chip_wait{l + 1}")
            g_in_half, g_out_half = final_sums(p_in, q_in, p_out, q_out)
        comm = [_pair_send(gw_out, "out")]
        if up:
            comm += [_sibling_send(g_in_half, "in"), _sibling_send(g_out_half, "out")]
        if last:
            comm.append(_small_ici(_pack_small({n: jnp.stack([small_g[k][n] for k in range(L)]) for n in _SMALL})))
        res = _mm(xb_l, dh, mode="tn", out_dtype=BF16, name="d_w_in", tm=1024, tn=768, tk=4096, comm=comm)
        gw_in, rb_out = _take(res, 2)
        p_out_l = _pair_sum(gw_out, rb_out, half_axis=1, name="pair_sum_out")
        if up:
            g_in_full, g_out_full = _take(res, 2)
            bufs_in = _adamw_layer(g_in_full, w_in, m_w_in, v_w_in, l + 1, bufs_in, name="adamw_w_in")
            bufs_out = _adamw_layer(g_out_full, w_out, m_w_out, v_w_out, l + 1, bufs_out, name="adamw_w_out")
        if last:
            rb_in, small_all = _comm_call([_pair_send(gw_in, "in"), _small_d2d(res.pop(0))], name="tail_d2d")
        else:
            dxn, rb_in = _mm(dh, wg_in[l], mode="nt", out_dtype=F32, name="d_x", tk=2688, add=dz, add_scale=alpha,
                             comm=[_pair_send(gw_in, "in")])
        p_in_l = _pair_sum(gw_in, rb_in, half_axis=0, name="pair_sum_in")
        ce_parts = [_chip_send(p_in_l, "in"), _chip_send(p_out_l, "out")]
        ce = (ce_parts, *_split_start(ce_parts, name=f"chip_start{l}"))
    dxn = _mm(dh, wg_in[0], mode="nt", out_dtype=F32, name="d_x", tk=2688, add=dz, add_scale=alpha, deps=(ce[2],))
    grad_x = dxn.reshape(B, S, D)
    ((p_in,), (q_in,)), ((p_out,), (q_out,)) = _split_wait(ce[0], ce[1], [dxn] + (bufs_in or []) + (bufs_out or []),
                                                            name="chip_wait0")
    g_in_half, g_out_half = final_sums(p_in, q_in, p_out, q_out)
    g_in0, g_out0 = _comm_call([_sibling_send(g_in_half, "in"), _sibling_send(g_out_half, "out")], name="sibling0")
    big = {"w_in": _adamw_layer(g_in0, w_in, m_w_in, v_w_in, 0, bufs_in, name="adamw_w_in"),
           "w_out": _adamw_layer(g_out0, w_out, m_w_out, v_w_out, 0, bufs_out, name="adamw_w_out")}

    like = {n: w[n] for n in _SMALL}
    like_full = dict(like, conv_a_w=caw_full, conv_c_w=ccw_full)
    g_small = _unpack_small(_sum_devices(small_all, name="sum_small"), like_full)
    for n in ("conv_a_w", "conv_c_w"):
        g_small[n] = lax.dynamic_slice_in_dim(g_small[n], chip * (W // N_CHIPS), W // N_CHIPS, axis=2)

    d_s, m_s, v_s = _adamw(_pack_small(g_small), _pack_small(like), _pack_small({n: mom[n] for n in _SMALL}),
                           _pack_small({n: vel[n] for n in _SMALL}), name="adamw_small")
    grads = dict(g_small)
    delta, new_m, new_v = _unpack_small(d_s, like), _unpack_small(m_s, like), _unpack_small(v_s, like)
    for n in ("w_in", "w_out"):
        grads[n], delta[n], new_m[n], new_v[n] = big[n]

    return (loss, grad_x, *[grads[n] for n in names], *[delta[n] for n in names], *[new_m[n] for n in names],
            *[new_v[n] for n in names])
```

```python
import functools
import math

import jax
import jax.numpy as jnp
from jax import lax
from jax.experimental import pallas as pl
from jax.experimental.pallas import tpu as pltpu

F32 = jnp.float32
BF16 = jnp.bfloat16
_MXU_DTYPE = jnp.bfloat16

HEAD_DIM = 64
KV_GROUP = 8
BLOCK = 128
N_RG_HEADS = 8
RG_C = 8.0
LN_EPS = 1e-5
RMS_EPS = 1e-6
NEG_INF = -1e30
ADAM_LR, ADAM_B1, ADAM_B2, ADAM_EPS, ADAM_WD, ADAM_STEP = 0.001, 0.9, 0.999, 1e-08, 0.01, 10
N_CHIPS = 4
N_DEV = 8
SMALL_ROWS = 280
VMEM_LIMIT = 56 * 1024 * 1024

MESH = pl.DeviceIdType.MESH
ANY = pl.BlockSpec(memory_space=pl.ANY)


def _pcall(body, *, name, **kw):
    return pl.pallas_call(body, name=name, **kw)


def _params(sem=None):
    return pltpu.CompilerParams(dimension_semantics=sem, vmem_limit_bytes=VMEM_LIMIT)


def _tile(dim, pref, mult=128):
    best = None
    for t in range(mult, min(dim, pref) + 1, mult):
        if dim % t == 0:
            best = t
    return best if best is not None else dim


def _dot(a, b, dims):
    return lax.dot_general(a.astype(_MXU_DTYPE), b.astype(_MXU_DTYPE), (dims, ((), ())),
                           preferred_element_type=F32)


NN = ((1,), (0,))
NT = ((1,), (1,))
TN = ((0,), (0,))


def _mm(a, b, *, mode, out_dtype, name, tm=1024, tn=1024, tk=512, add=None, add_scale=1.0, comm=(), deps=()):
    if mode == "nn":
        (M, K), N = a.shape, b.shape[1]
    elif mode == "nt":
        (M, K), N = a.shape, b.shape[0]
    else:
        (K, M), N = a.shape, b.shape[1]
    tm, tn, tk = _tile(M, tm), _tile(N, tn), _tile(K, tk)
    ni, nj, nk = M // tm, N // tn, K // tk
    dims = {"nn": NN, "nt": NT, "tn": TN}[mode]
    n_cin = sum(len(p.ins) for p in comm)
    n_cout = sum(len(p.outs) for p in comm)

    def body(*refs):
        refs = list(refs)
        a_ref, b_ref = _take(refs, 2)
        add_ref = refs.pop(0) if add is not None else None
        _take(refs, len(deps))
        cin = _take(refs, n_cin)
        o_ref = refs.pop(0)
        cout = _take(refs, n_cout)
        acc = refs.pop(0) if nk > 1 else None
        i, j, k = pl.program_id(0), pl.program_id(1), pl.program_id(2)

        if comm:
            @pl.when((i == 0) & (j == 0) & (k == 0))
            def _():
                _comm_run(comm, "start", cin, cout, *refs)

        def finish(r):
            if add_ref is not None:
                r = r + add_scale * add_ref[...]
            o_ref[...] = r.astype(out_dtype)

        if nk == 1:
            finish(_dot(a_ref[...], b_ref[...], dims))
        else:
            @pl.when(k == 0)
            def _():
                acc[...] = jnp.zeros_like(acc)

            acc[...] += _dot(a_ref[...], b_ref[...], dims)

            @pl.when(k == nk - 1)
            def _():
                finish(acc[...])

        if comm:
            @pl.when((i == ni - 1) & (j == nj - 1) & (k == nk - 1))
            def _():
                _comm_run(comm, "finish", cin, cout, *refs)

    a_spec = {"nn": pl.BlockSpec((tm, tk), lambda i, j, k: (i, k)),
              "nt": pl.BlockSpec((tm, tk), lambda i, j, k: (i, k)),
              "tn": pl.BlockSpec((tk, tm), lambda i, j, k: (k, i))}[mode]
    b_spec = {"nn": pl.BlockSpec((tk, tn), lambda i, j, k: (k, j)),
              "nt": pl.BlockSpec((tn, tk), lambda i, j, k: (j, k)),
              "tn": pl.BlockSpec((tk, tn), lambda i, j, k: (k, j))}[mode]
    in_specs, operands = [a_spec, b_spec], [a, b]
    if add is not None:
        in_specs.append(pl.BlockSpec((tm, tn), lambda i, j, k: (i, j)))
        operands.append(add)
    in_specs += [ANY] * len(deps)
    operands += list(deps)
    aliases = _comm_aliases(comm, len(operands), 1)
    in_specs += [ANY] * n_cin
    operands += [arr for p in comm for arr in p.ins]
    out_shape = [jax.ShapeDtypeStruct((M, N), out_dtype)] + [s for p in comm for s in p.outs]
    out_specs = [pl.BlockSpec((tm, tn), lambda i, j, k: (i, j))] + [ANY] * n_cout
    sem = ("arbitrary",) * 3 if comm else ("parallel", "parallel", "arbitrary")
    res = _pcall(body, name=name, out_shape=out_shape, grid=(ni, nj, nk), in_specs=in_specs, out_specs=out_specs,
                 scratch_shapes=([pltpu.VMEM((tm, tn), F32)] if nk > 1 else []) + _comm_scratch(comm),
                 input_output_aliases=aliases,
                 compiler_params=_params(sem))(*operands)
    return list(res) if comm else res[0]


def _colspecs(off, width, rows, rowmap):
    bw = math.gcd(off, width) if off else width
    specs = [pl.BlockSpec((rows, bw), functools.partial(lambda cb, *g: (rowmap(*g), cb), off // bw + i))
             for i in range(width // bw)]
    return specs, bw


def _cat(refs):
    vals = [r[...] for r in refs]
    return vals[0] if len(vals) == 1 else jnp.concatenate(vals, axis=1)


def _take(refs, n):
    out = refs[:n]
    del refs[:n]
    return out


def _sigmoid(x):
    return 0.5 * jnp.tanh(0.5 * x) + 0.5


def _rms(y, gamma):
    rstd = lax.rsqrt(jnp.mean(y * y, axis=-1, keepdims=True) + RMS_EPS)
    xn = y * rstd
    return xn, rstd, xn * gamma


def _rms_bwd(dn, xn, rstd, gamma):
    dng = dn * gamma
    return rstd * (dng - xn * jnp.mean(dng * xn, axis=-1, keepdims=True))


def _shift_down(x, s, carry8):
    rolled = pltpu.roll(x, s, 0)
    cr = pltpu.roll(carry8, s, 0)
    row8 = lax.broadcasted_iota(jnp.int32, carry8.shape, 0)
    top = jnp.where(row8 < s, cr, rolled[0:8])
    return jnp.concatenate([top, rolled[8:]], axis=0)


def _shift_up(x, s, carry8):
    n = x.shape[0]
    rolled = pltpu.roll(x, n - s, 0)
    cr = pltpu.roll(carry8, 8 - s, 0)
    row8 = lax.broadcasted_iota(jnp.int32, carry8.shape, 0)
    bot = jnp.where(row8 >= 8 - s, cr, rolled[n - 8:])
    return jnp.concatenate([rolled[:n - 8], bot], axis=0)


def _chunk_scan(a, b):
    n = a.shape[0]
    r8 = lax.broadcasted_iota(jnp.int32, a.shape, 0) & 7
    for d in (1, 2, 4):
        ok = r8 >= d
        a_sh = jnp.where(ok, pltpu.roll(a, d, 0), 1.0)
        b_sh = jnp.where(ok, pltpu.roll(b, d, 0), 0.0)
        b = a * b_sh + b
        a = a * a_sh
    return a, b


def _chunk_scan_rev(c, b):
    n = c.shape[0]
    r8 = lax.broadcasted_iota(jnp.int32, c.shape, 0) & 7
    for d in (1, 2, 4):
        ok = r8 + d <= 7
        c_sh = jnp.where(ok, pltpu.roll(c, n - d, 0), 1.0)
        b_sh = jnp.where(ok, pltpu.roll(b, n - d, 0), 0.0)
        b = b + c * b_sh
        c = c * c_sh
    return c, b


def _log1p(x):
    w = 1.0 + x
    return jnp.where(w == 1.0, x, jnp.log(w) * (x / (w - 1.0)))


def _log_sigmoid(x):
    return jnp.minimum(x, 0.0) - _log1p(jnp.exp(-jnp.abs(x)))


def _expm1(x):
    u = jnp.exp(x)
    lu = jnp.log(u)
    small = jnp.where(u == 1.0, x, (u - 1.0) * (x / jnp.where(lu == 0.0, 1.0, lu)))
    return jnp.where(jnp.abs(x) < 0.5, small, u - 1.0)


def _gates(xc, wr_ref, wi_ref, br, bi, lam):
    hw = xc.shape[1] // N_RG_HEADS
    gr = jnp.concatenate([_dot(xc[:, h * hw:(h + 1) * hw], wr_ref[h], NN) for h in range(N_RG_HEADS)], axis=1) + br
    gi = jnp.concatenate([_dot(xc[:, h * hw:(h + 1) * hw], wi_ref[h], NN) for h in range(N_RG_HEADS)], axis=1) + bi
    r, i = _sigmoid(gr), _sigmoid(gi)
    ls = _log_sigmoid(lam)
    la = RG_C * r * ls
    a = jnp.exp(la)
    sq = jnp.sqrt(-_expm1(2.0 * la))
    return r, i, ls, a, sq


def _ac_fwd(h, caw, ccw, pv, wr, wi, *, S, D, tt, name):
    T = h.shape[0]
    W = D // 4
    nt = S // tt
    rowmap = lambda s, t: s * nt + t
    c_off = D + D // 2 + 2 * (D // 16) + D // 2
    offs = [0, W, 2 * W, 3 * W, c_off, c_off + W]
    in_specs, counts = [], []
    for off in offs:
        specs, _ = _colspecs(off, W, tt, rowmap)
        in_specs += specs
        counts.append(len(specs))
    full = lambda shape: pl.BlockSpec(shape, lambda s, t: (0,) * len(shape))
    in_specs += [full(caw.shape), full(ccw.shape), full(pv.shape), full(wr.shape), full(wi.shape)]

    def body(*refs):
        refs = list(refs)
        ab, ac, ax, ag, cx, cg = [_cat(_take(refs, n)) for n in counts]
        caw_ref, ccw_ref, pv_ref, wr_ref, wi_ref = _take(refs, 5)
        mixac_ref, cv_ref, xc_ref, yc_ref = _take(refs, 4)
        carry_p, carry_cx, carry_h, a_s, b_s = refs
        t = pl.program_id(1)

        @pl.when(t == 0)
        def _():
            carry_p[...] = jnp.zeros_like(carry_p)
            carry_cx[...] = jnp.zeros_like(carry_cx)
            carry_h[...] = jnp.zeros_like(carry_h)

        ccb, br, bi, lam, na, nc = [pv_ref[k:k + 1, :] for k in range(6)]
        p = ac * ax
        cp = carry_p[...]
        cv = caw_ref[2:3, :] * p + caw_ref[1:2, :] * _shift_down(p, 1, cp) + caw_ref[0:1, :] * _shift_down(p, 2, cp)
        carry_p[...] = p[tt - 8:tt]
        cv_ref[...] = cv
        _, _, n_a = _rms(ab * cv, na)
        mix_a = n_a * (ag * _sigmoid(ag))
        ccx = carry_cx[...]
        xc = (ccw_ref[3:4, :] * cx + ccw_ref[2:3, :] * _shift_down(cx, 1, ccx) + ccw_ref[1:2, :] * _shift_down(cx, 2, ccx)
              + ccw_ref[0:1, :] * _shift_down(cx, 3, ccx) + ccb)
        carry_cx[...] = cx[tt - 8:tt]
        xc_ref[...] = xc
        r, i, ls, a, sq = _gates(xc, wr_ref, wi_ref, br, bi, lam)
        u = sq * (i * xc)
        a_c, b_c = _chunk_scan(a, u)
        a_s[...] = a_c
        b_s[...] = b_c

        def step(k, hprev):
            rows = pl.ds(pl.multiple_of(k * 8, 8), 8)
            hc = a_s[rows, :] * hprev + b_s[rows, :]
            yc_ref[rows, :] = hc
            return hc[7:8, :]

        hlast = lax.fori_loop(0, tt // 8, step, carry_h[0:1, :])
        carry_h[...] = jnp.broadcast_to(hlast, carry_h.shape)
        _, _, n_c = _rms(yc_ref[...], nc)
        mix_c = n_c * (cg * _sigmoid(cg))
        mixac_ref[...] = jnp.concatenate([mix_a, mix_c], axis=1).astype(mixac_ref.dtype)

    row_blk = lambda w: pl.BlockSpec((tt, w), lambda s, t: (rowmap(s, t), 0))
    return _pcall(
        body, name=name, grid=(T // S, nt), in_specs=in_specs,
        out_shape=(jax.ShapeDtypeStruct((T, 4 * W), BF16), jax.ShapeDtypeStruct((T, W), F32),
                   jax.ShapeDtypeStruct((T, W), F32), jax.ShapeDtypeStruct((T, W), F32)),
        out_specs=(pl.BlockSpec((tt, 2 * W), lambda s, t: (rowmap(s, t), 1)), row_blk(W), row_blk(W), row_blk(W)),
        scratch_shapes=[pltpu.VMEM((8, W), F32), pltpu.VMEM((8, W), F32), pltpu.VMEM((8, W), F32),
                        pltpu.VMEM((tt, W), F32), pltpu.VMEM((tt, W), F32)],
        compiler_params=_params(("arbitrary", "arbitrary")),
    )(*([h] * sum(counts)), caw, ccw, pv, wr, wi)


def _ac_bwd(h, cv, xc, yc, dmix, caw, ccw, pv, wr, wi, *, S, D, tt, name):
    T = h.shape[0]
    W = D // 4
    nt = S // tt
    rowmap = lambda s, t: s * nt + (nt - 1 - t)
    c_off = D + D // 2 + 2 * (D // 16) + D // 2
    offs = [0, W, 2 * W, 3 * W, c_off, c_off + W]
    in_specs, counts = [], []
    for off in offs:
        specs, _ = _colspecs(off, W, tt, rowmap)
        in_specs += specs
        counts.append(len(specs))
    row_blk = lambda w, cb=0: pl.BlockSpec((tt, w), lambda s, t: (rowmap(s, t), cb))
    in_specs += [row_blk(W), row_blk(W), row_blk(W)]
    in_specs.append(pl.BlockSpec((8, W), lambda s, t: (jnp.maximum(rowmap(s, t) * (tt // 8) - 1, 0), 0)))
    in_specs += [row_blk(W, 2), row_blk(W, 3)]
    full = lambda shape: pl.BlockSpec(shape, lambda s, t: (0,) * len(shape))
    in_specs += [full(caw.shape), full(ccw.shape), full(pv.shape), full(wr.shape), full(wi.shape)]

    def body(*refs):
        refs = list(refs)
        ab, ac, ax, ag, cx, cg = [_cat(_take(refs, n)) for n in counts]
        cv_ref, xc_ref, yc_ref, halo_ref, dma_ref, dmc_ref, caw_ref, ccw_ref, pv_ref, wr_ref, wi_ref = _take(refs, 11)
        dha_ref, dhc_ref, vec_ref, dwr_ref, dwi_ref = _take(refs, 5)
        carry_dcv, carry_dxc, carry_a, carry_g, c_s, b_s, g_s = refs
        s_id, t = pl.program_id(0), pl.program_id(1)

        @pl.when(t == 0)
        def _():
            for cr in (carry_dcv, carry_dxc, carry_a, carry_g):
                cr[...] = jnp.zeros_like(cr)

        @pl.when((t == 0) & (s_id == 0))
        def _():
            vec_ref[...] = jnp.zeros_like(vec_ref)
            dwr_ref[...] = jnp.zeros_like(dwr_ref)
            dwi_ref[...] = jnp.zeros_like(dwi_ref)

        def acc_row(k, val):
            vec_ref[k:k + 1, :] += jnp.sum(val, axis=0, keepdims=True)

        ccb, br, bi, lam, na, nc = [pv_ref[k:k + 1, :] for k in range(6)]
        cv = cv_ref[...]
        dmix_a = dma_ref[...]
        p = ac * ax
        xn, rstd, n_a = _rms(ab * cv, na)
        sg = _sigmoid(ag)
        dn = dmix_a * (ag * sg)
        dag = dmix_a * n_a * (sg * (1.0 + ag * (1.0 - sg)))
        acc_row(3, dn * xn)
        dya = _rms_bwd(dn, xn, rstd, na)
        dab = dya * cv
        dcv = dya * ab
        cd = carry_dcv[...]
        d1, d2 = _shift_up(dcv, 1, cd), _shift_up(dcv, 2, cd)
        dp = caw_ref[2:3, :] * dcv + caw_ref[1:2, :] * d1 + caw_ref[0:1, :] * d2
        acc_row(2, p * dcv)
        acc_row(1, p * d1)
        acc_row(0, p * d2)
        carry_dcv[...] = dcv[0:8]
        dha_ref[...] = jnp.concatenate([dab, dp * ax, dp * ac, dag], axis=1).astype(dha_ref.dtype)
        xc = xc_ref[...]
        yc = yc_ref[...]
        dmix_c = dmc_ref[...]
        xn, rstd, n_c = _rms(yc, nc)
        sg = _sigmoid(cg)
        dn = dmix_c * (cg * sg)
        dcg = dmix_c * n_c * (sg * (1.0 + cg * (1.0 - sg)))
        acc_row(12, dn * xn)
        dyc = _rms_bwd(dn, xn, rstd, nc)
        r, i, ls, a, sq = _gates(xc, wr_ref, wi_ref, br, bi, lam)
        halo = jnp.where(t == nt - 1, 0.0, halo_ref[...])
        hprev = _shift_down(yc, 1, halo)
        c_c, b_c = _chunk_scan_rev(_shift_up(a, 1, carry_a[...]), dyc)
        c_s[...] = c_c
        b_s[...] = b_c

        def step(k, gnext):
            rows = pl.ds(pl.multiple_of((tt // 8 - 1 - k) * 8, 8), 8)
            gc = b_s[rows, :] + c_s[rows, :] * gnext
            g_s[rows, :] = gc
            return gc[0:1, :]

        lax.fori_loop(0, tt // 8, step, carry_g[0:1, :])
        g = g_s[...]
        carry_g[...] = g[0:8]
        carry_a[...] = a[0:8]
        da = g * hprev
        ixc = i * xc
        dsq = g * ixc
        di = g * sq * xc
        dxc = g * sq * i
        dla = da * a - dsq * (a * a) / sq
        dr = dla * (RG_C * ls)
        acc_row(11, dla * (RG_C * r) * (1.0 / (1.0 + jnp.exp(lam))))
        dgr = dr * r * (1.0 - r)
        dgi = di * i * (1.0 - i)
        acc_row(9, dgr)
        acc_row(10, dgi)
        hw = W // N_RG_HEADS
        parts = []
        for hd in range(N_RG_HEADS):
            sl = slice(hd * hw, (hd + 1) * hw)
            dwr_ref[hd] += _dot(xc[:, sl], dgr[:, sl], TN)
            dwi_ref[hd] += _dot(xc[:, sl], dgi[:, sl], TN)
            parts.append(_dot(dgr[:, sl], wr_ref[hd], NT) + _dot(dgi[:, sl], wi_ref[hd], NT))
        dxc = dxc + jnp.concatenate(parts, axis=1)
        ce = carry_dxc[...]
        e1, e2, e3 = _shift_up(dxc, 1, ce), _shift_up(dxc, 2, ce), _shift_up(dxc, 3, ce)
        dcx = ccw_ref[3:4, :] * dxc + ccw_ref[2:3, :] * e1 + ccw_ref[1:2, :] * e2 + ccw_ref[0:1, :] * e3
        acc_row(7, cx * dxc)
        acc_row(6, cx * e1)
        acc_row(5, cx * e2)
        acc_row(4, cx * e3)
        acc_row(8, dxc)
        carry_dxc[...] = dxc[0:8]
        dhc_ref[...] = jnp.concatenate([dcx, dcg], axis=1).astype(dhc_ref.dtype)

    const = lambda shape: pl.BlockSpec(shape, lambda s, t: (0,) * len(shape))
    return _pcall(
        body, name=name, grid=(T // S, nt), in_specs=in_specs,
        out_shape=(jax.ShapeDtypeStruct((T, 4 * W), BF16), jax.ShapeDtypeStruct((T, 2 * W), BF16),
                   jax.ShapeDtypeStruct((16, W), F32), jax.ShapeDtypeStruct(wr.shape, F32),
                   jax.ShapeDtypeStruct(wi.shape, F32)),
        out_specs=(row_blk(4 * W), row_blk(2 * W), const((16, W)), const(wr.shape), const(wi.shape)),
        scratch_shapes=[pltpu.VMEM((8, W), F32)] * 4 + [pltpu.VMEM((tt, W), F32)] * 3,
        compiler_params=_params(("arbitrary", "arbitrary")),
    )(*([h] * sum(counts)), cv, xc, yc, yc, dmix, dmix, caw, ccw, pv, wr, wi)


def _lo_mask():
    return lax.broadcasted_iota(jnp.int32, (1, 2 * HEAD_DIM), 1) < HEAD_DIM


def _dup(blk, odd, lo):
    rot = pltpu.roll(blk, HEAD_DIM, 1)
    return jnp.where(lo, rot, blk) if odd else jnp.where(lo, blk, rot)


def _stack_heads(x, hh, lo, masked):
    parts = []
    for g in range(KV_GROUP):
        jq = hh * KV_GROUP + g
        pb = x[:, (jq // 2) * 128:(jq // 2 + 1) * 128]
        if masked:
            pb = jnp.where(lo if jq % 2 == 0 else jnp.logical_not(lo), pb, 0.0)
        parts.append(pb)
    return jnp.concatenate(parts, axis=0)


def _unstack_pairs_t(st_t):
    hi = lax.broadcasted_iota(jnp.int32, (2 * HEAD_DIM, BLOCK), 0) >= HEAD_DIM
    return [jnp.where(hi, st_t[:, (2 * pi + 1) * BLOCK:(2 * pi + 2) * BLOCK], st_t[:, (2 * pi) * BLOCK:(2 * pi + 1) * BLOCK]).T
            for pi in range(KV_GROUP // 2)]


def _window(ref, n):
    prev = ref[pl.ds(pl.multiple_of(jnp.maximum(n - 1, 0) * BLOCK, BLOCK), BLOCK), :]
    cur = ref[pl.ds(pl.multiple_of(n * BLOCK, BLOCK), BLOCK), :]
    return jnp.concatenate([prev, cur], axis=0)


def _valid_mask_t(n):
    cols = KV_GROUP * BLOCK
    kj = lax.broadcasted_iota(jnp.int32, (2 * BLOCK, cols), 0)
    qi = lax.broadcasted_iota(jnp.int32, (2 * BLOCK, cols), 1) & (BLOCK - 1)
    dist = qi + BLOCK - kj
    return (dist >= 0) & (dist < BLOCK) & ((n > 0) | (kj >= BLOCK))


def _sink_row(sinks_ref, layer, hh):
    return jnp.concatenate([jnp.full((1, BLOCK), sinks_ref[layer, hh * KV_GROUP + g], F32) for g in range(KV_GROUP)],
                           axis=1)


def _softmax_t(qs, kdup, valid, sink):
    s = jnp.where(valid, _dot(kdup, qs, NT), NEG_INF)
    m = jnp.maximum(jnp.max(s, axis=0, keepdims=True), sink)
    e = jnp.exp(s - m)
    es = jnp.exp(sink - m)
    r = 1.0 / (jnp.sum(e, axis=0, keepdims=True) + es)
    return e * r, es * r


def _lane_sums_row(x):
    hi = x.astype(BF16)
    lo = (x - hi.astype(F32)).astype(BF16)
    ones = jnp.ones((8, x.shape[1]), BF16)
    dims = (NT, ((), ()))
    return (lax.dot_general(ones, hi, dims, preferred_element_type=F32)
            + lax.dot_general(ones, lo, dims, preferred_element_type=F32))[0:1]


def _attn_fwd(h, sinks, layer, *, S, D, name):
    T = h.shape[0]
    WB, KVW = D // 2, D // 16
    nb = S // BLOCK
    n_kv = KVW // HEAD_DIM

    def body(q_ref, k_ref, v_ref, sinks_ref, o_ref):
        n = pl.program_id(1)
        lo = _lo_mask()
        q = q_ref[...] * (HEAD_DIM ** -0.5)
        kk, vv = _window(k_ref, n), _window(v_ref, n)
        valid = _valid_mask_t(n)
        blocks = []
        for hh in range(n_kv):
            cb = slice((hh // 2) * 128, (hh // 2 + 1) * 128)
            kdup, vdup = _dup(kk[:, cb], hh % 2, lo), _dup(vv[:, cb], hh % 2, lo)
            p_t, _ = _softmax_t(_stack_heads(q, hh, lo, True), kdup, valid, _sink_row(sinks_ref, layer, hh))
            blocks += _unstack_pairs_t(_dot(vdup, p_t, TN))
        o_ref[...] = jnp.concatenate(blocks, axis=1)

    return _pcall(
        body, name=name, grid=(T // S, nb),
        in_specs=[pl.BlockSpec((BLOCK, WB), lambda s, n: (s * nb + n, D // WB)),
                  pl.BlockSpec((S, KVW), lambda s, n: (s, (D + WB) // KVW)),
                  pl.BlockSpec((S, KVW), lambda s, n: (s, (D + WB) // KVW + 1)),
                  pl.BlockSpec(memory_space=pltpu.SMEM)],
        out_shape=jax.ShapeDtypeStruct((T, WB), F32),
        out_specs=pl.BlockSpec((BLOCK, WB), lambda s, n: (s * nb + n, 0)),
        compiler_params=_params(("arbitrary", "arbitrary")),
    )(h, h, h, sinks)


def _attn_bwd(h, yb, dyb, sinks, layer, *, S, D, name):
    T = h.shape[0]
    WB, KVW = D // 2, D // 16
    nb = S // BLOCK
    n_kv = KVW // HEAD_DIM

    def body(q_ref, k_ref, v_ref, o_ref, do_ref, sinks_ref, dq_ref, dk_ref, dv_ref, dsink_ref, dk_acc, dv_acc):
        s_id, n = pl.program_id(0), pl.program_id(1)
        lo = _lo_mask()

        @pl.when(n == 0)
        def _():
            dk_acc[...] = jnp.zeros_like(dk_acc)
            dv_acc[...] = jnp.zeros_like(dv_acc)

        @pl.when((n == 0) & (s_id == 0))
        def _():
            dsink_ref[...] = jnp.zeros_like(dsink_ref)

        scale = HEAD_DIM ** -0.5
        q, o, do = q_ref[...] * scale, o_ref[...], do_ref[...]
        kk, vv = _window(k_ref, n), _window(v_ref, n)
        valid = _valid_mask_t(n)
        lane = lax.broadcasted_iota(jnp.int32, dsink_ref.shape, 1)
        dq_blocks, dk_heads, dv_heads = [], [], []
        dsink = jnp.zeros(dsink_ref.shape, F32)
        for hh in range(n_kv):
            cb = slice((hh // 2) * 128, (hh // 2 + 1) * 128)
            kdup, vdup = _dup(kk[:, cb], hh % 2, lo), _dup(vv[:, cb], hh % 2, lo)
            qs = _stack_heads(q, hh, lo, True)
            dos = _stack_heads(do, hh, lo, True)
            delta = _lane_sums_row(dos * _stack_heads(o, hh, lo, False))
            p_t, psink = _softmax_t(qs, kdup, valid, _sink_row(sinks_ref, layer, hh))
            dvr = _dot(p_t, dos, NN)
            dv_heads.append(dvr + pltpu.roll(dvr, HEAD_DIM, 1))
            ds_t = p_t * (_dot(vdup, dos, NT) - delta)
            dq_blocks += [b * scale for b in _unstack_pairs_t(_dot(kdup, ds_t, TN))]
            dkr = _dot(ds_t, qs, NN)
            dk_heads.append(dkr + pltpu.roll(dkr, HEAD_DIM, 1))
            dsk = -psink * delta
            for g in range(KV_GROUP):
                tot = jnp.sum(dsk[:, g * BLOCK:(g + 1) * BLOCK], axis=1, keepdims=True)
                dsink = dsink + jnp.where(lane == hh * KV_GROUP + g, tot, 0.0)
        dsink_ref[...] += dsink
        dq_ref[...] = jnp.concatenate(dq_blocks, axis=1).astype(dq_ref.dtype)
        pair = lambda hs: jnp.concatenate([jnp.where(lo, hs[2 * m], hs[2 * m + 1]) for m in range(n_kv // 2)], axis=1)
        dkk, dvv = pair(dk_heads), pair(dv_heads)
        prev = pl.ds(pl.multiple_of(jnp.maximum(n - 1, 0) * BLOCK, BLOCK), BLOCK)
        cur = pl.ds(pl.multiple_of(n * BLOCK, BLOCK), BLOCK)
        dk_acc[prev, :] += dkk[:BLOCK]
        dk_acc[cur, :] += dkk[BLOCK:]
        dv_acc[prev, :] += dvv[:BLOCK]
        dv_acc[cur, :] += dvv[BLOCK:]

        @pl.when(n == nb - 1)
        def _():
            dk_ref[...] = dk_acc[...].astype(dk_ref.dtype)
            dv_ref[...] = dv_acc[...].astype(dv_ref.dtype)

    blk = lambda cb=0: pl.BlockSpec((BLOCK, WB), lambda s, n: (s * nb + n, cb))
    seq = lambda cb=0: pl.BlockSpec((S, KVW), lambda s, n: (s, cb))
    return _pcall(
        body, name=name, grid=(T // S, nb),
        in_specs=[blk(D // WB), seq((D + WB) // KVW), seq((D + WB) // KVW + 1), blk(), blk(),
                  pl.BlockSpec(memory_space=pltpu.SMEM)],
        out_shape=(jax.ShapeDtypeStruct((T, WB), BF16), jax.ShapeDtypeStruct((T, KVW), BF16),
                   jax.ShapeDtypeStruct((T, KVW), BF16), jax.ShapeDtypeStruct((8, 128), F32)),
        out_specs=(blk(), seq(), seq(), pl.BlockSpec((8, 128), lambda s, n: (0, 0))),
        scratch_shapes=[pltpu.VMEM((S, KVW), F32), pltpu.VMEM((S, KVW), F32)],
        compiler_params=_params(("arbitrary", "arbitrary")),
    )(h, h, h, yb, dyb, sinks)


def _bg_specs(D, tm):
    return _colspecs(D + D // 2 + 2 * (D // 16), D // 2, tm, lambda i: i)


def _mixb_fwd(yb, h, nb_g, mix, *, D, tm, name):
    T, WB = yb.shape
    bg_specs, _ = _bg_specs(D, tm)

    def body(*refs):
        refs = list(refs)
        yb_ref = refs.pop(0)
        bg = _cat(_take(refs, len(bg_specs)))
        g_ref, _, o_ref = refs
        _, _, nrm = _rms(yb_ref[...], g_ref[...])
        o_ref[...] = (nrm * (bg * _sigmoid(bg))).astype(o_ref.dtype)

    row = pl.BlockSpec((tm, WB), lambda i: (i, 0))
    return _pcall(body, name=name, grid=(T // tm,),
                  in_specs=[row] + bg_specs + [pl.BlockSpec((1, WB), lambda i: (0, 0)), ANY],
                  out_shape=jax.ShapeDtypeStruct(mix.shape, mix.dtype), out_specs=row,
                  input_output_aliases={len(bg_specs) + 2: 0},
                  compiler_params=_params(("arbitrary",)))(yb, *([h] * len(bg_specs)), nb_g, mix)


def _mixb_bwd(yb, h, dmix, nb_g, *, D, tm, name):
    T, WB = yb.shape
    bg_specs, _ = _bg_specs(D, tm)
    dm_specs, _ = _colspecs(0, WB, tm, lambda i: i)

    def body(*refs):
        refs = list(refs)
        yb_ref = refs.pop(0)
        bg = _cat(_take(refs, len(bg_specs)))
        dmix_b = _cat(_take(refs, len(dm_specs)))
        g_ref, dyb_ref, dbg_ref, dg_ref = refs

        @pl.when(pl.program_id(0) == 0)
        def _():
            dg_ref[...] = jnp.zeros_like(dg_ref)

        gamma = g_ref[...]
        xn, rstd, nrm = _rms(yb_ref[...], gamma)
        sg = _sigmoid(bg)
        dn = dmix_b * (bg * sg)
        dbg_ref[...] = (dmix_b * nrm * (sg * (1.0 + bg * (1.0 - sg)))).astype(dbg_ref.dtype)
        dg_ref[0:1, :] += jnp.sum(dn * xn, axis=0, keepdims=True)
        dyb_ref[...] = _rms_bwd(dn, xn, rstd, gamma)

    row = pl.BlockSpec((tm, WB), lambda i: (i, 0))
    return _pcall(body, name=name, grid=(T // tm,),
                  in_specs=[row] + bg_specs + dm_specs + [pl.BlockSpec((1, WB), lambda i: (0, 0))],
                  out_shape=(jax.ShapeDtypeStruct((T, WB), F32), jax.ShapeDtypeStruct((T, WB), BF16),
                             jax.ShapeDtypeStruct((8, WB), F32)),
                  out_specs=(row, row, pl.BlockSpec((8, WB), lambda i: (0, 0))),
                  compiler_params=_params(("arbitrary",)))(yb, *([h] * len(bg_specs)), *([dmix] * len(dm_specs)), nb_g)


def _concat_cols(parts, *, tm, name):
    parts = [p if isinstance(p, tuple) else (p, 0, p.shape[1]) for p in parts]
    T = parts[0][0].shape[0]
    total = sum(w for _, _, w in parts)

    def body(*refs):
        refs[-1][...] = jnp.concatenate([r[...] for r in refs[:-1]], axis=1)

    return _pcall(body, name=name, grid=(T // tm,),
                  in_specs=[pl.BlockSpec((tm, w), functools.partial(lambda cb, i: (i, cb), cb)) for _, cb, w in parts],
                  out_shape=jax.ShapeDtypeStruct((T, total), parts[0][0].dtype),
                  out_specs=pl.BlockSpec((tm, total), lambda i: (i, 0)),
                  compiler_params=_params(("parallel",)))(*[a for a, _, _ in parts])


def _ln_fwd(z, g, b, *, tm, name):
    T, D = z.shape

    def body(z_ref, g_ref, b_ref, y_ref, yb_ref):
        zv = z_ref[...]
        mu = jnp.mean(zv, axis=-1, keepdims=True)
        zc = zv - mu
        var = jnp.mean(zc * zc, axis=-1, keepdims=True)
        y = zc * lax.rsqrt(var + LN_EPS) * g_ref[...] + b_ref[...]
        y_ref[...] = y
        yb_ref[...] = y.astype(BF16)

    row = pl.BlockSpec((tm, D), lambda i: (i, 0))
    vec = pl.BlockSpec((1, D), lambda i: (0, 0))
    return _pcall(body, name=name, grid=(T // tm,), in_specs=[row, vec, vec],
                  out_shape=(jax.ShapeDtypeStruct((T, D), F32), jax.ShapeDtypeStruct((T, D), BF16)),
                  out_specs=(row, row), compiler_params=_params(("parallel",)))(z, g, b)


def _ln_bwd(z, dy, g, *, tm, name, deps=()):
    T, D = z.shape

    def body(z_ref, dy_ref, g_ref, *rest):
        dz_ref, dzb_ref, dgb_ref = rest[len(deps):]

        @pl.when(pl.program_id(0) == 0)
        def _():
            dgb_ref[...] = jnp.zeros_like(dgb_ref)

        zv, dyv = z_ref[...], dy_ref[...]
        mu = jnp.mean(zv, axis=-1, keepdims=True)
        zc = zv - mu
        rstd = lax.rsqrt(jnp.mean(zc * zc, axis=-1, keepdims=True) + LN_EPS)
        xh = zc * rstd
        dxh = dyv * g_ref[...]
        dz = rstd * (dxh - jnp.mean(dxh, axis=-1, keepdims=True) - xh * jnp.mean(dxh * xh, axis=-1, keepdims=True))
        dz_ref[...] = dz
        dzb_ref[...] = dz.astype(BF16)
        dgb_ref[0:1, :] += jnp.sum(dyv * xh, axis=0, keepdims=True)
        dgb_ref[1:2, :] += jnp.sum(dyv, axis=0, keepdims=True)

    row = pl.BlockSpec((tm, D), lambda i: (i, 0))
    return _pcall(body, name=name, grid=(T // tm,),
                  in_specs=[row, row, pl.BlockSpec((1, D), lambda i: (0, 0))] + [ANY] * len(deps),
                  out_shape=(jax.ShapeDtypeStruct((T, D), F32), jax.ShapeDtypeStruct((T, D), BF16),
                             jax.ShapeDtypeStruct((8, D), F32)),
                  out_specs=(row, row, pl.BlockSpec((8, D), lambda i: (0, 0))),
                  compiler_params=_params(("arbitrary",)))(z, dy, g, *deps)


def _loss_head(y, target, *, tm, name):
    T, D = y.shape

    def body(y_ref, t_ref, dy_ref, loss_ref):
        @pl.when(pl.program_id(0) == 0)
        def _():
            loss_ref[...] = jnp.zeros_like(loss_ref)

        err = y_ref[...] - t_ref[...]
        dy_ref[...] = err / D
        loss_ref[...] += 0.5 * jnp.sum(jnp.mean(err * err, axis=-1, keepdims=True), axis=0, keepdims=True)

    row = pl.BlockSpec((tm, D), lambda i: (i, 0))
    return _pcall(body, name=name, grid=(T // tm,), in_specs=[row, row],
                  out_shape=(jax.ShapeDtypeStruct((T, D), F32), jax.ShapeDtypeStruct((1, 1), F32)),
                  out_specs=(row, pl.BlockSpec((1, 1), lambda i: (0, 0))),
                  compiler_params=_params(("arbitrary",)))(y, target)


def _cast_bf16(w, layer, *, name, deps=()):
    _, R, C = w.shape
    tr = _tile(R, 512, 8)

    def body(w_ref, *rest):
        rest[-1][...] = w_ref[...].astype(BF16)

    return _pcall(body, name=name, grid=(R // tr,),
                  in_specs=[pl.BlockSpec((None, tr, C), lambda i: (layer, i, 0))] + [ANY] * len(deps),
                  out_shape=jax.ShapeDtypeStruct((R, C), BF16), out_specs=pl.BlockSpec((tr, C), lambda i: (i, 0)),
                  compiler_params=_params(("parallel",)))(w, *deps)


def _cast_shard(w, layer, kind, *, name, deps=()):
    _, R, C = w.shape
    tr = _tile(R, 512, 16)
    nrb = R // tr
    if kind == "in":
        full, o_idx = (R, N_CHIPS * C), lambda i: (i, _my_chip())
    else:
        full, o_idx = (N_CHIPS * R, C), lambda i: (_out_pos(_my_chip()) * nrb + i, 0)

    def body(w_ref, *rest):
        rest[-1][...] = w_ref[...].astype(BF16)

    return _pcall(body, name=name, grid=(nrb,),
                  in_specs=[pl.BlockSpec((None, tr, C), lambda i: (layer, i, 0))] + [ANY] * len(deps),
                  out_shape=jax.ShapeDtypeStruct(full, BF16), out_specs=pl.BlockSpec((tr, C), o_idx),
                  compiler_params=_params(("parallel",)))(w, *deps)


def _adamw_layer(g, w, m, v, layer, bufs, *, name):
    L, R, C = w.shape
    tr = _tile(R, max(8, (1 << 19) // C // 8 * 8), 8)
    if bufs is None:
        bufs = [lax.empty((L, R, C), F32) for _ in range(4)]

    def body(g_ref, w_ref, m_ref, v_ref, b0, b1, b2, b3, go_ref, d_ref, nm_ref, nv_ref):
        gv = g_ref[...]
        nm = ADAM_B1 * m_ref[...] + (1.0 - ADAM_B1) * gv
        nv = ADAM_B2 * v_ref[...] + (1.0 - ADAM_B2) * (gv * gv)
        m_hat = nm / (1.0 - ADAM_B1 ** ADAM_STEP)
        v_hat = nv / (1.0 - ADAM_B2 ** ADAM_STEP)
        go_ref[...] = gv
        d_ref[...] = -ADAM_LR * (m_hat / (jnp.sqrt(v_hat) + ADAM_EPS) + ADAM_WD * w_ref[...])
        nm_ref[...] = nm
        nv_ref[...] = nv

    lay = pl.BlockSpec((None, tr, C), lambda i: (layer, i, 0))
    shp = jax.ShapeDtypeStruct((L, R, C), F32)
    return list(_pcall(body, name=name, grid=(R // tr,),
                       in_specs=[pl.BlockSpec((tr, C), lambda i: (i, 0)), lay, lay, lay] + [ANY] * 4,
                       out_shape=(shp,) * 4, out_specs=(lay,) * 4, input_output_aliases={4 + k: k for k in range(4)},
                       compiler_params=_params(("parallel",)))(g, w, m, v, *bufs))


def _adamw(g, w, m, v, *, name):
    R, C = g.shape
    tr = _tile(R, max(8, (1 << 19) // C // 8 * 8), 8)

    def body(g_ref, w_ref, m_ref, v_ref, d_ref, nm_ref, nv_ref):
        gv = g_ref[...]
        nm = ADAM_B1 * m_ref[...] + (1.0 - ADAM_B1) * gv
        nv = ADAM_B2 * v_ref[...] + (1.0 - ADAM_B2) * (gv * gv)
        m_hat = nm / (1.0 - ADAM_B1 ** ADAM_STEP)
        v_hat = nv / (1.0 - ADAM_B2 ** ADAM_STEP)
        d_ref[...] = -ADAM_LR * (m_hat / (jnp.sqrt(v_hat) + ADAM_EPS) + ADAM_WD * w_ref[...])
        nm_ref[...] = nm
        nv_ref[...] = nv

    blk = pl.BlockSpec((tr, C), lambda i: (i, 0))
    shp = jax.ShapeDtypeStruct((R, C), F32)
    return _pcall(body, name=name, grid=(R // tr,), in_specs=[blk] * 4, out_shape=(shp, shp, shp),
                  out_specs=(blk, blk, blk), compiler_params=_params(("parallel",)))(g, w, m, v)


def _my_core():
    return lax.axis_index("c")


def _my_chip():
    return 2 * lax.axis_index("x") + lax.axis_index("y")


def _out_pos(chip):
    assert N_CHIPS == 4
    return jnp.where(chip == 3, 3, (chip + 2) % 3)


def _pair_sum(mine, theirs, *, half_axis, name):
    R, C = theirs.shape
    tr, tc = _tile(R, 512, 16), _tile(C, 2048)
    nrb, ncb = R // tr, C // tc

    def body(a_ref, b_ref, o_ref):
        o_ref[...] = (a_ref[...].astype(F32) + b_ref[...].astype(F32)).astype(BF16)

    if half_axis == 0:
        a_idx = lambda i, j: (_my_core() * nrb + i, j)
    else:
        a_idx = lambda i, j: (i, _my_core() * ncb + j)
    blk = pl.BlockSpec((tr, tc), lambda i, j: (i, j))
    return _pcall(body, name=name, grid=(nrb, ncb), in_specs=[pl.BlockSpec((tr, tc), a_idx), blk], out_specs=blk,
                  out_shape=jax.ShapeDtypeStruct(theirs.shape, BF16),
                  compiler_params=_params(("parallel", "parallel")))(mine, theirs)


def _final_sum(own, got, *, own_axis, out_shape, out_axis, name):
    _, R, C = got.shape
    tr, tc = _tile(R, 512, 16), _tile(C, 1024)
    nrb, ncb = R // tr, C // tc

    def body(a_ref, q_ref, o_ref):
        o_ref[...] = ((a_ref[...].astype(F32) + q_ref[0].astype(F32)) + q_ref[1].astype(F32)) + q_ref[2].astype(F32)

    if own_axis == 1:
        a_idx = lambda i, j: (i, _my_chip() * ncb + j)
    else:
        a_idx = lambda i, j: (_out_pos(_my_chip()) * nrb + i, j)
    if out_axis == 0:
        o_idx = lambda i, j: (_my_core() * nrb + i, j)
    else:
        o_idx = lambda i, j: (i, _my_core() * ncb + j)
    return _pcall(body, name=name, grid=(nrb, ncb),
                  in_specs=[pl.BlockSpec((tr, tc), a_idx), pl.BlockSpec((3, tr, tc), lambda i, j: (0, i, j))],
                  out_specs=pl.BlockSpec((tr, tc), o_idx), out_shape=jax.ShapeDtypeStruct(out_shape, F32),
                  compiler_params=_params(("parallel", "parallel")))(own, got)


def _sum_devices(gathered, *, name):
    _, R, C = gathered.shape
    tr = _tile(R, 280, 8)

    def body(g_ref, o_ref):
        acc = g_ref[0]
        for d in range(1, N_DEV):
            acc = acc + g_ref[d]
        o_ref[...] = acc

    return _pcall(body, name=name, grid=(R // tr,), in_specs=[pl.BlockSpec((N_DEV, tr, C), lambda i: (0, i, 0))],
                  out_shape=jax.ShapeDtypeStruct((R, C), F32), out_specs=pl.BlockSpec((tr, C), lambda i: (i, 0)),
                  compiler_params=_params(("parallel",)))(gathered)


def _position():
    x, y, c = lax.axis_index("x"), lax.axis_index("y"), lax.axis_index("c")
    chips = [(1 - x, y), (x, 1 - y), (1 - x, 1 - y)]
    return x, y, c, chips


def _remote(src, dst, send_sems, recv_sems, k, to):
    return pltpu.make_async_remote_copy(src_ref=src, dst_ref=dst, send_sem=send_sems.at[k], recv_sem=recv_sems.at[k],
                                        device_id=to, device_id_type=MESH)


def _r(ref, start, n):
    return ref.at[pl.ds(pl.multiple_of(start, 16), n), :]


def _c(ref, start, n):
    return ref.at[:, pl.ds(pl.multiple_of(start, 128), n)]


class _part:
    def __init__(self, ins, outs, plan, n, n_local=0, aliased=0):
        self.ins, self.outs, self.plan, self.n, self.n_local, self.aliased = ins, outs, plan, n, n_local, aliased


def _comm_scratch(parts):
    if not parts:
        return []
    n, nl = sum(p.n for p in parts), sum(p.n_local for p in parts)
    return [pltpu.SemaphoreType.DMA((n,)), pltpu.SemaphoreType.DMA((n,)), pltpu.SemaphoreType.DMA((max(nl, 1),))]


def _comm_aliases(parts, in_base, out_base):
    aliases, ii, oi = {}, in_base, out_base
    for p in parts:
        aliases.update({ii + k: oi + k for k in range(p.aliased)})
        ii += len(p.ins)
        oi += len(p.outs)
    return aliases


def _comm_run(parts, phase, in_refs, out_refs, send_sems, recv_sems, local_sems):
    pos = _position()
    me = pos[:3]
    ii = oi = si = li = 0
    for p in parts:
        sends, recvs, locs = p.plan(in_refs[ii:ii + len(p.ins)], out_refs[oi:oi + len(p.outs)], pos)
        assert len(sends) == len(recvs) == p.n and len(locs) == p.n_local
        if phase == "start":
            for k, (src, dst) in enumerate(locs):
                pltpu.make_async_copy(src, dst, local_sems.at[li + k]).start()
            for k, (src, dst, to) in enumerate(sends):
                _remote(src, dst, send_sems, recv_sems, si + k, to).start()
        else:
            for k, dst in enumerate(recvs):
                _remote(dst, dst, send_sems, recv_sems, si + k, me).wait_recv()
            for k, (src, dst, to) in enumerate(sends):
                _remote(src, dst, send_sems, recv_sems, si + k, to).wait_send()
            for k, (src, dst) in enumerate(locs):
                pltpu.make_async_copy(src, dst, local_sems.at[li + k]).wait()
        ii, oi, si, li = ii + len(p.ins), oi + len(p.outs), si + p.n, li + p.n_local


def _comm_call(parts, *, name):
    n_in = sum(len(p.ins) for p in parts)
    n_out = sum(len(p.outs) for p in parts)

    def body(*refs):
        refs = list(refs)
        cin, cout = _take(refs, n_in), _take(refs, n_out)
        _comm_run(parts, "start", cin, cout, *refs)
        _comm_run(parts, "finish", cin, cout, *refs)

    return list(_pcall(body, name=name, in_specs=[ANY] * n_in, out_specs=[ANY] * n_out,
                       out_shape=[s for p in parts for s in p.outs], scratch_shapes=_comm_scratch(parts),
                       input_output_aliases=_comm_aliases(parts, 0, 0))(*[a for p in parts for a in p.ins]))


HBM = pl.BlockSpec(memory_space=pltpu.HBM)
SEM = pl.BlockSpec(memory_space=pltpu.SEMAPHORE)
EFFECT = pltpu.SideEffectType.DATAFLOW_SIDE_EFFECTING


def _split_refs(parts, arr):
    out, i = [], 0
    for p in parts:
        ins = arr[i:i + len(p.ins)]
        i += len(p.ins)
        lands = arr[i:i + len(p.outs) - p.aliased]
        i += len(lands)
        out.append((ins, list(ins[:p.aliased]) + list(lands)))
    return out


def _split_start(parts, *, name, deps=()):
    assert all(p.n_local == 0 for p in parts)
    arrays = []
    for p in parts:
        arrays += list(p.ins) + [lax.empty(s.shape, s.dtype) for s in p.outs[p.aliased:]]
    n, na = sum(p.n for p in parts), len(arrays)

    def body(*refs):
        refs = list(refs)
        arr = _take(refs, na)
        _take(refs, len(deps))
        sems = _take(refs, 2 * n)
        token = refs[na]
        pos = _position()
        k = 0
        for p, (ins, outs) in zip(parts, _split_refs(parts, arr)):
            sends, _, _ = p.plan(ins, outs, pos)
            for src, dst, to in sends:
                pltpu.make_async_remote_copy(src_ref=src, dst_ref=dst, send_sem=sems[k], recv_sem=sems[n + k],
                                             device_id=to, device_id_type=MESH).start()
                k += 1
        token[...] = jnp.zeros_like(token)

    res = _pcall(
        body, name=name,
        out_shape=[pltpu.SemaphoreType.DMA(())] * (2 * n) + [pltpu.HBM(a.shape, a.dtype) for a in arrays]
        + [jax.ShapeDtypeStruct((8, 128), F32)],
        in_specs=[HBM] * na + [ANY] * len(deps),
        out_specs=[SEM] * (2 * n) + [HBM] * na + [pl.BlockSpec(memory_space=pltpu.VMEM)],
        input_output_aliases={i: 2 * n + i for i in range(na)},
        compiler_params=pltpu.CompilerParams(has_side_effects=EFFECT),
    )(*[pltpu.with_memory_space_constraint(a, pltpu.HBM) for a in arrays], *deps)
    return (list(res[:2 * n]), list(res[2 * n:2 * n + na])), res[-1]


def _split_wait(parts, state, after, *, name):
    sems, arrays = state
    n, na = len(sems) // 2, len(arrays)

    def body(*refs):
        refs = list(refs)
        arr = _take(refs, na)
        sm = _take(refs, 2 * n)
        pos = _position()
        me = pos[:3]
        k = 0
        for p, (ins, outs) in zip(parts, _split_refs(parts, arr)):
            sends, recvs, _ = p.plan(ins, outs, pos)
            for (src, dst, to), land in zip(sends, recvs):
                pltpu.make_async_remote_copy(src_ref=src, dst_ref=dst, send_sem=sm[k], recv_sem=sm[n + k],
                                             device_id=to, device_id_type=MESH).wait_send()
                pltpu.make_async_remote_copy(src_ref=land, dst_ref=land, send_sem=sm[k], recv_sem=sm[n + k],
                                             device_id=me, device_id_type=MESH).wait_recv()
                k += 1

    res = _pcall(
        body, name=name, out_shape=[pltpu.HBM(a.shape, a.dtype) for a in arrays],
        in_specs=[HBM] * na + [SEM] * (2 * n) + [ANY] * len(after), out_specs=[HBM] * na,
        input_output_aliases={i: i for i in range(na)},
        compiler_params=pltpu.CompilerParams(has_side_effects=EFFECT),
    )(*arrays, *sems, *after)
    return _split_refs(parts, list(res))


def _slab(wg, kind, chip, half):
    if kind == "in":
        d, ns = wg.shape[0], wg.shape[1] // N_CHIPS
        return _c(_r(wg, half * (d // 2), d // 2), chip * ns, ns)
    rs = wg.shape[0] // N_CHIPS
    return _r(wg, _out_pos(chip) * rs + half * (rs // 2), rs // 2)


def _gather_ici(wg, kind):
    def plan(ins, outs, pos):
        x, y, c, chips = pos
        (ref,) = outs
        mine = _slab(ref, kind, 2 * x + y, c)
        return [(mine, mine, (*chip, c)) for chip in chips], [_slab(ref, kind, 2 * px + py, c) for px, py in chips], []

    return _part([wg], [jax.ShapeDtypeStruct(wg.shape, wg.dtype)], plan, 3, aliased=1)


def _gather_d2d(wg, kind):
    def plan(ins, outs, pos):
        x, y, c, chips = pos
        (ref,) = outs
        sends = [(_slab(ref, kind, 2 * px + py, c), _slab(ref, kind, 2 * px + py, c), (x, y, 1 - c)) for px, py in chips]
        return sends, [_slab(ref, kind, 2 * px + py, 1 - c) for px, py in chips], []

    return _part([wg], [jax.ShapeDtypeStruct(wg.shape, wg.dtype)], plan, 3, aliased=1)


def _pair_send(gw, kind):
    rows, cols = gw.shape
    half = (rows // 2, cols) if kind == "in" else (rows, cols // 2)

    def plan(ins, outs, pos):
        x, y, c, _ = pos
        (src,), (rb,) = ins, outs
        theirs = _r(src, (1 - c) * half[0], half[0]) if kind == "in" else _c(src, (1 - c) * half[1], half[1])
        return [(theirs, rb, (x, y, 1 - c))], [rb], []

    return _part([gw], [jax.ShapeDtypeStruct(half, gw.dtype)], plan, 1)


def _chip_send(p, kind):
    rows, cols = p.shape
    shard = (rows, cols // N_CHIPS) if kind == "in" else (rows // N_CHIPS, cols)

    def plan(ins, outs, pos):
        x, y, c, chips = pos
        (src,), (q,) = ins, outs
        piece = lambda jk: (_c(src, jk * shard[1], shard[1]) if kind == "in"
                            else _r(src, _out_pos(jk) * shard[0], shard[0]))
        sends = [(piece(2 * px + py), q.at[kk], (px, py, c)) for kk, (px, py) in enumerate(chips)]
        return sends, [q.at[kk] for kk in range(3)], []

    return _part([p], [jax.ShapeDtypeStruct((3,) + shard, p.dtype)], plan, 3)


def _sibling_send(g, kind):
    rows, cols = g.shape

    def plan(ins, outs, pos):
        x, y, c, _ = pos
        (ref,) = outs
        half = (lambda h: _r(ref, h * (rows // 2), rows // 2)) if kind == "in" else (
            lambda h: _c(ref, h * (cols // 2), cols // 2))
        return [(half(c), half(c), (x, y, 1 - c))], [half(1 - c)], []

    return _part([g], [jax.ShapeDtypeStruct(g.shape, g.dtype)], plan, 1, aliased=1)


def _small_ici(block):
    def plan(ins, outs, pos):
        x, y, c, chips = pos
        (src,), (out,) = ins, outs
        mine = out.at[4 * x + 2 * y + c]
        peers = [(x, y, 1 - c)] + [(px, py, c) for px, py in chips]
        return [(src, mine, p) for p in peers], [out.at[4 * px + 2 * py + pc] for px, py, pc in peers], [(src, mine)]

    return _part([block], [jax.ShapeDtypeStruct((N_DEV,) + block.shape, block.dtype)], plan, 4, 1)


def _small_d2d(gathered):
    def plan(ins, outs, pos):
        x, y, c, chips = pos
        (out,) = outs
        sends = [(out.at[4 * px + 2 * py + c], out.at[4 * px + 2 * py + c], (x, y, 1 - c)) for px, py in chips]
        return sends, [out.at[4 * px + 2 * py + (1 - c)] for px, py in chips], []

    return _part([gathered], [jax.ShapeDtypeStruct(gathered.shape, gathered.dtype)], plan, 3, aliased=1)


_SMALL = ["gate_r_w", "gate_i_w", "conv_a_w", "conv_c_w", "sinks", "conv_c_b", "gate_r_b", "gate_i_b", "rg_lambda",
          "norm_a", "norm_b", "norm_c", "ln_g", "ln_b"]


def _pack_small(p):
    L = p["ln_g"].shape[0]
    rows = []
    for n in _SMALL:
        a = p[n]
        if n in ("gate_r_w", "gate_i_w", "norm_b", "ln_g", "ln_b"):
            a = a.reshape(L, -1, 1024)
        elif a.ndim == 2:
            a = a[:, None, :]
        if a.shape[-1] < 1024:
            a = jnp.pad(a, ((0, 0), (0, 0), (0, 1024 - a.shape[-1])))
        rows.append(a)
    out = jnp.concatenate(rows, axis=1)
    assert out.shape[1] == SMALL_ROWS
    return out.reshape(L * SMALL_ROWS, 1024)


def _unpack_small(flat, like):
    L = like["ln_g"].shape[0]
    a = flat.reshape(L, SMALL_ROWS, 1024)
    out, r = {}, 0
    for n in _SMALL:
        shp = like[n].shape
        nrows = max(1, math.prod(shp[1:]) // 1024) if n in ("gate_r_w", "gate_i_w", "norm_b", "ln_g", "ln_b") else (
            shp[1] if len(shp) == 3 else 1)
        blk = a[:, r:r + nrows, :]
        if n in ("gate_r_w", "gate_i_w", "norm_b", "ln_g", "ln_b"):
            out[n] = blk.reshape(shp)
        elif len(shp) == 3:
            out[n] = blk[:, :, :shp[2]]
        else:
            out[n] = blk[:, 0, :shp[1]]
        r += nrows
    return out


def kernel(x, w_in, conv_a_w, sinks, conv_c_w, conv_c_b, gate_r_w, gate_r_b, gate_i_w, gate_i_b, rg_lambda, norm_a, norm_b, norm_c, w_out, ln_g, ln_b, loss_target, m_w_in, m_conv_a_w, m_sinks, m_conv_c_w, m_conv_c_b, m_gate_r_w, m_gate_r_b, m_gate_i_w, m_gate_i_b, m_rg_lambda, m_norm_a, m_norm_b, m_norm_c, m_w_out, m_ln_g, m_ln_b, v_w_in, v_conv_a_w, v_sinks, v_conv_c_w, v_conv_c_b, v_gate_r_w, v_gate_r_b, v_gate_i_w, v_gate_i_b, v_rg_lambda, v_norm_a, v_norm_b, v_norm_c, v_w_out, v_ln_g, v_ln_b):
    names = ["w_in", "conv_a_w", "sinks", "conv_c_w", "conv_c_b", "gate_r_w", "gate_r_b", "gate_i_w", "gate_i_b",
             "rg_lambda", "norm_a", "norm_b", "norm_c", "w_out", "ln_g", "ln_b"]
    w = dict(zip(names, [w_in, conv_a_w, sinks, conv_c_w, conv_c_b, gate_r_w, gate_r_b, gate_i_w, gate_i_b, rg_lambda,
                         norm_a, norm_b, norm_c, w_out, ln_g, ln_b]))
    mom = dict(zip(names, [m_w_in, m_conv_a_w, m_sinks, m_conv_c_w, m_conv_c_b, m_gate_r_w, m_gate_r_b, m_gate_i_w,
                           m_gate_i_b, m_rg_lambda, m_norm_a, m_norm_b, m_norm_c, m_w_out, m_ln_g, m_ln_b]))
    vel = dict(zip(names, [v_w_in, v_conv_a_w, v_sinks, v_conv_c_w, v_conv_c_b, v_gate_r_w, v_gate_r_b, v_gate_i_w,
                           v_gate_i_b, v_rg_lambda, v_norm_a, v_norm_b, v_norm_c, v_w_out, v_ln_g, v_ln_b]))
    B, S, D = x.shape
    T = B * S
    L, _, NS = w_in.shape
    RS = w_out.shape[1]
    W = D // 4
    alpha = (2.0 * L) ** 0.25
    tt = _tile(S, 128, 8)
    tm_row = _tile(T, 256, 8)
    chip = _my_chip()

    wg_in, wg_out = [None] * L, [None] * L
    conv_local = jnp.concatenate([conv_a_w, conv_c_w], axis=1).reshape(L * 7, W // N_CHIPS)
    conv_local = jnp.pad(conv_local, ((0, (-L * 7) % 8), (0, 0)))
    (conv_all,) = _comm_call([_small_ici(conv_local)], name="conv_ici")
    ws_in, ws_out = [_cast_shard(w_in, 0, "in", name="cast_w_in")], [_cast_shard(w_out, 0, "out", name="cast_w_out")]
    g_parts = [_gather_ici(ws_in[0], "in"), _gather_ici(ws_out[0], "out")]
    g_state, g_token = _split_start(g_parts, name="gather_start0", deps=(conv_all,))
    ws_in += [_cast_shard(w_in, l, "in", name="cast_w_in", deps=(g_token,)) for l in range(1, L)]
    ws_out += [_cast_shard(w_out, l, "out", name="cast_w_out", deps=(g_token,)) for l in range(1, L)]
    xf = x.reshape(T, D)
    xb = _cast_bf16(xf[None], 0, name="cast_x", deps=(g_token,))
    (_, (part_in,)), (_, (part_out,)) = _split_wait(g_parts, g_state, ws_in[1:] + ws_out[1:] + [xb], name="gather_wait0")
    wg_in[0], wg_out[0], conv_all = _comm_call(
        [_gather_d2d(part_in, "in"), _gather_d2d(part_out, "out"), _small_d2d(conv_all)], name="gather0_d2d")
    conv_full = jnp.concatenate([conv_all[2 * jj][:L * 7] for jj in range(N_CHIPS)], axis=1).reshape(L, 7, W)
    caw_full, ccw_full = conv_full[:, :3], conv_full[:, 3:]

    saved = []
    for l in range(L):
        nxt = l + 1 < L
        deps = ()
        if nxt:
            g_parts = [_gather_ici(ws_in[l + 1], "in"), _gather_ici(ws_out[l + 1], "out")]
            g_state, g_token = _split_start(g_parts, name=f"gather_start{l + 1}")
            deps = (g_token,)
        comm = [_gather_d2d(part_out, "out")] if l else []
        res = _mm(xb, wg_in[l], mode="nn", out_dtype=F32, name="proj_in", tm=1024, tn=768, tk=4096, comm=comm, deps=deps)
        h = res if not comm else res.pop(0)
        if l:
            wg_out[l] = res.pop(0)
        pv = jnp.stack([conv_c_b[l], gate_r_b[l], gate_i_b[l], rg_lambda[l], norm_a[l], norm_c[l]])
        mix, cv, xc, yc = _ac_fwd(h, caw_full[l], ccw_full[l], pv, gate_r_w[l], gate_i_w[l], S=S, D=D, tt=tt,
                                  name="ac_fwd")
        yb = _attn_fwd(h, sinks, l, S=S, D=D, name="attn_fwd")
        mix = _mixb_fwd(yb, h, norm_b[l][None], mix, D=D, tm=tm_row, name="mixb_fwd")
        comm = []
        if nxt:
            (_, (part_in,)), (_, (part_out,)) = _split_wait(g_parts, g_state, [mix], name=f"gather_wait{l + 1}")
            comm = [_gather_d2d(part_in, "in")]
        res = _mm(mix, wg_out[l], mode="nn", out_dtype=F32, name="proj_out", tn=1024, tk=4096, add=xf, add_scale=alpha,
                  comm=comm)
        z = res if not comm else res.pop(0)
        if nxt:
            wg_in[l + 1] = res.pop(0)
        saved.append((xb, h, cv, xc, yc, yb, mix, z, pv))
        xf, xb = _ln_fwd(z, ln_g[l][None], ln_b[l][None], tm=tm_row, name="ln_fwd")
    dxn, loss_part = _loss_head(xf, loss_target.reshape(T, D), tm=tm_row, name="loss_head")
    loss = lax.psum(loss_part[0, 0], ("x", "y", "c"))

    def final_sums(p_in, q_in, p_out, q_out):
        return (_final_sum(p_in, q_in, own_axis=1, out_shape=(D, NS), out_axis=0, name="final_sum_in"),
                _final_sum(p_out, q_out, own_axis=0, out_shape=(RS, D), out_axis=1, name="final_sum_out"))

    bufs_in = bufs_out = None
    small_g = [None] * L
    ce = None
    for l in reversed(range(L)):
        up, last = l + 1 < L, l == 0
        xb_l, h, cv, xc, yc, yb, mix, z, pv = saved[l]
        dz, dzb, dgb = _ln_bwd(z, dxn, ln_g[l][None], tm=tm_row, name="ln_bwd", deps=(ce[2],) if up else ())
        dmix = _mm(dzb, wg_out[l], mode="nt", out_dtype=F32, name="d_mix", tk=4096)
        gw_out = _mm(mix, dzb, mode="tn", out_dtype=BF16, name="d_w_out", tk=4096)
        dha, dhc, vec, dwr, dwi = _ac_bwd(h, cv, xc, yc, dmix, caw_full[l], ccw_full[l], pv, gate_r_w[l], gate_i_w[l],
                                          S=S, D=D, tt=tt, name="ac_bwd")
        dyb, dbg, dnb = _mixb_bwd(yb, h, dmix, norm_b[l][None], D=D, tm=tm_row, name="mixb_bwd")
        dq, dk, dv, dsk = _attn_bwd(h, yb, dyb, sinks, l, S=S, D=D, name="attn_bwd")
        dh = _concat_cols([dha, dq, dk, dv, dbg, dhc], tm=tm_row, name="concat_dh")
        small_g[l] = dict(gate_r_w=dwr, gate_i_w=dwi, conv_a_w=vec[0:3], conv_c_w=vec[4:8], sinks=dsk[0, :2 * D // 256],
                          conv_c_b=vec[8], gate_r_b=vec[9], gate_i_b=vec[10], rg_lambda=vec[11], norm_a=vec[3],
                          norm_b=dnb[0], norm_c=vec[12], ln_g=dgb[0], ln_b=dgb[1])
        if up:
            ((p_in,), (q_in,)), ((p_out,), (q_out,)) = _split_wait(ce[0], ce[1], [dh], name=f"chip_wait{l + 1}")
            g_in_half, g_out_half = final_sums(p_in, q_in, p_out, q_out)
        comm = [_pair_send(gw_out, "out")]
        if up:
            comm += [_sibling_send(g_in_half, "in"), _sibling_send(g_out_half, "out")]
        if last:
            comm.append(_small_ici(_pack_small({n: jnp.stack([small_g[k][n] for k in range(L)]) for n in _SMALL})))
        res = _mm(xb_l, dh, mode="tn", out_dtype=BF16, name="d_w_in", tm=1024, tn=768, tk=4096, comm=comm)
        gw_in, rb_out = _take(res, 2)
        p_out_l = _pair_sum(gw_out, rb_out, half_axis=1, name="pair_sum_out")
        if up:
            g_in_full, g_out_full = _take(res, 2)
            bufs_in = _adamw_layer(g_in_full, w_in, m_w_in, v_w_in, l + 1, bufs_in, name="adamw_w_in")
            bufs_out = _adamw_layer(g_out_full, w_out, m_w_out, v_w_out, l + 1, bufs_out, name="adamw_w_out")
        if last:
            rb_in, small_all = _comm_call([_pair_send(gw_in, "in"), _small_d2d(res.pop(0))], name="tail_d2d")
        else:
            dxn, rb_in = _mm(dh, wg_in[l], mode="nt", out_dtype=F32, name="d_x", tk=3584, add=dz, add_scale=alpha,
                             comm=[_pair_send(gw_in, "in")])
        p_in_l = _pair_sum(gw_in, rb_in, half_axis=0, name="pair_sum_in")
        ce_parts = [_chip_send(p_in_l, "in"), _chip_send(p_out_l, "out")]
        ce = (ce_parts, *_split_start(ce_parts, name=f"chip_start{l}"))
    dxn = _mm(dh, wg_in[0], mode="nt", out_dtype=F32, name="d_x", tk=3584, add=dz, add_scale=alpha, deps=(ce[2],))
    grad_x = dxn.reshape(B, S, D)
    ((p_in,), (q_in,)), ((p_out,), (q_out,)) = _split_wait(ce[0], ce[1], [dxn] + (bufs_in or []) + (bufs_out or []),
                                                            name="chip_wait0")
    g_in_half, g_out_half = final_sums(p_in, q_in, p_out, q_out)
    g_in0, g_out0 = _comm_call([_sibling_send(g_in_half, "in"), _sibling_send(g_out_half, "out")], name="sibling0")
    big = {"w_in": _adamw_layer(g_in0, w_in, m_w_in, v_w_in, 0, bufs_in, name="adamw_w_in"),
           "w_out": _adamw_layer(g_out0, w_out, m_w_out, v_w_out, 0, bufs_out, name="adamw_w_out")}

    like = {n: w[n] for n in _SMALL}
    like_full = dict(like, conv_a_w=caw_full, conv_c_w=ccw_full)
    g_small = _unpack_small(_sum_devices(small_all, name="sum_small"), like_full)
    for n in ("conv_a_w", "conv_c_w"):
        g_small[n] = lax.dynamic_slice_in_dim(g_small[n], chip * (W // N_CHIPS), W // N_CHIPS, axis=2)

    d_s, m_s, v_s = _adamw(_pack_small(g_small), _pack_small(like), _pack_small({n: mom[n] for n in _SMALL}),
                           _pack_small({n: vel[n] for n in _SMALL}), name="adamw_small")
    grads = dict(g_small)
    delta, new_m, new_v = _unpack_small(d_s, like), _unpack_small(m_s, like), _unpack_small(v_s, like)
    for n in ("w_in", "w_out"):
        grads[n], delta[n], new_m[n], new_v[n] = big[n]

    return (loss, grad_x, *[grads[n] for n in names], *[delta[n] for n in names], *[new_m[n] for n in names],
            *[new_v[n] for n in names])
```

```python
import functools
import math

import jax
import jax.numpy as jnp
from jax import lax
from jax.experimental import pallas as pl
from jax.experimental.pallas import tpu as pltpu

F32 = jnp.float32
BF16 = jnp.bfloat16
_MXU_DTYPE = jnp.bfloat16

HEAD_DIM = 64
KV_GROUP = 8
BLOCK = 128
N_RG_HEADS = 8
RG_C = 8.0
LN_EPS = 1e-5
RMS_EPS = 1e-6
NEG_INF = -1e30
ADAM_LR, ADAM_B1, ADAM_B2, ADAM_EPS, ADAM_WD, ADAM_STEP = 0.001, 0.9, 0.999, 1e-08, 0.01, 10
N_CHIPS = 4
N_DEV = 8
SMALL_ROWS = 280
VMEM_LIMIT = 56 * 1024 * 1024

MESH = pl.DeviceIdType.MESH
ANY = pl.BlockSpec(memory_space=pl.ANY)


def _pcall(body, *, name, **kw):
    return pl.pallas_call(body, name=name, **kw)


def _params(sem=None):
    return pltpu.CompilerParams(dimension_semantics=sem, vmem_limit_bytes=VMEM_LIMIT)


def _tile(dim, pref, mult=128):
    best = None
    for t in range(mult, min(dim, pref) + 1, mult):
        if dim % t == 0:
            best = t
    return best if best is not None else dim


def _dot(a, b, dims):
    return lax.dot_general(a.astype(_MXU_DTYPE), b.astype(_MXU_DTYPE), (dims, ((), ())),
                           preferred_element_type=F32)


NN = ((1,), (0,))
NT = ((1,), (1,))
TN = ((0,), (0,))


def _mm(a, b, *, mode, out_dtype, name, tm=1024, tn=1024, tk=512, add=None, add_scale=1.0, comm=(), deps=()):
    if mode == "nn":
        (M, K), N = a.shape, b.shape[1]
    elif mode == "nt":
        (M, K), N = a.shape, b.shape[0]
    else:
        (K, M), N = a.shape, b.shape[1]
    tm, tn, tk = _tile(M, tm), _tile(N, tn), _tile(K, tk)
    ni, nj, nk = M // tm, N // tn, K // tk
    dims = {"nn": NN, "nt": NT, "tn": TN}[mode]
    n_cin = sum(len(p.ins) for p in comm)
    n_cout = sum(len(p.outs) for p in comm)

    def body(*refs):
        refs = list(refs)
        a_ref, b_ref = _take(refs, 2)
        add_ref = refs.pop(0) if add is not None else None
        _take(refs, len(deps))
        cin = _take(refs, n_cin)
        o_ref = refs.pop(0)
        cout = _take(refs, n_cout)
        acc = refs.pop(0) if nk > 1 else None
        i, j, k = pl.program_id(0), pl.program_id(1), pl.program_id(2)

        if comm:
            @pl.when((i == 0) & (j == 0) & (k == 0))
            def _():
                _comm_run(comm, "start", cin, cout, *refs)

        def finish(r):
            if add_ref is not None:
                r = r + add_scale * add_ref[...]
            o_ref[...] = r.astype(out_dtype)

        if nk == 1:
            finish(_dot(a_ref[...], b_ref[...], dims))
        else:
            @pl.when(k == 0)
            def _():
                acc[...] = jnp.zeros_like(acc)

            acc[...] += _dot(a_ref[...], b_ref[...], dims)

            @pl.when(k == nk - 1)
            def _():
                finish(acc[...])

        if comm:
            @pl.when((i == ni - 1) & (j == nj - 1) & (k == nk - 1))
            def _():
                _comm_run(comm, "finish", cin, cout, *refs)

    a_spec = {"nn": pl.BlockSpec((tm, tk), lambda i, j, k: (i, k)),
              "nt": pl.BlockSpec((tm, tk), lambda i, j, k: (i, k)),
              "tn": pl.BlockSpec((tk, tm), lambda i, j, k: (k, i))}[mode]
    b_spec = {"nn": pl.BlockSpec((tk, tn), lambda i, j, k: (k, j)),
              "nt": pl.BlockSpec((tn, tk), lambda i, j, k: (j, k)),
              "tn": pl.BlockSpec((tk, tn), lambda i, j, k: (k, j))}[mode]
    in_specs, operands = [a_spec, b_spec], [a, b]
    if add is not None:
        in_specs.append(pl.BlockSpec((tm, tn), lambda i, j, k: (i, j)))
        operands.append(add)
    in_specs += [ANY] * len(deps)
    operands += list(deps)
    aliases = _comm_aliases(comm, len(operands), 1)
    in_specs += [ANY] * n_cin
    operands += [arr for p in comm for arr in p.ins]
    out_shape = [jax.ShapeDtypeStruct((M, N), out_dtype)] + [s for p in comm for s in p.outs]
    out_specs = [pl.BlockSpec((tm, tn), lambda i, j, k: (i, j))] + [ANY] * n_cout
    sem = ("arbitrary",) * 3 if comm else ("parallel", "parallel", "arbitrary")
    res = _pcall(body, name=name, out_shape=out_shape, grid=(ni, nj, nk), in_specs=in_specs, out_specs=out_specs,
                 scratch_shapes=([pltpu.VMEM((tm, tn), F32)] if nk > 1 else []) + _comm_scratch(comm),
                 input_output_aliases=aliases,
                 compiler_params=_params(sem))(*operands)
    return list(res) if comm else res[0]


def _colspecs(off, width, rows, rowmap):
    bw = math.gcd(off, width) if off else width
    specs = [pl.BlockSpec((rows, bw), functools.partial(lambda cb, *g: (rowmap(*g), cb), off // bw + i))
             for i in range(width // bw)]
    return specs, bw


def _cat(refs):
    vals = [r[...] for r in refs]
    return vals[0] if len(vals) == 1 else jnp.concatenate(vals, axis=1)


def _take(refs, n):
    out = refs[:n]
    del refs[:n]
    return out


def _sigmoid(x):
    return 0.5 * jnp.tanh(0.5 * x) + 0.5


def _rms(y, gamma):
    rstd = lax.rsqrt(jnp.mean(y * y, axis=-1, keepdims=True) + RMS_EPS)
    xn = y * rstd
    return xn, rstd, xn * gamma


def _rms_bwd(dn, xn, rstd, gamma):
    dng = dn * gamma
    return rstd * (dng - xn * jnp.mean(dng * xn, axis=-1, keepdims=True))


def _shift_down(x, s, carry8):
    rolled = pltpu.roll(x, s, 0)
    cr = pltpu.roll(carry8, s, 0)
    row8 = lax.broadcasted_iota(jnp.int32, carry8.shape, 0)
    top = jnp.where(row8 < s, cr, rolled[0:8])
    return jnp.concatenate([top, rolled[8:]], axis=0)


def _shift_up(x, s, carry8):
    n = x.shape[0]
    rolled = pltpu.roll(x, n - s, 0)
    cr = pltpu.roll(carry8, 8 - s, 0)
    row8 = lax.broadcasted_iota(jnp.int32, carry8.shape, 0)
    bot = jnp.where(row8 >= 8 - s, cr, rolled[n - 8:])
    return jnp.concatenate([rolled[:n - 8], bot], axis=0)


def _chunk_scan(a, b):
    n = a.shape[0]
    r8 = lax.broadcasted_iota(jnp.int32, a.shape, 0) & 7
    for d in (1, 2, 4):
        ok = r8 >= d
        a_sh = jnp.where(ok, pltpu.roll(a, d, 0), 1.0)
        b_sh = jnp.where(ok, pltpu.roll(b, d, 0), 0.0)
        b = a * b_sh + b
        a = a * a_sh
    return a, b


def _chunk_scan_rev(c, b):
    n = c.shape[0]
    r8 = lax.broadcasted_iota(jnp.int32, c.shape, 0) & 7
    for d in (1, 2, 4):
        ok = r8 + d <= 7
        c_sh = jnp.where(ok, pltpu.roll(c, n - d, 0), 1.0)
        b_sh = jnp.where(ok, pltpu.roll(b, n - d, 0), 0.0)
        b = b + c * b_sh
        c = c * c_sh
    return c, b


def _log1p(x):
    w = 1.0 + x
    return jnp.where(w == 1.0, x, jnp.log(w) * (x / (w - 1.0)))


def _log_sigmoid(x):
    return jnp.minimum(x, 0.0) - _log1p(jnp.exp(-jnp.abs(x)))


def _expm1(x):
    u = jnp.exp(x)
    lu = jnp.log(u)
    small = jnp.where(u == 1.0, x, (u - 1.0) * (x / jnp.where(lu == 0.0, 1.0, lu)))
    return jnp.where(jnp.abs(x) < 0.5, small, u - 1.0)


def _gates(xc, wr_ref, wi_ref, br, bi, lam):
    hw = xc.shape[1] // N_RG_HEADS
    gr = jnp.concatenate([_dot(xc[:, h * hw:(h + 1) * hw], wr_ref[h], NN) for h in range(N_RG_HEADS)], axis=1) + br
    gi = jnp.concatenate([_dot(xc[:, h * hw:(h + 1) * hw], wi_ref[h], NN) for h in range(N_RG_HEADS)], axis=1) + bi
    r, i = _sigmoid(gr), _sigmoid(gi)
    ls = _log_sigmoid(lam)
    la = RG_C * r * ls
    a = jnp.exp(la)
    sq = jnp.sqrt(-_expm1(2.0 * la))
    return r, i, ls, a, sq


def _ac_fwd(h, caw, ccw, pv, wr, wi, *, S, D, tt, name):
    T = h.shape[0]
    W = D // 4
    nt = S // tt
    rowmap = lambda s, t: s * nt + t
    c_off = D + D // 2 + 2 * (D // 16) + D // 2
    offs = [0, W, 2 * W, 3 * W, c_off, c_off + W]
    in_specs, counts = [], []
    for off in offs:
        specs, _ = _colspecs(off, W, tt, rowmap)
        in_specs += specs
        counts.append(len(specs))
    full = lambda shape: pl.BlockSpec(shape, lambda s, t: (0,) * len(shape))
    in_specs += [full(caw.shape), full(ccw.shape), full(pv.shape), full(wr.shape), full(wi.shape)]

    def body(*refs):
        refs = list(refs)
        ab, ac, ax, ag, cx, cg = [_cat(_take(refs, n)) for n in counts]
        caw_ref, ccw_ref, pv_ref, wr_ref, wi_ref = _take(refs, 5)
        mixac_ref, cv_ref, xc_ref, yc_ref = _take(refs, 4)
        carry_p, carry_cx, carry_h, a_s, b_s = refs
        t = pl.program_id(1)

        @pl.when(t == 0)
        def _():
            carry_p[...] = jnp.zeros_like(carry_p)
            carry_cx[...] = jnp.zeros_like(carry_cx)
            carry_h[...] = jnp.zeros_like(carry_h)

        ccb, br, bi, lam, na, nc = [pv_ref[k:k + 1, :] for k in range(6)]
        p = ac * ax
        cp = carry_p[...]
        cv = caw_ref[2:3, :] * p + caw_ref[1:2, :] * _shift_down(p, 1, cp) + caw_ref[0:1, :] * _shift_down(p, 2, cp)
        carry_p[...] = p[tt - 8:tt]
        cv_ref[...] = cv
        _, _, n_a = _rms(ab * cv, na)
        mix_a = n_a * (ag * _sigmoid(ag))
        ccx = carry_cx[...]
        xc = (ccw_ref[3:4, :] * cx + ccw_ref[2:3, :] * _shift_down(cx, 1, ccx) + ccw_ref[1:2, :] * _shift_down(cx, 2, ccx)
              + ccw_ref[0:1, :] * _shift_down(cx, 3, ccx) + ccb)
        carry_cx[...] = cx[tt - 8:tt]
        xc_ref[...] = xc
        r, i, ls, a, sq = _gates(xc, wr_ref, wi_ref, br, bi, lam)
        u = sq * (i * xc)
        a_c, b_c = _chunk_scan(a, u)
        a_s[...] = a_c
        b_s[...] = b_c

        def step(k, hprev):
            rows = pl.ds(pl.multiple_of(k * 8, 8), 8)
            hc = a_s[rows, :] * hprev + b_s[rows, :]
            yc_ref[rows, :] = hc
            return hc[7:8, :]

        hlast = lax.fori_loop(0, tt // 8, step, carry_h[0:1, :])
        carry_h[...] = jnp.broadcast_to(hlast, carry_h.shape)
        _, _, n_c = _rms(yc_ref[...], nc)
        mix_c = n_c * (cg * _sigmoid(cg))
        mixac_ref[...] = jnp.concatenate([mix_a, mix_c], axis=1).astype(mixac_ref.dtype)

    row_blk = lambda w: pl.BlockSpec((tt, w), lambda s, t: (rowmap(s, t), 0))
    return _pcall(
        body, name=name, grid=(T // S, nt), in_specs=in_specs,
        out_shape=(jax.ShapeDtypeStruct((T, 4 * W), BF16), jax.ShapeDtypeStruct((T, W), F32),
                   jax.ShapeDtypeStruct((T, W), F32), jax.ShapeDtypeStruct((T, W), F32)),
        out_specs=(pl.BlockSpec((tt, 2 * W), lambda s, t: (rowmap(s, t), 1)), row_blk(W), row_blk(W), row_blk(W)),
        scratch_shapes=[pltpu.VMEM((8, W), F32), pltpu.VMEM((8, W), F32), pltpu.VMEM((8, W), F32),
                        pltpu.VMEM((tt, W), F32), pltpu.VMEM((tt, W), F32)],
        compiler_params=_params(("arbitrary", "arbitrary")),
    )(*([h] * sum(counts)), caw, ccw, pv, wr, wi)


def _ac_bwd(h, cv, xc, yc, dmix, caw, ccw, pv, wr, wi, *, S, D, tt, name):
    T = h.shape[0]
    W = D // 4
    nt = S // tt
    rowmap = lambda s, t: s * nt + (nt - 1 - t)
    c_off = D + D // 2 + 2 * (D // 16) + D // 2
    offs = [0, W, 2 * W, 3 * W, c_off, c_off + W]
    in_specs, counts = [], []
    for off in offs:
        specs, _ = _colspecs(off, W, tt, rowmap)
        in_specs += specs
        counts.append(len(specs))
    row_blk = lambda w, cb=0: pl.BlockSpec((tt, w), lambda s, t: (rowmap(s, t), cb))
    in_specs += [row_blk(W), row_blk(W), row_blk(W)]
    in_specs.append(pl.BlockSpec((8, W), lambda s, t: (jnp.maximum(rowmap(s, t) * (tt // 8) - 1, 0), 0)))
    in_specs += [row_blk(W, 2), row_blk(W, 3)]
    full = lambda shape: pl.BlockSpec(shape, lambda s, t: (0,) * len(shape))
    in_specs += [full(caw.shape), full(ccw.shape), full(pv.shape), full(wr.shape), full(wi.shape)]

    def body(*refs):
        refs = list(refs)
        ab, ac, ax, ag, cx, cg = [_cat(_take(refs, n)) for n in counts]
        cv_ref, xc_ref, yc_ref, halo_ref, dma_ref, dmc_ref, caw_ref, ccw_ref, pv_ref, wr_ref, wi_ref = _take(refs, 11)
        dha_ref, dhc_ref, vec_ref, dwr_ref, dwi_ref = _take(refs, 5)
        carry_dcv, carry_dxc, carry_a, carry_g, c_s, b_s, g_s = refs
        s_id, t = pl.program_id(0), pl.program_id(1)

        @pl.when(t == 0)
        def _():
            for cr in (carry_dcv, carry_dxc, carry_a, carry_g):
                cr[...] = jnp.zeros_like(cr)

        @pl.when((t == 0) & (s_id == 0))
        def _():
            vec_ref[...] = jnp.zeros_like(vec_ref)
            dwr_ref[...] = jnp.zeros_like(dwr_ref)
            dwi_ref[...] = jnp.zeros_like(dwi_ref)

        def acc_row(k, val):
            vec_ref[k:k + 1, :] += jnp.sum(val, axis=0, keepdims=True)

        ccb, br, bi, lam, na, nc = [pv_ref[k:k + 1, :] for k in range(6)]
        cv = cv_ref[...]
        dmix_a = dma_ref[...]
        p = ac * ax
        xn, rstd, n_a = _rms(ab * cv, na)
        sg = _sigmoid(ag)
        dn = dmix_a * (ag * sg)
        dag = dmix_a * n_a * (sg * (1.0 + ag * (1.0 - sg)))
        acc_row(3, dn * xn)
        dya = _rms_bwd(dn, xn, rstd, na)
        dab = dya * cv
        dcv = dya * ab
        cd = carry_dcv[...]
        d1, d2 = _shift_up(dcv, 1, cd), _shift_up(dcv, 2, cd)
        dp = caw_ref[2:3, :] * dcv + caw_ref[1:2, :] * d1 + caw_ref[0:1, :] * d2
        acc_row(2, p * dcv)
        acc_row(1, p * d1)
        acc_row(0, p * d2)
        carry_dcv[...] = dcv[0:8]
        dha_ref[...] = jnp.concatenate([dab, dp * ax, dp * ac, dag], axis=1).astype(dha_ref.dtype)
        xc = xc_ref[...]
        yc = yc_ref[...]
        dmix_c = dmc_ref[...]
        xn, rstd, n_c = _rms(yc, nc)
        sg = _sigmoid(cg)
        dn = dmix_c * (cg * sg)
        dcg = dmix_c * n_c * (sg * (1.0 + cg * (1.0 - sg)))
        acc_row(12, dn * xn)
        dyc = _rms_bwd(dn, xn, rstd, nc)
        r, i, ls, a, sq = _gates(xc, wr_ref, wi_ref, br, bi, lam)
        halo = jnp.where(t == nt - 1, 0.0, halo_ref[...])
        hprev = _shift_down(yc, 1, halo)
        c_c, b_c = _chunk_scan_rev(_shift_up(a, 1, carry_a[...]), dyc)
        c_s[...] = c_c
        b_s[...] = b_c

        def step(k, gnext):
            rows = pl.ds(pl.multiple_of((tt // 8 - 1 - k) * 8, 8), 8)
            gc = b_s[rows, :] + c_s[rows, :] * gnext
            g_s[rows, :] = gc
            return gc[0:1, :]

        lax.fori_loop(0, tt // 8, step, carry_g[0:1, :])
        g = g_s[...]
        carry_g[...] = g[0:8]
        carry_a[...] = a[0:8]
        da = g * hprev
        ixc = i * xc
        dsq = g * ixc
        di = g * sq * xc
        dxc = g * sq * i
        dla = da * a - dsq * (a * a) / sq
        dr = dla * (RG_C * ls)
        acc_row(11, dla * (RG_C * r) * (1.0 / (1.0 + jnp.exp(lam))))
        dgr = dr * r * (1.0 - r)
        dgi = di * i * (1.0 - i)
        acc_row(9, dgr)
        acc_row(10, dgi)
        hw = W // N_RG_HEADS
        parts = []
        for hd in range(N_RG_HEADS):
            sl = slice(hd * hw, (hd + 1) * hw)
            dwr_ref[hd] += _dot(xc[:, sl], dgr[:, sl], TN)
            dwi_ref[hd] += _dot(xc[:, sl], dgi[:, sl], TN)
            parts.append(_dot(dgr[:, sl], wr_ref[hd], NT) + _dot(dgi[:, sl], wi_ref[hd], NT))
        dxc = dxc + jnp.concatenate(parts, axis=1)
        ce = carry_dxc[...]
        e1, e2, e3 = _shift_up(dxc, 1, ce), _shift_up(dxc, 2, ce), _shift_up(dxc, 3, ce)
        dcx = ccw_ref[3:4, :] * dxc + ccw_ref[2:3, :] * e1 + ccw_ref[1:2, :] * e2 + ccw_ref[0:1, :] * e3
        acc_row(7, cx * dxc)
        acc_row(6, cx * e1)
        acc_row(5, cx * e2)
        acc_row(4, cx * e3)
        acc_row(8, dxc)
        carry_dxc[...] = dxc[0:8]
        dhc_ref[...] = jnp.concatenate([dcx, dcg], axis=1).astype(dhc_ref.dtype)

    const = lambda shape: pl.BlockSpec(shape, lambda s, t: (0,) * len(shape))
    return _pcall(
        body, name=name, grid=(T // S, nt), in_specs=in_specs,
        out_shape=(jax.ShapeDtypeStruct((T, 4 * W), BF16), jax.ShapeDtypeStruct((T, 2 * W), BF16),
                   jax.ShapeDtypeStruct((16, W), F32), jax.ShapeDtypeStruct(wr.shape, F32),
                   jax.ShapeDtypeStruct(wi.shape, F32)),
        out_specs=(row_blk(4 * W), row_blk(2 * W), const((16, W)), const(wr.shape), const(wi.shape)),
        scratch_shapes=[pltpu.VMEM((8, W), F32)] * 4 + [pltpu.VMEM((tt, W), F32)] * 3,
        compiler_params=_params(("arbitrary", "arbitrary")),
    )(*([h] * sum(counts)), cv, xc, yc, yc, dmix, dmix, caw, ccw, pv, wr, wi)


def _lo_mask():
    return lax.broadcasted_iota(jnp.int32, (1, 2 * HEAD_DIM), 1) < HEAD_DIM


def _dup(blk, odd, lo):
    rot = pltpu.roll(blk, HEAD_DIM, 1)
    return jnp.where(lo, rot, blk) if odd else jnp.where(lo, blk, rot)


def _stack_heads(x, hh, lo, masked):
    parts = []
    for g in range(KV_GROUP):
        jq = hh * KV_GROUP + g
        pb = x[:, (jq // 2) * 128:(jq // 2 + 1) * 128]
        if masked:
            pb = jnp.where(lo if jq % 2 == 0 else jnp.logical_not(lo), pb, 0.0)
        parts.append(pb)
    return jnp.concatenate(parts, axis=0)


def _unstack_pairs_t(st_t):
    hi = lax.broadcasted_iota(jnp.int32, (2 * HEAD_DIM, BLOCK), 0) >= HEAD_DIM
    return [jnp.where(hi, st_t[:, (2 * pi + 1) * BLOCK:(2 * pi + 2) * BLOCK], st_t[:, (2 * pi) * BLOCK:(2 * pi + 1) * BLOCK]).T
            for pi in range(KV_GROUP // 2)]


def _window(ref, n):
    prev = ref[pl.ds(pl.multiple_of(jnp.maximum(n - 1, 0) * BLOCK, BLOCK), BLOCK), :]
    cur = ref[pl.ds(pl.multiple_of(n * BLOCK, BLOCK), BLOCK), :]
    return jnp.concatenate([prev, cur], axis=0)


def _mask_bias():
    kj = lax.broadcasted_iota(jnp.int32, (2 * BLOCK, KV_GROUP * BLOCK), 0)
    qi = lax.broadcasted_iota(jnp.int32, (2 * BLOCK, KV_GROUP * BLOCK), 1) & (BLOCK - 1)
    dist = qi + BLOCK - kj
    band = (dist >= 0) & (dist < BLOCK)
    return jnp.stack([jnp.where(band & (kj >= BLOCK), 0.0, NEG_INF), jnp.where(band, 0.0, NEG_INF)]).astype(F32)


def _bias_spec():
    return pl.BlockSpec((None, 2 * BLOCK, KV_GROUP * BLOCK), lambda s, n: (jnp.minimum(n, 1), 0, 0))


def _sink_row(sinks_ref, layer, hh):
    return jnp.concatenate([jnp.full((1, BLOCK), sinks_ref[layer, hh * KV_GROUP + g], F32) for g in range(KV_GROUP)],
                           axis=1)


def _softmax_t(qs, kdup, bias, sink):
    s = _dot(kdup, qs, NT) + bias
    m = jnp.maximum(jnp.max(s, axis=0, keepdims=True), sink)
    e = jnp.exp(s - m)
    es = jnp.exp(sink - m)
    r = 1.0 / (jnp.sum(e, axis=0, keepdims=True) + es)
    return e * r, es * r


def _lane_sums_row(x):
    hi = x.astype(BF16)
    lo = (x - hi.astype(F32)).astype(BF16)
    ones = jnp.ones((8, x.shape[1]), BF16)
    dims = (NT, ((), ()))
    return (lax.dot_general(ones, hi, dims, preferred_element_type=F32)
            + lax.dot_general(ones, lo, dims, preferred_element_type=F32))[0:1]


def _attn_fwd(h, sinks, bias, layer, *, S, D, name):
    T = h.shape[0]
    WB, KVW = D // 2, D // 16
    nb = S // BLOCK
    n_kv = KVW // HEAD_DIM

    def body(q_ref, k_ref, v_ref, sinks_ref, bias_ref, o_ref):
        n = pl.program_id(1)
        lo = _lo_mask()
        q = q_ref[...] * (HEAD_DIM ** -0.5)
        kk, vv = _window(k_ref, n), _window(v_ref, n)
        valid = bias_ref[...]
        blocks = []
        for hh in range(n_kv):
            cb = slice((hh // 2) * 128, (hh // 2 + 1) * 128)
            kdup, vdup = _dup(kk[:, cb], hh % 2, lo), _dup(vv[:, cb], hh % 2, lo)
            p_t, _ = _softmax_t(_stack_heads(q, hh, lo, True), kdup, valid, _sink_row(sinks_ref, layer, hh))
            blocks += _unstack_pairs_t(_dot(vdup, p_t, TN))
        o_ref[...] = jnp.concatenate(blocks, axis=1)

    return _pcall(
        body, name=name, grid=(T // S, nb),
        in_specs=[pl.BlockSpec((BLOCK, WB), lambda s, n: (s * nb + n, D // WB)),
                  pl.BlockSpec((S, KVW), lambda s, n: (s, (D + WB) // KVW)),
                  pl.BlockSpec((S, KVW), lambda s, n: (s, (D + WB) // KVW + 1)),
                  pl.BlockSpec(memory_space=pltpu.SMEM), _bias_spec()],
        out_shape=jax.ShapeDtypeStruct((T, WB), F32),
        out_specs=pl.BlockSpec((BLOCK, WB), lambda s, n: (s * nb + n, 0)),
        compiler_params=_params(("arbitrary", "arbitrary")),
    )(h, h, h, sinks, bias)


def _attn_bwd(h, yb, dyb, sinks, bias, layer, *, S, D, name):
    T = h.shape[0]
    WB, KVW = D // 2, D // 16
    nb = S // BLOCK
    n_kv = KVW // HEAD_DIM

    def body(q_ref, k_ref, v_ref, o_ref, do_ref, sinks_ref, bias_ref, dq_ref, dk_ref, dv_ref, dsink_ref, dk_acc, dv_acc):
        s_id, n = pl.program_id(0), pl.program_id(1)
        lo = _lo_mask()

        @pl.when(n == 0)
        def _():
            dk_acc[...] = jnp.zeros_like(dk_acc)
            dv_acc[...] = jnp.zeros_like(dv_acc)

        @pl.when((n == 0) & (s_id == 0))
        def _():
            dsink_ref[...] = jnp.zeros_like(dsink_ref)

        scale = HEAD_DIM ** -0.5
        q, o, do = q_ref[...] * scale, o_ref[...], do_ref[...]
        kk, vv = _window(k_ref, n), _window(v_ref, n)
        valid = bias_ref[...]
        lane = lax.broadcasted_iota(jnp.int32, dsink_ref.shape, 1)
        dq_blocks, dk_heads, dv_heads = [], [], []
        dsink = jnp.zeros(dsink_ref.shape, F32)
        for hh in range(n_kv):
            cb = slice((hh // 2) * 128, (hh // 2 + 1) * 128)
            kdup, vdup = _dup(kk[:, cb], hh % 2, lo), _dup(vv[:, cb], hh % 2, lo)
            qs = _stack_heads(q, hh, lo, True)
            dos = _stack_heads(do, hh, lo, True)
            delta = _lane_sums_row(dos * _stack_heads(o, hh, lo, False))
            p_t, psink = _softmax_t(qs, kdup, valid, _sink_row(sinks_ref, layer, hh))
            dvr = _dot(p_t, dos, NN)
            dv_heads.append(dvr + pltpu.roll(dvr, HEAD_DIM, 1))
            ds_t = p_t * (_dot(vdup, dos, NT) - delta)
            dq_blocks += [b * scale for b in _unstack_pairs_t(_dot(kdup, ds_t, TN))]
            dkr = _dot(ds_t, qs, NN)
            dk_heads.append(dkr + pltpu.roll(dkr, HEAD_DIM, 1))
            dsk = -psink * delta
            for g in range(KV_GROUP):
                tot = jnp.sum(dsk[:, g * BLOCK:(g + 1) * BLOCK], axis=1, keepdims=True)
                dsink = dsink + jnp.where(lane == hh * KV_GROUP + g, tot, 0.0)
        dsink_ref[...] += dsink
        dq_ref[...] = jnp.concatenate(dq_blocks, axis=1).astype(dq_ref.dtype)
        pair = lambda hs: jnp.concatenate([jnp.where(lo, hs[2 * m], hs[2 * m + 1]) for m in range(n_kv // 2)], axis=1)
        dkk, dvv = pair(dk_heads), pair(dv_heads)
        prev = pl.ds(pl.multiple_of(jnp.maximum(n - 1, 0) * BLOCK, BLOCK), BLOCK)
        cur = pl.ds(pl.multiple_of(n * BLOCK, BLOCK), BLOCK)
        dk_acc[prev, :] += dkk[:BLOCK]
        dk_acc[cur, :] += dkk[BLOCK:]
        dv_acc[prev, :] += dvv[:BLOCK]
        dv_acc[cur, :] += dvv[BLOCK:]

        @pl.when(n == nb - 1)
        def _():
            dk_ref[...] = dk_acc[...].astype(dk_ref.dtype)
            dv_ref[...] = dv_acc[...].astype(dv_ref.dtype)

    blk = lambda cb=0: pl.BlockSpec((BLOCK, WB), lambda s, n: (s * nb + n, cb))
    seq = lambda cb=0: pl.BlockSpec((S, KVW), lambda s, n: (s, cb))
    return _pcall(
        body, name=name, grid=(T // S, nb),
        in_specs=[blk(D // WB), seq((D + WB) // KVW), seq((D + WB) // KVW + 1), blk(), blk(),
                  pl.BlockSpec(memory_space=pltpu.SMEM), _bias_spec()],
        out_shape=(jax.ShapeDtypeStruct((T, WB), BF16), jax.ShapeDtypeStruct((T, KVW), BF16),
                   jax.ShapeDtypeStruct((T, KVW), BF16), jax.ShapeDtypeStruct((8, 128), F32)),
        out_specs=(blk(), seq(), seq(), pl.BlockSpec((8, 128), lambda s, n: (0, 0))),
        scratch_shapes=[pltpu.VMEM((S, KVW), F32), pltpu.VMEM((S, KVW), F32)],
        compiler_params=_params(("arbitrary", "arbitrary")),
    )(h, h, h, yb, dyb, sinks, bias)


def _bg_specs(D, tm):
    return _colspecs(D + D // 2 + 2 * (D // 16), D // 2, tm, lambda i: i)


def _mixb_fwd(yb, h, nb_g, mix, *, D, tm, name):
    T, WB = yb.shape
    bg_specs, _ = _bg_specs(D, tm)

    def body(*refs):
        refs = list(refs)
        yb_ref = refs.pop(0)
        bg = _cat(_take(refs, len(bg_specs)))
        g_ref, _, o_ref = refs
        _, _, nrm = _rms(yb_ref[...], g_ref[...])
        o_ref[...] = (nrm * (bg * _sigmoid(bg))).astype(o_ref.dtype)

    row = pl.BlockSpec((tm, WB), lambda i: (i, 0))
    return _pcall(body, name=name, grid=(T // tm,),
                  in_specs=[row] + bg_specs + [pl.BlockSpec((1, WB), lambda i: (0, 0)), ANY],
                  out_shape=jax.ShapeDtypeStruct(mix.shape, mix.dtype), out_specs=row,
                  input_output_aliases={len(bg_specs) + 2: 0},
                  compiler_params=_params(("arbitrary",)))(yb, *([h] * len(bg_specs)), nb_g, mix)


def _mixb_bwd(yb, h, dmix, nb_g, *, D, tm, name):
    T, WB = yb.shape
    bg_specs, _ = _bg_specs(D, tm)
    dm_specs, _ = _colspecs(0, WB, tm, lambda i: i)

    def body(*refs):
        refs = list(refs)
        yb_ref = refs.pop(0)
        bg = _cat(_take(refs, len(bg_specs)))
        dmix_b = _cat(_take(refs, len(dm_specs)))
        g_ref, dyb_ref, dbg_ref, dg_ref = refs

        @pl.when(pl.program_id(0) == 0)
        def _():
            dg_ref[...] = jnp.zeros_like(dg_ref)

        gamma = g_ref[...]
        xn, rstd, nrm = _rms(yb_ref[...], gamma)
        sg = _sigmoid(bg)
        dn = dmix_b * (bg * sg)
        dbg_ref[...] = (dmix_b * nrm * (sg * (1.0 + bg * (1.0 - sg)))).astype(dbg_ref.dtype)
        dg_ref[0:1, :] += jnp.sum(dn * xn, axis=0, keepdims=True)
        dyb_ref[...] = _rms_bwd(dn, xn, rstd, gamma)

    row = pl.BlockSpec((tm, WB), lambda i: (i, 0))
    return _pcall(body, name=name, grid=(T // tm,),
                  in_specs=[row] + bg_specs + dm_specs + [pl.BlockSpec((1, WB), lambda i: (0, 0))],
                  out_shape=(jax.ShapeDtypeStruct((T, WB), F32), jax.ShapeDtypeStruct((T, WB), BF16),
                             jax.ShapeDtypeStruct((8, WB), F32)),
                  out_specs=(row, row, pl.BlockSpec((8, WB), lambda i: (0, 0))),
                  compiler_params=_params(("arbitrary",)))(yb, *([h] * len(bg_specs)), *([dmix] * len(dm_specs)), nb_g)


def _concat_cols(parts, *, tm, name):
    parts = [p if isinstance(p, tuple) else (p, 0, p.shape[1]) for p in parts]
    T = parts[0][0].shape[0]
    total = sum(w for _, _, w in parts)

    def body(*refs):
        refs[-1][...] = jnp.concatenate([r[...] for r in refs[:-1]], axis=1)

    return _pcall(body, name=name, grid=(T // tm,),
                  in_specs=[pl.BlockSpec((tm, w), functools.partial(lambda cb, i: (i, cb), cb)) for _, cb, w in parts],
                  out_shape=jax.ShapeDtypeStruct((T, total), parts[0][0].dtype),
                  out_specs=pl.BlockSpec((tm, total), lambda i: (i, 0)),
                  compiler_params=_params(("parallel",)))(*[a for a, _, _ in parts])


def _ln_fwd(z, g, b, *, tm, name):
    T, D = z.shape

    def body(z_ref, g_ref, b_ref, y_ref, yb_ref):
        zv = z_ref[...]
        mu = jnp.mean(zv, axis=-1, keepdims=True)
        zc = zv - mu
        var = jnp.mean(zc * zc, axis=-1, keepdims=True)
        y = zc * lax.rsqrt(var + LN_EPS) * g_ref[...] + b_ref[...]
        y_ref[...] = y
        yb_ref[...] = y.astype(BF16)

    row = pl.BlockSpec((tm, D), lambda i: (i, 0))
    vec = pl.BlockSpec((1, D), lambda i: (0, 0))
    return _pcall(body, name=name, grid=(T // tm,), in_specs=[row, vec, vec],
                  out_shape=(jax.ShapeDtypeStruct((T, D), F32), jax.ShapeDtypeStruct((T, D), BF16)),
                  out_specs=(row, row), compiler_params=_params(("parallel",)))(z, g, b)


def _ln_bwd(z, dy, g, *, tm, name, deps=()):
    T, D = z.shape

    def body(z_ref, dy_ref, g_ref, *rest):
        dz_ref, dzb_ref, dgb_ref = rest[len(deps):]

        @pl.when(pl.program_id(0) == 0)
        def _():
            dgb_ref[...] = jnp.zeros_like(dgb_ref)

        zv, dyv = z_ref[...], dy_ref[...]
        mu = jnp.mean(zv, axis=-1, keepdims=True)
        zc = zv - mu
        rstd = lax.rsqrt(jnp.mean(zc * zc, axis=-1, keepdims=True) + LN_EPS)
        xh = zc * rstd
        dxh = dyv * g_ref[...]
        dz = rstd * (dxh - jnp.mean(dxh, axis=-1, keepdims=True) - xh * jnp.mean(dxh * xh, axis=-1, keepdims=True))
        dz_ref[...] = dz
        dzb_ref[...] = dz.astype(BF16)
        dgb_ref[0:1, :] += jnp.sum(dyv * xh, axis=0, keepdims=True)
        dgb_ref[1:2, :] += jnp.sum(dyv, axis=0, keepdims=True)

    row = pl.BlockSpec((tm, D), lambda i: (i, 0))
    return _pcall(body, name=name, grid=(T // tm,),
                  in_specs=[row, row, pl.BlockSpec((1, D), lambda i: (0, 0))] + [ANY] * len(deps),
                  out_shape=(jax.ShapeDtypeStruct((T, D), F32), jax.ShapeDtypeStruct((T, D), BF16),
                             jax.ShapeDtypeStruct((8, D), F32)),
                  out_specs=(row, row, pl.BlockSpec((8, D), lambda i: (0, 0))),
                  compiler_params=_params(("arbitrary",)))(z, dy, g, *deps)


def _loss_head(y, target, *, tm, name):
    T, D = y.shape

    def body(y_ref, t_ref, dy_ref, loss_ref):
        @pl.when(pl.program_id(0) == 0)
        def _():
            loss_ref[...] = jnp.zeros_like(loss_ref)

        err = y_ref[...] - t_ref[...]
        dy_ref[...] = err / D
        loss_ref[...] += 0.5 * jnp.sum(jnp.mean(err * err, axis=-1, keepdims=True), axis=0, keepdims=True)

    row = pl.BlockSpec((tm, D), lambda i: (i, 0))
    return _pcall(body, name=name, grid=(T // tm,), in_specs=[row, row],
                  out_shape=(jax.ShapeDtypeStruct((T, D), F32), jax.ShapeDtypeStruct((1, 1), F32)),
                  out_specs=(row, pl.BlockSpec((1, 1), lambda i: (0, 0))),
                  compiler_params=_params(("arbitrary",)))(y, target)


def _cast_bf16(w, layer, *, name, deps=()):
    _, R, C = w.shape
    tr = _tile(R, 512, 8)

    def body(w_ref, *rest):
        rest[-1][...] = w_ref[...].astype(BF16)

    return _pcall(body, name=name, grid=(R // tr,),
                  in_specs=[pl.BlockSpec((None, tr, C), lambda i: (layer, i, 0))] + [ANY] * len(deps),
                  out_shape=jax.ShapeDtypeStruct((R, C), BF16), out_specs=pl.BlockSpec((tr, C), lambda i: (i, 0)),
                  compiler_params=_params(("parallel",)))(w, *deps)


def _cast_shard(w, layer, kind, *, name, deps=()):
    _, R, C = w.shape
    tr = _tile(R, 512, 16)
    nrb = R // tr
    if kind == "in":
        full, o_idx = (R, N_CHIPS * C), lambda i: (i, _my_chip())
    else:
        full, o_idx = (N_CHIPS * R, C), lambda i: (_out_pos(_my_chip()) * nrb + i, 0)

    def body(w_ref, *rest):
        rest[-1][...] = w_ref[...].astype(BF16)

    return _pcall(body, name=name, grid=(nrb,),
                  in_specs=[pl.BlockSpec((None, tr, C), lambda i: (layer, i, 0))] + [ANY] * len(deps),
                  out_shape=jax.ShapeDtypeStruct(full, BF16), out_specs=pl.BlockSpec((tr, C), o_idx),
                  compiler_params=_params(("parallel",)))(w, *deps)


def _adamw_layer(g, w, m, v, layer, bufs, *, name):
    L, R, C = w.shape
    tr = _tile(R, max(8, (1 << 19) // C // 8 * 8), 8)
    if bufs is None:
        bufs = [lax.empty((L, R, C), F32) for _ in range(4)]

    def body(g_ref, w_ref, m_ref, v_ref, b0, b1, b2, b3, go_ref, d_ref, nm_ref, nv_ref):
        gv = g_ref[...]
        nm = ADAM_B1 * m_ref[...] + (1.0 - ADAM_B1) * gv
        nv = ADAM_B2 * v_ref[...] + (1.0 - ADAM_B2) * (gv * gv)
        m_hat = nm / (1.0 - ADAM_B1 ** ADAM_STEP)
        v_hat = nv / (1.0 - ADAM_B2 ** ADAM_STEP)
        go_ref[...] = gv
        d_ref[...] = -ADAM_LR * (m_hat / (jnp.sqrt(v_hat) + ADAM_EPS) + ADAM_WD * w_ref[...])
        nm_ref[...] = nm
        nv_ref[...] = nv

    lay = pl.BlockSpec((None, tr, C), lambda i: (layer, i, 0))
    shp = jax.ShapeDtypeStruct((L, R, C), F32)
    return list(_pcall(body, name=name, grid=(R // tr,),
                       in_specs=[pl.BlockSpec((tr, C), lambda i: (i, 0)), lay, lay, lay] + [ANY] * 4,
                       out_shape=(shp,) * 4, out_specs=(lay,) * 4, input_output_aliases={4 + k: k for k in range(4)},
                       compiler_params=_params(("parallel",)))(g, w, m, v, *bufs))


def _adamw(g, w, m, v, *, name):
    R, C = g.shape
    tr = _tile(R, max(8, (1 << 19) // C // 8 * 8), 8)

    def body(g_ref, w_ref, m_ref, v_ref, d_ref, nm_ref, nv_ref):
        gv = g_ref[...]
        nm = ADAM_B1 * m_ref[...] + (1.0 - ADAM_B1) * gv
        nv = ADAM_B2 * v_ref[...] + (1.0 - ADAM_B2) * (gv * gv)
        m_hat = nm / (1.0 - ADAM_B1 ** ADAM_STEP)
        v_hat = nv / (1.0 - ADAM_B2 ** ADAM_STEP)
        d_ref[...] = -ADAM_LR * (m_hat / (jnp.sqrt(v_hat) + ADAM_EPS) + ADAM_WD * w_ref[...])
        nm_ref[...] = nm
        nv_ref[...] = nv

    blk = pl.BlockSpec((tr, C), lambda i: (i, 0))
    shp = jax.ShapeDtypeStruct((R, C), F32)
    return _pcall(body, name=name, grid=(R // tr,), in_specs=[blk] * 4, out_shape=(shp, shp, shp),
                  out_specs=(blk, blk, blk), compiler_params=_params(("parallel",)))(g, w, m, v)


def _my_core():
    return lax.axis_index("c")


def _my_chip():
    return 2 * lax.axis_index("x") + lax.axis_index("y")


def _out_pos(chip):
    assert N_CHIPS == 4
    return jnp.where(chip == 3, 3, (chip + 2) % 3)


def _pair_sum(mine, theirs, *, half_axis, name):
    R, C = theirs.shape
    tr, tc = _tile(R, 512, 16), _tile(C, 2048)
    nrb, ncb = R // tr, C // tc

    def body(a_ref, b_ref, o_ref):
        o_ref[...] = (a_ref[...].astype(F32) + b_ref[...].astype(F32)).astype(BF16)

    if half_axis == 0:
        a_idx = lambda i, j: (_my_core() * nrb + i, j)
    else:
        a_idx = lambda i, j: (i, _my_core() * ncb + j)
    blk = pl.BlockSpec((tr, tc), lambda i, j: (i, j))
    return _pcall(body, name=name, grid=(nrb, ncb), in_specs=[pl.BlockSpec((tr, tc), a_idx), blk], out_specs=blk,
                  out_shape=jax.ShapeDtypeStruct(theirs.shape, BF16),
                  compiler_params=_params(("parallel", "parallel")))(mine, theirs)


def _final_sum(own, got, *, own_axis, out_shape, out_axis, name):
    _, R, C = got.shape
    tr, tc = _tile(R, 512, 16), _tile(C, 1024)
    nrb, ncb = R // tr, C // tc

    def body(a_ref, q_ref, o_ref):
        o_ref[...] = ((a_ref[...].astype(F32) + q_ref[0].astype(F32)) + q_ref[1].astype(F32)) + q_ref[2].astype(F32)

    if own_axis == 1:
        a_idx = lambda i, j: (i, _my_chip() * ncb + j)
    else:
        a_idx = lambda i, j: (_out_pos(_my_chip()) * nrb + i, j)
    if out_axis == 0:
        o_idx = lambda i, j: (_my_core() * nrb + i, j)
    else:
        o_idx = lambda i, j: (i, _my_core() * ncb + j)
    return _pcall(body, name=name, grid=(nrb, ncb),
                  in_specs=[pl.BlockSpec((tr, tc), a_idx), pl.BlockSpec((3, tr, tc), lambda i, j: (0, i, j))],
                  out_specs=pl.BlockSpec((tr, tc), o_idx), out_shape=jax.ShapeDtypeStruct(out_shape, F32),
                  compiler_params=_params(("parallel", "parallel")))(own, got)


def _sum_devices(gathered, *, name):
    _, R, C = gathered.shape
    tr = _tile(R, 280, 8)

    def body(g_ref, o_ref):
        acc = g_ref[0]
        for d in range(1, N_DEV):
            acc = acc + g_ref[d]
        o_ref[...] = acc

    return _pcall(body, name=name, grid=(R // tr,), in_specs=[pl.BlockSpec((N_DEV, tr, C), lambda i: (0, i, 0))],
                  out_shape=jax.ShapeDtypeStruct((R, C), F32), out_specs=pl.BlockSpec((tr, C), lambda i: (i, 0)),
                  compiler_params=_params(("parallel",)))(gathered)


def _position():
    x, y, c = lax.axis_index("x"), lax.axis_index("y"), lax.axis_index("c")
    chips = [(1 - x, y), (x, 1 - y), (1 - x, 1 - y)]
    return x, y, c, chips


def _remote(src, dst, send_sems, recv_sems, k, to):
    return pltpu.make_async_remote_copy(src_ref=src, dst_ref=dst, send_sem=send_sems.at[k], recv_sem=recv_sems.at[k],
                                        device_id=to, device_id_type=MESH)


def _r(ref, start, n):
    return ref.at[pl.ds(pl.multiple_of(start, 16), n), :]


def _c(ref, start, n):
    return ref.at[:, pl.ds(pl.multiple_of(start, 128), n)]


class _part:
    def __init__(self, ins, outs, plan, n, n_local=0, aliased=0):
        self.ins, self.outs, self.plan, self.n, self.n_local, self.aliased = ins, outs, plan, n, n_local, aliased


def _comm_scratch(parts):
    if not parts:
        return []
    n, nl = sum(p.n for p in parts), sum(p.n_local for p in parts)
    return [pltpu.SemaphoreType.DMA((n,)), pltpu.SemaphoreType.DMA((n,)), pltpu.SemaphoreType.DMA((max(nl, 1),))]


def _comm_aliases(parts, in_base, out_base):
    aliases, ii, oi = {}, in_base, out_base
    for p in parts:
        aliases.update({ii + k: oi + k for k in range(p.aliased)})
        ii += len(p.ins)
        oi += len(p.outs)
    return aliases


def _comm_run(parts, phase, in_refs, out_refs, send_sems, recv_sems, local_sems):
    pos = _position()
    me = pos[:3]
    ii = oi = si = li = 0
    for p in parts:
        sends, recvs, locs = p.plan(in_refs[ii:ii + len(p.ins)], out_refs[oi:oi + len(p.outs)], pos)
        assert len(sends) == len(recvs) == p.n and len(locs) == p.n_local
        if phase == "start":
            for k, (src, dst) in enumerate(locs):
                pltpu.make_async_copy(src, dst, local_sems.at[li + k]).start()
            for k, (src, dst, to) in enumerate(sends):
                _remote(src, dst, send_sems, recv_sems, si + k, to).start()
        else:
            for k, dst in enumerate(recvs):
                _remote(dst, dst, send_sems, recv_sems, si + k, me).wait_recv()
            for k, (src, dst, to) in enumerate(sends):
                _remote(src, dst, send_sems, recv_sems, si + k, to).wait_send()
            for k, (src, dst) in enumerate(locs):
                pltpu.make_async_copy(src, dst, local_sems.at[li + k]).wait()
        ii, oi, si, li = ii + len(p.ins), oi + len(p.outs), si + p.n, li + p.n_local


def _comm_call(parts, *, name):
    n_in = sum(len(p.ins) for p in parts)
    n_out = sum(len(p.outs) for p in parts)

    def body(*refs):
        refs = list(refs)
        cin, cout = _take(refs, n_in), _take(refs, n_out)
        _comm_run(parts, "start", cin, cout, *refs)
        _comm_run(parts, "finish", cin, cout, *refs)

    return list(_pcall(body, name=name, in_specs=[ANY] * n_in, out_specs=[ANY] * n_out,
                       out_shape=[s for p in parts for s in p.outs], scratch_shapes=_comm_scratch(parts),
                       input_output_aliases=_comm_aliases(parts, 0, 0))(*[a for p in parts for a in p.ins]))


HBM = pl.BlockSpec(memory_space=pltpu.HBM)
SEM = pl.BlockSpec(memory_space=pltpu.SEMAPHORE)
EFFECT = pltpu.SideEffectType.DATAFLOW_SIDE_EFFECTING


def _split_refs(parts, arr):
    out, i = [], 0
    for p in parts:
        ins = arr[i:i + len(p.ins)]
        i += len(p.ins)
        lands = arr[i:i + len(p.outs) - p.aliased]
        i += len(lands)
        out.append((ins, list(ins[:p.aliased]) + list(lands)))
    return out


def _split_start(parts, *, name, deps=()):
    assert all(p.n_local == 0 for p in parts)
    arrays = []
    for p in parts:
        arrays += list(p.ins) + [lax.empty(s.shape, s.dtype) for s in p.outs[p.aliased:]]
    n, na = sum(p.n for p in parts), len(arrays)

    def body(*refs):
        refs = list(refs)
        arr = _take(refs, na)
        _take(refs, len(deps))
        sems = _take(refs, 2 * n)
        token = refs[na]
        pos = _position()
        k = 0
        for p, (ins, outs) in zip(parts, _split_refs(parts, arr)):
            sends, _, _ = p.plan(ins, outs, pos)
            for src, dst, to in sends:
                pltpu.make_async_remote_copy(src_ref=src, dst_ref=dst, send_sem=sems[k], recv_sem=sems[n + k],
                                             device_id=to, device_id_type=MESH).start()
                k += 1
        token[...] = jnp.zeros_like(token)

    res = _pcall(
        body, name=name,
        out_shape=[pltpu.SemaphoreType.DMA(())] * (2 * n) + [pltpu.HBM(a.shape, a.dtype) for a in arrays]
        + [jax.ShapeDtypeStruct((8, 128), F32)],
        in_specs=[HBM] * na + [ANY] * len(deps),
        out_specs=[SEM] * (2 * n) + [HBM] * na + [pl.BlockSpec(memory_space=pltpu.VMEM)],
        input_output_aliases={i: 2 * n + i for i in range(na)},
        compiler_params=pltpu.CompilerParams(has_side_effects=EFFECT),
    )(*[pltpu.with_memory_space_constraint(a, pltpu.HBM) for a in arrays], *deps)
    return (list(res[:2 * n]), list(res[2 * n:2 * n + na])), res[-1]


def _split_wait(parts, state, after, *, name):
    sems, arrays = state
    n, na = len(sems) // 2, len(arrays)

    def body(*refs):
        refs = list(refs)
        arr = _take(refs, na)
        sm = _take(refs, 2 * n)
        pos = _position()
        me = pos[:3]
        k = 0
        for p, (ins, outs) in zip(parts, _split_refs(parts, arr)):
            sends, recvs, _ = p.plan(ins, outs, pos)
            for (src, dst, to), land in zip(sends, recvs):
                pltpu.make_async_remote_copy(src_ref=src, dst_ref=dst, send_sem=sm[k], recv_sem=sm[n + k],
                                             device_id=to, device_id_type=MESH).wait_send()
                pltpu.make_async_remote_copy(src_ref=land, dst_ref=land, send_sem=sm[k], recv_sem=sm[n + k],
                                             device_id=me, device_id_type=MESH).wait_recv()
                k += 1

    res = _pcall(
        body, name=name, out_shape=[pltpu.HBM(a.shape, a.dtype) for a in arrays],
        in_specs=[HBM] * na + [SEM] * (2 * n) + [ANY] * len(after), out_specs=[HBM] * na,
        input_output_aliases={i: i for i in range(na)},
        compiler_params=pltpu.CompilerParams(has_side_effects=EFFECT),
    )(*arrays, *sems, *after)
    return _split_refs(parts, list(res))


def _slab(wg, kind, chip, half):
    if kind == "in":
        d, ns = wg.shape[0], wg.shape[1] // N_CHIPS
        return _c(_r(wg, half * (d // 2), d // 2), chip * ns, ns)
    rs = wg.shape[0] // N_CHIPS
    return _r(wg, _out_pos(chip) * rs + half * (rs // 2), rs // 2)


def _gather_ici(wg, kind):
    def plan(ins, outs, pos):
        x, y, c, chips = pos
        (ref,) = outs
        mine = _slab(ref, kind, 2 * x + y, c)
        return [(mine, mine, (*chip, c)) for chip in chips], [_slab(ref, kind, 2 * px + py, c) for px, py in chips], []

    return _part([wg], [jax.ShapeDtypeStruct(wg.shape, wg.dtype)], plan, 3, aliased=1)


def _gather_d2d(wg, kind):
    def plan(ins, outs, pos):
        x, y, c, chips = pos
        (ref,) = outs
        sends = [(_slab(ref, kind, 2 * px + py, c), _slab(ref, kind, 2 * px + py, c), (x, y, 1 - c)) for px, py in chips]
        return sends, [_slab(ref, kind, 2 * px + py, 1 - c) for px, py in chips], []

    return _part([wg], [jax.ShapeDtypeStruct(wg.shape, wg.dtype)], plan, 3, aliased=1)


def _pair_send(gw, kind):
    rows, cols = gw.shape
    half = (rows // 2, cols) if kind == "in" else (rows, cols // 2)

    def plan(ins, outs, pos):
        x, y, c, _ = pos
        (src,), (rb,) = ins, outs
        theirs = _r(src, (1 - c) * half[0], half[0]) if kind == "in" else _c(src, (1 - c) * half[1], half[1])
        return [(theirs, rb, (x, y, 1 - c))], [rb], []

    return _part([gw], [jax.ShapeDtypeStruct(half, gw.dtype)], plan, 1)


def _chip_send(p, kind):
    rows, cols = p.shape
    shard = (rows, cols // N_CHIPS) if kind == "in" else (rows // N_CHIPS, cols)

    def plan(ins, outs, pos):
        x, y, c, chips = pos
        (src,), (q,) = ins, outs
        piece = lambda jk: (_c(src, jk * shard[1], shard[1]) if kind == "in"
                            else _r(src, _out_pos(jk) * shard[0], shard[0]))
        sends = [(piece(2 * px + py), q.at[kk], (px, py, c)) for kk, (px, py) in enumerate(chips)]
        return sends, [q.at[kk] for kk in range(3)], []

    return _part([p], [jax.ShapeDtypeStruct((3,) + shard, p.dtype)], plan, 3)


def _sibling_send(g, kind):
    rows, cols = g.shape

    def plan(ins, outs, pos):
        x, y, c, _ = pos
        (ref,) = outs
        half = (lambda h: _r(ref, h * (rows // 2), rows // 2)) if kind == "in" else (
            lambda h: _c(ref, h * (cols // 2), cols // 2))
        return [(half(c), half(c), (x, y, 1 - c))], [half(1 - c)], []

    return _part([g], [jax.ShapeDtypeStruct(g.shape, g.dtype)], plan, 1, aliased=1)


def _small_ici(block):
    def plan(ins, outs, pos):
        x, y, c, chips = pos
        (src,), (out,) = ins, outs
        mine = out.at[4 * x + 2 * y + c]
        peers = [(x, y, 1 - c)] + [(px, py, c) for px, py in chips]
        return [(src, mine, p) for p in peers], [out.at[4 * px + 2 * py + pc] for px, py, pc in peers], [(src, mine)]

    return _part([block], [jax.ShapeDtypeStruct((N_DEV,) + block.shape, block.dtype)], plan, 4, 1)


def _small_d2d(gathered):
    def plan(ins, outs, pos):
        x, y, c, chips = pos
        (out,) = outs
        sends = [(out.at[4 * px + 2 * py + c], out.at[4 * px + 2 * py + c], (x, y, 1 - c)) for px, py in chips]
        return sends, [out.at[4 * px + 2 * py + (1 - c)] for px, py in chips], []

    return _part([gathered], [jax.ShapeDtypeStruct(gathered.shape, gathered.dtype)], plan, 3, aliased=1)


_SMALL = ["gate_r_w", "gate_i_w", "conv_a_w", "conv_c_w", "sinks", "conv_c_b", "gate_r_b", "gate_i_b", "rg_lambda",
          "norm_a", "norm_b", "norm_c", "ln_g", "ln_b"]


def _pack_small(p):
    L = p["ln_g"].shape[0]
    rows = []
    for n in _SMALL:
        a = p[n]
        if n in ("gate_r_w", "gate_i_w", "norm_b", "ln_g", "ln_b"):
            a = a.reshape(L, -1, 1024)
        elif a.ndim == 2:
            a = a[:, None, :]
        if a.shape[-1] < 1024:
            a = jnp.pad(a, ((0, 0), (0, 0), (0, 1024 - a.shape[-1])))
        rows.append(a)
    out = jnp.concatenate(rows, axis=1)
    assert out.shape[1] == SMALL_ROWS
    return out.reshape(L * SMALL_ROWS, 1024)


def _unpack_small(flat, like):
    L = like["ln_g"].shape[0]
    a = flat.reshape(L, SMALL_ROWS, 1024)
    out, r = {}, 0
    for n in _SMALL:
        shp = like[n].shape
        nrows = max(1, math.prod(shp[1:]) // 1024) if n in ("gate_r_w", "gate_i_w", "norm_b", "ln_g", "ln_b") else (
            shp[1] if len(shp) == 3 else 1)
        blk = a[:, r:r + nrows, :]
        if n in ("gate_r_w", "gate_i_w", "norm_b", "ln_g", "ln_b"):
            out[n] = blk.reshape(shp)
        elif len(shp) == 3:
            out[n] = blk[:, :, :shp[2]]
        else:
            out[n] = blk[:, 0, :shp[1]]
        r += nrows
    return out


def kernel(x, w_in, conv_a_w, sinks, conv_c_w, conv_c_b, gate_r_w, gate_r_b, gate_i_w, gate_i_b, rg_lambda, norm_a, norm_b, norm_c, w_out, ln_g, ln_b, loss_target, m_w_in, m_conv_a_w, m_sinks, m_conv_c_w, m_conv_c_b, m_gate_r_w, m_gate_r_b, m_gate_i_w, m_gate_i_b, m_rg_lambda, m_norm_a, m_norm_b, m_norm_c, m_w_out, m_ln_g, m_ln_b, v_w_in, v_conv_a_w, v_sinks, v_conv_c_w, v_conv_c_b, v_gate_r_w, v_gate_r_b, v_gate_i_w, v_gate_i_b, v_rg_lambda, v_norm_a, v_norm_b, v_norm_c, v_w_out, v_ln_g, v_ln_b):
    names = ["w_in", "conv_a_w", "sinks", "conv_c_w", "conv_c_b", "gate_r_w", "gate_r_b", "gate_i_w", "gate_i_b",
             "rg_lambda", "norm_a", "norm_b", "norm_c", "w_out", "ln_g", "ln_b"]
    w = dict(zip(names, [w_in, conv_a_w, sinks, conv_c_w, conv_c_b, gate_r_w, gate_r_b, gate_i_w, gate_i_b, rg_lambda,
                         norm_a, norm_b, norm_c, w_out, ln_g, ln_b]))
    mom = dict(zip(names, [m_w_in, m_conv_a_w, m_sinks, m_conv_c_w, m_conv_c_b, m_gate_r_w, m_gate_r_b, m_gate_i_w,
                           m_gate_i_b, m_rg_lambda, m_norm_a, m_norm_b, m_norm_c, m_w_out, m_ln_g, m_ln_b]))
    vel = dict(zip(names, [v_w_in, v_conv_a_w, v_sinks, v_conv_c_w, v_conv_c_b, v_gate_r_w, v_gate_r_b, v_gate_i_w,
                           v_gate_i_b, v_rg_lambda, v_norm_a, v_norm_b, v_norm_c, v_w_out, v_ln_g, v_ln_b]))
    B, S, D = x.shape
    T = B * S
    L, _, NS = w_in.shape
    RS = w_out.shape[1]
    W = D // 4
    alpha = (2.0 * L) ** 0.25
    tt = _tile(S, 128, 8)
    tm_row = _tile(T, 256, 8)
    chip = _my_chip()

    wg_in, wg_out = [None] * L, [None] * L
    conv_local = jnp.concatenate([conv_a_w, conv_c_w], axis=1).reshape(L * 7, W // N_CHIPS)
    conv_local = jnp.pad(conv_local, ((0, (-L * 7) % 8), (0, 0)))
    (conv_all,) = _comm_call([_small_ici(conv_local)], name="conv_ici")
    ws_in, ws_out = [_cast_shard(w_in, 0, "in", name="cast_w_in")], [_cast_shard(w_out, 0, "out", name="cast_w_out")]
    g_parts = [_gather_ici(ws_in[0], "in"), _gather_ici(ws_out[0], "out")]
    g_state, g_token = _split_start(g_parts, name="gather_start0", deps=(conv_all,))
    ws_in += [_cast_shard(w_in, l, "in", name="cast_w_in", deps=(g_token,)) for l in range(1, L)]
    ws_out += [_cast_shard(w_out, l, "out", name="cast_w_out", deps=(g_token,)) for l in range(1, L)]
    xf = x.reshape(T, D)
    xb = _cast_bf16(xf[None], 0, name="cast_x", deps=(g_token,))
    (_, (part_in,)), (_, (part_out,)) = _split_wait(g_parts, g_state, ws_in[1:] + ws_out[1:] + [xb], name="gather_wait0")
    wg_in[0], wg_out[0], conv_all = _comm_call(
        [_gather_d2d(part_in, "in"), _gather_d2d(part_out, "out"), _small_d2d(conv_all)], name="gather0_d2d")
    conv_full = jnp.concatenate([conv_all[2 * jj][:L * 7] for jj in range(N_CHIPS)], axis=1).reshape(L, 7, W)
    caw_full, ccw_full = conv_full[:, :3], conv_full[:, 3:]

    def start_gather(layer, deps):
        parts = [_gather_ici(ws_in[layer], "in"), _gather_ici(ws_out[layer], "out")]
        return (parts, *_split_start(parts, name=f"gather_start{layer}", deps=deps))

    mask_bias = _mask_bias()
    saved = []
    flight = start_gather(1, (part_in,)) if L > 1 else None
    for l in range(L):
        nxt = l + 1 < L
        comm = [_gather_d2d(part_out, "out")] if l else []
        res = _mm(xb, wg_in[l], mode="nn", out_dtype=F32, name="proj_in", tm=1024, tn=768, tk=4096, comm=comm,
                  deps=(flight[2],) if nxt else ())
        h = res if not comm else res.pop(0)
        if l:
            wg_out[l] = res.pop(0)
        pv = jnp.stack([conv_c_b[l], gate_r_b[l], gate_i_b[l], rg_lambda[l], norm_a[l], norm_c[l]])
        mix, cv, xc, yc = _ac_fwd(h, caw_full[l], ccw_full[l], pv, gate_r_w[l], gate_i_w[l], S=S, D=D, tt=tt,
                                  name="ac_fwd")
        yb = _attn_fwd(h, sinks, mask_bias, l, S=S, D=D, name="attn_fwd")
        mix = _mixb_fwd(yb, h, norm_b[l][None], mix, D=D, tm=tm_row, name="mixb_fwd")
        comm, deps = [], ()
        if nxt:
            (_, (part_in,)), (_, (part_out,)) = _split_wait(flight[0], flight[1], [mix], name=f"gather_wait{l + 1}")
            comm = [_gather_d2d(part_in, "in")]
            flight = start_gather(l + 2, (part_in,)) if l + 2 < L else None
            deps = (flight[2],) if flight else ()
        res = _mm(mix, wg_out[l], mode="nn", out_dtype=F32, name="proj_out", tn=1024, tk=4096, add=xf, add_scale=alpha,
                  comm=comm, deps=deps)
        z = res if not comm else res.pop(0)
        if nxt:
            wg_in[l + 1] = res.pop(0)
        saved.append((xb, h, cv, xc, yc, yb, mix, z, pv))
        xf, xb = _ln_fwd(z, ln_g[l][None], ln_b[l][None], tm=tm_row, name="ln_fwd")
    dxn, loss_part = _loss_head(xf, loss_target.reshape(T, D), tm=tm_row, name="loss_head")
    loss = lax.psum(loss_part[0, 0], ("x", "y", "c"))

    def final_sums(p_in, q_in, p_out, q_out):
        return (_final_sum(p_in, q_in, own_axis=1, out_shape=(D, NS), out_axis=0, name="final_sum_in"),
                _final_sum(p_out, q_out, own_axis=0, out_shape=(RS, D), out_axis=1, name="final_sum_out"))

    bufs_in = bufs_out = None
    small_g = [None] * L
    ce = None
    for l in reversed(range(L)):
        up, last = l + 1 < L, l == 0
        xb_l, h, cv, xc, yc, yb, mix, z, pv = saved[l]
        dz, dzb, dgb = _ln_bwd(z, dxn, ln_g[l][None], tm=tm_row, name="ln_bwd", deps=(ce[2],) if up else ())
        dmix = _mm(dzb, wg_out[l], mode="nt", out_dtype=F32, name="d_mix", tk=4096)
        gw_out = _mm(mix, dzb, mode="tn", out_dtype=BF16, name="d_w_out", tk=4096)
        dha, dhc, vec, dwr, dwi = _ac_bwd(h, cv, xc, yc, dmix, caw_full[l], ccw_full[l], pv, gate_r_w[l], gate_i_w[l],
                                          S=S, D=D, tt=tt, name="ac_bwd")
        dyb, dbg, dnb = _mixb_bwd(yb, h, dmix, norm_b[l][None], D=D, tm=tm_row, name="mixb_bwd")
        dq, dk, dv, dsk = _attn_bwd(h, yb, dyb, sinks, mask_bias, l, S=S, D=D, name="attn_bwd")
        dh = _concat_cols([dha, dq, dk, dv, dbg, dhc], tm=tm_row, name="concat_dh")
        small_g[l] = dict(gate_r_w=dwr, gate_i_w=dwi, conv_a_w=vec[0:3], conv_c_w=vec[4:8], sinks=dsk[0, :2 * D // 256],
                          conv_c_b=vec[8], gate_r_b=vec[9], gate_i_b=vec[10], rg_lambda=vec[11], norm_a=vec[3],
                          norm_b=dnb[0], norm_c=vec[12], ln_g=dgb[0], ln_b=dgb[1])
        if up:
            ((p_in,), (q_in,)), ((p_out,), (q_out,)) = _split_wait(ce[0], ce[1], [dh], name=f"chip_wait{l + 1}")
            g_in_half, g_out_half = final_sums(p_in, q_in, p_out, q_out)
        comm = [_pair_send(gw_out, "out")]
        if up:
            comm += [_sibling_send(g_in_half, "in"), _sibling_send(g_out_half, "out")]
        if last:
            comm.append(_small_ici(_pack_small({n: jnp.stack([small_g[k][n] for k in range(L)]) for n in _SMALL})))
        res = _mm(xb_l, dh, mode="tn", out_dtype=BF16, name="d_w_in", tm=1024, tn=768, tk=4096, comm=comm)
        gw_in, rb_out = _take(res, 2)
        p_out_l = _pair_sum(gw_out, rb_out, half_axis=1, name="pair_sum_out")
        if up:
            g_in_full, g_out_full = _take(res, 2)
            bufs_in = _adamw_layer(g_in_full, w_in, m_w_in, v_w_in, l + 1, bufs_in, name="adamw_w_in")
            bufs_out = _adamw_layer(g_out_full, w_out, m_w_out, v_w_out, l + 1, bufs_out, name="adamw_w_out")
        if last:
            rb_in, small_all = _comm_call([_pair_send(gw_in, "in"), _small_d2d(res.pop(0))], name="tail_d2d")
        else:
            dxn, rb_in = _mm(dh, wg_in[l], mode="nt", out_dtype=F32, name="d_x", tk=3584, add=dz, add_scale=alpha,
                             comm=[_pair_send(gw_in, "in")])
        p_in_l = _pair_sum(gw_in, rb_in, half_axis=0, name="pair_sum_in")
        ce_parts = [_chip_send(p_in_l, "in"), _chip_send(p_out_l, "out")]
        ce = (ce_parts, *_split_start(ce_parts, name=f"chip_start{l}"))
    dxn = _mm(dh, wg_in[0], mode="nt", out_dtype=F32, name="d_x", tk=3584, add=dz, add_scale=alpha, deps=(ce[2],))
    grad_x = dxn.reshape(B, S, D)
    ((p_in,), (q_in,)), ((p_out,), (q_out,)) = _split_wait(ce[0], ce[1], [dxn] + (bufs_in or []) + (bufs_out or []),
                                                            name="chip_wait0")
    g_in_half, g_out_half = final_sums(p_in, q_in, p_out, q_out)
    g_in0, g_out0 = _comm_call([_sibling_send(g_in_half, "in"), _sibling_send(g_out_half, "out")], name="sibling0")
    big = {"w_in": _adamw_layer(g_in0, w_in, m_w_in, v_w_in, 0, bufs_in, name="adamw_w_in"),
           "w_out": _adamw_layer(g_out0, w_out, m_w_out, v_w_out, 0, bufs_out, name="adamw_w_out")}

    like = {n: w[n] for n in _SMALL}
    like_full = dict(like, conv_a_w=caw_full, conv_c_w=ccw_full)
    g_small = _unpack_small(_sum_devices(small_all, name="sum_small"), like_full)
    for n in ("conv_a_w", "conv_c_w"):
        g_small[n] = lax.dynamic_slice_in_dim(g_small[n], chip * (W // N_CHIPS), W // N_CHIPS, axis=2)

    d_s, m_s, v_s = _adamw(_pack_small(g_small), _pack_small(like), _pack_small({n: mom[n] for n in _SMALL}),
                           _pack_small({n: vel[n] for n in _SMALL}), name="adamw_small")
    grads = dict(g_small)
    delta, new_m, new_v = _unpack_small(d_s, like), _unpack_small(m_s, like), _unpack_small(v_s, like)
    for n in ("w_in", "w_out"):
        grads[n], delta[n], new_m[n], new_v[n] = big[n]

    return (loss, grad_x, *[grads[n] for n in names], *[delta[n] for n in names], *[new_m[n] for n in names],
            *[new_v[n] for n in names])
```

```python
import functools
import math

import jax
import jax.numpy as jnp
from jax import lax
from jax.experimental import pallas as pl
from jax.experimental.pallas import tpu as pltpu

F32 = jnp.float32
BF16 = jnp.bfloat16
_MXU_DTYPE = jnp.bfloat16

HEAD_DIM = 64
KV_GROUP = 8
BLOCK = 128
N_RG_HEADS = 8
RG_C = 8.0
LN_EPS = 1e-5
RMS_EPS = 1e-6
NEG_INF = -1e30
ADAM_LR, ADAM_B1, ADAM_B2, ADAM_EPS, ADAM_WD, ADAM_STEP = 0.001, 0.9, 0.999, 1e-08, 0.01, 10
N_CHIPS = 4
N_DEV = 8
SMALL_ROWS = 280
VMEM_LIMIT = 56 * 1024 * 1024

MESH = pl.DeviceIdType.MESH
ANY = pl.BlockSpec(memory_space=pl.ANY)


def _pcall(body, *, name, **kw):
    return pl.pallas_call(body, name=name, **kw)


def _params(sem=None):
    return pltpu.CompilerParams(dimension_semantics=sem, vmem_limit_bytes=VMEM_LIMIT)


def _tile(dim, pref, mult=128):
    best = None
    for t in range(mult, min(dim, pref) + 1, mult):
        if dim % t == 0:
            best = t
    return best if best is not None else dim


def _dot(a, b, dims):
    return lax.dot_general(a.astype(_MXU_DTYPE), b.astype(_MXU_DTYPE), (dims, ((), ())),
                           preferred_element_type=F32)


NN = ((1,), (0,))
NT = ((1,), (1,))
TN = ((0,), (0,))


def _mm(a, b, *, mode, out_dtype, name, tm=1024, tn=1024, tk=512, add=None, add_scale=1.0, comm=(), deps=()):
    if mode == "nn":
        (M, K), N = a.shape, b.shape[1]
    elif mode == "nt":
        (M, K), N = a.shape, b.shape[0]
    else:
        (K, M), N = a.shape, b.shape[1]
    tm, tn, tk = _tile(M, tm), _tile(N, tn), _tile(K, tk)
    ni, nj, nk = M // tm, N // tn, K // tk
    dims = {"nn": NN, "nt": NT, "tn": TN}[mode]
    n_cin = sum(len(p.ins) for p in comm)
    n_cout = sum(len(p.outs) for p in comm)

    def body(*refs):
        refs = list(refs)
        a_ref, b_ref = _take(refs, 2)
        add_ref = refs.pop(0) if add is not None else None
        _take(refs, len(deps))
        cin = _take(refs, n_cin)
        o_ref = refs.pop(0)
        cout = _take(refs, n_cout)
        acc = refs.pop(0) if nk > 1 else None
        i, j, k = pl.program_id(0), pl.program_id(1), pl.program_id(2)

        if comm:
            @pl.when((i == 0) & (j == 0) & (k == 0))
            def _():
                _comm_run(comm, "start", cin, cout, *refs)

        def finish(r):
            if add_ref is not None:
                r = r + add_scale * add_ref[...]
            o_ref[...] = r.astype(out_dtype)

        if nk == 1:
            finish(_dot(a_ref[...], b_ref[...], dims))
        else:
            @pl.when(k == 0)
            def _():
                acc[...] = jnp.zeros_like(acc)

            acc[...] += _dot(a_ref[...], b_ref[...], dims)

            @pl.when(k == nk - 1)
            def _():
                finish(acc[...])

        if comm:
            @pl.when((i == ni - 1) & (j == nj - 1) & (k == nk - 1))
            def _():
                _comm_run(comm, "finish", cin, cout, *refs)

    a_spec = {"nn": pl.BlockSpec((tm, tk), lambda i, j, k: (i, k)),
              "nt": pl.BlockSpec((tm, tk), lambda i, j, k: (i, k)),
              "tn": pl.BlockSpec((tk, tm), lambda i, j, k: (k, i))}[mode]
    b_spec = {"nn": pl.BlockSpec((tk, tn), lambda i, j, k: (k, j)),
              "nt": pl.BlockSpec((tn, tk), lambda i, j, k: (j, k)),
              "tn": pl.BlockSpec((tk, tn), lambda i, j, k: (k, j))}[mode]
    in_specs, operands = [a_spec, b_spec], [a, b]
    if add is not None:
        in_specs.append(pl.BlockSpec((tm, tn), lambda i, j, k: (i, j)))
        operands.append(add)
    in_specs += [ANY] * len(deps)
    operands += list(deps)
    aliases = _comm_aliases(comm, len(operands), 1)
    in_specs += [ANY] * n_cin
    operands += [arr for p in comm for arr in p.ins]
    out_shape = [jax.ShapeDtypeStruct((M, N), out_dtype)] + [s for p in comm for s in p.outs]
    out_specs = [pl.BlockSpec((tm, tn), lambda i, j, k: (i, j))] + [ANY] * n_cout
    sem = ("arbitrary",) * 3 if comm else ("parallel", "parallel", "arbitrary")
    res = _pcall(body, name=name, out_shape=out_shape, grid=(ni, nj, nk), in_specs=in_specs, out_specs=out_specs,
                 scratch_shapes=([pltpu.VMEM((tm, tn), F32)] if nk > 1 else []) + _comm_scratch(comm),
                 input_output_aliases=aliases,
                 compiler_params=_params(sem))(*operands)
    return list(res) if comm else res[0]


def _colspecs(off, width, rows, rowmap):
    bw = math.gcd(off, width) if off else width
    specs = [pl.BlockSpec((rows, bw), functools.partial(lambda cb, *g: (rowmap(*g), cb), off // bw + i))
             for i in range(width // bw)]
    return specs, bw


def _cat(refs):
    vals = [r[...] for r in refs]
    return vals[0] if len(vals) == 1 else jnp.concatenate(vals, axis=1)


def _take(refs, n):
    out = refs[:n]
    del refs[:n]
    return out


def _write_tile(buf, sems, step, n_steps, tile, dst):
    slot = step % 2
    copy = lambda s: pltpu.make_async_copy(buf.at[s], dst, sems.at[s])

    @pl.when(step >= 2)
    def _():
        copy(slot).wait()

    buf[slot] = tile.astype(buf.dtype)
    copy(slot).start()

    @pl.when(step == n_steps - 1)
    def _():
        copy(slot).wait()
        if n_steps > 1:
            copy(1 - slot).wait()


def _sigmoid(x):
    return 0.5 * jnp.tanh(0.5 * x) + 0.5


def _rms(y, gamma):
    rstd = lax.rsqrt(jnp.mean(y * y, axis=-1, keepdims=True) + RMS_EPS)
    xn = y * rstd
    return xn, rstd, xn * gamma


def _rms_bwd(dn, xn, rstd, gamma):
    dng = dn * gamma
    return rstd * (dng - xn * jnp.mean(dng * xn, axis=-1, keepdims=True))


def _shift_down(x, s, carry8):
    rolled = pltpu.roll(x, s, 0)
    cr = pltpu.roll(carry8, s, 0)
    row8 = lax.broadcasted_iota(jnp.int32, carry8.shape, 0)
    top = jnp.where(row8 < s, cr, rolled[0:8])
    return jnp.concatenate([top, rolled[8:]], axis=0)


def _shift_up(x, s, carry8):
    n = x.shape[0]
    rolled = pltpu.roll(x, n - s, 0)
    cr = pltpu.roll(carry8, 8 - s, 0)
    row8 = lax.broadcasted_iota(jnp.int32, carry8.shape, 0)
    bot = jnp.where(row8 >= 8 - s, cr, rolled[n - 8:])
    return jnp.concatenate([rolled[:n - 8], bot], axis=0)


def _chunk_scan(a, b):
    n = a.shape[0]
    r8 = lax.broadcasted_iota(jnp.int32, a.shape, 0) & 7
    for d in (1, 2, 4):
        ok = r8 >= d
        a_sh = jnp.where(ok, pltpu.roll(a, d, 0), 1.0)
        b_sh = jnp.where(ok, pltpu.roll(b, d, 0), 0.0)
        b = a * b_sh + b
        a = a * a_sh
    return a, b


def _chunk_scan_rev(c, b):
    n = c.shape[0]
    r8 = lax.broadcasted_iota(jnp.int32, c.shape, 0) & 7
    for d in (1, 2, 4):
        ok = r8 + d <= 7
        c_sh = jnp.where(ok, pltpu.roll(c, n - d, 0), 1.0)
        b_sh = jnp.where(ok, pltpu.roll(b, n - d, 0), 0.0)
        b = b + c * b_sh
        c = c * c_sh
    return c, b


def _log1p(x):
    w = 1.0 + x
    return jnp.where(w == 1.0, x, jnp.log(w) * (x / (w - 1.0)))


def _log_sigmoid(x):
    return jnp.minimum(x, 0.0) - _log1p(jnp.exp(-jnp.abs(x)))


def _expm1(x):
    u = jnp.exp(x)
    lu = jnp.log(u)
    small = jnp.where(u == 1.0, x, (u - 1.0) * (x / jnp.where(lu == 0.0, 1.0, lu)))
    return jnp.where(jnp.abs(x) < 0.5, small, u - 1.0)


def _gates(xc, wr_ref, wi_ref, br, bi, lam):
    hw = xc.shape[1] // N_RG_HEADS
    gr = jnp.concatenate([_dot(xc[:, h * hw:(h + 1) * hw], wr_ref[h], NN) for h in range(N_RG_HEADS)], axis=1) + br
    gi = jnp.concatenate([_dot(xc[:, h * hw:(h + 1) * hw], wi_ref[h], NN) for h in range(N_RG_HEADS)], axis=1) + bi
    r, i = _sigmoid(gr), _sigmoid(gi)
    ls = _log_sigmoid(lam)
    la = RG_C * r * ls
    a = jnp.exp(la)
    sq = jnp.sqrt(-_expm1(2.0 * la))
    return r, i, ls, a, sq


def _ac_fwd(h, caw, ccw, pv, wr, wi, *, S, D, tt, name):
    T = h.shape[0]
    W = D // 4
    nt = S // tt
    rowmap = lambda s, t: s * nt + t
    c_off = D + D // 2 + 2 * (D // 16) + D // 2
    offs = [0, W, 2 * W, 3 * W, c_off, c_off + W]
    in_specs, counts = [], []
    for off in offs:
        specs, _ = _colspecs(off, W, tt, rowmap)
        in_specs += specs
        counts.append(len(specs))
    full = lambda shape: pl.BlockSpec(shape, lambda s, t: (0,) * len(shape))
    in_specs += [full(caw.shape), full(ccw.shape), full(pv.shape), full(wr.shape), full(wi.shape)]

    def body(*refs):
        refs = list(refs)
        ab, ac, ax, ag, cx, cg = [_cat(_take(refs, n)) for n in counts]
        caw_ref, ccw_ref, pv_ref, wr_ref, wi_ref = _take(refs, 5)
        mixac_ref, cv_ref, xc_ref, yc_ref = _take(refs, 4)
        carry_p, carry_cx, carry_h, a_s, b_s = refs
        t = pl.program_id(1)

        @pl.when(t == 0)
        def _():
            carry_p[...] = jnp.zeros_like(carry_p)
            carry_cx[...] = jnp.zeros_like(carry_cx)
            carry_h[...] = jnp.zeros_like(carry_h)

        ccb, br, bi, lam, na, nc = [pv_ref[k:k + 1, :] for k in range(6)]
        p = ac * ax
        cp = carry_p[...]
        cv = caw_ref[2:3, :] * p + caw_ref[1:2, :] * _shift_down(p, 1, cp) + caw_ref[0:1, :] * _shift_down(p, 2, cp)
        carry_p[...] = p[tt - 8:tt]
        cv_ref[...] = cv
        _, _, n_a = _rms(ab * cv, na)
        mix_a = n_a * (ag * _sigmoid(ag))
        ccx = carry_cx[...]
        xc = (ccw_ref[3:4, :] * cx + ccw_ref[2:3, :] * _shift_down(cx, 1, ccx) + ccw_ref[1:2, :] * _shift_down(cx, 2, ccx)
              + ccw_ref[0:1, :] * _shift_down(cx, 3, ccx) + ccb)
        carry_cx[...] = cx[tt - 8:tt]
        xc_ref[...] = xc
        r, i, ls, a, sq = _gates(xc, wr_ref, wi_ref, br, bi, lam)
        u = sq * (i * xc)
        a_c, b_c = _chunk_scan(a, u)
        a_s[...] = a_c
        b_s[...] = b_c

        def step(k, hprev):
            rows = pl.ds(pl.multiple_of(k * 8, 8), 8)
            hc = a_s[rows, :] * hprev + b_s[rows, :]
            yc_ref[rows, :] = hc
            return hc[7:8, :]

        hlast = lax.fori_loop(0, tt // 8, step, carry_h[0:1, :])
        carry_h[...] = jnp.broadcast_to(hlast, carry_h.shape)
        _, _, n_c = _rms(yc_ref[...], nc)
        mix_c = n_c * (cg * _sigmoid(cg))
        mixac_ref[...] = jnp.concatenate([mix_a, mix_c], axis=1).astype(mixac_ref.dtype)

    row_blk = lambda w: pl.BlockSpec((tt, w), lambda s, t: (rowmap(s, t), 0))
    return _pcall(
        body, name=name, grid=(T // S, nt), in_specs=in_specs,
        out_shape=(jax.ShapeDtypeStruct((T, 4 * W), BF16), jax.ShapeDtypeStruct((T, W), F32),
                   jax.ShapeDtypeStruct((T, W), F32), jax.ShapeDtypeStruct((T, W), F32)),
        out_specs=(pl.BlockSpec((tt, 2 * W), lambda s, t: (rowmap(s, t), 1)), row_blk(W), row_blk(W), row_blk(W)),
        scratch_shapes=[pltpu.VMEM((8, W), F32), pltpu.VMEM((8, W), F32), pltpu.VMEM((8, W), F32),
                        pltpu.VMEM((tt, W), F32), pltpu.VMEM((tt, W), F32)],
        compiler_params=_params(("arbitrary", "arbitrary")),
    )(*([h] * sum(counts)), caw, ccw, pv, wr, wi)


def _ac_bwd(h, cv, xc, yc, dmix, caw, ccw, pv, wr, wi, *, S, D, tt, name):
    T = h.shape[0]
    W = D // 4
    nt = S // tt
    rowmap = lambda s, t: s * nt + (nt - 1 - t)
    c_off = D + D // 2 + 2 * (D // 16) + D // 2
    offs = [0, W, 2 * W, 3 * W, c_off, c_off + W]
    in_specs, counts = [], []
    for off in offs:
        specs, _ = _colspecs(off, W, tt, rowmap)
        in_specs += specs
        counts.append(len(specs))
    row_blk = lambda w, cb=0: pl.BlockSpec((tt, w), lambda s, t: (rowmap(s, t), cb))
    in_specs += [row_blk(W), row_blk(W), row_blk(W)]
    in_specs.append(pl.BlockSpec((8, W), lambda s, t: (jnp.maximum(rowmap(s, t) * (tt // 8) - 1, 0), 0)))
    in_specs += [row_blk(W, 2), row_blk(W, 3)]
    full = lambda shape: pl.BlockSpec(shape, lambda s, t: (0,) * len(shape))
    in_specs += [full(caw.shape), full(ccw.shape), full(pv.shape), full(wr.shape), full(wi.shape)]

    def body(*refs):
        refs = list(refs)
        ab, ac, ax, ag, cx, cg = [_cat(_take(refs, n)) for n in counts]
        cv_ref, xc_ref, yc_ref, halo_ref, dma_ref, dmc_ref, caw_ref, ccw_ref, pv_ref, wr_ref, wi_ref = _take(refs, 11)
        dh_ref, vec_ref, dwr_ref, dwi_ref = _take(refs, 4)
        carry_dcv, carry_dxc, carry_a, carry_g, c_s, b_s, g_s, buf_a, buf_c, sems_a, sems_c = refs
        s_id, t = pl.program_id(0), pl.program_id(1)
        grid_step, n_grid_steps = s_id * nt + t, (T // S) * nt
        rows = pl.ds(pl.multiple_of(rowmap(s_id, t) * tt, 16), tt)

        @pl.when(t == 0)
        def _():
            for cr in (carry_dcv, carry_dxc, carry_a, carry_g):
                cr[...] = jnp.zeros_like(cr)

        @pl.when((t == 0) & (s_id == 0))
        def _():
            vec_ref[...] = jnp.zeros_like(vec_ref)
            dwr_ref[...] = jnp.zeros_like(dwr_ref)
            dwi_ref[...] = jnp.zeros_like(dwi_ref)

        def acc_row(k, val):
            vec_ref[k:k + 1, :] += jnp.sum(val, axis=0, keepdims=True)

        ccb, br, bi, lam, na, nc = [pv_ref[k:k + 1, :] for k in range(6)]
        cv = cv_ref[...]
        dmix_a = dma_ref[...]
        p = ac * ax
        xn, rstd, n_a = _rms(ab * cv, na)
        sg = _sigmoid(ag)
        dn = dmix_a * (ag * sg)
        dag = dmix_a * n_a * (sg * (1.0 + ag * (1.0 - sg)))
        acc_row(3, dn * xn)
        dya = _rms_bwd(dn, xn, rstd, na)
        dab = dya * cv
        dcv = dya * ab
        cd = carry_dcv[...]
        d1, d2 = _shift_up(dcv, 1, cd), _shift_up(dcv, 2, cd)
        dp = caw_ref[2:3, :] * dcv + caw_ref[1:2, :] * d1 + caw_ref[0:1, :] * d2
        acc_row(2, p * dcv)
        acc_row(1, p * d1)
        acc_row(0, p * d2)
        carry_dcv[...] = dcv[0:8]
        _write_tile(buf_a, sems_a, grid_step, n_grid_steps, jnp.concatenate([dab, dp * ax, dp * ac, dag], axis=1),
                    dh_ref.at[rows, pl.ds(0, 4 * W)])
        xc = xc_ref[...]
        yc = yc_ref[...]
        dmix_c = dmc_ref[...]
        xn, rstd, n_c = _rms(yc, nc)
        sg = _sigmoid(cg)
        dn = dmix_c * (cg * sg)
        dcg = dmix_c * n_c * (sg * (1.0 + cg * (1.0 - sg)))
        acc_row(12, dn * xn)
        dyc = _rms_bwd(dn, xn, rstd, nc)
        r, i, ls, a, sq = _gates(xc, wr_ref, wi_ref, br, bi, lam)
        halo = jnp.where(t == nt - 1, 0.0, halo_ref[...])
        hprev = _shift_down(yc, 1, halo)
        c_c, b_c = _chunk_scan_rev(_shift_up(a, 1, carry_a[...]), dyc)
        c_s[...] = c_c
        b_s[...] = b_c

        def step(k, gnext):
            rows = pl.ds(pl.multiple_of((tt // 8 - 1 - k) * 8, 8), 8)
            gc = b_s[rows, :] + c_s[rows, :] * gnext
            g_s[rows, :] = gc
            return gc[0:1, :]

        lax.fori_loop(0, tt // 8, step, carry_g[0:1, :])
        g = g_s[...]
        carry_g[...] = g[0:8]
        carry_a[...] = a[0:8]
        da = g * hprev
        ixc = i * xc
        dsq = g * ixc
        di = g * sq * xc
        dxc = g * sq * i
        dla = da * a - dsq * (a * a) / sq
        dr = dla * (RG_C * ls)
        acc_row(11, dla * (RG_C * r) * (1.0 / (1.0 + jnp.exp(lam))))
        dgr = dr * r * (1.0 - r)
        dgi = di * i * (1.0 - i)
        acc_row(9, dgr)
        acc_row(10, dgi)
        hw = W // N_RG_HEADS
        parts = []
        for hd in range(N_RG_HEADS):
            sl = slice(hd * hw, (hd + 1) * hw)
            dwr_ref[hd] += _dot(xc[:, sl], dgr[:, sl], TN)
            dwi_ref[hd] += _dot(xc[:, sl], dgi[:, sl], TN)
            parts.append(_dot(dgr[:, sl], wr_ref[hd], NT) + _dot(dgi[:, sl], wi_ref[hd], NT))
        dxc = dxc + jnp.concatenate(parts, axis=1)
        ce = carry_dxc[...]
        e1, e2, e3 = _shift_up(dxc, 1, ce), _shift_up(dxc, 2, ce), _shift_up(dxc, 3, ce)
        dcx = ccw_ref[3:4, :] * dxc + ccw_ref[2:3, :] * e1 + ccw_ref[1:2, :] * e2 + ccw_ref[0:1, :] * e3
        acc_row(7, cx * dxc)
        acc_row(6, cx * e1)
        acc_row(5, cx * e2)
        acc_row(4, cx * e3)
        acc_row(8, dxc)
        carry_dxc[...] = dxc[0:8]
        _write_tile(buf_c, sems_c, grid_step, n_grid_steps, jnp.concatenate([dcx, dcg], axis=1),
                    dh_ref.at[rows, pl.ds(c_off, 2 * W)])

    const = lambda shape: pl.BlockSpec(shape, lambda s, t: (0,) * len(shape))
    return _pcall(
        body, name=name, grid=(T // S, nt), in_specs=in_specs,
        out_shape=(jax.ShapeDtypeStruct((T, c_off + 2 * W), BF16), jax.ShapeDtypeStruct((16, W), F32),
                   jax.ShapeDtypeStruct(wr.shape, F32), jax.ShapeDtypeStruct(wi.shape, F32)),
        out_specs=(ANY, const((16, W)), const(wr.shape), const(wi.shape)),
        scratch_shapes=[pltpu.VMEM((8, W), F32)] * 4 + [pltpu.VMEM((tt, W), F32)] * 3 + [
            pltpu.VMEM((2, tt, 4 * W), BF16), pltpu.VMEM((2, tt, 2 * W), BF16),
            pltpu.SemaphoreType.DMA((2,)), pltpu.SemaphoreType.DMA((2,))],
        compiler_params=_params(("arbitrary", "arbitrary")),
    )(*([h] * sum(counts)), cv, xc, yc, yc, dmix, dmix, caw, ccw, pv, wr, wi)


def _lo_mask():
    return lax.broadcasted_iota(jnp.int32, (1, 2 * HEAD_DIM), 1) < HEAD_DIM


def _dup(blk, odd, lo):
    rot = pltpu.roll(blk, HEAD_DIM, 1)
    return jnp.where(lo, rot, blk) if odd else jnp.where(lo, blk, rot)


def _stack_heads(x, hh, lo, masked):
    parts = []
    for g in range(KV_GROUP):
        jq = hh * KV_GROUP + g
        pb = x[:, (jq // 2) * 128:(jq // 2 + 1) * 128]
        if masked:
            pb = jnp.where(lo if jq % 2 == 0 else jnp.logical_not(lo), pb, 0.0)
        parts.append(pb)
    return jnp.concatenate(parts, axis=0)


def _unstack_pairs_t(st_t):
    hi = lax.broadcasted_iota(jnp.int32, (2 * HEAD_DIM, BLOCK), 0) >= HEAD_DIM
    return [jnp.where(hi, st_t[:, (2 * pi + 1) * BLOCK:(2 * pi + 2) * BLOCK], st_t[:, (2 * pi) * BLOCK:(2 * pi + 1) * BLOCK]).T
            for pi in range(KV_GROUP // 2)]


def _window(ref, n):
    prev = ref[pl.ds(pl.multiple_of(jnp.maximum(n - 1, 0) * BLOCK, BLOCK), BLOCK), :]
    cur = ref[pl.ds(pl.multiple_of(n * BLOCK, BLOCK), BLOCK), :]
    return jnp.concatenate([prev, cur], axis=0)


def _mask_bias():
    kj = lax.broadcasted_iota(jnp.int32, (2 * BLOCK, KV_GROUP * BLOCK), 0)
    qi = lax.broadcasted_iota(jnp.int32, (2 * BLOCK, KV_GROUP * BLOCK), 1) & (BLOCK - 1)
    dist = qi + BLOCK - kj
    band = (dist >= 0) & (dist < BLOCK)
    return jnp.stack([jnp.where(band & (kj >= BLOCK), 0.0, NEG_INF), jnp.where(band, 0.0, NEG_INF)]).astype(F32)


def _bias_spec():
    return pl.BlockSpec((None, 2 * BLOCK, KV_GROUP * BLOCK), lambda s, n: (jnp.minimum(n, 1), 0, 0))


def _sink_row(sinks_ref, layer, hh):
    return jnp.concatenate([jnp.full((1, BLOCK), sinks_ref[layer, hh * KV_GROUP + g], F32) for g in range(KV_GROUP)],
                           axis=1)


def _softmax_t(qs, kdup, bias, sink):
    s = _dot(kdup, qs, NT) + bias
    m = jnp.maximum(jnp.max(s, axis=0, keepdims=True), sink)
    e = jnp.exp(s - m)
    es = jnp.exp(sink - m)
    r = 1.0 / (jnp.sum(e, axis=0, keepdims=True) + es)
    return e * r, es * r


def _lane_sums_row(x):
    hi = x.astype(BF16)
    lo = (x - hi.astype(F32)).astype(BF16)
    ones = jnp.ones((8, x.shape[1]), BF16)
    dims = (NT, ((), ()))
    return (lax.dot_general(ones, hi, dims, preferred_element_type=F32)
            + lax.dot_general(ones, lo, dims, preferred_element_type=F32))[0:1]


def _attn_fwd(h, sinks, bias, layer, *, S, D, name):
    T = h.shape[0]
    WB, KVW = D // 2, D // 16
    nb = S // BLOCK
    n_kv = KVW // HEAD_DIM

    def body(q_ref, k_ref, v_ref, sinks_ref, bias_ref, o_ref):
        n = pl.program_id(1)
        lo = _lo_mask()
        q = q_ref[...] * (HEAD_DIM ** -0.5)
        kk, vv = _window(k_ref, n), _window(v_ref, n)
        valid = bias_ref[...]
        blocks = []
        for hh in range(n_kv):
            cb = slice((hh // 2) * 128, (hh // 2 + 1) * 128)
            kdup, vdup = _dup(kk[:, cb], hh % 2, lo), _dup(vv[:, cb], hh % 2, lo)
            p_t, _ = _softmax_t(_stack_heads(q, hh, lo, True), kdup, valid, _sink_row(sinks_ref, layer, hh))
            blocks += _unstack_pairs_t(_dot(vdup, p_t, TN))
        o_ref[...] = jnp.concatenate(blocks, axis=1)

    return _pcall(
        body, name=name, grid=(T // S, nb),
        in_specs=[pl.BlockSpec((BLOCK, WB), lambda s, n: (s * nb + n, D // WB)),
                  pl.BlockSpec((S, KVW), lambda s, n: (s, (D + WB) // KVW)),
                  pl.BlockSpec((S, KVW), lambda s, n: (s, (D + WB) // KVW + 1)),
                  pl.BlockSpec(memory_space=pltpu.SMEM), _bias_spec()],
        out_shape=jax.ShapeDtypeStruct((T, WB), F32),
        out_specs=pl.BlockSpec((BLOCK, WB), lambda s, n: (s * nb + n, 0)),
        compiler_params=_params(("arbitrary", "arbitrary")),
    )(h, h, h, sinks, bias)


def _attn_bwd(h, yb, dyb, sinks, bias, dh, layer, *, S, D, name):
    T = h.shape[0]
    WB, KVW = D // 2, D // 16
    nb = S // BLOCK
    n_kv = KVW // HEAD_DIM

    def body(q_ref, k_ref, v_ref, o_ref, do_ref, sinks_ref, bias_ref, _, dh_ref, dsink_ref, dk_acc, dv_acc,
             q_buf, q_sems, kv_buf, kv_sems):
        s_id, n = pl.program_id(0), pl.program_id(1)
        lo = _lo_mask()

        @pl.when(n == 0)
        def _():
            dk_acc[...] = jnp.zeros_like(dk_acc)
            dv_acc[...] = jnp.zeros_like(dv_acc)

        @pl.when((n == 0) & (s_id == 0))
        def _():
            dsink_ref[...] = jnp.zeros_like(dsink_ref)

        scale = HEAD_DIM ** -0.5
        q, o, do = q_ref[...] * scale, o_ref[...], do_ref[...]
        kk, vv = _window(k_ref, n), _window(v_ref, n)
        valid = bias_ref[...]
        lane = lax.broadcasted_iota(jnp.int32, dsink_ref.shape, 1)
        dq_blocks, dk_heads, dv_heads = [], [], []
        dsink = jnp.zeros(dsink_ref.shape, F32)
        for hh in range(n_kv):
            cb = slice((hh // 2) * 128, (hh // 2 + 1) * 128)
            kdup, vdup = _dup(kk[:, cb], hh % 2, lo), _dup(vv[:, cb], hh % 2, lo)
            qs = _stack_heads(q, hh, lo, True)
            dos = _stack_heads(do, hh, lo, True)
            delta = _lane_sums_row(dos * _stack_heads(o, hh, lo, False))
            p_t, psink = _softmax_t(qs, kdup, valid, _sink_row(sinks_ref, layer, hh))
            dvr = _dot(p_t, dos, NN)
            dv_heads.append(dvr + pltpu.roll(dvr, HEAD_DIM, 1))
            ds_t = p_t * (_dot(vdup, dos, NT) - delta)
            dq_blocks += [b * scale for b in _unstack_pairs_t(_dot(kdup, ds_t, TN))]
            dkr = _dot(ds_t, qs, NN)
            dk_heads.append(dkr + pltpu.roll(dkr, HEAD_DIM, 1))
            dsk = -psink * delta
            for g in range(KV_GROUP):
                tot = jnp.sum(dsk[:, g * BLOCK:(g + 1) * BLOCK], axis=1, keepdims=True)
                dsink = dsink + jnp.where(lane == hh * KV_GROUP + g, tot, 0.0)
        dsink_ref[...] += dsink
        _write_tile(q_buf, q_sems, s_id * nb + n, (T // S) * nb, jnp.concatenate(dq_blocks, axis=1),
                    dh_ref.at[pl.ds(pl.multiple_of((s_id * nb + n) * BLOCK, BLOCK), BLOCK), pl.ds(D, WB)])
        pair = lambda hs: jnp.concatenate([jnp.where(lo, hs[2 * m], hs[2 * m + 1]) for m in range(n_kv // 2)], axis=1)
        dkk, dvv = pair(dk_heads), pair(dv_heads)
        prev = pl.ds(pl.multiple_of(jnp.maximum(n - 1, 0) * BLOCK, BLOCK), BLOCK)
        cur = pl.ds(pl.multiple_of(n * BLOCK, BLOCK), BLOCK)
        dk_acc[prev, :] += dkk[:BLOCK]
        dk_acc[cur, :] += dkk[BLOCK:]
        dv_acc[prev, :] += dvv[:BLOCK]
        dv_acc[cur, :] += dvv[BLOCK:]

        @pl.when(n == nb - 1)
        def _():
            seq_rows = pl.ds(pl.multiple_of(s_id * S, BLOCK), S)
            kv_buf[0] = dk_acc[...].astype(kv_buf.dtype)
            kv_buf[1] = dv_acc[...].astype(kv_buf.dtype)
            copies = [pltpu.make_async_copy(kv_buf.at[j], dh_ref.at[seq_rows, pl.ds(D + WB + j * KVW, KVW)], kv_sems.at[j])
                      for j in range(2)]
            for cp in copies:
                cp.start()
            for cp in copies:
                cp.wait()

    blk = lambda cb=0: pl.BlockSpec((BLOCK, WB), lambda s, n: (s * nb + n, cb))
    seq = lambda cb=0: pl.BlockSpec((S, KVW), lambda s, n: (s, cb))
    return _pcall(
        body, name=name, grid=(T // S, nb),
        in_specs=[blk(D // WB), seq((D + WB) // KVW), seq((D + WB) // KVW + 1), blk(), blk(),
                  pl.BlockSpec(memory_space=pltpu.SMEM), _bias_spec(), ANY],
        out_shape=(jax.ShapeDtypeStruct(dh.shape, dh.dtype), jax.ShapeDtypeStruct((8, 128), F32)),
        out_specs=(ANY, pl.BlockSpec((8, 128), lambda s, n: (0, 0))),
        scratch_shapes=[pltpu.VMEM((S, KVW), F32), pltpu.VMEM((S, KVW), F32),
                        pltpu.VMEM((2, BLOCK, WB), BF16), pltpu.SemaphoreType.DMA((2,)),
                        pltpu.VMEM((2, S, KVW), BF16), pltpu.SemaphoreType.DMA((2,))],
        input_output_aliases={7: 0},
        compiler_params=_params(("arbitrary", "arbitrary")),
    )(h, h, h, yb, dyb, sinks, bias, dh)


def _bg_specs(D, tm):
    return _colspecs(D + D // 2 + 2 * (D // 16), D // 2, tm, lambda i: i)


def _mixb_fwd(yb, h, nb_g, mix, *, D, tm, name):
    T, WB = yb.shape
    bg_specs, _ = _bg_specs(D, tm)

    def body(*refs):
        refs = list(refs)
        yb_ref = refs.pop(0)
        bg = _cat(_take(refs, len(bg_specs)))
        g_ref, _, o_ref = refs
        _, _, nrm = _rms(yb_ref[...], g_ref[...])
        o_ref[...] = (nrm * (bg * _sigmoid(bg))).astype(o_ref.dtype)

    row = pl.BlockSpec((tm, WB), lambda i: (i, 0))
    return _pcall(body, name=name, grid=(T // tm,),
                  in_specs=[row] + bg_specs + [pl.BlockSpec((1, WB), lambda i: (0, 0)), ANY],
                  out_shape=jax.ShapeDtypeStruct(mix.shape, mix.dtype), out_specs=row,
                  input_output_aliases={len(bg_specs) + 2: 0},
                  compiler_params=_params(("arbitrary",)))(yb, *([h] * len(bg_specs)), nb_g, mix)


def _mixb_bwd(yb, h, dmix, nb_g, dh, *, D, tm, name):
    T, WB = yb.shape
    bg_off = D + D // 2 + 2 * (D // 16)
    bg_specs, _ = _bg_specs(D, tm)
    dm_specs, _ = _colspecs(0, WB, tm, lambda i: i)

    def body(*refs):
        refs = list(refs)
        yb_ref = refs.pop(0)
        bg = _cat(_take(refs, len(bg_specs)))
        dmix_b = _cat(_take(refs, len(dm_specs)))
        g_ref, _, dyb_ref, dh_ref, dg_ref, buf, sems = refs
        i = pl.program_id(0)

        @pl.when(i == 0)
        def _():
            dg_ref[...] = jnp.zeros_like(dg_ref)

        gamma = g_ref[...]
        xn, rstd, nrm = _rms(yb_ref[...], gamma)
        sg = _sigmoid(bg)
        dn = dmix_b * (bg * sg)
        _write_tile(buf, sems, i, T // tm, dmix_b * nrm * (sg * (1.0 + bg * (1.0 - sg))),
                    dh_ref.at[pl.ds(pl.multiple_of(i * tm, 16), tm), pl.ds(bg_off, WB)])
        dg_ref[0:1, :] += jnp.sum(dn * xn, axis=0, keepdims=True)
        dyb_ref[...] = _rms_bwd(dn, xn, rstd, gamma)

    row = pl.BlockSpec((tm, WB), lambda i: (i, 0))
    n_in = 1 + len(bg_specs) + len(dm_specs) + 1
    return _pcall(body, name=name, grid=(T // tm,),
                  in_specs=[row] + bg_specs + dm_specs + [pl.BlockSpec((1, WB), lambda i: (0, 0)), ANY],
                  out_shape=(jax.ShapeDtypeStruct((T, WB), F32), jax.ShapeDtypeStruct(dh.shape, dh.dtype),
                             jax.ShapeDtypeStruct((8, WB), F32)),
                  out_specs=(row, ANY, pl.BlockSpec((8, WB), lambda i: (0, 0))),
                  scratch_shapes=[pltpu.VMEM((2, tm, WB), BF16), pltpu.SemaphoreType.DMA((2,))],
                  input_output_aliases={n_in: 1},
                  compiler_params=_params(("arbitrary",)))(yb, *([h] * len(bg_specs)), *([dmix] * len(dm_specs)), nb_g, dh)


def _ln_fwd(z, g, b, *, tm, name):
    T, D = z.shape

    def body(z_ref, g_ref, b_ref, y_ref, yb_ref):
        zv = z_ref[...]
        mu = jnp.mean(zv, axis=-1, keepdims=True)
        zc = zv - mu
        var = jnp.mean(zc * zc, axis=-1, keepdims=True)
        y = zc * lax.rsqrt(var + LN_EPS) * g_ref[...] + b_ref[...]
        y_ref[...] = y
        yb_ref[...] = y.astype(BF16)

    row = pl.BlockSpec((tm, D), lambda i: (i, 0))
    vec = pl.BlockSpec((1, D), lambda i: (0, 0))
    return _pcall(body, name=name, grid=(T // tm,), in_specs=[row, vec, vec],
                  out_shape=(jax.ShapeDtypeStruct((T, D), F32), jax.ShapeDtypeStruct((T, D), BF16)),
                  out_specs=(row, row), compiler_params=_params(("parallel",)))(z, g, b)


def _ln_bwd(z, dy, g, *, tm, name, deps=()):
    T, D = z.shape

    def body(z_ref, dy_ref, g_ref, *rest):
        dz_ref, dzb_ref, dgb_ref = rest[len(deps):]

        @pl.when(pl.program_id(0) == 0)
        def _():
            dgb_ref[...] = jnp.zeros_like(dgb_ref)

        zv, dyv = z_ref[...], dy_ref[...]
        mu = jnp.mean(zv, axis=-1, keepdims=True)
        zc = zv - mu
        rstd = lax.rsqrt(jnp.mean(zc * zc, axis=-1, keepdims=True) + LN_EPS)
        xh = zc * rstd
        dxh = dyv * g_ref[...]
        dz = rstd * (dxh - jnp.mean(dxh, axis=-1, keepdims=True) - xh * jnp.mean(dxh * xh, axis=-1, keepdims=True))
        dz_ref[...] = dz
        dzb_ref[...] = dz.astype(BF16)
        dgb_ref[0:1, :] += jnp.sum(dyv * xh, axis=0, keepdims=True)
        dgb_ref[1:2, :] += jnp.sum(dyv, axis=0, keepdims=True)

    row = pl.BlockSpec((tm, D), lambda i: (i, 0))
    return _pcall(body, name=name, grid=(T // tm,),
                  in_specs=[row, row, pl.BlockSpec((1, D), lambda i: (0, 0))] + [ANY] * len(deps),
                  out_shape=(jax.ShapeDtypeStruct((T, D), F32), jax.ShapeDtypeStruct((T, D), BF16),
                             jax.ShapeDtypeStruct((8, D), F32)),
                  out_specs=(row, row, pl.BlockSpec((8, D), lambda i: (0, 0))),
                  compiler_params=_params(("arbitrary",)))(z, dy, g, *deps)


def _loss_head(y, target, *, tm, name):
    T, D = y.shape

    def body(y_ref, t_ref, dy_ref, loss_ref):
        @pl.when(pl.program_id(0) == 0)
        def _():
            loss_ref[...] = jnp.zeros_like(loss_ref)

        err = y_ref[...] - t_ref[...]
        dy_ref[...] = err / D
        loss_ref[...] += 0.5 * jnp.sum(jnp.mean(err * err, axis=-1, keepdims=True), axis=0, keepdims=True)

    row = pl.BlockSpec((tm, D), lambda i: (i, 0))
    return _pcall(body, name=name, grid=(T // tm,), in_specs=[row, row],
                  out_shape=(jax.ShapeDtypeStruct((T, D), F32), jax.ShapeDtypeStruct((1, 1), F32)),
                  out_specs=(row, pl.BlockSpec((1, 1), lambda i: (0, 0))),
                  compiler_params=_params(("arbitrary",)))(y, target)


def _cast_bf16(w, layer, *, name, deps=()):
    _, R, C = w.shape
    tr = _tile(R, 512, 8)

    def body(w_ref, *rest):
        rest[-1][...] = w_ref[...].astype(BF16)

    return _pcall(body, name=name, grid=(R // tr,),
                  in_specs=[pl.BlockSpec((None, tr, C), lambda i: (layer, i, 0))] + [ANY] * len(deps),
                  out_shape=jax.ShapeDtypeStruct((R, C), BF16), out_specs=pl.BlockSpec((tr, C), lambda i: (i, 0)),
                  compiler_params=_params(("parallel",)))(w, *deps)


def _cast_shard(w, layer, kind, *, name, deps=()):
    _, R, C = w.shape
    tr = _tile(R, 512, 16)
    nrb = R // tr
    if kind == "in":
        full, o_idx = (R, N_CHIPS * C), lambda i: (i, _my_chip())
    else:
        full, o_idx = (N_CHIPS * R, C), lambda i: (_out_pos(_my_chip()) * nrb + i, 0)

    def body(w_ref, *rest):
        rest[-1][...] = w_ref[...].astype(BF16)

    return _pcall(body, name=name, grid=(nrb,),
                  in_specs=[pl.BlockSpec((None, tr, C), lambda i: (layer, i, 0))] + [ANY] * len(deps),
                  out_shape=jax.ShapeDtypeStruct(full, BF16), out_specs=pl.BlockSpec((tr, C), o_idx),
                  compiler_params=_params(("parallel",)))(w, *deps)


def _adamw_layer(g, w, m, v, layer, bufs, *, name):
    L, R, C = w.shape
    tr = _tile(R, max(8, (1 << 19) // C // 8 * 8), 8)
    if bufs is None:
        bufs = [lax.empty((L, R, C), F32) for _ in range(4)]

    def body(g_ref, w_ref, m_ref, v_ref, b0, b1, b2, b3, go_ref, d_ref, nm_ref, nv_ref):
        gv = g_ref[...]
        nm = ADAM_B1 * m_ref[...] + (1.0 - ADAM_B1) * gv
        nv = ADAM_B2 * v_ref[...] + (1.0 - ADAM_B2) * (gv * gv)
        m_hat = nm / (1.0 - ADAM_B1 ** ADAM_STEP)
        v_hat = nv / (1.0 - ADAM_B2 ** ADAM_STEP)
        go_ref[...] = gv
        d_ref[...] = -ADAM_LR * (m_hat / (jnp.sqrt(v_hat) + ADAM_EPS) + ADAM_WD * w_ref[...])
        nm_ref[...] = nm
        nv_ref[...] = nv

    lay = pl.BlockSpec((None, tr, C), lambda i: (layer, i, 0))
    shp = jax.ShapeDtypeStruct((L, R, C), F32)
    return list(_pcall(body, name=name, grid=(R // tr,),
                       in_specs=[pl.BlockSpec((tr, C), lambda i: (i, 0)), lay, lay, lay] + [ANY] * 4,
                       out_shape=(shp,) * 4, out_specs=(lay,) * 4, input_output_aliases={4 + k: k for k in range(4)},
                       compiler_params=_params(("parallel",)))(g, w, m, v, *bufs))


def _adamw(g, w, m, v, *, name):
    R, C = g.shape
    tr = _tile(R, max(8, (1 << 19) // C // 8 * 8), 8)

    def body(g_ref, w_ref, m_ref, v_ref, d_ref, nm_ref, nv_ref):
        gv = g_ref[...]
        nm = ADAM_B1 * m_ref[...] + (1.0 - ADAM_B1) * gv
        nv = ADAM_B2 * v_ref[...] + (1.0 - ADAM_B2) * (gv * gv)
        m_hat = nm / (1.0 - ADAM_B1 ** ADAM_STEP)
        v_hat = nv / (1.0 - ADAM_B2 ** ADAM_STEP)
        d_ref[...] = -ADAM_LR * (m_hat / (jnp.sqrt(v_hat) + ADAM_EPS) + ADAM_WD * w_ref[...])
        nm_ref[...] = nm
        nv_ref[...] = nv

    blk = pl.BlockSpec((tr, C), lambda i: (i, 0))
    shp = jax.ShapeDtypeStruct((R, C), F32)
    return _pcall(body, name=name, grid=(R // tr,), in_specs=[blk] * 4, out_shape=(shp, shp, shp),
                  out_specs=(blk, blk, blk), compiler_params=_params(("parallel",)))(g, w, m, v)


def _my_core():
    return lax.axis_index("c")


def _my_chip():
    return 2 * lax.axis_index("x") + lax.axis_index("y")


def _out_pos(chip):
    assert N_CHIPS == 4
    return jnp.where(chip == 3, 3, (chip + 2) % 3)


def _pair_sum(mine, theirs, *, half_axis, name):
    R, C = theirs.shape
    tr, tc = _tile(R, 512, 16), _tile(C, 2048)
    nrb, ncb = R // tr, C // tc

    def body(a_ref, b_ref, o_ref):
        o_ref[...] = (a_ref[...].astype(F32) + b_ref[...].astype(F32)).astype(BF16)

    if half_axis == 0:
        a_idx = lambda i, j: (_my_core() * nrb + i, j)
    else:
        a_idx = lambda i, j: (i, _my_core() * ncb + j)
    blk = pl.BlockSpec((tr, tc), lambda i, j: (i, j))
    return _pcall(body, name=name, grid=(nrb, ncb), in_specs=[pl.BlockSpec((tr, tc), a_idx), blk], out_specs=blk,
                  out_shape=jax.ShapeDtypeStruct(theirs.shape, BF16),
                  compiler_params=_params(("parallel", "parallel")))(mine, theirs)


def _final_sum(own, got, *, own_axis, out_shape, out_axis, name):
    _, R, C = got.shape
    tr, tc = _tile(R, 512, 16), _tile(C, 1024)
    nrb, ncb = R // tr, C // tc

    def body(a_ref, q_ref, o_ref):
        o_ref[...] = ((a_ref[...].astype(F32) + q_ref[0].astype(F32)) + q_ref[1].astype(F32)) + q_ref[2].astype(F32)

    if own_axis == 1:
        a_idx = lambda i, j: (i, _my_chip() * ncb + j)
    else:
        a_idx = lambda i, j: (_out_pos(_my_chip()) * nrb + i, j)
    if out_axis == 0:
        o_idx = lambda i, j: (_my_core() * nrb + i, j)
    else:
        o_idx = lambda i, j: (i, _my_core() * ncb + j)
    return _pcall(body, name=name, grid=(nrb, ncb),
                  in_specs=[pl.BlockSpec((tr, tc), a_idx), pl.BlockSpec((3, tr, tc), lambda i, j: (0, i, j))],
                  out_specs=pl.BlockSpec((tr, tc), o_idx), out_shape=jax.ShapeDtypeStruct(out_shape, F32),
                  compiler_params=_params(("parallel", "parallel")))(own, got)


def _sum_devices(gathered, *, name):
    _, R, C = gathered.shape
    tr = _tile(R, 280, 8)

    def body(g_ref, o_ref):
        acc = g_ref[0]
        for d in range(1, N_DEV):
            acc = acc + g_ref[d]
        o_ref[...] = acc

    return _pcall(body, name=name, grid=(R // tr,), in_specs=[pl.BlockSpec((N_DEV, tr, C), lambda i: (0, i, 0))],
                  out_shape=jax.ShapeDtypeStruct((R, C), F32), out_specs=pl.BlockSpec((tr, C), lambda i: (i, 0)),
                  compiler_params=_params(("parallel",)))(gathered)


def _position():
    x, y, c = lax.axis_index("x"), lax.axis_index("y"), lax.axis_index("c")
    chips = [(1 - x, y), (x, 1 - y), (1 - x, 1 - y)]
    return x, y, c, chips


def _remote(src, dst, send_sems, recv_sems, k, to):
    return pltpu.make_async_remote_copy(src_ref=src, dst_ref=dst, send_sem=send_sems.at[k], recv_sem=recv_sems.at[k],
                                        device_id=to, device_id_type=MESH)


def _r(ref, start, n):
    return ref.at[pl.ds(pl.multiple_of(start, 16), n), :]


def _c(ref, start, n):
    return ref.at[:, pl.ds(pl.multiple_of(start, 128), n)]


class _part:
    def __init__(self, ins, outs, plan, n, n_local=0, aliased=0):
        self.ins, self.outs, self.plan, self.n, self.n_local, self.aliased = ins, outs, plan, n, n_local, aliased


def _comm_scratch(parts):
    if not parts:
        return []
    n, nl = sum(p.n for p in parts), sum(p.n_local for p in parts)
    return [pltpu.SemaphoreType.DMA((n,)), pltpu.SemaphoreType.DMA((n,)), pltpu.SemaphoreType.DMA((max(nl, 1),))]


def _comm_aliases(parts, in_base, out_base):
    aliases, ii, oi = {}, in_base, out_base
    for p in parts:
        aliases.update({ii + k: oi + k for k in range(p.aliased)})
        ii += len(p.ins)
        oi += len(p.outs)
    return aliases


def _comm_run(parts, phase, in_refs, out_refs, send_sems, recv_sems, local_sems):
    pos = _position()
    me = pos[:3]
    ii = oi = si = li = 0
    for p in parts:
        sends, recvs, locs = p.plan(in_refs[ii:ii + len(p.ins)], out_refs[oi:oi + len(p.outs)], pos)
        assert len(sends) == len(recvs) == p.n and len(locs) == p.n_local
        if phase == "start":
            for k, (src, dst) in enumerate(locs):
                pltpu.make_async_copy(src, dst, local_sems.at[li + k]).start()
            for k, (src, dst, to) in enumerate(sends):
                _remote(src, dst, send_sems, recv_sems, si + k, to).start()
        else:
            for k, dst in enumerate(recvs):
                _remote(dst, dst, send_sems, recv_sems, si + k, me).wait_recv()
            for k, (src, dst, to) in enumerate(sends):
                _remote(src, dst, send_sems, recv_sems, si + k, to).wait_send()
            for k, (src, dst) in enumerate(locs):
                pltpu.make_async_copy(src, dst, local_sems.at[li + k]).wait()
        ii, oi, si, li = ii + len(p.ins), oi + len(p.outs), si + p.n, li + p.n_local


def _comm_call(parts, *, name):
    n_in = sum(len(p.ins) for p in parts)
    n_out = sum(len(p.outs) for p in parts)

    def body(*refs):
        refs = list(refs)
        cin, cout = _take(refs, n_in), _take(refs, n_out)
        _comm_run(parts, "start", cin, cout, *refs)
        _comm_run(parts, "finish", cin, cout, *refs)

    return list(_pcall(body, name=name, in_specs=[ANY] * n_in, out_specs=[ANY] * n_out,
                       out_shape=[s for p in parts for s in p.outs], scratch_shapes=_comm_scratch(parts),
                       input_output_aliases=_comm_aliases(parts, 0, 0))(*[a for p in parts for a in p.ins]))


HBM = pl.BlockSpec(memory_space=pltpu.HBM)
SEM = pl.BlockSpec(memory_space=pltpu.SEMAPHORE)
EFFECT = pltpu.SideEffectType.DATAFLOW_SIDE_EFFECTING


def _split_refs(parts, arr):
    out, i = [], 0
    for p in parts:
        ins = arr[i:i + len(p.ins)]
        i += len(p.ins)
        lands = arr[i:i + len(p.outs) - p.aliased]
        i += len(lands)
        out.append((ins, list(ins[:p.aliased]) + list(lands)))
    return out


def _split_start(parts, *, name, deps=()):
    assert all(p.n_local == 0 for p in parts)
    arrays = []
    for p in parts:
        arrays += list(p.ins) + [lax.empty(s.shape, s.dtype) for s in p.outs[p.aliased:]]
    n, na = sum(p.n for p in parts), len(arrays)

    def body(*refs):
        refs = list(refs)
        arr = _take(refs, na)
        _take(refs, len(deps))
        sems = _take(refs, 2 * n)
        token = refs[na]
        pos = _position()
        k = 0
        for p, (ins, outs) in zip(parts, _split_refs(parts, arr)):
            sends, _, _ = p.plan(ins, outs, pos)
            for src, dst, to in sends:
                pltpu.make_async_remote_copy(src_ref=src, dst_ref=dst, send_sem=sems[k], recv_sem=sems[n + k],
                                             device_id=to, device_id_type=MESH).start()
                k += 1
        token[...] = jnp.zeros_like(token)

    res = _pcall(
        body, name=name,
        out_shape=[pltpu.SemaphoreType.DMA(())] * (2 * n) + [pltpu.HBM(a.shape, a.dtype) for a in arrays]
        + [jax.ShapeDtypeStruct((8, 128), F32)],
        in_specs=[HBM] * na + [ANY] * len(deps),
        out_specs=[SEM] * (2 * n) + [HBM] * na + [pl.BlockSpec(memory_space=pltpu.VMEM)],
        input_output_aliases={i: 2 * n + i for i in range(na)},
        compiler_params=pltpu.CompilerParams(has_side_effects=EFFECT),
    )(*[pltpu.with_memory_space_constraint(a, pltpu.HBM) for a in arrays], *deps)
    return (list(res[:2 * n]), list(res[2 * n:2 * n + na])), res[-1]


def _split_wait(parts, state, after, *, name):
    sems, arrays = state
    n, na = len(sems) // 2, len(arrays)

    def body(*refs):
        refs = list(refs)
        arr = _take(refs, na)
        sm = _take(refs, 2 * n)
        pos = _position()
        me = pos[:3]
        k = 0
        for p, (ins, outs) in zip(parts, _split_refs(parts, arr)):
            sends, recvs, _ = p.plan(ins, outs, pos)
            for (src, dst, to), land in zip(sends, recvs):
                pltpu.make_async_remote_copy(src_ref=src, dst_ref=dst, send_sem=sm[k], recv_sem=sm[n + k],
                                             device_id=to, device_id_type=MESH).wait_send()
                pltpu.make_async_remote_copy(src_ref=land, dst_ref=land, send_sem=sm[k], recv_sem=sm[n + k],
                                             device_id=me, device_id_type=MESH).wait_recv()
                k += 1

    res = _pcall(
        body, name=name, out_shape=[pltpu.HBM(a.shape, a.dtype) for a in arrays],
        in_specs=[HBM] * na + [SEM] * (2 * n) + [ANY] * len(after), out_specs=[HBM] * na,
        input_output_aliases={i: i for i in range(na)},
        compiler_params=pltpu.CompilerParams(has_side_effects=EFFECT),
    )(*arrays, *sems, *after)
    return _split_refs(parts, list(res))


def _slab(wg, kind, chip, half):
    if kind == "in":
        d, ns = wg.shape[0], wg.shape[1] // N_CHIPS
        return _c(_r(wg, half * (d // 2), d // 2), chip * ns, ns)
    rs = wg.shape[0] // N_CHIPS
    return _r(wg, _out_pos(chip) * rs + half * (rs // 2), rs // 2)


def _gather_ici(wg, kind):
    def plan(ins, outs, pos):
        x, y, c, chips = pos
        (ref,) = outs
        mine = _slab(ref, kind, 2 * x + y, c)
        return [(mine, mine, (*chip, c)) for chip in chips], [_slab(ref, kind, 2 * px + py, c) for px, py in chips], []

    return _part([wg], [jax.ShapeDtypeStruct(wg.shape, wg.dtype)], plan, 3, aliased=1)


def _gather_d2d(wg, kind):
    def plan(ins, outs, pos):
        x, y, c, chips = pos
        (ref,) = outs
        sends = [(_slab(ref, kind, 2 * px + py, c), _slab(ref, kind, 2 * px + py, c), (x, y, 1 - c)) for px, py in chips]
        return sends, [_slab(ref, kind, 2 * px + py, 1 - c) for px, py in chips], []

    return _part([wg], [jax.ShapeDtypeStruct(wg.shape, wg.dtype)], plan, 3, aliased=1)


def _pair_send(gw, kind):
    rows, cols = gw.shape
    half = (rows // 2, cols) if kind == "in" else (rows, cols // 2)

    def plan(ins, outs, pos):
        x, y, c, _ = pos
        (src,), (rb,) = ins, outs
        theirs = _r(src, (1 - c) * half[0], half[0]) if kind == "in" else _c(src, (1 - c) * half[1], half[1])
        return [(theirs, rb, (x, y, 1 - c))], [rb], []

    return _part([gw], [jax.ShapeDtypeStruct(half, gw.dtype)], plan, 1)


def _chip_send(p, kind):
    rows, cols = p.shape
    shard = (rows, cols // N_CHIPS) if kind == "in" else (rows // N_CHIPS, cols)

    def plan(ins, outs, pos):
        x, y, c, chips = pos
        (src,), (q,) = ins, outs
        piece = lambda jk: (_c(src, jk * shard[1], shard[1]) if kind == "in"
                            else _r(src, _out_pos(jk) * shard[0], shard[0]))
        sends = [(piece(2 * px + py), q.at[kk], (px, py, c)) for kk, (px, py) in enumerate(chips)]
        return sends, [q.at[kk] for kk in range(3)], []

    return _part([p], [jax.ShapeDtypeStruct((3,) + shard, p.dtype)], plan, 3)


def _sibling_send(g, kind):
    rows, cols = g.shape

    def plan(ins, outs, pos):
        x, y, c, _ = pos
        (ref,) = outs
        half = (lambda h: _r(ref, h * (rows // 2), rows // 2)) if kind == "in" else (
            lambda h: _c(ref, h * (cols // 2), cols // 2))
        return [(half(c), half(c), (x, y, 1 - c))], [half(1 - c)], []

    return _part([g], [jax.ShapeDtypeStruct(g.shape, g.dtype)], plan, 1, aliased=1)


def _small_ici(block):
    def plan(ins, outs, pos):
        x, y, c, chips = pos
        (src,), (out,) = ins, outs
        mine = out.at[4 * x + 2 * y + c]
        peers = [(x, y, 1 - c)] + [(px, py, c) for px, py in chips]
        return [(src, mine, p) for p in peers], [out.at[4 * px + 2 * py + pc] for px, py, pc in peers], [(src, mine)]

    return _part([block], [jax.ShapeDtypeStruct((N_DEV,) + block.shape, block.dtype)], plan, 4, 1)


def _small_d2d(gathered):
    def plan(ins, outs, pos):
        x, y, c, chips = pos
        (out,) = outs
        sends = [(out.at[4 * px + 2 * py + c], out.at[4 * px + 2 * py + c], (x, y, 1 - c)) for px, py in chips]
        return sends, [out.at[4 * px + 2 * py + (1 - c)] for px, py in chips], []

    return _part([gathered], [jax.ShapeDtypeStruct(gathered.shape, gathered.dtype)], plan, 3, aliased=1)


_SMALL = ["gate_r_w", "gate_i_w", "conv_a_w", "conv_c_w", "sinks", "conv_c_b", "gate_r_b", "gate_i_b", "rg_lambda",
          "norm_a", "norm_b", "norm_c", "ln_g", "ln_b"]


def _pack_small(p):
    L = p["ln_g"].shape[0]
    rows = []
    for n in _SMALL:
        a = p[n]
        if n in ("gate_r_w", "gate_i_w", "norm_b", "ln_g", "ln_b"):
            a = a.reshape(L, -1, 1024)
        elif a.ndim == 2:
            a = a[:, None, :]
        if a.shape[-1] < 1024:
            a = jnp.pad(a, ((0, 0), (0, 0), (0, 1024 - a.shape[-1])))
        rows.append(a)
    out = jnp.concatenate(rows, axis=1)
    assert out.shape[1] == SMALL_ROWS
    return out.reshape(L * SMALL_ROWS, 1024)


def _unpack_small(flat, like):
    L = like["ln_g"].shape[0]
    a = flat.reshape(L, SMALL_ROWS, 1024)
    out, r = {}, 0
    for n in _SMALL:
        shp = like[n].shape
        nrows = max(1, math.prod(shp[1:]) // 1024) if n in ("gate_r_w", "gate_i_w", "norm_b", "ln_g", "ln_b") else (
            shp[1] if len(shp) == 3 else 1)
        blk = a[:, r:r + nrows, :]
        if n in ("gate_r_w", "gate_i_w", "norm_b", "ln_g", "ln_b"):
            out[n] = blk.reshape(shp)
        elif len(shp) == 3:
            out[n] = blk[:, :, :shp[2]]
        else:
            out[n] = blk[:, 0, :shp[1]]
        r += nrows
    return out


def kernel(x, w_in, conv_a_w, sinks, conv_c_w, conv_c_b, gate_r_w, gate_r_b, gate_i_w, gate_i_b, rg_lambda, norm_a, norm_b, norm_c, w_out, ln_g, ln_b, loss_target, m_w_in, m_conv_a_w, m_sinks, m_conv_c_w, m_conv_c_b, m_gate_r_w, m_gate_r_b, m_gate_i_w, m_gate_i_b, m_rg_lambda, m_norm_a, m_norm_b, m_norm_c, m_w_out, m_ln_g, m_ln_b, v_w_in, v_conv_a_w, v_sinks, v_conv_c_w, v_conv_c_b, v_gate_r_w, v_gate_r_b, v_gate_i_w, v_gate_i_b, v_rg_lambda, v_norm_a, v_norm_b, v_norm_c, v_w_out, v_ln_g, v_ln_b):
    names = ["w_in", "conv_a_w", "sinks", "conv_c_w", "conv_c_b", "gate_r_w", "gate_r_b", "gate_i_w", "gate_i_b",
             "rg_lambda", "norm_a", "norm_b", "norm_c", "w_out", "ln_g", "ln_b"]
    w = dict(zip(names, [w_in, conv_a_w, sinks, conv_c_w, conv_c_b, gate_r_w, gate_r_b, gate_i_w, gate_i_b, rg_lambda,
                         norm_a, norm_b, norm_c, w_out, ln_g, ln_b]))
    mom = dict(zip(names, [m_w_in, m_conv_a_w, m_sinks, m_conv_c_w, m_conv_c_b, m_gate_r_w, m_gate_r_b, m_gate_i_w,
                           m_gate_i_b, m_rg_lambda, m_norm_a, m_norm_b, m_norm_c, m_w_out, m_ln_g, m_ln_b]))
    vel = dict(zip(names, [v_w_in, v_conv_a_w, v_sinks, v_conv_c_w, v_conv_c_b, v_gate_r_w, v_gate_r_b, v_gate_i_w,
                           v_gate_i_b, v_rg_lambda, v_norm_a, v_norm_b, v_norm_c, v_w_out, v_ln_g, v_ln_b]))
    B, S, D = x.shape
    T = B * S
    L, _, NS = w_in.shape
    RS = w_out.shape[1]
    W = D // 4
    alpha = (2.0 * L) ** 0.25
    tt = _tile(S, 128, 8)
    tm_row = _tile(T, 256, 8)
    chip = _my_chip()

    wg_in, wg_out = [None] * L, [None] * L
    conv_local = jnp.concatenate([conv_a_w, conv_c_w], axis=1).reshape(L * 7, W // N_CHIPS)
    conv_local = jnp.pad(conv_local, ((0, (-L * 7) % 8), (0, 0)))
    (conv_all,) = _comm_call([_small_ici(conv_local)], name="conv_ici")
    ws_in, ws_out = [_cast_shard(w_in, 0, "in", name="cast_w_in")], [_cast_shard(w_out, 0, "out", name="cast_w_out")]
    g_parts = [_gather_ici(ws_in[0], "in"), _gather_ici(ws_out[0], "out")]
    g_state, g_token = _split_start(g_parts, name="gather_start0", deps=(conv_all,))
    ws_in += [_cast_shard(w_in, l, "in", name="cast_w_in", deps=(g_token,)) for l in range(1, L)]
    ws_out += [_cast_shard(w_out, l, "out", name="cast_w_out", deps=(g_token,)) for l in range(1, L)]
    xf = x.reshape(T, D)
    xb = _cast_bf16(xf[None], 0, name="cast_x", deps=(g_token,))
    (_, (part_in,)), (_, (part_out,)) = _split_wait(g_parts, g_state, ws_in[1:] + ws_out[1:] + [xb], name="gather_wait0")
    wg_in[0], wg_out[0], conv_all = _comm_call(
        [_gather_d2d(part_in, "in"), _gather_d2d(part_out, "out"), _small_d2d(conv_all)], name="gather0_d2d")
    conv_full = jnp.concatenate([conv_all[2 * jj][:L * 7] for jj in range(N_CHIPS)], axis=1).reshape(L, 7, W)
    caw_full, ccw_full = conv_full[:, :3], conv_full[:, 3:]

    def start_gather(layer, deps):
        parts = [_gather_ici(ws_in[layer], "in"), _gather_ici(ws_out[layer], "out")]
        return (parts, *_split_start(parts, name=f"gather_start{layer}", deps=deps))

    mask_bias = _mask_bias()
    saved = []
    flight = start_gather(1, (part_in,)) if L > 1 else None
    for l in range(L):
        nxt = l + 1 < L
        comm = [_gather_d2d(part_out, "out")] if l else []
        res = _mm(xb, wg_in[l], mode="nn", out_dtype=F32, name="proj_in", tm=1024, tn=768, tk=4096, comm=comm,
                  deps=(flight[2],) if nxt else ())
        h = res if not comm else res.pop(0)
        if l:
            wg_out[l] = res.pop(0)
        pv = jnp.stack([conv_c_b[l], gate_r_b[l], gate_i_b[l], rg_lambda[l], norm_a[l], norm_c[l]])
        mix, cv, xc, yc = _ac_fwd(h, caw_full[l], ccw_full[l], pv, gate_r_w[l], gate_i_w[l], S=S, D=D, tt=tt,
                                  name="ac_fwd")
        yb = _attn_fwd(h, sinks, mask_bias, l, S=S, D=D, name="attn_fwd")
        mix = _mixb_fwd(yb, h, norm_b[l][None], mix, D=D, tm=tm_row, name="mixb_fwd")
        comm, deps = [], ()
        if nxt:
            (_, (part_in,)), (_, (part_out,)) = _split_wait(flight[0], flight[1], [mix], name=f"gather_wait{l + 1}")
            comm = [_gather_d2d(part_in, "in")]
            flight = start_gather(l + 2, (part_in,)) if l + 2 < L else None
            deps = (flight[2],) if flight else ()
        res = _mm(mix, wg_out[l], mode="nn", out_dtype=F32, name="proj_out", tn=1024, tk=4096, add=xf, add_scale=alpha,
                  comm=comm, deps=deps)
        z = res if not comm else res.pop(0)
        if nxt:
            wg_in[l + 1] = res.pop(0)
        saved.append((xb, h, cv, xc, yc, yb, mix, z, pv))
        xf, xb = _ln_fwd(z, ln_g[l][None], ln_b[l][None], tm=tm_row, name="ln_fwd")
    dxn, loss_part = _loss_head(xf, loss_target.reshape(T, D), tm=tm_row, name="loss_head")
    loss = lax.psum(loss_part[0, 0], ("x", "y", "c"))

    def final_sums(p_in, q_in, p_out, q_out):
        return (_final_sum(p_in, q_in, own_axis=1, out_shape=(D, NS), out_axis=0, name="final_sum_in"),
                _final_sum(p_out, q_out, own_axis=0, out_shape=(RS, D), out_axis=1, name="final_sum_out"))

    bufs_in = bufs_out = None
    small_g = [None] * L
    ce = None
    for l in reversed(range(L)):
        up, last = l + 1 < L, l == 0
        xb_l, h, cv, xc, yc, yb, mix, z, pv = saved[l]
        dz, dzb, dgb = _ln_bwd(z, dxn, ln_g[l][None], tm=tm_row, name="ln_bwd", deps=(ce[2],) if up else ())
        dmix = _mm(dzb, wg_out[l], mode="nt", out_dtype=F32, name="d_mix", tk=4096)
        gw_out = _mm(mix, dzb, mode="tn", out_dtype=BF16, name="d_w_out", tk=4096)
        dh, vec, dwr, dwi = _ac_bwd(h, cv, xc, yc, dmix, caw_full[l], ccw_full[l], pv, gate_r_w[l], gate_i_w[l],
                                    S=S, D=D, tt=tt, name="ac_bwd")
        dyb, dh, dnb = _mixb_bwd(yb, h, dmix, norm_b[l][None], dh, D=D, tm=tm_row, name="mixb_bwd")
        dh, dsk = _attn_bwd(h, yb, dyb, sinks, mask_bias, dh, l, S=S, D=D, name="attn_bwd")
        small_g[l] = dict(gate_r_w=dwr, gate_i_w=dwi, conv_a_w=vec[0:3], conv_c_w=vec[4:8], sinks=dsk[0, :2 * D // 256],
                          conv_c_b=vec[8], gate_r_b=vec[9], gate_i_b=vec[10], rg_lambda=vec[11], norm_a=vec[3],
                          norm_b=dnb[0], norm_c=vec[12], ln_g=dgb[0], ln_b=dgb[1])
        if up:
            ((p_in,), (q_in,)), ((p_out,), (q_out,)) = _split_wait(ce[0], ce[1], [dh], name=f"chip_wait{l + 1}")
            g_in_half, g_out_half = final_sums(p_in, q_in, p_out, q_out)
        comm = [_pair_send(gw_out, "out")]
        if up:
            comm += [_sibling_send(g_in_half, "in"), _sibling_send(g_out_half, "out")]
        if last:
            comm.append(_small_ici(_pack_small({n: jnp.stack([small_g[k][n] for k in range(L)]) for n in _SMALL})))
        res = _mm(xb_l, dh, mode="tn", out_dtype=BF16, name="d_w_in", tm=1024, tn=768, tk=4096, comm=comm)
        gw_in, rb_out = _take(res, 2)
        p_out_l = _pair_sum(gw_out, rb_out, half_axis=1, name="pair_sum_out")
        if up:
            g_in_full, g_out_full = _take(res, 2)
            bufs_in = _adamw_layer(g_in_full, w_in, m_w_in, v_w_in, l + 1, bufs_in, name="adamw_w_in")
            bufs_out = _adamw_layer(g_out_full, w_out, m_w_out, v_w_out, l + 1, bufs_out, name="adamw_w_out")
        if last:
            rb_in, small_all = _comm_call([_pair_send(gw_in, "in"), _small_d2d(res.pop(0))], name="tail_d2d")
        else:
            dxn, rb_in = _mm(dh, wg_in[l], mode="nt", out_dtype=F32, name="d_x", tk=3584, add=dz, add_scale=alpha,
                             comm=[_pair_send(gw_in, "in")])
        p_in_l = _pair_sum(gw_in, rb_in, half_axis=0, name="pair_sum_in")
        ce_parts = [_chip_send(p_in_l, "in"), _chip_send(p_out_l, "out")]
        ce = (ce_parts, *_split_start(ce_parts, name=f"chip_start{l}"))
    dxn = _mm(dh, wg_in[0], mode="nt", out_dtype=F32, name="d_x", tk=3584, add=dz, add_scale=alpha, deps=(ce[2],))
    grad_x = dxn.reshape(B, S, D)
    ((p_in,), (q_in,)), ((p_out,), (q_out,)) = _split_wait(ce[0], ce[1], [dxn] + (bufs_in or []) + (bufs_out or []),
                                                            name="chip_wait0")
    g_in_half, g_out_half = final_sums(p_in, q_in, p_out, q_out)
    g_in0, g_out0 = _comm_call([_sibling_send(g_in_half, "in"), _sibling_send(g_out_half, "out")], name="sibling0")
    big = {"w_in": _adamw_layer(g_in0, w_in, m_w_in, v_w_in, 0, bufs_in, name="adamw_w_in"),
           "w_out": _adamw_layer(g_out0, w_out, m_w_out, v_w_out, 0, bufs_out, name="adamw_w_out")}

    like = {n: w[n] for n in _SMALL}
    like_full = dict(like, conv_a_w=caw_full, conv_c_w=ccw_full)
    g_small = _unpack_small(_sum_devices(small_all, name="sum_small"), like_full)
    for n in ("conv_a_w", "conv_c_w"):
        g_small[n] = lax.dynamic_slice_in_dim(g_small[n], chip * (W // N_CHIPS), W // N_CHIPS, axis=2)

    d_s, m_s, v_s = _adamw(_pack_small(g_small), _pack_small(like), _pack_small({n: mom[n] for n in _SMALL}),
                           _pack_small({n: vel[n] for n in _SMALL}), name="adamw_small")
    grads = dict(g_small)
    delta, new_m, new_v = _unpack_small(d_s, like), _unpack_small(m_s, like), _unpack_small(v_s, like)
    for n in ("w_in", "w_out"):
        grads[n], delta[n], new_m[n], new_v[n] = big[n]

    return (loss, grad_x, *[grads[n] for n in names], *[delta[n] for n in names], *[new_m[n] for n in names],
            *[new_v[n] for n in names])
```

```python
import functools
import math

import jax
import jax.numpy as jnp
from jax import lax
from jax.experimental import pallas as pl
from jax.experimental.pallas import tpu as pltpu

F32 = jnp.float32
BF16 = jnp.bfloat16
_MXU_DTYPE = jnp.bfloat16

HEAD_DIM = 64
KV_GROUP = 8
BLOCK = 128
N_RG_HEADS = 8
RG_C = 8.0
LN_EPS = 1e-5
RMS_EPS = 1e-6
NEG_INF = -1e30
ADAM_LR, ADAM_B1, ADAM_B2, ADAM_EPS, ADAM_WD, ADAM_STEP = 0.001, 0.9, 0.999, 1e-08, 0.01, 10
N_CHIPS = 4
N_DEV = 8
SMALL_ROWS = 280
VMEM_LIMIT = 56 * 1024 * 1024

MESH = pl.DeviceIdType.MESH
ANY = pl.BlockSpec(memory_space=pl.ANY)


def _pcall(body, *, name, **kw):
    return pl.pallas_call(body, name=name, **kw)


def _params(sem=None):
    return pltpu.CompilerParams(dimension_semantics=sem, vmem_limit_bytes=VMEM_LIMIT)


def _tile(dim, pref, mult=128):
    best = None
    for t in range(mult, min(dim, pref) + 1, mult):
        if dim % t == 0:
            best = t
    return best if best is not None else dim


def _dot(a, b, dims):
    return lax.dot_general(a.astype(_MXU_DTYPE), b.astype(_MXU_DTYPE), (dims, ((), ())),
                           preferred_element_type=F32)


NN = ((1,), (0,))
NT = ((1,), (1,))
TN = ((0,), (0,))


def _mm(a, b, *, mode, out_dtype, name, tm=1024, tn=1024, tk=512, add=None, add_scale=1.0, comm=(), deps=()):
    if mode == "nn":
        (M, K), N = a.shape, b.shape[1]
    elif mode == "nt":
        (M, K), N = a.shape, b.shape[0]
    else:
        (K, M), N = a.shape, b.shape[1]
    tm, tn, tk = _tile(M, tm), _tile(N, tn), _tile(K, tk)
    ni, nj, nk = M // tm, N // tn, K // tk
    dims = {"nn": NN, "nt": NT, "tn": TN}[mode]
    n_cin = sum(len(p.ins) for p in comm)
    n_cout = sum(len(p.outs) for p in comm)

    def body(*refs):
        refs = list(refs)
        a_ref, b_ref = _take(refs, 2)
        add_ref = refs.pop(0) if add is not None else None
        _take(refs, len(deps))
        cin = _take(refs, n_cin)
        o_ref = refs.pop(0)
        cout = _take(refs, n_cout)
        acc = refs.pop(0) if nk > 1 else None
        i, j, k = pl.program_id(0), pl.program_id(1), pl.program_id(2)

        if comm:
            @pl.when((i == 0) & (j == 0) & (k == 0))
            def _():
                _comm_run(comm, "start", cin, cout, *refs)

        def finish(r):
            if add_ref is not None:
                r = r + add_scale * add_ref[...]
            o_ref[...] = r.astype(out_dtype)

        if nk == 1:
            finish(_dot(a_ref[...], b_ref[...], dims))
        else:
            @pl.when(k == 0)
            def _():
                acc[...] = jnp.zeros_like(acc)

            acc[...] += _dot(a_ref[...], b_ref[...], dims)

            @pl.when(k == nk - 1)
            def _():
                finish(acc[...])

        if comm:
            @pl.when((i == ni - 1) & (j == nj - 1) & (k == nk - 1))
            def _():
                _comm_run(comm, "finish", cin, cout, *refs)

    a_spec = {"nn": pl.BlockSpec((tm, tk), lambda i, j, k: (i, k)),
              "nt": pl.BlockSpec((tm, tk), lambda i, j, k: (i, k)),
              "tn": pl.BlockSpec((tk, tm), lambda i, j, k: (k, i))}[mode]
    b_spec = {"nn": pl.BlockSpec((tk, tn), lambda i, j, k: (k, j)),
              "nt": pl.BlockSpec((tn, tk), lambda i, j, k: (j, k)),
              "tn": pl.BlockSpec((tk, tn), lambda i, j, k: (k, j))}[mode]
    in_specs, operands = [a_spec, b_spec], [a, b]
    if add is not None:
        in_specs.append(pl.BlockSpec((tm, tn), lambda i, j, k: (i, j)))
        operands.append(add)
    in_specs += [ANY] * len(deps)
    operands += list(deps)
    aliases = _comm_aliases(comm, len(operands), 1)
    in_specs += [ANY] * n_cin
    operands += [arr for p in comm for arr in p.ins]
    out_shape = [jax.ShapeDtypeStruct((M, N), out_dtype)] + [s for p in comm for s in p.outs]
    out_specs = [pl.BlockSpec((tm, tn), lambda i, j, k: (i, j))] + [ANY] * n_cout
    sem = ("arbitrary",) * 3 if comm else ("parallel", "parallel", "arbitrary")
    res = _pcall(body, name=name, out_shape=out_shape, grid=(ni, nj, nk), in_specs=in_specs, out_specs=out_specs,
                 scratch_shapes=([pltpu.VMEM((tm, tn), F32)] if nk > 1 else []) + _comm_scratch(comm),
                 input_output_aliases=aliases,
                 compiler_params=_params(sem))(*operands)
    return list(res) if comm else res[0]


def _colspecs(off, width, rows, rowmap):
    bw = math.gcd(off, width) if off else width
    specs = [pl.BlockSpec((rows, bw), functools.partial(lambda cb, *g: (rowmap(*g), cb), off // bw + i))
             for i in range(width // bw)]
    return specs, bw


def _cat(refs):
    vals = [r[...] for r in refs]
    return vals[0] if len(vals) == 1 else jnp.concatenate(vals, axis=1)


def _take(refs, n):
    out = refs[:n]
    del refs[:n]
    return out


def _write_tile(buf, sems, step, n_steps, tile, dst):
    slot = step % 2
    copy = lambda s: pltpu.make_async_copy(buf.at[s], dst, sems.at[s])

    @pl.when(step >= 2)
    def _():
        copy(slot).wait()

    buf[slot] = tile.astype(buf.dtype)
    copy(slot).start()

    @pl.when(step == n_steps - 1)
    def _():
        copy(slot).wait()
        if n_steps > 1:
            copy(1 - slot).wait()


def _sigmoid(x):
    return 0.5 * jnp.tanh(0.5 * x) + 0.5


def _rms(y, gamma):
    rstd = lax.rsqrt(jnp.mean(y * y, axis=-1, keepdims=True) + RMS_EPS)
    xn = y * rstd
    return xn, rstd, xn * gamma


def _rms_bwd(dn, xn, rstd, gamma):
    dng = dn * gamma
    return rstd * (dng - xn * jnp.mean(dng * xn, axis=-1, keepdims=True))


def _shift_down(x, s, carry8):
    rolled = pltpu.roll(x, s, 0)
    cr = pltpu.roll(carry8, s, 0)
    row8 = lax.broadcasted_iota(jnp.int32, carry8.shape, 0)
    top = jnp.where(row8 < s, cr, rolled[0:8])
    return jnp.concatenate([top, rolled[8:]], axis=0)


def _shift_up(x, s, carry8):
    n = x.shape[0]
    rolled = pltpu.roll(x, n - s, 0)
    cr = pltpu.roll(carry8, 8 - s, 0)
    row8 = lax.broadcasted_iota(jnp.int32, carry8.shape, 0)
    bot = jnp.where(row8 >= 8 - s, cr, rolled[n - 8:])
    return jnp.concatenate([rolled[:n - 8], bot], axis=0)


def _chunk_scan(a, b):
    n = a.shape[0]
    r8 = lax.broadcasted_iota(jnp.int32, a.shape, 0) & 7
    for d in (1, 2, 4):
        ok = r8 >= d
        a_sh = jnp.where(ok, pltpu.roll(a, d, 0), 1.0)
        b_sh = jnp.where(ok, pltpu.roll(b, d, 0), 0.0)
        b = a * b_sh + b
        a = a * a_sh
    return a, b


def _chunk_scan_rev(c, b):
    n = c.shape[0]
    r8 = lax.broadcasted_iota(jnp.int32, c.shape, 0) & 7
    for d in (1, 2, 4):
        ok = r8 + d <= 7
        c_sh = jnp.where(ok, pltpu.roll(c, n - d, 0), 1.0)
        b_sh = jnp.where(ok, pltpu.roll(b, n - d, 0), 0.0)
        b = b + c * b_sh
        c = c * c_sh
    return c, b


def _log1p(x):
    w = 1.0 + x
    return jnp.where(w == 1.0, x, jnp.log(w) * (x / (w - 1.0)))


def _log_sigmoid(x):
    return jnp.minimum(x, 0.0) - _log1p(jnp.exp(-jnp.abs(x)))


def _expm1(x):
    u = jnp.exp(x)
    lu = jnp.log(u)
    small = jnp.where(u == 1.0, x, (u - 1.0) * (x / jnp.where(lu == 0.0, 1.0, lu)))
    return jnp.where(jnp.abs(x) < 0.5, small, u - 1.0)


def _gates(xc, wr_ref, wi_ref, br, bi, lam):
    hw = xc.shape[1] // N_RG_HEADS
    gr = jnp.concatenate([_dot(xc[:, h * hw:(h + 1) * hw], wr_ref[h], NN) for h in range(N_RG_HEADS)], axis=1) + br
    gi = jnp.concatenate([_dot(xc[:, h * hw:(h + 1) * hw], wi_ref[h], NN) for h in range(N_RG_HEADS)], axis=1) + bi
    r, i = _sigmoid(gr), _sigmoid(gi)
    ls = _log_sigmoid(lam)
    la = RG_C * r * ls
    a = jnp.exp(la)
    sq = jnp.sqrt(-_expm1(2.0 * la))
    return r, i, ls, a, sq


def _ac_fwd(h, caw, ccw, pv, wr, wi, *, S, D, tt, name):
    T = h.shape[0]
    W = D // 4
    nt = S // tt
    rowmap = lambda s, t: s * nt + t
    c_off = D + D // 2 + 2 * (D // 16) + D // 2
    offs = [0, W, 2 * W, 3 * W, c_off, c_off + W]
    in_specs, counts = [], []
    for off in offs:
        specs, _ = _colspecs(off, W, tt, rowmap)
        in_specs += specs
        counts.append(len(specs))
    full = lambda shape: pl.BlockSpec(shape, lambda s, t: (0,) * len(shape))
    in_specs += [full(caw.shape), full(ccw.shape), full(pv.shape), full(wr.shape), full(wi.shape)]

    def body(*refs):
        refs = list(refs)
        ab, ac, ax, ag, cx, cg = [_cat(_take(refs, n)) for n in counts]
        caw_ref, ccw_ref, pv_ref, wr_ref, wi_ref = _take(refs, 5)
        mixac_ref, cv_ref, xc_ref, yc_ref = _take(refs, 4)
        carry_p, carry_cx, carry_h, a_s, b_s = refs
        t = pl.program_id(1)

        @pl.when(t == 0)
        def _():
            carry_p[...] = jnp.zeros_like(carry_p)
            carry_cx[...] = jnp.zeros_like(carry_cx)
            carry_h[...] = jnp.zeros_like(carry_h)

        ccb, br, bi, lam, na, nc = [pv_ref[k:k + 1, :] for k in range(6)]
        p = ac * ax
        cp = carry_p[...]
        cv = caw_ref[2:3, :] * p + caw_ref[1:2, :] * _shift_down(p, 1, cp) + caw_ref[0:1, :] * _shift_down(p, 2, cp)
        carry_p[...] = p[tt - 8:tt]
        cv_ref[...] = cv
        _, _, n_a = _rms(ab * cv, na)
        mix_a = n_a * (ag * _sigmoid(ag))
        ccx = carry_cx[...]
        xc = (ccw_ref[3:4, :] * cx + ccw_ref[2:3, :] * _shift_down(cx, 1, ccx) + ccw_ref[1:2, :] * _shift_down(cx, 2, ccx)
              + ccw_ref[0:1, :] * _shift_down(cx, 3, ccx) + ccb)
        carry_cx[...] = cx[tt - 8:tt]
        xc_ref[...] = xc
        r, i, ls, a, sq = _gates(xc, wr_ref, wi_ref, br, bi, lam)
        u = sq * (i * xc)
        a_c, b_c = _chunk_scan(a, u)
        a_s[...] = a_c
        b_s[...] = b_c

        def step(k, hprev):
            rows = pl.ds(pl.multiple_of(k * 8, 8), 8)
            hc = a_s[rows, :] * hprev + b_s[rows, :]
            yc_ref[rows, :] = hc
            return hc[7:8, :]

        hlast = lax.fori_loop(0, tt // 8, step, carry_h[0:1, :])
        carry_h[...] = jnp.broadcast_to(hlast, carry_h.shape)
        _, _, n_c = _rms(yc_ref[...], nc)
        mix_c = n_c * (cg * _sigmoid(cg))
        mixac_ref[...] = jnp.concatenate([mix_a, mix_c], axis=1).astype(mixac_ref.dtype)

    row_blk = lambda w: pl.BlockSpec((tt, w), lambda s, t: (rowmap(s, t), 0))
    return _pcall(
        body, name=name, grid=(T // S, nt), in_specs=in_specs,
        out_shape=(jax.ShapeDtypeStruct((T, 4 * W), BF16), jax.ShapeDtypeStruct((T, W), F32),
                   jax.ShapeDtypeStruct((T, W), F32), jax.ShapeDtypeStruct((T, W), F32)),
        out_specs=(pl.BlockSpec((tt, 2 * W), lambda s, t: (rowmap(s, t), 1)), row_blk(W), row_blk(W), row_blk(W)),
        scratch_shapes=[pltpu.VMEM((8, W), F32), pltpu.VMEM((8, W), F32), pltpu.VMEM((8, W), F32),
                        pltpu.VMEM((tt, W), F32), pltpu.VMEM((tt, W), F32)],
        compiler_params=_params(("arbitrary", "arbitrary")),
    )(*([h] * sum(counts)), caw, ccw, pv, wr, wi)


def _ac_bwd(h, cv, xc, yc, dmix, caw, ccw, pv, wr, wi, *, S, D, tt, name):
    T = h.shape[0]
    W = D // 4
    nt = S // tt
    rowmap = lambda s, t: s * nt + (nt - 1 - t)
    c_off = D + D // 2 + 2 * (D // 16) + D // 2
    offs = [0, W, 2 * W, 3 * W, c_off, c_off + W]
    in_specs, counts = [], []
    for off in offs:
        specs, _ = _colspecs(off, W, tt, rowmap)
        in_specs += specs
        counts.append(len(specs))
    row_blk = lambda w, cb=0: pl.BlockSpec((tt, w), lambda s, t: (rowmap(s, t), cb))
    in_specs += [row_blk(W), row_blk(W), row_blk(W)]
    in_specs.append(pl.BlockSpec((8, W), lambda s, t: (jnp.maximum(rowmap(s, t) * (tt // 8) - 1, 0), 0)))
    in_specs += [row_blk(W, 2), row_blk(W, 3)]
    full = lambda shape: pl.BlockSpec(shape, lambda s, t: (0,) * len(shape))
    in_specs += [full(caw.shape), full(ccw.shape), full(pv.shape), full(wr.shape), full(wi.shape)]

    def body(*refs):
        refs = list(refs)
        ab, ac, ax, ag, cx, cg = [_cat(_take(refs, n)) for n in counts]
        cv_ref, xc_ref, yc_ref, halo_ref, dma_ref, dmc_ref, caw_ref, ccw_ref, pv_ref, wr_ref, wi_ref = _take(refs, 11)
        dh_ref, vec_ref, dwr_ref, dwi_ref = _take(refs, 4)
        carry_dcv, carry_dxc, carry_a, carry_g, c_s, b_s, g_s, buf_a, buf_c, sems_a, sems_c = refs
        s_id, t = pl.program_id(0), pl.program_id(1)
        grid_step, n_grid_steps = s_id * nt + t, (T // S) * nt
        rows = pl.ds(pl.multiple_of(rowmap(s_id, t) * tt, 16), tt)

        @pl.when(t == 0)
        def _():
            for cr in (carry_dcv, carry_dxc, carry_a, carry_g):
                cr[...] = jnp.zeros_like(cr)

        @pl.when((t == 0) & (s_id == 0))
        def _():
            vec_ref[...] = jnp.zeros_like(vec_ref)
            dwr_ref[...] = jnp.zeros_like(dwr_ref)
            dwi_ref[...] = jnp.zeros_like(dwi_ref)

        def acc_row(k, val):
            vec_ref[k:k + 1, :] += jnp.sum(val, axis=0, keepdims=True)

        ccb, br, bi, lam, na, nc = [pv_ref[k:k + 1, :] for k in range(6)]
        cv = cv_ref[...]
        dmix_a = dma_ref[...]
        p = ac * ax
        xn, rstd, n_a = _rms(ab * cv, na)
        sg = _sigmoid(ag)
        dn = dmix_a * (ag * sg)
        dag = dmix_a * n_a * (sg * (1.0 + ag * (1.0 - sg)))
        acc_row(3, dn * xn)
        dya = _rms_bwd(dn, xn, rstd, na)
        dab = dya * cv
        dcv = dya * ab
        cd = carry_dcv[...]
        d1, d2 = _shift_up(dcv, 1, cd), _shift_up(dcv, 2, cd)
        dp = caw_ref[2:3, :] * dcv + caw_ref[1:2, :] * d1 + caw_ref[0:1, :] * d2
        acc_row(2, p * dcv)
        acc_row(1, p * d1)
        acc_row(0, p * d2)
        carry_dcv[...] = dcv[0:8]
        _write_tile(buf_a, sems_a, grid_step, n_grid_steps, jnp.concatenate([dab, dp * ax, dp * ac, dag], axis=1),
                    dh_ref.at[rows, pl.ds(0, 4 * W)])
        xc = xc_ref[...]
        yc = yc_ref[...]
        dmix_c = dmc_ref[...]
        xn, rstd, n_c = _rms(yc, nc)
        sg = _sigmoid(cg)
        dn = dmix_c * (cg * sg)
        dcg = dmix_c * n_c * (sg * (1.0 + cg * (1.0 - sg)))
        acc_row(12, dn * xn)
        dyc = _rms_bwd(dn, xn, rstd, nc)
        r, i, ls, a, sq = _gates(xc, wr_ref, wi_ref, br, bi, lam)
        halo = jnp.where(t == nt - 1, 0.0, halo_ref[...])
        hprev = _shift_down(yc, 1, halo)
        c_c, b_c = _chunk_scan_rev(_shift_up(a, 1, carry_a[...]), dyc)
        c_s[...] = c_c
        b_s[...] = b_c

        def step(k, gnext):
            rows = pl.ds(pl.multiple_of((tt // 8 - 1 - k) * 8, 8), 8)
            gc = b_s[rows, :] + c_s[rows, :] * gnext
            g_s[rows, :] = gc
            return gc[0:1, :]

        lax.fori_loop(0, tt // 8, step, carry_g[0:1, :])
        g = g_s[...]
        carry_g[...] = g[0:8]
        carry_a[...] = a[0:8]
        da = g * hprev
        ixc = i * xc
        dsq = g * ixc
        di = g * sq * xc
        dxc = g * sq * i
        dla = da * a - dsq * (a * a) / sq
        dr = dla * (RG_C * ls)
        acc_row(11, dla * (RG_C * r) * (1.0 / (1.0 + jnp.exp(lam))))
        dgr = dr * r * (1.0 - r)
        dgi = di * i * (1.0 - i)
        acc_row(9, dgr)
        acc_row(10, dgi)
        hw = W // N_RG_HEADS
        parts = []
        for hd in range(N_RG_HEADS):
            sl = slice(hd * hw, (hd + 1) * hw)
            dwr_ref[hd] += _dot(xc[:, sl], dgr[:, sl], TN)
            dwi_ref[hd] += _dot(xc[:, sl], dgi[:, sl], TN)
            parts.append(_dot(dgr[:, sl], wr_ref[hd], NT) + _dot(dgi[:, sl], wi_ref[hd], NT))
        dxc = dxc + jnp.concatenate(parts, axis=1)
        ce = carry_dxc[...]
        e1, e2, e3 = _shift_up(dxc, 1, ce), _shift_up(dxc, 2, ce), _shift_up(dxc, 3, ce)
        dcx = ccw_ref[3:4, :] * dxc + ccw_ref[2:3, :] * e1 + ccw_ref[1:2, :] * e2 + ccw_ref[0:1, :] * e3
        acc_row(7, cx * dxc)
        acc_row(6, cx * e1)
        acc_row(5, cx * e2)
        acc_row(4, cx * e3)
        acc_row(8, dxc)
        carry_dxc[...] = dxc[0:8]
        _write_tile(buf_c, sems_c, grid_step, n_grid_steps, jnp.concatenate([dcx, dcg], axis=1),
                    dh_ref.at[rows, pl.ds(c_off, 2 * W)])

    const = lambda shape: pl.BlockSpec(shape, lambda s, t: (0,) * len(shape))
    return _pcall(
        body, name=name, grid=(T // S, nt), in_specs=in_specs,
        out_shape=(jax.ShapeDtypeStruct((T, c_off + 2 * W), BF16), jax.ShapeDtypeStruct((16, W), F32),
                   jax.ShapeDtypeStruct(wr.shape, F32), jax.ShapeDtypeStruct(wi.shape, F32)),
        out_specs=(ANY, const((16, W)), const(wr.shape), const(wi.shape)),
        scratch_shapes=[pltpu.VMEM((8, W), F32)] * 4 + [pltpu.VMEM((tt, W), F32)] * 3 + [
            pltpu.VMEM((2, tt, 4 * W), BF16), pltpu.VMEM((2, tt, 2 * W), BF16),
            pltpu.SemaphoreType.DMA((2,)), pltpu.SemaphoreType.DMA((2,))],
        compiler_params=_params(("arbitrary", "arbitrary")),
    )(*([h] * sum(counts)), cv, xc, yc, yc, dmix, dmix, caw, ccw, pv, wr, wi)


def _lo_mask():
    return lax.broadcasted_iota(jnp.int32, (1, 2 * HEAD_DIM), 1) < HEAD_DIM


def _dup(blk, odd, lo):
    rot = pltpu.roll(blk, HEAD_DIM, 1)
    return jnp.where(lo, rot, blk) if odd else jnp.where(lo, blk, rot)


def _stack_heads(x, hh, lo, masked):
    parts = []
    for g in range(KV_GROUP):
        jq = hh * KV_GROUP + g
        pb = x[:, (jq // 2) * 128:(jq // 2 + 1) * 128]
        if masked:
            pb = jnp.where(lo if jq % 2 == 0 else jnp.logical_not(lo), pb, 0.0)
        parts.append(pb)
    return jnp.concatenate(parts, axis=0)


def _unstack_pairs_t(st_t):
    hi = lax.broadcasted_iota(jnp.int32, (2 * HEAD_DIM, BLOCK), 0) >= HEAD_DIM
    return [jnp.where(hi, st_t[:, (2 * pi + 1) * BLOCK:(2 * pi + 2) * BLOCK], st_t[:, (2 * pi) * BLOCK:(2 * pi + 1) * BLOCK]).T
            for pi in range(KV_GROUP // 2)]


def _window(ref, n):
    prev = ref[pl.ds(pl.multiple_of(jnp.maximum(n - 1, 0) * BLOCK, BLOCK), BLOCK), :]
    cur = ref[pl.ds(pl.multiple_of(n * BLOCK, BLOCK), BLOCK), :]
    return jnp.concatenate([prev, cur], axis=0)


def _mask_bias():
    kj = lax.broadcasted_iota(jnp.int32, (2 * BLOCK, KV_GROUP * BLOCK), 0)
    qi = lax.broadcasted_iota(jnp.int32, (2 * BLOCK, KV_GROUP * BLOCK), 1) & (BLOCK - 1)
    dist = qi + BLOCK - kj
    band = (dist >= 0) & (dist < BLOCK)
    return jnp.stack([jnp.where(band & (kj >= BLOCK), 0.0, NEG_INF), jnp.where(band, 0.0, NEG_INF)]).astype(F32)


def _bias_spec():
    return pl.BlockSpec((None, 2 * BLOCK, KV_GROUP * BLOCK), lambda s, n: (jnp.minimum(n, 1), 0, 0))


def _sink_row(sinks_ref, layer, hh):
    return jnp.concatenate([jnp.full((1, BLOCK), sinks_ref[layer, hh * KV_GROUP + g], F32) for g in range(KV_GROUP)],
                           axis=1)


def _softmax_t(qs, kdup, bias, sink):
    s = _dot(kdup, qs, NT) + bias
    m = jnp.maximum(jnp.max(s, axis=0, keepdims=True), sink)
    e = jnp.exp(s - m)
    es = jnp.exp(sink - m)
    r = 1.0 / (jnp.sum(e, axis=0, keepdims=True) + es)
    return e * r, es * r


def _lane_sums_row(x):
    hi = x.astype(BF16)
    lo = (x - hi.astype(F32)).astype(BF16)
    ones = jnp.ones((8, x.shape[1]), BF16)
    dims = (NT, ((), ()))
    return (lax.dot_general(ones, hi, dims, preferred_element_type=F32)
            + lax.dot_general(ones, lo, dims, preferred_element_type=F32))[0:1]


def _attn_fwd(h, sinks, bias, layer, *, S, D, name):
    T = h.shape[0]
    WB, KVW = D // 2, D // 16
    nb = S // BLOCK
    n_kv = KVW // HEAD_DIM

    def body(q_ref, k_ref, v_ref, sinks_ref, bias_ref, o_ref):
        n = pl.program_id(1)
        lo = _lo_mask()
        q = q_ref[...] * (HEAD_DIM ** -0.5)
        kk, vv = _window(k_ref, n), _window(v_ref, n)
        valid = bias_ref[...]
        blocks = []
        for hh in range(n_kv):
            cb = slice((hh // 2) * 128, (hh // 2 + 1) * 128)
            kdup, vdup = _dup(kk[:, cb], hh % 2, lo), _dup(vv[:, cb], hh % 2, lo)
            p_t, _ = _softmax_t(_stack_heads(q, hh, lo, True), kdup, valid, _sink_row(sinks_ref, layer, hh))
            blocks += _unstack_pairs_t(_dot(vdup, p_t, TN))
        o_ref[...] = jnp.concatenate(blocks, axis=1)

    return _pcall(
        body, name=name, grid=(T // S, nb),
        in_specs=[pl.BlockSpec((BLOCK, WB), lambda s, n: (s * nb + n, D // WB)),
                  pl.BlockSpec((S, KVW), lambda s, n: (s, (D + WB) // KVW)),
                  pl.BlockSpec((S, KVW), lambda s, n: (s, (D + WB) // KVW + 1)),
                  pl.BlockSpec(memory_space=pltpu.SMEM), _bias_spec()],
        out_shape=jax.ShapeDtypeStruct((T, WB), F32),
        out_specs=pl.BlockSpec((BLOCK, WB), lambda s, n: (s * nb + n, 0)),
        compiler_params=_params(("arbitrary", "arbitrary")),
    )(h, h, h, sinks, bias)


def _attn_bwd(h, yb, dyb, sinks, bias, dh, layer, *, S, D, name):
    T = h.shape[0]
    WB, KVW = D // 2, D // 16
    nb = S // BLOCK
    n_kv = KVW // HEAD_DIM

    def body(q_ref, k_ref, v_ref, o_ref, do_ref, sinks_ref, bias_ref, _, dh_ref, dsink_ref, dk_acc, dv_acc,
             q_buf, q_sems, kv_buf, kv_sems):
        s_id, n = pl.program_id(0), pl.program_id(1)
        lo = _lo_mask()

        @pl.when(n == 0)
        def _():
            dk_acc[...] = jnp.zeros_like(dk_acc)
            dv_acc[...] = jnp.zeros_like(dv_acc)

        @pl.when((n == 0) & (s_id == 0))
        def _():
            dsink_ref[...] = jnp.zeros_like(dsink_ref)

        scale = HEAD_DIM ** -0.5
        q, o, do = q_ref[...] * scale, o_ref[...], do_ref[...]
        kk, vv = _window(k_ref, n), _window(v_ref, n)
        valid = bias_ref[...]
        lane = lax.broadcasted_iota(jnp.int32, dsink_ref.shape, 1)
        dq_blocks, dk_heads, dv_heads = [], [], []
        dsink = jnp.zeros(dsink_ref.shape, F32)
        for hh in range(n_kv):
            cb = slice((hh // 2) * 128, (hh // 2 + 1) * 128)
            kdup, vdup = _dup(kk[:, cb], hh % 2, lo), _dup(vv[:, cb], hh % 2, lo)
            qs = _stack_heads(q, hh, lo, True)
            dos = _stack_heads(do, hh, lo, True)
            delta = _lane_sums_row(dos * _stack_heads(o, hh, lo, False))
            p_t, psink = _softmax_t(qs, kdup, valid, _sink_row(sinks_ref, layer, hh))
            dvr = _dot(p_t, dos, NN)
            dv_heads.append(dvr + pltpu.roll(dvr, HEAD_DIM, 1))
            ds_t = p_t * (_dot(vdup, dos, NT) - delta)
            dq_blocks += [b * scale for b in _unstack_pairs_t(_dot(kdup, ds_t, TN))]
            dkr = _dot(ds_t, qs, NN)
            dk_heads.append(dkr + pltpu.roll(dkr, HEAD_DIM, 1))
            dsk = -psink * delta
            for g in range(KV_GROUP):
                tot = jnp.sum(dsk[:, g * BLOCK:(g + 1) * BLOCK], axis=1, keepdims=True)
                dsink = dsink + jnp.where(lane == hh * KV_GROUP + g, tot, 0.0)
        dsink_ref[...] += dsink
        _write_tile(q_buf, q_sems, s_id * nb + n, (T // S) * nb, jnp.concatenate(dq_blocks, axis=1),
                    dh_ref.at[pl.ds(pl.multiple_of((s_id * nb + n) * BLOCK, BLOCK), BLOCK), pl.ds(D, WB)])
        pair = lambda hs: jnp.concatenate([jnp.where(lo, hs[2 * m], hs[2 * m + 1]) for m in range(n_kv // 2)], axis=1)
        dkk, dvv = pair(dk_heads), pair(dv_heads)
        prev = pl.ds(pl.multiple_of(jnp.maximum(n - 1, 0) * BLOCK, BLOCK), BLOCK)
        cur = pl.ds(pl.multiple_of(n * BLOCK, BLOCK), BLOCK)
        dk_acc[prev, :] += dkk[:BLOCK]
        dk_acc[cur, :] += dkk[BLOCK:]
        dv_acc[prev, :] += dvv[:BLOCK]
        dv_acc[cur, :] += dvv[BLOCK:]

        @pl.when(n == nb - 1)
        def _():
            seq_rows = pl.ds(pl.multiple_of(s_id * S, BLOCK), S)
            kv_buf[0] = dk_acc[...].astype(kv_buf.dtype)
            kv_buf[1] = dv_acc[...].astype(kv_buf.dtype)
            copies = [pltpu.make_async_copy(kv_buf.at[j], dh_ref.at[seq_rows, pl.ds(D + WB + j * KVW, KVW)], kv_sems.at[j])
                      for j in range(2)]
            for cp in copies:
                cp.start()
            for cp in copies:
                cp.wait()

    blk = lambda cb=0: pl.BlockSpec((BLOCK, WB), lambda s, n: (s * nb + n, cb))
    seq = lambda cb=0: pl.BlockSpec((S, KVW), lambda s, n: (s, cb))
    return _pcall(
        body, name=name, grid=(T // S, nb),
        in_specs=[blk(D // WB), seq((D + WB) // KVW), seq((D + WB) // KVW + 1), blk(), blk(),
                  pl.BlockSpec(memory_space=pltpu.SMEM), _bias_spec(), ANY],
        out_shape=(jax.ShapeDtypeStruct(dh.shape, dh.dtype), jax.ShapeDtypeStruct((8, 128), F32)),
        out_specs=(ANY, pl.BlockSpec((8, 128), lambda s, n: (0, 0))),
        scratch_shapes=[pltpu.VMEM((S, KVW), F32), pltpu.VMEM((S, KVW), F32),
                        pltpu.VMEM((2, BLOCK, WB), BF16), pltpu.SemaphoreType.DMA((2,)),
                        pltpu.VMEM((2, S, KVW), BF16), pltpu.SemaphoreType.DMA((2,))],
        input_output_aliases={7: 0},
        compiler_params=_params(("arbitrary", "arbitrary")),
    )(h, h, h, yb, dyb, sinks, bias, dh)


def _bg_specs(D, tm):
    return _colspecs(D + D // 2 + 2 * (D // 16), D // 2, tm, lambda i: i)


def _mixb_fwd(yb, h, nb_g, mix, *, D, tm, name):
    T, WB = yb.shape
    bg_specs, _ = _bg_specs(D, tm)

    def body(*refs):
        refs = list(refs)
        yb_ref = refs.pop(0)
        bg = _cat(_take(refs, len(bg_specs)))
        g_ref, _, o_ref = refs
        _, _, nrm = _rms(yb_ref[...], g_ref[...])
        o_ref[...] = (nrm * (bg * _sigmoid(bg))).astype(o_ref.dtype)

    row = pl.BlockSpec((tm, WB), lambda i: (i, 0))
    return _pcall(body, name=name, grid=(T // tm,),
                  in_specs=[row] + bg_specs + [pl.BlockSpec((1, WB), lambda i: (0, 0)), ANY],
                  out_shape=jax.ShapeDtypeStruct(mix.shape, mix.dtype), out_specs=row,
                  input_output_aliases={len(bg_specs) + 2: 0},
                  compiler_params=_params(("arbitrary",)))(yb, *([h] * len(bg_specs)), nb_g, mix)


def _mixb_bwd(yb, h, dmix, nb_g, dh, *, D, tm, name):
    T, WB = yb.shape
    bg_off = D + D // 2 + 2 * (D // 16)
    bg_specs, _ = _bg_specs(D, tm)
    dm_specs, _ = _colspecs(0, WB, tm, lambda i: i)

    def body(*refs):
        refs = list(refs)
        yb_ref = refs.pop(0)
        bg = _cat(_take(refs, len(bg_specs)))
        dmix_b = _cat(_take(refs, len(dm_specs)))
        g_ref, _, dyb_ref, dh_ref, dg_ref, buf, sems = refs
        i = pl.program_id(0)

        @pl.when(i == 0)
        def _():
            dg_ref[...] = jnp.zeros_like(dg_ref)

        gamma = g_ref[...]
        xn, rstd, nrm = _rms(yb_ref[...], gamma)
        sg = _sigmoid(bg)
        dn = dmix_b * (bg * sg)
        _write_tile(buf, sems, i, T // tm, dmix_b * nrm * (sg * (1.0 + bg * (1.0 - sg))),
                    dh_ref.at[pl.ds(pl.multiple_of(i * tm, 16), tm), pl.ds(bg_off, WB)])
        dg_ref[0:1, :] += jnp.sum(dn * xn, axis=0, keepdims=True)
        dyb_ref[...] = _rms_bwd(dn, xn, rstd, gamma)

    row = pl.BlockSpec((tm, WB), lambda i: (i, 0))
    n_in = 1 + len(bg_specs) + len(dm_specs) + 1
    return _pcall(body, name=name, grid=(T // tm,),
                  in_specs=[row] + bg_specs + dm_specs + [pl.BlockSpec((1, WB), lambda i: (0, 0)), ANY],
                  out_shape=(jax.ShapeDtypeStruct((T, WB), F32), jax.ShapeDtypeStruct(dh.shape, dh.dtype),
                             jax.ShapeDtypeStruct((8, WB), F32)),
                  out_specs=(row, ANY, pl.BlockSpec((8, WB), lambda i: (0, 0))),
                  scratch_shapes=[pltpu.VMEM((2, tm, WB), BF16), pltpu.SemaphoreType.DMA((2,))],
                  input_output_aliases={n_in: 1},
                  compiler_params=_params(("arbitrary",)))(yb, *([h] * len(bg_specs)), *([dmix] * len(dm_specs)), nb_g, dh)


def _ln_fwd(z, g, b, *, tm, name):
    T, D = z.shape

    def body(z_ref, g_ref, b_ref, y_ref, yb_ref):
        zv = z_ref[...]
        mu = jnp.mean(zv, axis=-1, keepdims=True)
        zc = zv - mu
        var = jnp.mean(zc * zc, axis=-1, keepdims=True)
        y = zc * lax.rsqrt(var + LN_EPS) * g_ref[...] + b_ref[...]
        y_ref[...] = y
        yb_ref[...] = y.astype(BF16)

    row = pl.BlockSpec((tm, D), lambda i: (i, 0))
    vec = pl.BlockSpec((1, D), lambda i: (0, 0))
    return _pcall(body, name=name, grid=(T // tm,), in_specs=[row, vec, vec],
                  out_shape=(jax.ShapeDtypeStruct((T, D), F32), jax.ShapeDtypeStruct((T, D), BF16)),
                  out_specs=(row, row), compiler_params=_params(("parallel",)))(z, g, b)


def _ln_bwd(z, dy, g, *, tm, name, deps=()):
    T, D = z.shape

    def body(z_ref, dy_ref, g_ref, *rest):
        dz_ref, dzb_ref, dgb_ref = rest[len(deps):]

        @pl.when(pl.program_id(0) == 0)
        def _():
            dgb_ref[...] = jnp.zeros_like(dgb_ref)

        zv, dyv = z_ref[...], dy_ref[...]
        mu = jnp.mean(zv, axis=-1, keepdims=True)
        zc = zv - mu
        rstd = lax.rsqrt(jnp.mean(zc * zc, axis=-1, keepdims=True) + LN_EPS)
        xh = zc * rstd
        dxh = dyv * g_ref[...]
        dz = rstd * (dxh - jnp.mean(dxh, axis=-1, keepdims=True) - xh * jnp.mean(dxh * xh, axis=-1, keepdims=True))
        dz_ref[...] = dz
        dzb_ref[...] = dz.astype(BF16)
        dgb_ref[0:1, :] += jnp.sum(dyv * xh, axis=0, keepdims=True)
        dgb_ref[1:2, :] += jnp.sum(dyv, axis=0, keepdims=True)

    row = pl.BlockSpec((tm, D), lambda i: (i, 0))
    return _pcall(body, name=name, grid=(T // tm,),
                  in_specs=[row, row, pl.BlockSpec((1, D), lambda i: (0, 0))] + [ANY] * len(deps),
                  out_shape=(jax.ShapeDtypeStruct((T, D), F32), jax.ShapeDtypeStruct((T, D), BF16),
                             jax.ShapeDtypeStruct((8, D), F32)),
                  out_specs=(row, row, pl.BlockSpec((8, D), lambda i: (0, 0))),
                  compiler_params=_params(("arbitrary",)))(z, dy, g, *deps)


def _loss_head(y, target, *, tm, name):
    T, D = y.shape

    def body(y_ref, t_ref, dy_ref, loss_ref):
        @pl.when(pl.program_id(0) == 0)
        def _():
            loss_ref[...] = jnp.zeros_like(loss_ref)

        err = y_ref[...] - t_ref[...]
        dy_ref[...] = err / D
        loss_ref[...] += 0.5 * jnp.sum(jnp.mean(err * err, axis=-1, keepdims=True), axis=0, keepdims=True)

    row = pl.BlockSpec((tm, D), lambda i: (i, 0))
    return _pcall(body, name=name, grid=(T // tm,), in_specs=[row, row],
                  out_shape=(jax.ShapeDtypeStruct((T, D), F32), jax.ShapeDtypeStruct((1, 1), F32)),
                  out_specs=(row, pl.BlockSpec((1, 1), lambda i: (0, 0))),
                  compiler_params=_params(("arbitrary",)))(y, target)


def _cast_bf16(w, layer, *, name, deps=()):
    _, R, C = w.shape
    tr = _tile(R, 512, 8)

    def body(w_ref, *rest):
        rest[-1][...] = w_ref[...].astype(BF16)

    return _pcall(body, name=name, grid=(R // tr,),
                  in_specs=[pl.BlockSpec((None, tr, C), lambda i: (layer, i, 0))] + [ANY] * len(deps),
                  out_shape=jax.ShapeDtypeStruct((R, C), BF16), out_specs=pl.BlockSpec((tr, C), lambda i: (i, 0)),
                  compiler_params=_params(("parallel",)))(w, *deps)


def _cast_shard(w, layer, kind, *, name, deps=()):
    _, R, C = w.shape
    tr = _tile(R, 512, 16)
    nrb = R // tr
    if kind == "in":
        full, o_idx = (R, N_CHIPS * C), lambda i: (i, _my_chip())
    else:
        full, o_idx = (N_CHIPS * R, C), lambda i: (_out_pos(_my_chip()) * nrb + i, 0)

    def body(w_ref, *rest):
        rest[-1][...] = w_ref[...].astype(BF16)

    return _pcall(body, name=name, grid=(nrb,),
                  in_specs=[pl.BlockSpec((None, tr, C), lambda i: (layer, i, 0))] + [ANY] * len(deps),
                  out_shape=jax.ShapeDtypeStruct(full, BF16), out_specs=pl.BlockSpec((tr, C), o_idx),
                  compiler_params=_params(("parallel",)))(w, *deps)


def _adamw_layer(g, w, m, v, layer, bufs, *, name):
    L, R, C = w.shape
    tr = _tile(R, max(8, (1 << 19) // C // 8 * 8), 8)
    if bufs is None:
        bufs = [lax.empty((L, R, C), F32) for _ in range(4)]

    def body(g_ref, w_ref, m_ref, v_ref, b0, b1, b2, b3, go_ref, d_ref, nm_ref, nv_ref):
        gv = g_ref[...]
        nm = ADAM_B1 * m_ref[...] + (1.0 - ADAM_B1) * gv
        nv = ADAM_B2 * v_ref[...] + (1.0 - ADAM_B2) * (gv * gv)
        m_hat = nm / (1.0 - ADAM_B1 ** ADAM_STEP)
        v_hat = nv / (1.0 - ADAM_B2 ** ADAM_STEP)
        go_ref[...] = gv
        d_ref[...] = -ADAM_LR * (m_hat / (jnp.sqrt(v_hat) + ADAM_EPS) + ADAM_WD * w_ref[...])
        nm_ref[...] = nm
        nv_ref[...] = nv

    lay = pl.BlockSpec((None, tr, C), lambda i: (layer, i, 0))
    shp = jax.ShapeDtypeStruct((L, R, C), F32)
    return list(_pcall(body, name=name, grid=(R // tr,),
                       in_specs=[pl.BlockSpec((tr, C), lambda i: (i, 0)), lay, lay, lay] + [ANY] * 4,
                       out_shape=(shp,) * 4, out_specs=(lay,) * 4, input_output_aliases={4 + k: k for k in range(4)},
                       compiler_params=_params(("parallel",)))(g, w, m, v, *bufs))


def _adamw(g, w, m, v, *, name):
    R, C = g.shape
    tr = _tile(R, max(8, (1 << 19) // C // 8 * 8), 8)

    def body(g_ref, w_ref, m_ref, v_ref, d_ref, nm_ref, nv_ref):
        gv = g_ref[...]
        nm = ADAM_B1 * m_ref[...] + (1.0 - ADAM_B1) * gv
        nv = ADAM_B2 * v_ref[...] + (1.0 - ADAM_B2) * (gv * gv)
        m_hat = nm / (1.0 - ADAM_B1 ** ADAM_STEP)
        v_hat = nv / (1.0 - ADAM_B2 ** ADAM_STEP)
        d_ref[...] = -ADAM_LR * (m_hat / (jnp.sqrt(v_hat) + ADAM_EPS) + ADAM_WD * w_ref[...])
        nm_ref[...] = nm
        nv_ref[...] = nv

    blk = pl.BlockSpec((tr, C), lambda i: (i, 0))
    shp = jax.ShapeDtypeStruct((R, C), F32)
    return _pcall(body, name=name, grid=(R // tr,), in_specs=[blk] * 4, out_shape=(shp, shp, shp),
                  out_specs=(blk, blk, blk), compiler_params=_params(("parallel",)))(g, w, m, v)


def _my_core():
    return lax.axis_index("c")


def _my_chip():
    return 2 * lax.axis_index("x") + lax.axis_index("y")


def _out_pos(chip):
    assert N_CHIPS == 4
    return jnp.where(chip == 3, 3, (chip + 2) % 3)


def _pair_sum(mine, theirs, *, half_axis, name):
    R, C = theirs.shape
    tr, tc = _tile(R, 512, 16), _tile(C, 2048)
    nrb, ncb = R // tr, C // tc

    def body(a_ref, b_ref, o_ref):
        o_ref[...] = (a_ref[...].astype(F32) + b_ref[...].astype(F32)).astype(BF16)

    if half_axis == 0:
        a_idx = lambda i, j: (_my_core() * nrb + i, j)
    else:
        a_idx = lambda i, j: (i, _my_core() * ncb + j)
    blk = pl.BlockSpec((tr, tc), lambda i, j: (i, j))
    return _pcall(body, name=name, grid=(nrb, ncb), in_specs=[pl.BlockSpec((tr, tc), a_idx), blk], out_specs=blk,
                  out_shape=jax.ShapeDtypeStruct(theirs.shape, BF16),
                  compiler_params=_params(("parallel", "parallel")))(mine, theirs)


def _final_sum(own, got, *, own_axis, out_shape, out_axis, name):
    _, R, C = got.shape
    tr, tc = _tile(R, 512, 16), _tile(C, 1024)
    nrb, ncb = R // tr, C // tc

    def body(a_ref, q_ref, o_ref):
        o_ref[...] = ((a_ref[...].astype(F32) + q_ref[0].astype(F32)) + q_ref[1].astype(F32)) + q_ref[2].astype(F32)

    if own_axis == 1:
        a_idx = lambda i, j: (i, _my_chip() * ncb + j)
    else:
        a_idx = lambda i, j: (_out_pos(_my_chip()) * nrb + i, j)
    if out_axis == 0:
        o_idx = lambda i, j: (_my_core() * nrb + i, j)
    else:
        o_idx = lambda i, j: (i, _my_core() * ncb + j)
    return _pcall(body, name=name, grid=(nrb, ncb),
                  in_specs=[pl.BlockSpec((tr, tc), a_idx), pl.BlockSpec((3, tr, tc), lambda i, j: (0, i, j))],
                  out_specs=pl.BlockSpec((tr, tc), o_idx), out_shape=jax.ShapeDtypeStruct(out_shape, F32),
                  compiler_params=_params(("parallel", "parallel")))(own, got)


def _sum_devices(gathered, *, name):
    _, R, C = gathered.shape
    tr = _tile(R, 280, 8)

    def body(g_ref, o_ref):
        acc = g_ref[0]
        for d in range(1, N_DEV):
            acc = acc + g_ref[d]
        o_ref[...] = acc

    return _pcall(body, name=name, grid=(R // tr,), in_specs=[pl.BlockSpec((N_DEV, tr, C), lambda i: (0, i, 0))],
                  out_shape=jax.ShapeDtypeStruct((R, C), F32), out_specs=pl.BlockSpec((tr, C), lambda i: (i, 0)),
                  compiler_params=_params(("parallel",)))(gathered)


def _position():
    x, y, c = lax.axis_index("x"), lax.axis_index("y"), lax.axis_index("c")
    chips = [(1 - x, y), (x, 1 - y), (1 - x, 1 - y)]
    return x, y, c, chips


def _remote(src, dst, send_sems, recv_sems, k, to):
    return pltpu.make_async_remote_copy(src_ref=src, dst_ref=dst, send_sem=send_sems.at[k], recv_sem=recv_sems.at[k],
                                        device_id=to, device_id_type=MESH)


def _r(ref, start, n):
    return ref.at[pl.ds(pl.multiple_of(start, 16), n), :]


def _c(ref, start, n):
    return ref.at[:, pl.ds(pl.multiple_of(start, 128), n)]


class _part:
    def __init__(self, ins, outs, plan, n, n_local=0, aliased=0):
        self.ins, self.outs, self.plan, self.n, self.n_local, self.aliased = ins, outs, plan, n, n_local, aliased


def _comm_scratch(parts):
    if not parts:
        return []
    n, nl = sum(p.n for p in parts), sum(p.n_local for p in parts)
    return [pltpu.SemaphoreType.DMA((n,)), pltpu.SemaphoreType.DMA((n,)), pltpu.SemaphoreType.DMA((max(nl, 1),))]


def _comm_aliases(parts, in_base, out_base):
    aliases, ii, oi = {}, in_base, out_base
    for p in parts:
        aliases.update({ii + k: oi + k for k in range(p.aliased)})
        ii += len(p.ins)
        oi += len(p.outs)
    return aliases


def _comm_run(parts, phase, in_refs, out_refs, send_sems, recv_sems, local_sems):
    pos = _position()
    me = pos[:3]
    ii = oi = si = li = 0
    for p in parts:
        sends, recvs, locs = p.plan(in_refs[ii:ii + len(p.ins)], out_refs[oi:oi + len(p.outs)], pos)
        assert len(sends) == len(recvs) == p.n and len(locs) == p.n_local
        if phase == "start":
            for k, (src, dst) in enumerate(locs):
                pltpu.make_async_copy(src, dst, local_sems.at[li + k]).start()
            for k, (src, dst, to) in enumerate(sends):
                _remote(src, dst, send_sems, recv_sems, si + k, to).start()
        else:
            for k, dst in enumerate(recvs):
                _remote(dst, dst, send_sems, recv_sems, si + k, me).wait_recv()
            for k, (src, dst, to) in enumerate(sends):
                _remote(src, dst, send_sems, recv_sems, si + k, to).wait_send()
            for k, (src, dst) in enumerate(locs):
                pltpu.make_async_copy(src, dst, local_sems.at[li + k]).wait()
        ii, oi, si, li = ii + len(p.ins), oi + len(p.outs), si + p.n, li + p.n_local


def _comm_call(parts, *, name):
    n_in = sum(len(p.ins) for p in parts)
    n_out = sum(len(p.outs) for p in parts)

    def body(*refs):
        refs = list(refs)
        cin, cout = _take(refs, n_in), _take(refs, n_out)
        _comm_run(parts, "start", cin, cout, *refs)
        _comm_run(parts, "finish", cin, cout, *refs)

    return list(_pcall(body, name=name, in_specs=[ANY] * n_in, out_specs=[ANY] * n_out,
                       out_shape=[s for p in parts for s in p.outs], scratch_shapes=_comm_scratch(parts),
                       input_output_aliases=_comm_aliases(parts, 0, 0))(*[a for p in parts for a in p.ins]))


HBM = pl.BlockSpec(memory_space=pltpu.HBM)
SEM = pl.BlockSpec(memory_space=pltpu.SEMAPHORE)
EFFECT = pltpu.SideEffectType.DATAFLOW_SIDE_EFFECTING


def _split_refs(parts, arr):
    out, i = [], 0
    for p in parts:
        ins = arr[i:i + len(p.ins)]
        i += len(p.ins)
        lands = arr[i:i + len(p.outs) - p.aliased]
        i += len(lands)
        out.append((ins, list(ins[:p.aliased]) + list(lands)))
    return out


def _split_start(parts, *, name, deps=()):
    assert all(p.n_local == 0 for p in parts)
    arrays = []
    for p in parts:
        arrays += list(p.ins) + [lax.empty(s.shape, s.dtype) for s in p.outs[p.aliased:]]
    n, na = sum(p.n for p in parts), len(arrays)

    def body(*refs):
        refs = list(refs)
        arr = _take(refs, na)
        _take(refs, len(deps))
        sems = _take(refs, 2 * n)
        token = refs[na]
        pos = _position()
        k = 0
        for p, (ins, outs) in zip(parts, _split_refs(parts, arr)):
            sends, _, _ = p.plan(ins, outs, pos)
            for src, dst, to in sends:
                pltpu.make_async_remote_copy(src_ref=src, dst_ref=dst, send_sem=sems[k], recv_sem=sems[n + k],
                                             device_id=to, device_id_type=MESH).start()
                k += 1
        token[...] = jnp.zeros_like(token)

    res = _pcall(
        body, name=name,
        out_shape=[pltpu.SemaphoreType.DMA(())] * (2 * n) + [pltpu.HBM(a.shape, a.dtype) for a in arrays]
        + [jax.ShapeDtypeStruct((8, 128), F32)],
        in_specs=[HBM] * na + [ANY] * len(deps),
        out_specs=[SEM] * (2 * n) + [HBM] * na + [pl.BlockSpec(memory_space=pltpu.VMEM)],
        input_output_aliases={i: 2 * n + i for i in range(na)},
        compiler_params=pltpu.CompilerParams(has_side_effects=EFFECT),
    )(*[pltpu.with_memory_space_constraint(a, pltpu.HBM) for a in arrays], *deps)
    return (list(res[:2 * n]), list(res[2 * n:2 * n + na])), res[-1]


def _split_wait(parts, state, after, *, name):
    sems, arrays = state
    n, na = len(sems) // 2, len(arrays)

    def body(*refs):
        refs = list(refs)
        arr = _take(refs, na)
        sm = _take(refs, 2 * n)
        pos = _position()
        me = pos[:3]
        k = 0
        for p, (ins, outs) in zip(parts, _split_refs(parts, arr)):
            sends, recvs, _ = p.plan(ins, outs, pos)
            for (src, dst, to), land in zip(sends, recvs):
                pltpu.make_async_remote_copy(src_ref=src, dst_ref=dst, send_sem=sm[k], recv_sem=sm[n + k],
                                             device_id=to, device_id_type=MESH).wait_send()
                pltpu.make_async_remote_copy(src_ref=land, dst_ref=land, send_sem=sm[k], recv_sem=sm[n + k],
                                             device_id=me, device_id_type=MESH).wait_recv()
                k += 1

    res = _pcall(
        body, name=name, out_shape=[pltpu.HBM(a.shape, a.dtype) for a in arrays],
        in_specs=[HBM] * na + [SEM] * (2 * n) + [ANY] * len(after), out_specs=[HBM] * na,
        input_output_aliases={i: i for i in range(na)},
        compiler_params=pltpu.CompilerParams(has_side_effects=EFFECT),
    )(*arrays, *sems, *after)
    return _split_refs(parts, list(res))


def _slab(wg, kind, chip, half):
    if kind == "in":
        d, ns = wg.shape[0], wg.shape[1] // N_CHIPS
        return _c(_r(wg, half * (d // 2), d // 2), chip * ns, ns)
    rs = wg.shape[0] // N_CHIPS
    return _r(wg, _out_pos(chip) * rs + half * (rs // 2), rs // 2)


def _gather_ici(wg, kind):
    def plan(ins, outs, pos):
        x, y, c, chips = pos
        (ref,) = outs
        mine = _slab(ref, kind, 2 * x + y, c)
        return [(mine, mine, (*chip, c)) for chip in chips], [_slab(ref, kind, 2 * px + py, c) for px, py in chips], []

    return _part([wg], [jax.ShapeDtypeStruct(wg.shape, wg.dtype)], plan, 3, aliased=1)


def _gather_d2d(wg, kind):
    def plan(ins, outs, pos):
        x, y, c, chips = pos
        (ref,) = outs
        sends = [(_slab(ref, kind, 2 * px + py, c), _slab(ref, kind, 2 * px + py, c), (x, y, 1 - c)) for px, py in chips]
        return sends, [_slab(ref, kind, 2 * px + py, 1 - c) for px, py in chips], []

    return _part([wg], [jax.ShapeDtypeStruct(wg.shape, wg.dtype)], plan, 3, aliased=1)


def _pair_send(gw, kind):
    rows, cols = gw.shape
    half = (rows // 2, cols) if kind == "in" else (rows, cols // 2)

    def plan(ins, outs, pos):
        x, y, c, _ = pos
        (src,), (rb,) = ins, outs
        theirs = _r(src, (1 - c) * half[0], half[0]) if kind == "in" else _c(src, (1 - c) * half[1], half[1])
        return [(theirs, rb, (x, y, 1 - c))], [rb], []

    return _part([gw], [jax.ShapeDtypeStruct(half, gw.dtype)], plan, 1)


def _chip_send(p, kind):
    rows, cols = p.shape
    shard = (rows, cols // N_CHIPS) if kind == "in" else (rows // N_CHIPS, cols)

    def plan(ins, outs, pos):
        x, y, c, chips = pos
        (src,), (q,) = ins, outs
        piece = lambda jk: (_c(src, jk * shard[1], shard[1]) if kind == "in"
                            else _r(src, _out_pos(jk) * shard[0], shard[0]))
        sends = [(piece(2 * px + py), q.at[kk], (px, py, c)) for kk, (px, py) in enumerate(chips)]
        return sends, [q.at[kk] for kk in range(3)], []

    return _part([p], [jax.ShapeDtypeStruct((3,) + shard, p.dtype)], plan, 3)


def _sibling_send(g, kind):
    rows, cols = g.shape

    def plan(ins, outs, pos):
        x, y, c, _ = pos
        (ref,) = outs
        half = (lambda h: _r(ref, h * (rows // 2), rows // 2)) if kind == "in" else (
            lambda h: _c(ref, h * (cols // 2), cols // 2))
        return [(half(c), half(c), (x, y, 1 - c))], [half(1 - c)], []

    return _part([g], [jax.ShapeDtypeStruct(g.shape, g.dtype)], plan, 1, aliased=1)


def _small_ici(block):
    def plan(ins, outs, pos):
        x, y, c, chips = pos
        (src,), (out,) = ins, outs
        mine = out.at[4 * x + 2 * y + c]
        peers = [(x, y, 1 - c)] + [(px, py, c) for px, py in chips]
        return [(src, mine, p) for p in peers], [out.at[4 * px + 2 * py + pc] for px, py, pc in peers], [(src, mine)]

    return _part([block], [jax.ShapeDtypeStruct((N_DEV,) + block.shape, block.dtype)], plan, 4, 1)


def _small_d2d(gathered):
    def plan(ins, outs, pos):
        x, y, c, chips = pos
        (out,) = outs
        sends = [(out.at[4 * px + 2 * py + c], out.at[4 * px + 2 * py + c], (x, y, 1 - c)) for px, py in chips]
        return sends, [out.at[4 * px + 2 * py + (1 - c)] for px, py in chips], []

    return _part([gathered], [jax.ShapeDtypeStruct(gathered.shape, gathered.dtype)], plan, 3, aliased=1)


_SMALL = ["gate_r_w", "gate_i_w", "conv_a_w", "conv_c_w", "sinks", "conv_c_b", "gate_r_b", "gate_i_b", "rg_lambda",
          "norm_a", "norm_b", "norm_c", "ln_g", "ln_b"]


def _pack_small(p):
    L = p["ln_g"].shape[0]
    rows = []
    for n in _SMALL:
        a = p[n]
        if n in ("gate_r_w", "gate_i_w", "norm_b", "ln_g", "ln_b"):
            a = a.reshape(L, -1, 1024)
        elif a.ndim == 2:
            a = a[:, None, :]
        if a.shape[-1] < 1024:
            a = jnp.pad(a, ((0, 0), (0, 0), (0, 1024 - a.shape[-1])))
        rows.append(a)
    out = jnp.concatenate(rows, axis=1)
    assert out.shape[1] == SMALL_ROWS
    return out.reshape(L * SMALL_ROWS, 1024)


def _unpack_small(flat, like):
    L = like["ln_g"].shape[0]
    a = flat.reshape(L, SMALL_ROWS, 1024)
    out, r = {}, 0
    for n in _SMALL:
        shp = like[n].shape
        nrows = max(1, math.prod(shp[1:]) // 1024) if n in ("gate_r_w", "gate_i_w", "norm_b", "ln_g", "ln_b") else (
            shp[1] if len(shp) == 3 else 1)
        blk = a[:, r:r + nrows, :]
        if n in ("gate_r_w", "gate_i_w", "norm_b", "ln_g", "ln_b"):
            out[n] = blk.reshape(shp)
        elif len(shp) == 3:
            out[n] = blk[:, :, :shp[2]]
        else:
            out[n] = blk[:, 0, :shp[1]]
        r += nrows
    return out


def kernel(x, w_in, conv_a_w, sinks, conv_c_w, conv_c_b, gate_r_w, gate_r_b, gate_i_w, gate_i_b, rg_lambda, norm_a, norm_b, norm_c, w_out, ln_g, ln_b, loss_target, m_w_in, m_conv_a_w, m_sinks, m_conv_c_w, m_conv_c_b, m_gate_r_w, m_gate_r_b, m_gate_i_w, m_gate_i_b, m_rg_lambda, m_norm_a, m_norm_b, m_norm_c, m_w_out, m_ln_g, m_ln_b, v_w_in, v_conv_a_w, v_sinks, v_conv_c_w, v_conv_c_b, v_gate_r_w, v_gate_r_b, v_gate_i_w, v_gate_i_b, v_rg_lambda, v_norm_a, v_norm_b, v_norm_c, v_w_out, v_ln_g, v_ln_b):
    names = ["w_in", "conv_a_w", "sinks", "conv_c_w", "conv_c_b", "gate_r_w", "gate_r_b", "gate_i_w", "gate_i_b",
             "rg_lambda", "norm_a", "norm_b", "norm_c", "w_out", "ln_g", "ln_b"]
    w = dict(zip(names, [w_in, conv_a_w, sinks, conv_c_w, conv_c_b, gate_r_w, gate_r_b, gate_i_w, gate_i_b, rg_lambda,
                         norm_a, norm_b, norm_c, w_out, ln_g, ln_b]))
    mom = dict(zip(names, [m_w_in, m_conv_a_w, m_sinks, m_conv_c_w, m_conv_c_b, m_gate_r_w, m_gate_r_b, m_gate_i_w,
                           m_gate_i_b, m_rg_lambda, m_norm_a, m_norm_b, m_norm_c, m_w_out, m_ln_g, m_ln_b]))
    vel = dict(zip(names, [v_w_in, v_conv_a_w, v_sinks, v_conv_c_w, v_conv_c_b, v_gate_r_w, v_gate_r_b, v_gate_i_w,
                           v_gate_i_b, v_rg_lambda, v_norm_a, v_norm_b, v_norm_c, v_w_out, v_ln_g, v_ln_b]))
    B, S, D = x.shape
    T = B * S
    L, _, NS = w_in.shape
    RS = w_out.shape[1]
    W = D // 4
    alpha = (2.0 * L) ** 0.25
    tt = _tile(S, 128, 8)
    tm_row = _tile(T, 256, 8)
    chip = _my_chip()

    wg_in, wg_out = [None] * L, [None] * L
    conv_local = jnp.concatenate([conv_a_w, conv_c_w], axis=1).reshape(L * 7, W // N_CHIPS)
    conv_local = jnp.pad(conv_local, ((0, (-L * 7) % 8), (0, 0)))
    (conv_all,) = _comm_call([_small_ici(conv_local)], name="conv_ici")
    ws_in, ws_out = [_cast_shard(w_in, 0, "in", name="cast_w_in")], [_cast_shard(w_out, 0, "out", name="cast_w_out")]
    g_parts = [_gather_ici(ws_in[0], "in"), _gather_ici(ws_out[0], "out")]
    g_state, g_token = _split_start(g_parts, name="gather_start0", deps=(conv_all,))
    ws_in += [_cast_shard(w_in, l, "in", name="cast_w_in", deps=(g_token,)) for l in range(1, L)]
    ws_out += [_cast_shard(w_out, l, "out", name="cast_w_out", deps=(g_token,)) for l in range(1, L)]
    xf = x.reshape(T, D)
    xb = _cast_bf16(xf[None], 0, name="cast_x", deps=(g_token,))
    (_, (part_in,)), (_, (part_out,)) = _split_wait(g_parts, g_state, ws_in[1:] + ws_out[1:] + [xb], name="gather_wait0")
    wg_in[0], wg_out[0], conv_all = _comm_call(
        [_gather_d2d(part_in, "in"), _gather_d2d(part_out, "out"), _small_d2d(conv_all)], name="gather0_d2d")
    conv_full = jnp.concatenate([conv_all[2 * jj][:L * 7] for jj in range(N_CHIPS)], axis=1).reshape(L, 7, W)
    caw_full, ccw_full = conv_full[:, :3], conv_full[:, 3:]

    def start_gather(layer, deps):
        parts = [_gather_ici(ws_in[layer], "in"), _gather_ici(ws_out[layer], "out")]
        return (parts, *_split_start(parts, name=f"gather_start{layer}", deps=deps))

    mask_bias = _mask_bias()
    saved = []
    flight = start_gather(1, (part_in,)) if L > 1 else None
    for l in range(L):
        nxt = l + 1 < L
        comm = [_gather_d2d(part_out, "out")] if l else []
        res = _mm(xb, wg_in[l], mode="nn", out_dtype=F32, name="proj_in", tm=1024, tn=768, tk=4096, comm=comm,
                  deps=(flight[2],) if nxt else ())
        h = res if not comm else res.pop(0)
        if l:
            wg_out[l] = res.pop(0)
        pv = jnp.stack([conv_c_b[l], gate_r_b[l], gate_i_b[l], rg_lambda[l], norm_a[l], norm_c[l]])
        mix, cv, xc, yc = _ac_fwd(h, caw_full[l], ccw_full[l], pv, gate_r_w[l], gate_i_w[l], S=S, D=D, tt=tt,
                                  name="ac_fwd")
        yb = _attn_fwd(h, sinks, mask_bias, l, S=S, D=D, name="attn_fwd")
        mix = _mixb_fwd(yb, h, norm_b[l][None], mix, D=D, tm=tm_row, name="mixb_fwd")
        comm, deps = [], ()
        if nxt:
            (_, (part_in,)), (_, (part_out,)) = _split_wait(flight[0], flight[1], [mix], name=f"gather_wait{l + 1}")
            comm = [_gather_d2d(part_in, "in")]
            flight = start_gather(l + 2, (part_in,)) if l + 2 < L else None
            deps = (flight[2],) if flight else ()
        res = _mm(mix, wg_out[l], mode="nn", out_dtype=F32, name="proj_out", tn=1024, tk=4096, add=xf, add_scale=alpha,
                  comm=comm, deps=deps)
        z = res if not comm else res.pop(0)
        if nxt:
            wg_in[l + 1] = res.pop(0)
        saved.append((xb, h, cv, xc, yc, yb, mix, z, pv))
        xf, xb = _ln_fwd(z, ln_g[l][None], ln_b[l][None], tm=tm_row, name="ln_fwd")
    dxn, loss_part = _loss_head(xf, loss_target.reshape(T, D), tm=tm_row, name="loss_head")
    loss = lax.psum(loss_part[0, 0], ("x", "y", "c"))

    def final_sums(p_in, q_in, p_out, q_out):
        return (_final_sum(p_in, q_in, own_axis=1, out_shape=(D, NS), out_axis=0, name="final_sum_in"),
                _final_sum(p_out, q_out, own_axis=0, out_shape=(RS, D), out_axis=1, name="final_sum_out"))

    bufs_in = bufs_out = None
    small_g = [None] * L
    ce = None
    for l in reversed(range(L)):
        up, last = l + 1 < L, l == 0
        xb_l, h, cv, xc, yc, yb, mix, z, pv = saved[l]
        dz, dzb, dgb = _ln_bwd(z, dxn, ln_g[l][None], tm=tm_row, name="ln_bwd", deps=(ce[2],) if up else ())
        dmix = _mm(dzb, wg_out[l], mode="nt", out_dtype=F32, name="d_mix", tk=4096)
        gw_out = _mm(mix, dzb, mode="tn", out_dtype=BF16, name="d_w_out", tk=4096)
        dh, vec, dwr, dwi = _ac_bwd(h, cv, xc, yc, dmix, caw_full[l], ccw_full[l], pv, gate_r_w[l], gate_i_w[l],
                                    S=S, D=D, tt=tt, name="ac_bwd")
        dyb, dh, dnb = _mixb_bwd(yb, h, dmix, norm_b[l][None], dh, D=D, tm=tm_row, name="mixb_bwd")
        dh, dsk = _attn_bwd(h, yb, dyb, sinks, mask_bias, dh, l, S=S, D=D, name="attn_bwd")
        small_g[l] = dict(gate_r_w=dwr, gate_i_w=dwi, conv_a_w=vec[0:3], conv_c_w=vec[4:8], sinks=dsk[0, :2 * D // 256],
                          conv_c_b=vec[8], gate_r_b=vec[9], gate_i_b=vec[10], rg_lambda=vec[11], norm_a=vec[3],
                          norm_b=dnb[0], norm_c=vec[12], ln_g=dgb[0], ln_b=dgb[1])
        comm = [_pair_send(gw_out, "out")]
        if last:
            comm.append(_small_ici(_pack_small({n: jnp.stack([small_g[k][n] for k in range(L)]) for n in _SMALL})))
        res = _mm(xb_l, dh, mode="tn", out_dtype=BF16, name="d_w_in", tm=1024, tn=768, tk=4096, comm=comm)
        gw_in, rb_out = _take(res, 2)
        p_out_l = _pair_sum(gw_out, rb_out, half_axis=1, name="pair_sum_out")
        d2d = [_pair_send(gw_in, "in")]
        if up:
            ((p_in,), (q_in,)), ((p_out,), (q_out,)) = _split_wait(ce[0], ce[1], [gw_in], name=f"chip_wait{l + 1}")
            g_in_half, g_out_half = final_sums(p_in, q_in, p_out, q_out)
            d2d += [_sibling_send(g_in_half, "in"), _sibling_send(g_out_half, "out")]
        if last:
            res = _comm_call(d2d + [_small_d2d(res.pop(0))], name="tail_d2d")
            small_all = res.pop()
        else:
            res = _mm(dh, wg_in[l], mode="nt", out_dtype=F32, name="d_x", tk=3584, add=dz, add_scale=alpha, comm=d2d)
            dxn = res.pop(0)
        rb_in = res.pop(0)
        if up:
            bufs_in = _adamw_layer(res[0], w_in, m_w_in, v_w_in, l + 1, bufs_in, name="adamw_w_in")
            bufs_out = _adamw_layer(res[1], w_out, m_w_out, v_w_out, l + 1, bufs_out, name="adamw_w_out")
        p_in_l = _pair_sum(gw_in, rb_in, half_axis=0, name="pair_sum_in")
        ce_parts = [_chip_send(p_in_l, "in"), _chip_send(p_out_l, "out")]
        ce = (ce_parts, *_split_start(ce_parts, name=f"chip_start{l}"))
    dxn = _mm(dh, wg_in[0], mode="nt", out_dtype=F32, name="d_x", tk=3584, add=dz, add_scale=alpha, deps=(ce[2],))
    grad_x = dxn.reshape(B, S, D)
    ((p_in,), (q_in,)), ((p_out,), (q_out,)) = _split_wait(ce[0], ce[1], [dxn] + (bufs_in or []) + (bufs_out or []),
                                                            name="chip_wait0")
    g_in_half, g_out_half = final_sums(p_in, q_in, p_out, q_out)
    g_in0, g_out0 = _comm_call([_sibling_send(g_in_half, "in"), _sibling_send(g_out_half, "out")], name="sibling0")
    big = {"w_in": _adamw_layer(g_in0, w_in, m_w_in, v_w_in, 0, bufs_in, name="adamw_w_in"),
           "w_out": _adamw_layer(g_out0, w_out, m_w_out, v_w_out, 0, bufs_out, name="adamw_w_out")}

    like = {n: w[n] for n in _SMALL}
    like_full = dict(like, conv_a_w=caw_full, conv_c_w=ccw_full)
    g_small = _unpack_small(_sum_devices(small_all, name="sum_small"), like_full)
    for n in ("conv_a_w", "conv_c_w"):
        g_small[n] = lax.dynamic_slice_in_dim(g_small[n], chip * (W // N_CHIPS), W // N_CHIPS, axis=2)

    d_s, m_s, v_s = _adamw(_pack_small(g_small), _pack_small(like), _pack_small({n: mom[n] for n in _SMALL}),
                           _pack_small({n: vel[n] for n in _SMALL}), name="adamw_small")
    grads = dict(g_small)
    delta, new_m, new_v = _unpack_small(d_s, like), _unpack_small(m_s, like), _unpack_small(v_s, like)
    for n in ("w_in", "w_out"):
        grads[n], delta[n], new_m[n], new_v[n] = big[n]

    return (loss, grad_x, *[grads[n] for n in names], *[delta[n] for n in names], *[new_m[n] for n in names],
            *[new_v[n] for n in names])
```

```python
import functools
import math

import jax
import jax.numpy as jnp
from jax import lax
from jax.experimental import pallas as pl
from jax.experimental.pallas import tpu as pltpu

F32 = jnp.float32
BF16 = jnp.bfloat16
_MXU_DTYPE = jnp.bfloat16

HEAD_DIM = 64
KV_GROUP = 8
BLOCK = 128
N_RG_HEADS = 8
RG_C = 8.0
LN_EPS = 1e-5
RMS_EPS = 1e-6
NEG_INF = -1e30
ADAM_LR, ADAM_B1, ADAM_B2, ADAM_EPS, ADAM_WD, ADAM_STEP = 0.001, 0.9, 0.999, 1e-08, 0.01, 10
N_CHIPS = 4
N_DEV = 8
SMALL_ROWS = 280
VMEM_LIMIT = 56 * 1024 * 1024

MESH = pl.DeviceIdType.MESH
ANY = pl.BlockSpec(memory_space=pl.ANY)


def _pcall(body, *, name, **kw):
    return pl.pallas_call(body, name=name, **kw)


def _params(sem=None):
    return pltpu.CompilerParams(dimension_semantics=sem, vmem_limit_bytes=VMEM_LIMIT)


def _tile(dim, pref, mult=128):
    best = None
    for t in range(mult, min(dim, pref) + 1, mult):
        if dim % t == 0:
            best = t
    return best if best is not None else dim


def _dot(a, b, dims):
    return lax.dot_general(a.astype(_MXU_DTYPE), b.astype(_MXU_DTYPE), (dims, ((), ())),
                           preferred_element_type=F32)


NN = ((1,), (0,))
NT = ((1,), (1,))
TN = ((0,), (0,))


def _mm(a, b, *, mode, out_dtype, name, tm=1024, tn=1024, tk=512, add=None, add_scale=1.0, comm=(), deps=()):
    if mode == "nn":
        (M, K), N = a.shape, b.shape[1]
    elif mode == "nt":
        (M, K), N = a.shape, b.shape[0]
    else:
        (K, M), N = a.shape, b.shape[1]
    tm, tn, tk = _tile(M, tm), _tile(N, tn), _tile(K, tk)
    ni, nj, nk = M // tm, N // tn, K // tk
    dims = {"nn": NN, "nt": NT, "tn": TN}[mode]
    n_cin = sum(len(p.ins) for p in comm)
    n_cout = sum(len(p.outs) for p in comm)

    def body(*refs):
        refs = list(refs)
        a_ref, b_ref = _take(refs, 2)
        add_ref = refs.pop(0) if add is not None else None
        _take(refs, len(deps))
        cin = _take(refs, n_cin)
        o_ref = refs.pop(0)
        cout = _take(refs, n_cout)
        acc = refs.pop(0) if nk > 1 else None
        i, j, k = pl.program_id(0), pl.program_id(1), pl.program_id(2)

        if comm:
            @pl.when((i == 0) & (j == 0) & (k == 0))
            def _():
                _comm_run(comm, "start", cin, cout, *refs)

        def finish(r):
            if add_ref is not None:
                r = r + add_scale * add_ref[...]
            o_ref[...] = r.astype(out_dtype)

        if nk == 1:
            finish(_dot(a_ref[...], b_ref[...], dims))
        else:
            @pl.when(k == 0)
            def _():
                acc[...] = jnp.zeros_like(acc)

            acc[...] += _dot(a_ref[...], b_ref[...], dims)

            @pl.when(k == nk - 1)
            def _():
                finish(acc[...])

        if comm:
            @pl.when((i == ni - 1) & (j == nj - 1) & (k == nk - 1))
            def _():
                _comm_run(comm, "finish", cin, cout, *refs)

    a_spec = {"nn": pl.BlockSpec((tm, tk), lambda i, j, k: (i, k)),
              "nt": pl.BlockSpec((tm, tk), lambda i, j, k: (i, k)),
              "tn": pl.BlockSpec((tk, tm), lambda i, j, k: (k, i))}[mode]
    b_spec = {"nn": pl.BlockSpec((tk, tn), lambda i, j, k: (k, j)),
              "nt": pl.BlockSpec((tn, tk), lambda i, j, k: (j, k)),
              "tn": pl.BlockSpec((tk, tn), lambda i, j, k: (k, j))}[mode]
    in_specs, operands = [a_spec, b_spec], [a, b]
    if add is not None:
        in_specs.append(pl.BlockSpec((tm, tn), lambda i, j, k: (i, j)))
        operands.append(add)
    in_specs += [ANY] * len(deps)
    operands += list(deps)
    aliases = _comm_aliases(comm, len(operands), 1)
    in_specs += [ANY] * n_cin
    operands += [arr for p in comm for arr in p.ins]
    out_shape = [jax.ShapeDtypeStruct((M, N), out_dtype)] + [s for p in comm for s in p.outs]
    out_specs = [pl.BlockSpec((tm, tn), lambda i, j, k: (i, j))] + [ANY] * n_cout
    sem = ("arbitrary",) * 3 if comm else ("parallel", "parallel", "arbitrary")
    res = _pcall(body, name=name, out_shape=out_shape, grid=(ni, nj, nk), in_specs=in_specs, out_specs=out_specs,
                 scratch_shapes=([pltpu.VMEM((tm, tn), F32)] if nk > 1 else []) + _comm_scratch(comm),
                 input_output_aliases=aliases,
                 compiler_params=_params(sem))(*operands)
    return list(res) if comm else res[0]


def _colspecs(off, width, rows, rowmap):
    bw = math.gcd(off, width) if off else width
    specs = [pl.BlockSpec((rows, bw), functools.partial(lambda cb, *g: (rowmap(*g), cb), off // bw + i))
             for i in range(width // bw)]
    return specs, bw


def _cat(refs):
    vals = [r[...] for r in refs]
    return vals[0] if len(vals) == 1 else jnp.concatenate(vals, axis=1)


def _take(refs, n):
    out = refs[:n]
    del refs[:n]
    return out


def _sigmoid(x):
    return 0.5 * jnp.tanh(0.5 * x) + 0.5


def _rms(y, gamma):
    rstd = lax.rsqrt(jnp.mean(y * y, axis=-1, keepdims=True) + RMS_EPS)
    xn = y * rstd
    return xn, rstd, xn * gamma


def _rms_bwd(dn, xn, rstd, gamma):
    dng = dn * gamma
    return rstd * (dng - xn * jnp.mean(dng * xn, axis=-1, keepdims=True))


def _shift_down(x, s, carry8):
    rolled = pltpu.roll(x, s, 0)
    cr = pltpu.roll(carry8, s, 0)
    row8 = lax.broadcasted_iota(jnp.int32, carry8.shape, 0)
    top = jnp.where(row8 < s, cr, rolled[0:8])
    return jnp.concatenate([top, rolled[8:]], axis=0)


def _shift_up(x, s, carry8):
    n = x.shape[0]
    rolled = pltpu.roll(x, n - s, 0)
    cr = pltpu.roll(carry8, 8 - s, 0)
    row8 = lax.broadcasted_iota(jnp.int32, carry8.shape, 0)
    bot = jnp.where(row8 >= 8 - s, cr, rolled[n - 8:])
    return jnp.concatenate([rolled[:n - 8], bot], axis=0)


def _chunk_scan(a, b):
    n = a.shape[0]
    r8 = lax.broadcasted_iota(jnp.int32, a.shape, 0) & 7
    for d in (1, 2, 4):
        ok = r8 >= d
        a_sh = jnp.where(ok, pltpu.roll(a, d, 0), 1.0)
        b_sh = jnp.where(ok, pltpu.roll(b, d, 0), 0.0)
        b = a * b_sh + b
        a = a * a_sh
    return a, b


def _chunk_scan_rev(c, b):
    n = c.shape[0]
    r8 = lax.broadcasted_iota(jnp.int32, c.shape, 0) & 7
    for d in (1, 2, 4):
        ok = r8 + d <= 7
        c_sh = jnp.where(ok, pltpu.roll(c, n - d, 0), 1.0)
        b_sh = jnp.where(ok, pltpu.roll(b, n - d, 0), 0.0)
        b = b + c * b_sh
        c = c * c_sh
    return c, b


def _log1p(x):
    w = 1.0 + x
    return jnp.where(w == 1.0, x, jnp.log(w) * (x / (w - 1.0)))


def _log_sigmoid(x):
    return jnp.minimum(x, 0.0) - _log1p(jnp.exp(-jnp.abs(x)))


def _expm1(x):
    u = jnp.exp(x)
    lu = jnp.log(u)
    small = jnp.where(u == 1.0, x, (u - 1.0) * (x / jnp.where(lu == 0.0, 1.0, lu)))
    return jnp.where(jnp.abs(x) < 0.5, small, u - 1.0)


def _gates(xc, wr_ref, wi_ref, br, bi, lam):
    hw = xc.shape[1] // N_RG_HEADS
    gr = jnp.concatenate([_dot(xc[:, h * hw:(h + 1) * hw], wr_ref[h], NN) for h in range(N_RG_HEADS)], axis=1) + br
    gi = jnp.concatenate([_dot(xc[:, h * hw:(h + 1) * hw], wi_ref[h], NN) for h in range(N_RG_HEADS)], axis=1) + bi
    r, i = _sigmoid(gr), _sigmoid(gi)
    ls = _log_sigmoid(lam)
    la = RG_C * r * ls
    a = jnp.exp(la)
    sq = jnp.sqrt(-_expm1(2.0 * la))
    return r, i, ls, a, sq


def _ac_fwd(h, caw, ccw, pv, wr, wi, *, S, D, tt, name):
    T = h.shape[0]
    W = D // 4
    nt = S // tt
    rowmap = lambda s, t: s * nt + t
    c_off = D + D // 2 + 2 * (D // 16) + D // 2
    offs = [0, W, 2 * W, 3 * W, c_off, c_off + W]
    in_specs, counts = [], []
    for off in offs:
        specs, _ = _colspecs(off, W, tt, rowmap)
        in_specs += specs
        counts.append(len(specs))
    full = lambda shape: pl.BlockSpec(shape, lambda s, t: (0,) * len(shape))
    in_specs += [full(caw.shape), full(ccw.shape), full(pv.shape), full(wr.shape), full(wi.shape)]

    def body(*refs):
        refs = list(refs)
        ab, ac, ax, ag, cx, cg = [_cat(_take(refs, n)) for n in counts]
        caw_ref, ccw_ref, pv_ref, wr_ref, wi_ref = _take(refs, 5)
        mixac_ref, cv_ref, xc_ref, yc_ref = _take(refs, 4)
        carry_p, carry_cx, carry_h, a_s, b_s = refs
        t = pl.program_id(1)

        @pl.when(t == 0)
        def _():
            carry_p[...] = jnp.zeros_like(carry_p)
            carry_cx[...] = jnp.zeros_like(carry_cx)
            carry_h[...] = jnp.zeros_like(carry_h)

        ccb, br, bi, lam, na, nc = [pv_ref[k:k + 1, :] for k in range(6)]
        p = ac * ax
        cp = carry_p[...]
        cv = caw_ref[2:3, :] * p + caw_ref[1:2, :] * _shift_down(p, 1, cp) + caw_ref[0:1, :] * _shift_down(p, 2, cp)
        carry_p[...] = p[tt - 8:tt]
        cv_ref[...] = cv
        _, _, n_a = _rms(ab * cv, na)
        mix_a = n_a * (ag * _sigmoid(ag))
        ccx = carry_cx[...]
        xc = (ccw_ref[3:4, :] * cx + ccw_ref[2:3, :] * _shift_down(cx, 1, ccx) + ccw_ref[1:2, :] * _shift_down(cx, 2, ccx)
              + ccw_ref[0:1, :] * _shift_down(cx, 3, ccx) + ccb)
        carry_cx[...] = cx[tt - 8:tt]
        xc_ref[...] = xc
        r, i, ls, a, sq = _gates(xc, wr_ref, wi_ref, br, bi, lam)
        u = sq * (i * xc)
        a_c, b_c = _chunk_scan(a, u)
        a_s[...] = a_c
        b_s[...] = b_c

        def step(k, hprev):
            rows = pl.ds(pl.multiple_of(k * 8, 8), 8)
            hc = a_s[rows, :] * hprev + b_s[rows, :]
            yc_ref[rows, :] = hc
            return hc[7:8, :]

        hlast = lax.fori_loop(0, tt // 8, step, carry_h[0:1, :])
        carry_h[...] = jnp.broadcast_to(hlast, carry_h.shape)
        _, _, n_c = _rms(yc_ref[...], nc)
        mix_c = n_c * (cg * _sigmoid(cg))
        mixac_ref[...] = jnp.concatenate([mix_a, mix_c], axis=1).astype(mixac_ref.dtype)

    row_blk = lambda w: pl.BlockSpec((tt, w), lambda s, t: (rowmap(s, t), 0))
    return _pcall(
        body, name=name, grid=(T // S, nt), in_specs=in_specs,
        out_shape=(jax.ShapeDtypeStruct((T, 4 * W), BF16), jax.ShapeDtypeStruct((T, W), F32),
                   jax.ShapeDtypeStruct((T, W), F32), jax.ShapeDtypeStruct((T, W), F32)),
        out_specs=(pl.BlockSpec((tt, 2 * W), lambda s, t: (rowmap(s, t), 1)), row_blk(W), row_blk(W), row_blk(W)),
        scratch_shapes=[pltpu.VMEM((8, W), F32), pltpu.VMEM((8, W), F32), pltpu.VMEM((8, W), F32),
                        pltpu.VMEM((tt, W), F32), pltpu.VMEM((tt, W), F32)],
        compiler_params=_params(("arbitrary", "arbitrary")),
    )(*([h] * sum(counts)), caw, ccw, pv, wr, wi)


def _ac_bwd(h, cv, xc, yc, dmix, caw, ccw, pv, wr, wi, *, S, D, tt, name):
    T = h.shape[0]
    W = D // 4
    nt = S // tt
    rowmap = lambda s, t: s * nt + (nt - 1 - t)
    c_off = D + D // 2 + 2 * (D // 16) + D // 2
    offs = [0, W, 2 * W, 3 * W, c_off, c_off + W]
    in_specs, counts = [], []
    for off in offs:
        specs, _ = _colspecs(off, W, tt, rowmap)
        in_specs += specs
        counts.append(len(specs))
    row_blk = lambda w, cb=0: pl.BlockSpec((tt, w), lambda s, t: (rowmap(s, t), cb))
    in_specs += [row_blk(W), row_blk(W), row_blk(W)]
    in_specs.append(pl.BlockSpec((8, W), lambda s, t: (jnp.maximum(rowmap(s, t) * (tt // 8) - 1, 0), 0)))
    in_specs += [row_blk(W, 2), row_blk(W, 3)]
    full = lambda shape: pl.BlockSpec(shape, lambda s, t: (0,) * len(shape))
    in_specs += [full(caw.shape), full(ccw.shape), full(pv.shape), full(wr.shape), full(wi.shape)]

    def body(*refs):
        refs = list(refs)
        ab, ac, ax, ag, cx, cg = [_cat(_take(refs, n)) for n in counts]
        cv_ref, xc_ref, yc_ref, halo_ref, dma_ref, dmc_ref, caw_ref, ccw_ref, pv_ref, wr_ref, wi_ref = _take(refs, 11)
        dha_ref, dhc_ref, vec_ref, dwr_ref, dwi_ref = _take(refs, 5)
        carry_dcv, carry_dxc, carry_a, carry_g, c_s, b_s, g_s = refs
        s_id, t = pl.program_id(0), pl.program_id(1)

        @pl.when(t == 0)
        def _():
            for cr in (carry_dcv, carry_dxc, carry_a, carry_g):
                cr[...] = jnp.zeros_like(cr)

        @pl.when((t == 0) & (s_id == 0))
        def _():
            vec_ref[...] = jnp.zeros_like(vec_ref)
            dwr_ref[...] = jnp.zeros_like(dwr_ref)
            dwi_ref[...] = jnp.zeros_like(dwi_ref)

        def acc_row(k, val):
            vec_ref[k:k + 1, :] += jnp.sum(val, axis=0, keepdims=True)

        ccb, br, bi, lam, na, nc = [pv_ref[k:k + 1, :] for k in range(6)]
        cv = cv_ref[...]
        dmix_a = dma_ref[...]
        p = ac * ax
        xn, rstd, n_a = _rms(ab * cv, na)
        sg = _sigmoid(ag)
        dn = dmix_a * (ag * sg)
        dag = dmix_a * n_a * (sg * (1.0 + ag * (1.0 - sg)))
        acc_row(3, dn * xn)
        dya = _rms_bwd(dn, xn, rstd, na)
        dab = dya * cv
        dcv = dya * ab
        cd = carry_dcv[...]
        d1, d2 = _shift_up(dcv, 1, cd), _shift_up(dcv, 2, cd)
        dp = caw_ref[2:3, :] * dcv + caw_ref[1:2, :] * d1 + caw_ref[0:1, :] * d2
        acc_row(2, p * dcv)
        acc_row(1, p * d1)
        acc_row(0, p * d2)
        carry_dcv[...] = dcv[0:8]
        dha_ref[...] = jnp.concatenate([dab, dp * ax, dp * ac, dag], axis=1).astype(dha_ref.dtype)
        xc = xc_ref[...]
        yc = yc_ref[...]
        dmix_c = dmc_ref[...]
        xn, rstd, n_c = _rms(yc, nc)
        sg = _sigmoid(cg)
        dn = dmix_c * (cg * sg)
        dcg = dmix_c * n_c * (sg * (1.0 + cg * (1.0 - sg)))
        acc_row(12, dn * xn)
        dyc = _rms_bwd(dn, xn, rstd, nc)
        r, i, ls, a, sq = _gates(xc, wr_ref, wi_ref, br, bi, lam)
        halo = jnp.where(t == nt - 1, 0.0, halo_ref[...])
        hprev = _shift_down(yc, 1, halo)
        c_c, b_c = _chunk_scan_rev(_shift_up(a, 1, carry_a[...]), dyc)
        c_s[...] = c_c
        b_s[...] = b_c

        def step(k, gnext):
            rows = pl.ds(pl.multiple_of((tt // 8 - 1 - k) * 8, 8), 8)
            gc = b_s[rows, :] + c_s[rows, :] * gnext
            g_s[rows, :] = gc
            return gc[0:1, :]

        lax.fori_loop(0, tt // 8, step, carry_g[0:1, :])
        g = g_s[...]
        carry_g[...] = g[0:8]
        carry_a[...] = a[0:8]
        da = g * hprev
        ixc = i * xc
        dsq = g * ixc
        di = g * sq * xc
        dxc = g * sq * i
        dla = da * a - dsq * (a * a) / sq
        dr = dla * (RG_C * ls)
        acc_row(11, dla * (RG_C * r) * (1.0 / (1.0 + jnp.exp(lam))))
        dgr = dr * r * (1.0 - r)
        dgi = di * i * (1.0 - i)
        acc_row(9, dgr)
        acc_row(10, dgi)
        hw = W // N_RG_HEADS
        parts = []
        for hd in range(N_RG_HEADS):
            sl = slice(hd * hw, (hd + 1) * hw)
            dwr_ref[hd] += _dot(xc[:, sl], dgr[:, sl], TN)
            dwi_ref[hd] += _dot(xc[:, sl], dgi[:, sl], TN)
            parts.append(_dot(dgr[:, sl], wr_ref[hd], NT) + _dot(dgi[:, sl], wi_ref[hd], NT))
        dxc = dxc + jnp.concatenate(parts, axis=1)
        ce = carry_dxc[...]
        e1, e2, e3 = _shift_up(dxc, 1, ce), _shift_up(dxc, 2, ce), _shift_up(dxc, 3, ce)
        dcx = ccw_ref[3:4, :] * dxc + ccw_ref[2:3, :] * e1 + ccw_ref[1:2, :] * e2 + ccw_ref[0:1, :] * e3
        acc_row(7, cx * dxc)
        acc_row(6, cx * e1)
        acc_row(5, cx * e2)
        acc_row(4, cx * e3)
        acc_row(8, dxc)
        carry_dxc[...] = dxc[0:8]
        dhc_ref[...] = jnp.concatenate([dcx, dcg], axis=1).astype(dhc_ref.dtype)

    const = lambda shape: pl.BlockSpec(shape, lambda s, t: (0,) * len(shape))
    return _pcall(
        body, name=name, grid=(T // S, nt), in_specs=in_specs,
        out_shape=(jax.ShapeDtypeStruct((T, 4 * W), BF16), jax.ShapeDtypeStruct((T, 2 * W), BF16),
                   jax.ShapeDtypeStruct((16, W), F32), jax.ShapeDtypeStruct(wr.shape, F32),
                   jax.ShapeDtypeStruct(wi.shape, F32)),
        out_specs=(row_blk(4 * W), row_blk(2 * W), const((16, W)), const(wr.shape), const(wi.shape)),
        scratch_shapes=[pltpu.VMEM((8, W), F32)] * 4 + [pltpu.VMEM((tt, W), F32)] * 3,
        compiler_params=_params(("arbitrary", "arbitrary")),
    )(*([h] * sum(counts)), cv, xc, yc, yc, dmix, dmix, caw, ccw, pv, wr, wi)


def _lo_mask():
    return lax.broadcasted_iota(jnp.int32, (1, 2 * HEAD_DIM), 1) < HEAD_DIM


def _dup(blk, odd, lo):
    rot = pltpu.roll(blk, HEAD_DIM, 1)
    return jnp.where(lo, rot, blk) if odd else jnp.where(lo, blk, rot)


def _stack_heads(x, hh, lo, masked):
    parts = []
    for g in range(KV_GROUP):
        jq = hh * KV_GROUP + g
        pb = x[:, (jq // 2) * 128:(jq // 2 + 1) * 128]
        if masked:
            pb = jnp.where(lo if jq % 2 == 0 else jnp.logical_not(lo), pb, 0.0)
        parts.append(pb)
    return jnp.concatenate(parts, axis=0)


def _unstack_pairs_t(st_t):
    hi = lax.broadcasted_iota(jnp.int32, (2 * HEAD_DIM, BLOCK), 0) >= HEAD_DIM
    return [jnp.where(hi, st_t[:, (2 * pi + 1) * BLOCK:(2 * pi + 2) * BLOCK], st_t[:, (2 * pi) * BLOCK:(2 * pi + 1) * BLOCK]).T
            for pi in range(KV_GROUP // 2)]


def _window(ref, n):
    prev = ref[pl.ds(pl.multiple_of(jnp.maximum(n - 1, 0) * BLOCK, BLOCK), BLOCK), :]
    cur = ref[pl.ds(pl.multiple_of(n * BLOCK, BLOCK), BLOCK), :]
    return jnp.concatenate([prev, cur], axis=0)


def _mask_bias():
    kj = lax.broadcasted_iota(jnp.int32, (2 * BLOCK, KV_GROUP * BLOCK), 0)
    qi = lax.broadcasted_iota(jnp.int32, (2 * BLOCK, KV_GROUP * BLOCK), 1) & (BLOCK - 1)
    dist = qi + BLOCK - kj
    band = (dist >= 0) & (dist < BLOCK)
    return jnp.stack([jnp.where(band & (kj >= BLOCK), 0.0, NEG_INF), jnp.where(band, 0.0, NEG_INF)]).astype(F32)


def _bias_spec():
    return pl.BlockSpec((None, 2 * BLOCK, KV_GROUP * BLOCK), lambda s, n: (jnp.minimum(n, 1), 0, 0))


def _sink_row(sinks_ref, layer, hh):
    return jnp.concatenate([jnp.full((1, BLOCK), sinks_ref[layer, hh * KV_GROUP + g], F32) for g in range(KV_GROUP)],
                           axis=1)


def _softmax_t(qs, kdup, bias, sink):
    s = _dot(kdup, qs, NT) + bias
    m = jnp.maximum(jnp.max(s, axis=0, keepdims=True), sink)
    e = jnp.exp(s - m)
    es = jnp.exp(sink - m)
    r = 1.0 / (jnp.sum(e, axis=0, keepdims=True) + es)
    return e * r, es * r


def _lane_sums_row(x):
    hi = x.astype(BF16)
    lo = (x - hi.astype(F32)).astype(BF16)
    ones = jnp.ones((8, x.shape[1]), BF16)
    dims = (NT, ((), ()))
    return (lax.dot_general(ones, hi, dims, preferred_element_type=F32)
            + lax.dot_general(ones, lo, dims, preferred_element_type=F32))[0:1]


def _attn_fwd(h, sinks, bias, layer, *, S, D, name):
    T = h.shape[0]
    WB, KVW = D // 2, D // 16
    nb = S // BLOCK
    n_kv = KVW // HEAD_DIM

    def body(q_ref, k_ref, v_ref, sinks_ref, bias_ref, o_ref):
        n = pl.program_id(1)
        lo = _lo_mask()
        q = q_ref[...] * (HEAD_DIM ** -0.5)
        kk, vv = _window(k_ref, n), _window(v_ref, n)
        valid = bias_ref[...]
        blocks = []
        for hh in range(n_kv):
            cb = slice((hh // 2) * 128, (hh // 2 + 1) * 128)
            kdup, vdup = _dup(kk[:, cb], hh % 2, lo), _dup(vv[:, cb], hh % 2, lo)
            p_t, _ = _softmax_t(_stack_heads(q, hh, lo, True), kdup, valid, _sink_row(sinks_ref, layer, hh))
            blocks += _unstack_pairs_t(_dot(vdup, p_t, TN))
        o_ref[...] = jnp.concatenate(blocks, axis=1)

    return _pcall(
        body, name=name, grid=(T // S, nb),
        in_specs=[pl.BlockSpec((BLOCK, WB), lambda s, n: (s * nb + n, D // WB)),
                  pl.BlockSpec((S, KVW), lambda s, n: (s, (D + WB) // KVW)),
                  pl.BlockSpec((S, KVW), lambda s, n: (s, (D + WB) // KVW + 1)),
                  pl.BlockSpec(memory_space=pltpu.SMEM), _bias_spec()],
        out_shape=jax.ShapeDtypeStruct((T, WB), F32),
        out_specs=pl.BlockSpec((BLOCK, WB), lambda s, n: (s * nb + n, 0)),
        compiler_params=_params(("arbitrary", "arbitrary")),
    )(h, h, h, sinks, bias)


def _attn_bwd(h, yb, dyb, sinks, bias, layer, *, S, D, name):
    T = h.shape[0]
    WB, KVW = D // 2, D // 16
    nb = S // BLOCK
    n_kv = KVW // HEAD_DIM

    def body(q_ref, k_ref, v_ref, o_ref, do_ref, sinks_ref, bias_ref, dq_ref, dk_ref, dv_ref, dsink_ref, dk_acc, dv_acc):
        s_id, n = pl.program_id(0), pl.program_id(1)
        lo = _lo_mask()

        @pl.when(n == 0)
        def _():
            dk_acc[...] = jnp.zeros_like(dk_acc)
            dv_acc[...] = jnp.zeros_like(dv_acc)

        @pl.when((n == 0) & (s_id == 0))
        def _():
            dsink_ref[...] = jnp.zeros_like(dsink_ref)

        scale = HEAD_DIM ** -0.5
        q, o, do = q_ref[...] * scale, o_ref[...], do_ref[...]
        kk, vv = _window(k_ref, n), _window(v_ref, n)
        valid = bias_ref[...]
        lane = lax.broadcasted_iota(jnp.int32, dsink_ref.shape, 1)
        dq_blocks, dk_heads, dv_heads = [], [], []
        dsink = jnp.zeros(dsink_ref.shape, F32)
        for hh in range(n_kv):
            cb = slice((hh // 2) * 128, (hh // 2 + 1) * 128)
            kdup, vdup = _dup(kk[:, cb], hh % 2, lo), _dup(vv[:, cb], hh % 2, lo)
            qs = _stack_heads(q, hh, lo, True)
            dos = _stack_heads(do, hh, lo, True)
            delta = _lane_sums_row(dos * _stack_heads(o, hh, lo, False))
            p_t, psink = _softmax_t(qs, kdup, valid, _sink_row(sinks_ref, layer, hh))
            dvr = _dot(p_t, dos, NN)
            dv_heads.append(dvr + pltpu.roll(dvr, HEAD_DIM, 1))
            ds_t = p_t * (_dot(vdup, dos, NT) - delta)
            dq_blocks += [b * scale for b in _unstack_pairs_t(_dot(kdup, ds_t, TN))]
            dkr = _dot(ds_t, qs, NN)
            dk_heads.append(dkr + pltpu.roll(dkr, HEAD_DIM, 1))
            dsk = -psink * delta
            for g in range(KV_GROUP):
                tot = jnp.sum(dsk[:, g * BLOCK:(g + 1) * BLOCK], axis=1, keepdims=True)
                dsink = dsink + jnp.where(lane == hh * KV_GROUP + g, tot, 0.0)
        dsink_ref[...] += dsink
        dq_ref[...] = jnp.concatenate(dq_blocks, axis=1).astype(dq_ref.dtype)
        pair = lambda hs: jnp.concatenate([jnp.where(lo, hs[2 * m], hs[2 * m + 1]) for m in range(n_kv // 2)], axis=1)
        dkk, dvv = pair(dk_heads), pair(dv_heads)
        prev = pl.ds(pl.multiple_of(jnp.maximum(n - 1, 0) * BLOCK, BLOCK), BLOCK)
        cur = pl.ds(pl.multiple_of(n * BLOCK, BLOCK), BLOCK)
        dk_acc[prev, :] += dkk[:BLOCK]
        dk_acc[cur, :] += dkk[BLOCK:]
        dv_acc[prev, :] += dvv[:BLOCK]
        dv_acc[cur, :] += dvv[BLOCK:]

        @pl.when(n == nb - 1)
        def _():
            dk_ref[...] = dk_acc[...].astype(dk_ref.dtype)
            dv_ref[...] = dv_acc[...].astype(dv_ref.dtype)

    blk = lambda cb=0: pl.BlockSpec((BLOCK, WB), lambda s, n: (s * nb + n, cb))
    seq = lambda cb=0: pl.BlockSpec((S, KVW), lambda s, n: (s, cb))
    return _pcall(
        body, name=name, grid=(T // S, nb),
        in_specs=[blk(D // WB), seq((D + WB) // KVW), seq((D + WB) // KVW + 1), blk(), blk(),
                  pl.BlockSpec(memory_space=pltpu.SMEM), _bias_spec()],
        out_shape=(jax.ShapeDtypeStruct((T, WB), BF16), jax.ShapeDtypeStruct((T, KVW), BF16),
                   jax.ShapeDtypeStruct((T, KVW), BF16), jax.ShapeDtypeStruct((8, 128), F32)),
        out_specs=(blk(), seq(), seq(), pl.BlockSpec((8, 128), lambda s, n: (0, 0))),
        scratch_shapes=[pltpu.VMEM((S, KVW), F32), pltpu.VMEM((S, KVW), F32)],
        compiler_params=_params(("arbitrary", "arbitrary")),
    )(h, h, h, yb, dyb, sinks, bias)


def _bg_specs(D, tm):
    return _colspecs(D + D // 2 + 2 * (D // 16), D // 2, tm, lambda i: i)


def _mixb_fwd(yb, h, nb_g, mix, *, D, tm, name):
    T, WB = yb.shape
    bg_specs, _ = _bg_specs(D, tm)

    def body(*refs):
        refs = list(refs)
        yb_ref = refs.pop(0)
        bg = _cat(_take(refs, len(bg_specs)))
        g_ref, _, o_ref = refs
        _, _, nrm = _rms(yb_ref[...], g_ref[...])
        o_ref[...] = (nrm * (bg * _sigmoid(bg))).astype(o_ref.dtype)

    row = pl.BlockSpec((tm, WB), lambda i: (i, 0))
    return _pcall(body, name=name, grid=(T // tm,),
                  in_specs=[row] + bg_specs + [pl.BlockSpec((1, WB), lambda i: (0, 0)), ANY],
                  out_shape=jax.ShapeDtypeStruct(mix.shape, mix.dtype), out_specs=row,
                  input_output_aliases={len(bg_specs) + 2: 0},
                  compiler_params=_params(("arbitrary",)))(yb, *([h] * len(bg_specs)), nb_g, mix)


def _mixb_bwd(yb, h, dmix, nb_g, *, D, tm, name):
    T, WB = yb.shape
    bg_specs, _ = _bg_specs(D, tm)
    dm_specs, _ = _colspecs(0, WB, tm, lambda i: i)

    def body(*refs):
        refs = list(refs)
        yb_ref = refs.pop(0)
        bg = _cat(_take(refs, len(bg_specs)))
        dmix_b = _cat(_take(refs, len(dm_specs)))
        g_ref, dyb_ref, dbg_ref, dg_ref = refs

        @pl.when(pl.program_id(0) == 0)
        def _():
            dg_ref[...] = jnp.zeros_like(dg_ref)

        gamma = g_ref[...]
        xn, rstd, nrm = _rms(yb_ref[...], gamma)
        sg = _sigmoid(bg)
        dn = dmix_b * (bg * sg)
        dbg_ref[...] = (dmix_b * nrm * (sg * (1.0 + bg * (1.0 - sg)))).astype(dbg_ref.dtype)
        dg_ref[0:1, :] += jnp.sum(dn * xn, axis=0, keepdims=True)
        dyb_ref[...] = _rms_bwd(dn, xn, rstd, gamma)

    row = pl.BlockSpec((tm, WB), lambda i: (i, 0))
    return _pcall(body, name=name, grid=(T // tm,),
                  in_specs=[row] + bg_specs + dm_specs + [pl.BlockSpec((1, WB), lambda i: (0, 0))],
                  out_shape=(jax.ShapeDtypeStruct((T, WB), F32), jax.ShapeDtypeStruct((T, WB), BF16),
                             jax.ShapeDtypeStruct((8, WB), F32)),
                  out_specs=(row, row, pl.BlockSpec((8, WB), lambda i: (0, 0))),
                  compiler_params=_params(("arbitrary",)))(yb, *([h] * len(bg_specs)), *([dmix] * len(dm_specs)), nb_g)


def _concat_cols(parts, *, tm, name):
    parts = [p if isinstance(p, tuple) else (p, 0, p.shape[1]) for p in parts]
    T = parts[0][0].shape[0]
    total = sum(w for _, _, w in parts)

    def body(*refs):
        refs[-1][...] = jnp.concatenate([r[...] for r in refs[:-1]], axis=1)

    return _pcall(body, name=name, grid=(T // tm,),
                  in_specs=[pl.BlockSpec((tm, w), functools.partial(lambda cb, i: (i, cb), cb)) for _, cb, w in parts],
                  out_shape=jax.ShapeDtypeStruct((T, total), parts[0][0].dtype),
                  out_specs=pl.BlockSpec((tm, total), lambda i: (i, 0)),
                  compiler_params=_params(("parallel",)))(*[a for a, _, _ in parts])


def _ln_fwd(z, g, b, *, tm, name):
    T, D = z.shape

    def body(z_ref, g_ref, b_ref, y_ref, yb_ref):
        zv = z_ref[...]
        mu = jnp.mean(zv, axis=-1, keepdims=True)
        zc = zv - mu
        var = jnp.mean(zc * zc, axis=-1, keepdims=True)
        y = zc * lax.rsqrt(var + LN_EPS) * g_ref[...] + b_ref[...]
        y_ref[...] = y
        yb_ref[...] = y.astype(BF16)

    row = pl.BlockSpec((tm, D), lambda i: (i, 0))
    vec = pl.BlockSpec((1, D), lambda i: (0, 0))
    return _pcall(body, name=name, grid=(T // tm,), in_specs=[row, vec, vec],
                  out_shape=(jax.ShapeDtypeStruct((T, D), F32), jax.ShapeDtypeStruct((T, D), BF16)),
                  out_specs=(row, row), compiler_params=_params(("parallel",)))(z, g, b)


def _ln_bwd(z, dy, g, *, tm, name, deps=()):
    T, D = z.shape

    def body(z_ref, dy_ref, g_ref, *rest):
        dz_ref, dzb_ref, dgb_ref = rest[len(deps):]

        @pl.when(pl.program_id(0) == 0)
        def _():
            dgb_ref[...] = jnp.zeros_like(dgb_ref)

        zv, dyv = z_ref[...], dy_ref[...]
        mu = jnp.mean(zv, axis=-1, keepdims=True)
        zc = zv - mu
        rstd = lax.rsqrt(jnp.mean(zc * zc, axis=-1, keepdims=True) + LN_EPS)
        xh = zc * rstd
        dxh = dyv * g_ref[...]
        dz = rstd * (dxh - jnp.mean(dxh, axis=-1, keepdims=True) - xh * jnp.mean(dxh * xh, axis=-1, keepdims=True))
        dz_ref[...] = dz
        dzb_ref[...] = dz.astype(BF16)
        dgb_ref[0:1, :] += jnp.sum(dyv * xh, axis=0, keepdims=True)
        dgb_ref[1:2, :] += jnp.sum(dyv, axis=0, keepdims=True)

    row = pl.BlockSpec((tm, D), lambda i: (i, 0))
    return _pcall(body, name=name, grid=(T // tm,),
                  in_specs=[row, row, pl.BlockSpec((1, D), lambda i: (0, 0))] + [ANY] * len(deps),
                  out_shape=(jax.ShapeDtypeStruct((T, D), F32), jax.ShapeDtypeStruct((T, D), BF16),
                             jax.ShapeDtypeStruct((8, D), F32)),
                  out_specs=(row, row, pl.BlockSpec((8, D), lambda i: (0, 0))),
                  compiler_params=_params(("arbitrary",)))(z, dy, g, *deps)


def _ln_loss(z, g, b, target, *, tm, name):
    T, D = z.shape

    def body(z_ref, g_ref, b_ref, t_ref, dy_ref, loss_ref):
        @pl.when(pl.program_id(0) == 0)
        def _():
            loss_ref[...] = jnp.zeros_like(loss_ref)

        zv = z_ref[...]
        mu = jnp.mean(zv, axis=-1, keepdims=True)
        zc = zv - mu
        var = jnp.mean(zc * zc, axis=-1, keepdims=True)
        err = zc * lax.rsqrt(var + LN_EPS) * g_ref[...] + b_ref[...] - t_ref[...]
        dy_ref[...] = err / D
        loss_ref[...] += 0.5 * jnp.sum(jnp.mean(err * err, axis=-1, keepdims=True), axis=0, keepdims=True)

    row = pl.BlockSpec((tm, D), lambda i: (i, 0))
    vec = pl.BlockSpec((1, D), lambda i: (0, 0))
    return _pcall(body, name=name, grid=(T // tm,), in_specs=[row, vec, vec, row],
                  out_shape=(jax.ShapeDtypeStruct((T, D), F32), jax.ShapeDtypeStruct((1, 1), F32)),
                  out_specs=(row, pl.BlockSpec((1, 1), lambda i: (0, 0))),
                  compiler_params=_params(("arbitrary",)))(z, g, b, target)


def _cast_bf16(w, layer, *, name, deps=()):
    _, R, C = w.shape
    tr = _tile(R, 512, 8)

    def body(w_ref, *rest):
        rest[-1][...] = w_ref[...].astype(BF16)

    return _pcall(body, name=name, grid=(R // tr,),
                  in_specs=[pl.BlockSpec((None, tr, C), lambda i: (layer, i, 0))] + [ANY] * len(deps),
                  out_shape=jax.ShapeDtypeStruct((R, C), BF16), out_specs=pl.BlockSpec((tr, C), lambda i: (i, 0)),
                  compiler_params=_params(("parallel",)))(w, *deps)


def _cast_shard(w, layer, kind, *, name, deps=()):
    _, R, C = w.shape
    tr = _tile(R, 512, 16)
    nrb = R // tr
    if kind == "in":
        full, o_idx = (R, N_CHIPS * C), lambda i: (i, _my_chip())
    else:
        full, o_idx = (N_CHIPS * R, C), lambda i: (_out_pos(_my_chip()) * nrb + i, 0)

    def body(w_ref, *rest):
        rest[-1][...] = w_ref[...].astype(BF16)

    return _pcall(body, name=name, grid=(nrb,),
                  in_specs=[pl.BlockSpec((None, tr, C), lambda i: (layer, i, 0))] + [ANY] * len(deps),
                  out_shape=jax.ShapeDtypeStruct(full, BF16), out_specs=pl.BlockSpec((tr, C), o_idx),
                  compiler_params=_params(("parallel",)))(w, *deps)


def _adamw_layer(g, w, m, v, layer, bufs, *, name):
    L, R, C = w.shape
    tr = _tile(R, max(8, (1 << 19) // C // 8 * 8), 8)
    if bufs is None:
        bufs = [lax.empty((L, R, C), F32) for _ in range(4)]

    def body(g_ref, w_ref, m_ref, v_ref, b0, b1, b2, b3, go_ref, d_ref, nm_ref, nv_ref):
        gv = g_ref[...]
        nm = ADAM_B1 * m_ref[...] + (1.0 - ADAM_B1) * gv
        nv = ADAM_B2 * v_ref[...] + (1.0 - ADAM_B2) * (gv * gv)
        m_hat = nm / (1.0 - ADAM_B1 ** ADAM_STEP)
        v_hat = nv / (1.0 - ADAM_B2 ** ADAM_STEP)
        go_ref[...] = gv
        d_ref[...] = -ADAM_LR * (m_hat / (jnp.sqrt(v_hat) + ADAM_EPS) + ADAM_WD * w_ref[...])
        nm_ref[...] = nm
        nv_ref[...] = nv

    lay = pl.BlockSpec((None, tr, C), lambda i: (layer, i, 0))
    shp = jax.ShapeDtypeStruct((L, R, C), F32)
    return list(_pcall(body, name=name, grid=(R // tr,),
                       in_specs=[pl.BlockSpec((tr, C), lambda i: (i, 0)), lay, lay, lay] + [ANY] * 4,
                       out_shape=(shp,) * 4, out_specs=(lay,) * 4, input_output_aliases={4 + k: k for k in range(4)},
                       compiler_params=_params(("parallel",)))(g, w, m, v, *bufs))


def _adamw(g, w, m, v, *, name):
    R, C = g.shape
    tr = _tile(R, max(8, (1 << 19) // C // 8 * 8), 8)

    def body(g_ref, w_ref, m_ref, v_ref, d_ref, nm_ref, nv_ref):
        gv = g_ref[...]
        nm = ADAM_B1 * m_ref[...] + (1.0 - ADAM_B1) * gv
        nv = ADAM_B2 * v_ref[...] + (1.0 - ADAM_B2) * (gv * gv)
        m_hat = nm / (1.0 - ADAM_B1 ** ADAM_STEP)
        v_hat = nv / (1.0 - ADAM_B2 ** ADAM_STEP)
        d_ref[...] = -ADAM_LR * (m_hat / (jnp.sqrt(v_hat) + ADAM_EPS) + ADAM_WD * w_ref[...])
        nm_ref[...] = nm
        nv_ref[...] = nv

    blk = pl.BlockSpec((tr, C), lambda i: (i, 0))
    shp = jax.ShapeDtypeStruct((R, C), F32)
    return _pcall(body, name=name, grid=(R // tr,), in_specs=[blk] * 4, out_shape=(shp, shp, shp),
                  out_specs=(blk, blk, blk), compiler_params=_params(("parallel",)))(g, w, m, v)


def _my_core():
    return lax.axis_index("c")


def _my_chip():
    return 2 * lax.axis_index("x") + lax.axis_index("y")


def _out_pos(chip):
    assert N_CHIPS == 4
    return jnp.where(chip == 3, 3, (chip + 2) % 3)


def _pair_sum(mine, theirs, *, half_axis, name):
    R, C = theirs.shape
    tr, tc = _tile(R, 512, 16), _tile(C, 2048)
    nrb, ncb = R // tr, C // tc

    def body(a_ref, b_ref, o_ref):
        o_ref[...] = (a_ref[...].astype(F32) + b_ref[...].astype(F32)).astype(BF16)

    if half_axis == 0:
        a_idx = lambda i, j: (_my_core() * nrb + i, j)
    else:
        a_idx = lambda i, j: (i, _my_core() * ncb + j)
    blk = pl.BlockSpec((tr, tc), lambda i, j: (i, j))
    return _pcall(body, name=name, grid=(nrb, ncb), in_specs=[pl.BlockSpec((tr, tc), a_idx), blk], out_specs=blk,
                  out_shape=jax.ShapeDtypeStruct(theirs.shape, BF16),
                  compiler_params=_params(("parallel", "parallel")))(mine, theirs)


def _final_sum(own, got, *, own_axis, out_shape, out_axis, name):
    _, R, C = got.shape
    tr, tc = _tile(R, 512, 16), _tile(C, 1024)
    nrb, ncb = R // tr, C // tc

    def body(a_ref, q_ref, o_ref):
        o_ref[...] = ((a_ref[...].astype(F32) + q_ref[0].astype(F32)) + q_ref[1].astype(F32)) + q_ref[2].astype(F32)

    if own_axis == 1:
        a_idx = lambda i, j: (i, _my_chip() * ncb + j)
    else:
        a_idx = lambda i, j: (_out_pos(_my_chip()) * nrb + i, j)
    if out_axis == 0:
        o_idx = lambda i, j: (_my_core() * nrb + i, j)
    else:
        o_idx = lambda i, j: (i, _my_core() * ncb + j)
    return _pcall(body, name=name, grid=(nrb, ncb),
                  in_specs=[pl.BlockSpec((tr, tc), a_idx), pl.BlockSpec((3, tr, tc), lambda i, j: (0, i, j))],
                  out_specs=pl.BlockSpec((tr, tc), o_idx), out_shape=jax.ShapeDtypeStruct(out_shape, F32),
                  compiler_params=_params(("parallel", "parallel")))(own, got)


def _sum_devices(gathered, *, name):
    _, R, C = gathered.shape
    tr = _tile(R, 280, 8)

    def body(g_ref, o_ref):
        acc = g_ref[0]
        for d in range(1, N_DEV):
            acc = acc + g_ref[d]
        o_ref[...] = acc

    return _pcall(body, name=name, grid=(R // tr,), in_specs=[pl.BlockSpec((N_DEV, tr, C), lambda i: (0, i, 0))],
                  out_shape=jax.ShapeDtypeStruct((R, C), F32), out_specs=pl.BlockSpec((tr, C), lambda i: (i, 0)),
                  compiler_params=_params(("parallel",)))(gathered)


def _position():
    x, y, c = lax.axis_index("x"), lax.axis_index("y"), lax.axis_index("c")
    chips = [(1 - x, y), (x, 1 - y), (1 - x, 1 - y)]
    return x, y, c, chips


def _remote(src, dst, send_sems, recv_sems, k, to):
    return pltpu.make_async_remote_copy(src_ref=src, dst_ref=dst, send_sem=send_sems.at[k], recv_sem=recv_sems.at[k],
                                        device_id=to, device_id_type=MESH)


def _r(ref, start, n):
    return ref.at[pl.ds(pl.multiple_of(start, 16), n), :]


def _c(ref, start, n):
    return ref.at[:, pl.ds(pl.multiple_of(start, 128), n)]


class _part:
    def __init__(self, ins, outs, plan, n, n_local=0, aliased=0):
        self.ins, self.outs, self.plan, self.n, self.n_local, self.aliased = ins, outs, plan, n, n_local, aliased


def _comm_scratch(parts):
    if not parts:
        return []
    n, nl = sum(p.n for p in parts), sum(p.n_local for p in parts)
    return [pltpu.SemaphoreType.DMA((n,)), pltpu.SemaphoreType.DMA((n,)), pltpu.SemaphoreType.DMA((max(nl, 1),))]


def _comm_aliases(parts, in_base, out_base):
    aliases, ii, oi = {}, in_base, out_base
    for p in parts:
        aliases.update({ii + k: oi + k for k in range(p.aliased)})
        ii += len(p.ins)
        oi += len(p.outs)
    return aliases


def _comm_run(parts, phase, in_refs, out_refs, send_sems, recv_sems, local_sems):
    pos = _position()
    me = pos[:3]
    ii = oi = si = li = 0
    for p in parts:
        sends, recvs, locs = p.plan(in_refs[ii:ii + len(p.ins)], out_refs[oi:oi + len(p.outs)], pos)
        assert len(sends) == len(recvs) == p.n and len(locs) == p.n_local
        if phase == "start":
            for k, (src, dst) in enumerate(locs):
                pltpu.make_async_copy(src, dst, local_sems.at[li + k]).start()
            for k, (src, dst, to) in enumerate(sends):
                _remote(src, dst, send_sems, recv_sems, si + k, to).start()
        else:
            for k, dst in enumerate(recvs):
                _remote(dst, dst, send_sems, recv_sems, si + k, me).wait_recv()
            for k, (src, dst, to) in enumerate(sends):
                _remote(src, dst, send_sems, recv_sems, si + k, to).wait_send()
            for k, (src, dst) in enumerate(locs):
                pltpu.make_async_copy(src, dst, local_sems.at[li + k]).wait()
        ii, oi, si, li = ii + len(p.ins), oi + len(p.outs), si + p.n, li + p.n_local


def _comm_call(parts, *, name):
    n_in = sum(len(p.ins) for p in parts)
    n_out = sum(len(p.outs) for p in parts)

    def body(*refs):
        refs = list(refs)
        cin, cout = _take(refs, n_in), _take(refs, n_out)
        _comm_run(parts, "start", cin, cout, *refs)
        _comm_run(parts, "finish", cin, cout, *refs)

    return list(_pcall(body, name=name, in_specs=[ANY] * n_in, out_specs=[ANY] * n_out,
                       out_shape=[s for p in parts for s in p.outs], scratch_shapes=_comm_scratch(parts),
                       input_output_aliases=_comm_aliases(parts, 0, 0))(*[a for p in parts for a in p.ins]))


HBM = pl.BlockSpec(memory_space=pltpu.HBM)
SEM = pl.BlockSpec(memory_space=pltpu.SEMAPHORE)
EFFECT = pltpu.SideEffectType.DATAFLOW_SIDE_EFFECTING


def _split_refs(parts, arr):
    out, i = [], 0
    for p in parts:
        ins = arr[i:i + len(p.ins)]
        i += len(p.ins)
        lands = arr[i:i + len(p.outs) - p.aliased]
        i += len(lands)
        out.append((ins, list(ins[:p.aliased]) + list(lands)))
    return out


def _split_start(parts, *, name, deps=()):
    assert all(p.n_local == 0 for p in parts)
    arrays = []
    for p in parts:
        arrays += list(p.ins) + [lax.empty(s.shape, s.dtype) for s in p.outs[p.aliased:]]
    n, na = sum(p.n for p in parts), len(arrays)

    def body(*refs):
        refs = list(refs)
        arr = _take(refs, na)
        _take(refs, len(deps))
        sems = _take(refs, 2 * n)
        token = refs[na]
        pos = _position()
        k = 0
        for p, (ins, outs) in zip(parts, _split_refs(parts, arr)):
            sends, _, _ = p.plan(ins, outs, pos)
            for src, dst, to in sends:
                pltpu.make_async_remote_copy(src_ref=src, dst_ref=dst, send_sem=sems[k], recv_sem=sems[n + k],
                                             device_id=to, device_id_type=MESH).start()
                k += 1
        token[...] = jnp.zeros_like(token)

    res = _pcall(
        body, name=name,
        out_shape=[pltpu.SemaphoreType.DMA(())] * (2 * n) + [pltpu.HBM(a.shape, a.dtype) for a in arrays]
        + [jax.ShapeDtypeStruct((8, 128), F32)],
        in_specs=[HBM] * na + [ANY] * len(deps),
        out_specs=[SEM] * (2 * n) + [HBM] * na + [pl.BlockSpec(memory_space=pltpu.VMEM)],
        input_output_aliases={i: 2 * n + i for i in range(na)},
        compiler_params=pltpu.CompilerParams(has_side_effects=EFFECT),
    )(*[pltpu.with_memory_space_constraint(a, pltpu.HBM) for a in arrays], *deps)
    return (list(res[:2 * n]), list(res[2 * n:2 * n + na])), res[-1]


def _split_wait(parts, state, after, *, name):
    sems, arrays = state
    n, na = len(sems) // 2, len(arrays)

    def body(*refs):
        refs = list(refs)
        arr = _take(refs, na)
        sm = _take(refs, 2 * n)
        pos = _position()
        me = pos[:3]
        k = 0
        for p, (ins, outs) in zip(parts, _split_refs(parts, arr)):
            sends, recvs, _ = p.plan(ins, outs, pos)
            for (src, dst, to), land in zip(sends, recvs):
                pltpu.make_async_remote_copy(src_ref=src, dst_ref=dst, send_sem=sm[k], recv_sem=sm[n + k],
                                             device_id=to, device_id_type=MESH).wait_send()
                pltpu.make_async_remote_copy(src_ref=land, dst_ref=land, send_sem=sm[k], recv_sem=sm[n + k],
                                             device_id=me, device_id_type=MESH).wait_recv()
                k += 1

    res = _pcall(
        body, name=name, out_shape=[pltpu.HBM(a.shape, a.dtype) for a in arrays],
        in_specs=[HBM] * na + [SEM] * (2 * n) + [ANY] * len(after), out_specs=[HBM] * na,
        input_output_aliases={i: i for i in range(na)},
        compiler_params=pltpu.CompilerParams(has_side_effects=EFFECT),
    )(*arrays, *sems, *after)
    return _split_refs(parts, list(res))


def _slab(wg, kind, chip, half):
    if kind == "in":
        d, ns = wg.shape[0], wg.shape[1] // N_CHIPS
        return _c(_r(wg, half * (d // 2), d // 2), chip * ns, ns)
    rs = wg.shape[0] // N_CHIPS
    return _r(wg, _out_pos(chip) * rs + half * (rs // 2), rs // 2)


def _gather_ici(wg, kind):
    def plan(ins, outs, pos):
        x, y, c, chips = pos
        (ref,) = outs
        mine = _slab(ref, kind, 2 * x + y, c)
        return [(mine, mine, (*chip, c)) for chip in chips], [_slab(ref, kind, 2 * px + py, c) for px, py in chips], []

    return _part([wg], [jax.ShapeDtypeStruct(wg.shape, wg.dtype)], plan, 3, aliased=1)


def _gather_d2d(wg, kind):
    def plan(ins, outs, pos):
        x, y, c, chips = pos
        (ref,) = outs
        sends = [(_slab(ref, kind, 2 * px + py, c), _slab(ref, kind, 2 * px + py, c), (x, y, 1 - c)) for px, py in chips]
        return sends, [_slab(ref, kind, 2 * px + py, 1 - c) for px, py in chips], []

    return _part([wg], [jax.ShapeDtypeStruct(wg.shape, wg.dtype)], plan, 3, aliased=1)


def _pair_send(gw, kind):
    rows, cols = gw.shape
    half = (rows // 2, cols) if kind == "in" else (rows, cols // 2)

    def plan(ins, outs, pos):
        x, y, c, _ = pos
        (src,), (rb,) = ins, outs
        theirs = _r(src, (1 - c) * half[0], half[0]) if kind == "in" else _c(src, (1 - c) * half[1], half[1])
        return [(theirs, rb, (x, y, 1 - c))], [rb], []

    return _part([gw], [jax.ShapeDtypeStruct(half, gw.dtype)], plan, 1)


def _chip_send(p, kind):
    rows, cols = p.shape
    shard = (rows, cols // N_CHIPS) if kind == "in" else (rows // N_CHIPS, cols)

    def plan(ins, outs, pos):
        x, y, c, chips = pos
        (src,), (q,) = ins, outs
        piece = lambda jk: (_c(src, jk * shard[1], shard[1]) if kind == "in"
                            else _r(src, _out_pos(jk) * shard[0], shard[0]))
        sends = [(piece(2 * px + py), q.at[kk], (px, py, c)) for kk, (px, py) in enumerate(chips)]
        return sends, [q.at[kk] for kk in range(3)], []

    return _part([p], [jax.ShapeDtypeStruct((3,) + shard, p.dtype)], plan, 3)


def _sibling_send(g, kind):
    rows, cols = g.shape

    def plan(ins, outs, pos):
        x, y, c, _ = pos
        (ref,) = outs
        half = (lambda h: _r(ref, h * (rows // 2), rows // 2)) if kind == "in" else (
            lambda h: _c(ref, h * (cols // 2), cols // 2))
        return [(half(c), half(c), (x, y, 1 - c))], [half(1 - c)], []

    return _part([g], [jax.ShapeDtypeStruct(g.shape, g.dtype)], plan, 1, aliased=1)


def _small_ici(block):
    def plan(ins, outs, pos):
        x, y, c, chips = pos
        (src,), (out,) = ins, outs
        mine = out.at[4 * x + 2 * y + c]
        peers = [(x, y, 1 - c)] + [(px, py, c) for px, py in chips]
        return [(src, mine, p) for p in peers], [out.at[4 * px + 2 * py + pc] for px, py, pc in peers], [(src, mine)]

    return _part([block], [jax.ShapeDtypeStruct((N_DEV,) + block.shape, block.dtype)], plan, 4, 1)


def _small_d2d(gathered):
    def plan(ins, outs, pos):
        x, y, c, chips = pos
        (out,) = outs
        sends = [(out.at[4 * px + 2 * py + c], out.at[4 * px + 2 * py + c], (x, y, 1 - c)) for px, py in chips]
        return sends, [out.at[4 * px + 2 * py + (1 - c)] for px, py in chips], []

    return _part([gathered], [jax.ShapeDtypeStruct(gathered.shape, gathered.dtype)], plan, 3, aliased=1)


_SMALL = ["gate_r_w", "gate_i_w", "conv_a_w", "conv_c_w", "sinks", "conv_c_b", "gate_r_b", "gate_i_b", "rg_lambda",
          "norm_a", "norm_b", "norm_c", "ln_g", "ln_b"]


def _pack_small(p):
    L = p["ln_g"].shape[0]
    rows = []
    for n in _SMALL:
        a = p[n]
        if n in ("gate_r_w", "gate_i_w", "norm_b", "ln_g", "ln_b"):
            a = a.reshape(L, -1, 1024)
        elif a.ndim == 2:
            a = a[:, None, :]
        if a.shape[-1] < 1024:
            a = jnp.pad(a, ((0, 0), (0, 0), (0, 1024 - a.shape[-1])))
        rows.append(a)
    out = jnp.concatenate(rows, axis=1)
    assert out.shape[1] == SMALL_ROWS
    return out.reshape(L * SMALL_ROWS, 1024)


def _unpack_small(flat, like):
    L = like["ln_g"].shape[0]
    a = flat.reshape(L, SMALL_ROWS, 1024)
    out, r = {}, 0
    for n in _SMALL:
        shp = like[n].shape
        nrows = max(1, math.prod(shp[1:]) // 1024) if n in ("gate_r_w", "gate_i_w", "norm_b", "ln_g", "ln_b") else (
            shp[1] if len(shp) == 3 else 1)
        blk = a[:, r:r + nrows, :]
        if n in ("gate_r_w", "gate_i_w", "norm_b", "ln_g", "ln_b"):
            out[n] = blk.reshape(shp)
        elif len(shp) == 3:
            out[n] = blk[:, :, :shp[2]]
        else:
            out[n] = blk[:, 0, :shp[1]]
        r += nrows
    return out


def kernel(x, w_in, conv_a_w, sinks, conv_c_w, conv_c_b, gate_r_w, gate_r_b, gate_i_w, gate_i_b, rg_lambda, norm_a, norm_b, norm_c, w_out, ln_g, ln_b, loss_target, m_w_in, m_conv_a_w, m_sinks, m_conv_c_w, m_conv_c_b, m_gate_r_w, m_gate_r_b, m_gate_i_w, m_gate_i_b, m_rg_lambda, m_norm_a, m_norm_b, m_norm_c, m_w_out, m_ln_g, m_ln_b, v_w_in, v_conv_a_w, v_sinks, v_conv_c_w, v_conv_c_b, v_gate_r_w, v_gate_r_b, v_gate_i_w, v_gate_i_b, v_rg_lambda, v_norm_a, v_norm_b, v_norm_c, v_w_out, v_ln_g, v_ln_b):
    names = ["w_in", "conv_a_w", "sinks", "conv_c_w", "conv_c_b", "gate_r_w", "gate_r_b", "gate_i_w", "gate_i_b",
             "rg_lambda", "norm_a", "norm_b", "norm_c", "w_out", "ln_g", "ln_b"]
    w = dict(zip(names, [w_in, conv_a_w, sinks, conv_c_w, conv_c_b, gate_r_w, gate_r_b, gate_i_w, gate_i_b, rg_lambda,
                         norm_a, norm_b, norm_c, w_out, ln_g, ln_b]))
    mom = dict(zip(names, [m_w_in, m_conv_a_w, m_sinks, m_conv_c_w, m_conv_c_b, m_gate_r_w, m_gate_r_b, m_gate_i_w,
                           m_gate_i_b, m_rg_lambda, m_norm_a, m_norm_b, m_norm_c, m_w_out, m_ln_g, m_ln_b]))
    vel = dict(zip(names, [v_w_in, v_conv_a_w, v_sinks, v_conv_c_w, v_conv_c_b, v_gate_r_w, v_gate_r_b, v_gate_i_w,
                           v_gate_i_b, v_rg_lambda, v_norm_a, v_norm_b, v_norm_c, v_w_out, v_ln_g, v_ln_b]))
    B, S, D = x.shape
    T = B * S
    L, _, NS = w_in.shape
    RS = w_out.shape[1]
    W = D // 4
    alpha = (2.0 * L) ** 0.25
    tt = _tile(S, 128, 8)
    tm_row = _tile(T, 256, 8)
    chip = _my_chip()

    wg_in, wg_out = [None] * L, [None] * L
    conv_local = jnp.concatenate([conv_a_w, conv_c_w], axis=1).reshape(L * 7, W // N_CHIPS)
    conv_local = jnp.pad(conv_local, ((0, (-L * 7) % 8), (0, 0)))
    (conv_all,) = _comm_call([_small_ici(conv_local)], name="conv_ici")
    ws_in, ws_out = [_cast_shard(w_in, 0, "in", name="cast_w_in")], [_cast_shard(w_out, 0, "out", name="cast_w_out")]
    g_parts = [_gather_ici(ws_in[0], "in"), _gather_ici(ws_out[0], "out")]
    g_state, g_token = _split_start(g_parts, name="gather_start0", deps=(conv_all,))
    ws_in += [_cast_shard(w_in, l, "in", name="cast_w_in", deps=(g_token,)) for l in range(1, L)]
    ws_out += [_cast_shard(w_out, l, "out", name="cast_w_out", deps=(g_token,)) for l in range(1, L)]
    xf = x.reshape(T, D)
    xb = _cast_bf16(xf[None], 0, name="cast_x", deps=(g_token,))
    (_, (part_in,)), (_, (part_out,)) = _split_wait(g_parts, g_state, ws_in[1:] + ws_out[1:] + [xb], name="gather_wait0")
    wg_in[0], wg_out[0], conv_all = _comm_call(
        [_gather_d2d(part_in, "in"), _gather_d2d(part_out, "out"), _small_d2d(conv_all)], name="gather0_d2d")
    conv_full = jnp.concatenate([conv_all[2 * jj][:L * 7] for jj in range(N_CHIPS)], axis=1).reshape(L, 7, W)
    caw_full, ccw_full = conv_full[:, :3], conv_full[:, 3:]

    def start_gather(layer, deps):
        parts = [_gather_ici(ws_in[layer], "in"), _gather_ici(ws_out[layer], "out")]
        return (parts, *_split_start(parts, name=f"gather_start{layer}", deps=deps))

    mask_bias = _mask_bias()
    saved = []
    flight = start_gather(1, (part_in,)) if L > 1 else None
    for l in range(L):
        nxt = l + 1 < L
        comm = [_gather_d2d(part_out, "out")] if l else []
        res = _mm(xb, wg_in[l], mode="nn", out_dtype=F32, name="proj_in", tm=1024, tn=768, tk=4096, comm=comm,
                  deps=(flight[2],) if nxt else ())
        h = res if not comm else res.pop(0)
        if l:
            wg_out[l] = res.pop(0)
        pv = jnp.stack([conv_c_b[l], gate_r_b[l], gate_i_b[l], rg_lambda[l], norm_a[l], norm_c[l]])
        mix, cv, xc, yc = _ac_fwd(h, caw_full[l], ccw_full[l], pv, gate_r_w[l], gate_i_w[l], S=S, D=D, tt=tt,
                                  name="ac_fwd")
        yb = _attn_fwd(h, sinks, mask_bias, l, S=S, D=D, name="attn_fwd")
        mix = _mixb_fwd(yb, h, norm_b[l][None], mix, D=D, tm=tm_row, name="mixb_fwd")
        comm, deps = [], ()
        if nxt:
            (_, (part_in,)), (_, (part_out,)) = _split_wait(flight[0], flight[1], [mix], name=f"gather_wait{l + 1}")
            comm = [_gather_d2d(part_in, "in")]
            flight = start_gather(l + 2, (part_in,)) if l + 2 < L else None
            deps = (flight[2],) if flight else ()
        res = _mm(mix, wg_out[l], mode="nn", out_dtype=F32, name="proj_out", tn=1024, tk=4096, add=xf, add_scale=alpha,
                  comm=comm, deps=deps)
        z = res if not comm else res.pop(0)
        if nxt:
            wg_in[l + 1] = res.pop(0)
        saved.append((xb, h, cv, xc, yc, yb, mix, z, pv))
        if nxt:
            xf, xb = _ln_fwd(z, ln_g[l][None], ln_b[l][None], tm=tm_row, name="ln_fwd")
    dxn, loss_part = _ln_loss(z, ln_g[L - 1][None], ln_b[L - 1][None], loss_target.reshape(T, D), tm=tm_row,
                              name="ln_loss")
    loss = lax.psum(loss_part[0, 0], ("x", "y", "c"))

    def final_sums(p_in, q_in, p_out, q_out):
        return (_final_sum(p_in, q_in, own_axis=1, out_shape=(D, NS), out_axis=0, name="final_sum_in"),
                _final_sum(p_out, q_out, own_axis=0, out_shape=(RS, D), out_axis=1, name="final_sum_out"))

    bufs_in = bufs_out = None
    small_g = [None] * L
    ce = None
    for l in reversed(range(L)):
        up, last = l + 1 < L, l == 0
        xb_l, h, cv, xc, yc, yb, mix, z, pv = saved[l]
        dz, dzb, dgb = _ln_bwd(z, dxn, ln_g[l][None], tm=tm_row, name="ln_bwd", deps=(ce[2],) if up else ())
        dmix = _mm(dzb, wg_out[l], mode="nt", out_dtype=F32, name="d_mix", tk=4096)
        gw_out = _mm(mix, dzb, mode="tn", out_dtype=BF16, name="d_w_out", tk=4096)
        dha, dhc, vec, dwr, dwi = _ac_bwd(h, cv, xc, yc, dmix, caw_full[l], ccw_full[l], pv, gate_r_w[l], gate_i_w[l],
                                          S=S, D=D, tt=tt, name="ac_bwd")
        dyb, dbg, dnb = _mixb_bwd(yb, h, dmix, norm_b[l][None], D=D, tm=tm_row, name="mixb_bwd")
        dq, dk, dv, dsk = _attn_bwd(h, yb, dyb, sinks, mask_bias, l, S=S, D=D, name="attn_bwd")
        dh = _concat_cols([dha, dq, dk, dv, dbg, dhc], tm=tm_row, name="concat_dh")
        small_g[l] = dict(gate_r_w=dwr, gate_i_w=dwi, conv_a_w=vec[0:3], conv_c_w=vec[4:8], sinks=dsk[0, :2 * D // 256],
                          conv_c_b=vec[8], gate_r_b=vec[9], gate_i_b=vec[10], rg_lambda=vec[11], norm_a=vec[3],
                          norm_b=dnb[0], norm_c=vec[12], ln_g=dgb[0], ln_b=dgb[1])
        if up:
            ((p_in,), (q_in,)), ((p_out,), (q_out,)) = _split_wait(ce[0], ce[1], [dh], name=f"chip_wait{l + 1}")
            g_in_half, g_out_half = final_sums(p_in, q_in, p_out, q_out)
        comm = [_pair_send(gw_out, "out")]
        if up:
            comm += [_sibling_send(g_in_half, "in"), _sibling_send(g_out_half, "out")]
        if last:
            comm.append(_small_ici(_pack_small({n: jnp.stack([small_g[k][n] for k in range(L)]) for n in _SMALL})))
        res = _mm(xb_l, dh, mode="tn", out_dtype=BF16, name="d_w_in", tm=1024, tn=768, tk=4096, comm=comm)
        gw_in, rb_out = _take(res, 2)
        p_out_l = _pair_sum(gw_out, rb_out, half_axis=1, name="pair_sum_out")
        if up:
            g_in_full, g_out_full = _take(res, 2)
            bufs_in = _adamw_layer(g_in_full, w_in, m_w_in, v_w_in, l + 1, bufs_in, name="adamw_w_in")
            bufs_out = _adamw_layer(g_out_full, w_out, m_w_out, v_w_out, l + 1, bufs_out, name="adamw_w_out")
        if last:
            rb_in, small_all = _comm_call([_pair_send(gw_in, "in"), _small_d2d(res.pop(0))], name="tail_d2d")
        else:
            dxn, rb_in = _mm(dh, wg_in[l], mode="nt", out_dtype=F32, name="d_x", tk=3584, add=dz, add_scale=alpha,
                             comm=[_pair_send(gw_in, "in")])
        p_in_l = _pair_sum(gw_in, rb_in, half_axis=0, name="pair_sum_in")
        ce_parts = [_chip_send(p_in_l, "in"), _chip_send(p_out_l, "out")]
        ce = (ce_parts, *_split_start(ce_parts, name=f"chip_start{l}"))
    dxn = _mm(dh, wg_in[0], mode="nt", out_dtype=F32, name="d_x", tk=3584, add=dz, add_scale=alpha, deps=(ce[2],))
    grad_x = dxn.reshape(B, S, D)
    ((p_in,), (q_in,)), ((p_out,), (q_out,)) = _split_wait(ce[0], ce[1], [dxn] + (bufs_in or []) + (bufs_out or []),
                                                            name="chip_wait0")
    g_in_half, g_out_half = final_sums(p_in, q_in, p_out, q_out)
    g_in0, g_out0 = _comm_call([_sibling_send(g_in_half, "in"), _sibling_send(g_out_half, "out")], name="sibling0")
    big = {"w_in": _adamw_layer(g_in0, w_in, m_w_in, v_w_in, 0, bufs_in, name="adamw_w_in"),
           "w_out": _adamw_layer(g_out0, w_out, m_w_out, v_w_out, 0, bufs_out, name="adamw_w_out")}

    like = {n: w[n] for n in _SMALL}
    like_full = dict(like, conv_a_w=caw_full, conv_c_w=ccw_full)
    g_small = _unpack_small(_sum_devices(small_all, name="sum_small"), like_full)
    for n in ("conv_a_w", "conv_c_w"):
        g_small[n] = lax.dynamic_slice_in_dim(g_small[n], chip * (W // N_CHIPS), W // N_CHIPS, axis=2)

    d_s, m_s, v_s = _adamw(_pack_small(g_small), _pack_small(like), _pack_small({n: mom[n] for n in _SMALL}),
                           _pack_small({n: vel[n] for n in _SMALL}), name="adamw_small")
    grads = dict(g_small)
    delta, new_m, new_v = _unpack_small(d_s, like), _unpack_small(m_s, like), _unpack_small(v_s, like)
    for n in ("w_in", "w_out"):
        grads[n], delta[n], new_m[n], new_v[n] = big[n]

    return (loss, grad_x, *[grads[n] for n in names], *[delta[n] for n in names], *[new_m[n] for n in names],
            *[new_v[n] for n in names])
```
